```python
import jax
import jax.numpy as jnp
from jax import lax
import numpy as np

D_MODEL = 2048
BATCH = 4
SEQ = 2048
DEPTH = 1
DEC_BATCH = 128
DEC_SEQ = 1
PAST_LEN = 16384
PAGE_SIZE = 128

D_POOL = D_MODEL // 4
POOL_WINDOWS = (2, 4, 8, 16)
POOL_GROUPS = len(POOL_WINDOWS)
POOL_CH = D_POOL // POOL_GROUPS
POOL_BUF = max(POOL_WINDOWS) - 1
D_RWKV = D_MODEL - D_POOL
HEAD_DIM = 64
RWKV_HEADS = D_RWKV // HEAD_DIM
DECAY_LORA = 96
AAA_LORA = 96
GATE_LORA = 256
N_SHIFT = 3 * D_RWKV + DECAY_LORA + AAA_LORA + GATE_LORA
N_IN = D_POOL + N_SHIFT
GN_EPS = 64e-5
N_EXPERTS = 256
TOP_K = 8
N_GROUPS = 8
TOPK_GROUPS = 4
D_EXPERT = 512
D_SHARED = 512
ROUTE_SCALE = 2.5
MOE_MIN_ROWS = 8
MOE_MAX_ROWS = 128
EPS = 1e-6
F32 = jnp.float32

kernel_name = 'hymba_pool_rwkv7_moe_decoder_step'


def _rmsnorm(x, g):
    xf = x.astype(F32)
    y = xf * lax.rsqrt(jnp.mean(xf * xf, axis=-1, keepdims=True) + EPS) * g.astype(F32)
    return y.astype(x.dtype)


def _pool_mixer(p, buf, start, pool_w, pool_scale):
    b, t, _ = p.shape
    ext = jnp.concatenate([buf.astype(p.dtype), p], axis=1)
    cs = jnp.cumsum(ext.astype(F32), axis=1)
    cs = jnp.concatenate([jnp.zeros((b, 1, D_POOL), F32), cs], axis=1)
    pos = start + jnp.arange(t, dtype=jnp.int32)
    means = []
    for gi, win in enumerate(POOL_WINDOWS):
        lo, hi = gi * POOL_CH, (gi + 1) * POOL_CH
        wsum = cs[:, POOL_BUF + 1:, lo:hi] - cs[:, POOL_BUF + 1 - win:POOL_BUF + 1 - win + t, lo:hi]
        cnt = jnp.minimum(pos + 1, win).astype(F32)[None, :, None]
        means.append(wsum / cnt)
    d = jnp.concatenate(means, axis=-1) - p.astype(F32)
    y = jnp.einsum('btgc,gcd->btgd', d.reshape(b, t, POOL_GROUPS, POOL_CH), pool_w.astype(F32))
    y = y.reshape(b, t, D_POOL) * pool_scale.astype(F32)
    return y.astype(p.dtype), ext[:, -POOL_BUF:]


def _wkv_step(S, inp):
    r_t, w_t, k_t, v_t, kk_t, b_t = inp
    sa = jnp.einsum('bhvk,bhk->bhv', S, -kk_t)
    S = S * w_t[:, :, None, :] + sa[..., None] * b_t[:, :, None, :] + v_t[..., None] * k_t[:, :, None, :]
    y = jnp.einsum('bhvk,bhk->bhv', S, r_t)
    return S, y


def _rwkv7(hx, S0, w0, w_decay_up, a0, w_aaa_up, w_gate_up, k_k, k_a, r_k, ln_x_w, ln_x_b):
    b, t, _ = hx.shape
    hf = hx.astype(F32)
    cuts = [D_RWKV, 2 * D_RWKV, 3 * D_RWKV, 3 * D_RWKV + DECAY_LORA, 3 * D_RWKV + DECAY_LORA + AAA_LORA]
    r, k, v, wl, al, gl = jnp.split(hf, cuts, axis=-1)
    logw = -jax.nn.softplus(-(w0.astype(F32) + jnp.tanh(wl) @ w_decay_up.astype(F32))) - 0.5
    decay = jnp.exp(-jnp.exp(logw))
    a = jax.nn.sigmoid(a0.astype(F32) + al @ w_aaa_up.astype(F32))
    g = jax.nn.sigmoid(gl) @ w_gate_up.astype(F32)
    heads = lambda z: z.reshape(b, t, RWKV_HEADS, HEAD_DIM)
    kk = heads(k * k_k.astype(F32))
    kk = kk / jnp.maximum(jnp.sqrt(jnp.sum(kk * kk, axis=-1, keepdims=True)), 1e-12)
    k = k * (1.0 + (a - 1.0) * k_a.astype(F32))
    rh, kh, vh, dh, ah = heads(r), heads(k), heads(v), heads(decay), heads(a)
    bh = kk * ah
    xs = tuple(jnp.moveaxis(z, 1, 0) for z in (rh, dh, kh, vh, kk, bh))
    S, ys = lax.scan(_wkv_step, S0.astype(F32), xs)
    y = jnp.moveaxis(ys, 0, 1)
    mu = jnp.mean(y, axis=-1, keepdims=True)
    var = jnp.mean(jnp.square(y - mu), axis=-1, keepdims=True)
    yn = (y - mu) * lax.rsqrt(var + GN_EPS)
    yn = yn * ln_x_w.astype(F32).reshape(RWKV_HEADS, HEAD_DIM) + ln_x_b.astype(F32).reshape(RWKV_HEADS, HEAD_DIM)
    bonus = jnp.sum(rh * kh * r_k.astype(F32), axis=-1, keepdims=True) * vh
    out = (yn + bonus).reshape(b, t, D_RWKV) * g
    return out.astype(hx.dtype), S.astype(S0.dtype)


def _token_mixer(u, shift_prev, pool_buf, wkv0, start, w_in, mu_shift, pool_w, pool_scale, w0, w_decay_up,
                 a0, w_aaa_up, w_gate_up, k_k, k_a, r_k, ln_x_w, ln_x_b, w_out):
    h = jnp.einsum('btd,dn->btn', u, w_in)
    hp = h[..., :D_POOL]
    hr = h[..., D_POOL:]
    hr_first = jnp.einsum('bd,dn->bn', shift_prev.astype(u.dtype), w_in[:, D_POOL:])
    hr_prev = jnp.concatenate([hr_first[:, None], hr[:, :-1]], axis=1)
    hx = hr + (hr_prev - hr) * mu_shift
    pool_out, new_buf = _pool_mixer(hp, pool_buf, start, pool_w, pool_scale)
    rwkv_out, new_wkv = _rwkv7(hx, wkv0, w0, w_decay_up, a0, w_aaa_up, w_gate_up, k_k, k_a, r_k, ln_x_w, ln_x_b)
    mixed = jnp.concatenate([pool_out, rwkv_out], axis=-1)
    return jnp.einsum('btc,cd->btd', mixed, w_out), new_buf, new_wkv


def _moe_rows(n_assign):
    per_expert = max(1, n_assign // N_EXPERTS)
    rows = MOE_MIN_ROWS
    while rows < MOE_MAX_ROWS and rows * 2 <= per_expert:
        rows *= 2
    return rows


def _routed_experts(xf, top_idx, top_w, exp_gate, exp_up, exp_down):
    n, d = xf.shape
    n_assign = n * TOP_K
    rows = _moe_rows(n_assign)
    n_groups = -(-(n_assign + N_EXPERTS * (rows - 1)) // rows)
    flat_e = top_idx.reshape(-1).astype(jnp.int32)
    flat_tok = jnp.arange(n_assign, dtype=jnp.int32) // TOP_K
    flat_w = top_w.reshape(-1)
    order = jnp.argsort(flat_e)
    sorted_e = flat_e[order]
    counts = jnp.bincount(flat_e, length=N_EXPERTS).astype(jnp.int32)
    groups_per_e = (counts + rows - 1) // rows
    group_end = jnp.cumsum(groups_per_e)
    group_start = group_end - groups_per_e
    row_start = jnp.cumsum(counts) - counts
    rank = jnp.arange(n_assign, dtype=jnp.int32) - row_start[sorted_e]
    slot = group_start[sorted_e] * rows + rank
    slot_row = jnp.full((n_groups * rows,), n_assign, jnp.int32).at[slot].set(jnp.arange(n_assign, dtype=jnp.int32))
    tok_pad = jnp.concatenate([flat_tok[order], jnp.zeros((1,), jnp.int32)])[slot_row].reshape(n_groups, rows)
    w_pad = jnp.concatenate([flat_w[order], jnp.zeros((1,), flat_w.dtype)])[slot_row].reshape(n_groups, rows)
    group_e = jnp.minimum(jnp.searchsorted(group_end, jnp.arange(n_groups, dtype=jnp.int32), side='right'),
                          N_EXPERTS - 1)

    def step(acc, inp):
        e, tok, w = inp
        xb = xf[tok]
        hb = jax.nn.silu(xb @ exp_gate[e]) * (xb @ exp_up[e])
        yb = (hb @ exp_down[e]).astype(F32) * w[:, None]
        return acc.at[tok].add(yb), None

    acc, _ = lax.scan(step, jnp.zeros((n, d), F32), (group_e, tok_pad, w_pad))
    return acc


def _moe(u, w_router, router_bias, exp_gate, exp_up, exp_down, sh_gate, sh_up, sh_down):
    b, t, d = u.shape
    n = b * t
    xf = u.reshape(n, d)
    scores = jax.nn.sigmoid(jnp.einsum('nd,de->ne', xf.astype(F32), w_router.astype(F32)))
    sel = scores + router_bias.astype(F32)
    grp = sel.reshape(n, N_GROUPS, N_EXPERTS // N_GROUPS)
    gscore = jnp.sum(lax.top_k(grp, 2)[0], axis=-1)
    _, gidx = lax.top_k(gscore, TOPK_GROUPS)
    gmask = jnp.sum(jax.nn.one_hot(gidx, N_GROUPS, dtype=F32), axis=1)
    emask = jnp.repeat(gmask, N_EXPERTS // N_GROUPS, axis=1) > 0
    _, top_idx = lax.top_k(jnp.where(emask, sel, -jnp.inf), TOP_K)
    top_w = jnp.take_along_axis(scores, top_idx, axis=-1)
    top_w = top_w / jnp.sum(top_w, axis=-1, keepdims=True) * ROUTE_SCALE
    routed = _routed_experts(xf, top_idx, top_w, exp_gate, exp_up, exp_down)
    shared = (jax.nn.silu(xf @ sh_gate) * (xf @ sh_up)) @ sh_down
    return (routed + shared.astype(F32)).astype(u.dtype).reshape(b, t, d)


def _layer(x, c, shift_prev, pool_buf, wkv0, start, lp):
    (w_ada, b_ada, n_pre_mix, n_post_mix, n_pre_ffn, n_post_ffn, w_in, mu_shift, pool_w, pool_scale,
     w0, w_decay_up, a0, w_aaa_up, w_gate_up, k_k, k_a, r_k, ln_x_w, ln_x_b, w_out,
     w_router, router_bias, exp_gate, exp_up, exp_down, sh_gate, sh_up, sh_down) = lp
    ada = jnp.einsum('bd,dk->bk', jax.nn.silu(c), w_ada) + b_ada
    sh1, sc1, g1, sh2, sc2, g2 = jnp.split(ada[:, None, :], 6, axis=-1)
    u = _rmsnorm(x, n_pre_mix) * (1 + sc1) + sh1
    m, new_buf, new_wkv = _token_mixer(u, shift_prev, pool_buf, wkv0, start, w_in, mu_shift, pool_w, pool_scale,
                                       w0, w_decay_up, a0, w_aaa_up, w_gate_up, k_k, k_a, r_k, ln_x_w, ln_x_b,
                                       w_out)
    x = x + g1 * _rmsnorm(m, n_post_mix)
    u2 = _rmsnorm(x, n_pre_ffn) * (1 + sc2) + sh2
    f = _moe(u2, w_router, router_bias, exp_gate, exp_up, exp_down, sh_gate, sh_up, sh_down)
    x = x + g2 * _rmsnorm(f, n_post_ffn)
    return x, u[:, -1], new_buf, new_wkv


def setup_inputs(seed: int = 0) -> dict:
    key = jax.random.key(seed)
    ks = iter(jax.random.split(key, 48))

    def nrm(shape, scale):
        return jax.random.normal(next(ks), shape, F32) * scale

    def unif(shape, lo, hi):
        return jax.random.uniform(next(ks), shape, F32, lo, hi)

    L, D = DEPTH, D_MODEL
    return {
        'x_prompt': nrm((BATCH, SEQ, D), 1.0),
        'x_sample': nrm((DEC_BATCH, DEC_SEQ, D), 1.0),
        'c_prompt': nrm((BATCH, D), 1.0),
        'c_sample': nrm((DEC_BATCH, D), 1.0),
        'state_shift': nrm((L, DEC_BATCH, D), 1.0),
        'state_pool': nrm((L, DEC_BATCH, POOL_BUF, D_POOL), 1.0),
        'state_wkv': nrm((L, DEC_BATCH, RWKV_HEADS, HEAD_DIM, HEAD_DIM), 0.5),
        'w_ada': nrm((L, D, 6 * D), 0.5 * D ** -0.5),
        'b_ada': nrm((L, 6 * D), 0.05),
        'norm_pre_mix': 1.0 + nrm((L, D), 0.05),
        'norm_post_mix': 1.0 + nrm((L, D), 0.05),
        'norm_pre_ffn': 1.0 + nrm((L, D), 0.05),
        'norm_post_ffn': 1.0 + nrm((L, D), 0.05),
        'w_in': nrm((L, D, N_IN), D ** -0.5),
        'mu_shift': unif((L, N_SHIFT), 0.0, 1.0),
        'pool_w': nrm((L, POOL_GROUPS, POOL_CH, POOL_CH), POOL_CH ** -0.5),
        'pool_scale': 1.0 + nrm((L, D_POOL), 0.1),
        'w0': unif((L, D_RWKV), -6.0, 0.0),
        'w_decay_up': nrm((L, DECAY_LORA, D_RWKV), 0.5 * DECAY_LORA ** -0.5),
        'a0': nrm((L, D_RWKV), 0.5),
        'w_aaa_up': nrm((L, AAA_LORA, D_RWKV), AAA_LORA ** -0.5),
        'w_gate_up': nrm((L, GATE_LORA, D_RWKV), GATE_LORA ** -0.5),
        'k_k': 0.85 + nrm((L, D_RWKV), 0.05),
        'k_a': 1.0 + nrm((L, D_RWKV), 0.05),
        'r_k': nrm((L, RWKV_HEADS, HEAD_DIM), 0.1),
        'ln_x_w': 1.0 + nrm((L, D_RWKV), 0.05),
        'ln_x_b': nrm((L, D_RWKV), 0.02),
        'w_out': nrm((L, D, D), D ** -0.5),
        'w_router': nrm((L, D, N_EXPERTS), D ** -0.5),
        'router_bias': nrm((L, N_EXPERTS), 0.01),
        'exp_gate': nrm((L, N_EXPERTS, D, D_EXPERT), D ** -0.5),
        'exp_up': nrm((L, N_EXPERTS, D, D_EXPERT), D ** -0.5),
        'exp_down': nrm((L, N_EXPERTS, D_EXPERT, D), D_EXPERT ** -0.5),
        'sh_gate': nrm((L, D, D_SHARED), D ** -0.5),
        'sh_up': nrm((L, D, D_SHARED), D ** -0.5),
        'sh_down': nrm((L, D_SHARED, D), D_SHARED ** -0.5),
    }


def reference(x_prompt, x_sample, c_prompt, c_sample, state_shift, state_pool, state_wkv, w_ada, b_ada,
              norm_pre_mix, norm_post_mix, norm_pre_ffn, norm_post_ffn, w_in, mu_shift, pool_w, pool_scale,
              w0, w_decay_up, a0, w_aaa_up, w_gate_up, k_k, k_a, r_k, ln_x_w, ln_x_b, w_out, w_router,
              router_bias, exp_gate, exp_up, exp_down, sh_gate, sh_up, sh_down):
    xp, xs = x_prompt, x_sample
    bp = xp.shape[0]
    zero_shift = jnp.zeros((bp, D_MODEL), xp.dtype)
    zero_pool = jnp.zeros((bp, POOL_BUF, D_POOL), xp.dtype)
    zero_wkv = jnp.zeros((bp, RWKV_HEADS, HEAD_DIM, HEAD_DIM), xp.dtype)
    sp_l, pp_l, wp_l, ss_l, ps_l, ws_l = [], [], [], [], [], []
    for l in range(DEPTH):
        lp = (w_ada[l], b_ada[l], norm_pre_mix[l], norm_post_mix[l], norm_pre_ffn[l], norm_post_ffn[l],
              w_in[l], mu_shift[l], pool_w[l], pool_scale[l], w0[l], w_decay_up[l], a0[l], w_aaa_up[l],
              w_gate_up[l], k_k[l], k_a[l], r_k[l], ln_x_w[l], ln_x_b[l], w_out[l], w_router[l],
              router_bias[l], exp_gate[l], exp_up[l], exp_down[l], sh_gate[l], sh_up[l], sh_down[l])
        xp, sp, pp, wp = _layer(xp, c_prompt, zero_shift, zero_pool, zero_wkv, 0, lp)
        xs, ss, ps, ws = _layer(xs, c_sample, state_shift[l], state_pool[l], state_wkv[l], PAST_LEN, lp)
        sp_l.append(sp); pp_l.append(pp); wp_l.append(wp)
        ss_l.append(ss); ps_l.append(ps); ws_l.append(ws)
    new_shift_prompt = jnp.stack(sp_l)
    new_pool_prompt = jnp.stack(pp_l)
    new_wkv_prompt = jnp.stack(wp_l)
    new_shift_sample = jnp.stack(ss_l)
    new_pool_sample = jnp.stack(ps_l)
    new_wkv_sample = jnp.stack(ws_l)
    return (xp, xs, new_shift_prompt, new_pool_prompt, new_wkv_prompt, new_shift_sample, new_pool_sample, new_wkv_sample)
```

```python
import functools

import jax
import jax.numpy as jnp
from jax import lax
from jax.experimental import pallas as pl
from jax.experimental.pallas import tpu as pltpu

F32 = jnp.float32
BF16 = jnp.bfloat16

D_MODEL = 2048
D_POOL = 512
POOL_WINDOWS = (2, 4, 8, 16)
POOL_CH = 128
POOL_BUF = 15
D_RWKV = 1536
HEAD_DIM = 64
RWKV_HEADS = 24
DECAY_LORA = 96
AAA_LORA = 96
GATE_LORA = 256
LORA_PAD = 128
GN_EPS = 64e-5
N_EXPERTS = 256
TOP_K = 8
N_GROUPS = 8
GROUP_SIZE = N_EXPERTS // N_GROUPS
TOPK_GROUPS = 4
D_EXPERT = 512
ROUTE_SCALE = 2.5
EPS = 1e-6
PAST_LEN = 16384

COL_R, COL_K, COL_V = 0, D_RWKV, 2 * D_RWKV
COL_POOL = 3 * D_RWKV
COL_LORA = COL_POOL + D_POOL
N_IN_PAD = COL_LORA + 2 * LORA_PAD + GATE_LORA
LANES = 128
VMEM_LIMIT = 56 * 1024 * 1024


def _dot(a, b):
    return jnp.dot(a, b, preferred_element_type=F32)


def _split2(x):
    hi = x.astype(BF16)
    lo = (x - hi.astype(F32)).astype(BF16)
    return hi, lo


def _split3(x):
    hi = x.astype(BF16)
    r = x - hi.astype(F32)
    mid = r.astype(BF16)
    lo = (r - mid.astype(F32)).astype(BF16)
    return hi, mid, lo


def _dot_x3(a, b):
    ah, al = _split2(a)
    bh, bl = _split2(b)
    return _dot(ah, bh) + (_dot(ah, bl) + _dot(al, bh))


def _dot_exact_rhs(a, b_bf16):
    h, m, l = _split3(a)
    return _dot(h, b_bf16) + (_dot(m, b_bf16) + _dot(l, b_bf16))


def _sigmoid(x):
    return 1.0 / (1.0 + jnp.exp(-x))


def _silu(x):
    return x * _sigmoid(x)


def _rms(x, g):
    return x * lax.rsqrt(jnp.mean(x * x, axis=-1, keepdims=True) + EPS) * g


def _ada_kernel(c_ref, w_ref, b_ref, o_ref):
    o_ref[...] = _dot_x3(_silu(c_ref[...]), w_ref[...]) + b_ref[...]


def _ada(c_all, w_ada, b_ada):
    m, d = c_all.shape
    n = w_ada.shape[1]
    tn = 512
    return pl.pallas_call(
        _ada_kernel,
        grid=(n // tn,),
        in_specs=[pl.BlockSpec((m, d), lambda j: (0, 0)),
                  pl.BlockSpec((d, tn), lambda j: (0, j)),
                  pl.BlockSpec((1, tn), lambda j: (0, j))],
        out_specs=pl.BlockSpec((m, tn), lambda j: (0, j)),
        out_shape=jax.ShapeDtypeStruct((m, n), F32),
        compiler_params=pltpu.CompilerParams(dimension_semantics=("arbitrary",),
                                             vmem_limit_bytes=VMEM_LIMIT),
    )(c_all, w_ada, b_ada.reshape(1, n))


def _in_proj_kernel(*refs, tiles_per_seq, explicit_prev, tm):
    if explicit_prev:
        x_ref, g_ref, sc_ref, sh_ref, prev_ref, w_ref, mu_ref, hx_ref, u_ref, ub_scr = refs
    else:
        x_ref, g_ref, sc_ref, sh_ref, w_ref, mu_ref, hx_ref, u_ref, ub_scr, carry_scr = refs
    i = pl.program_id(0)
    j = pl.program_id(1)

    @pl.when(j == 0)
    def _():
        u = _rms(x_ref[...], g_ref[...])
        u = u * (1.0 + sc_ref[0]) + sh_ref[0]
        ub_scr[...] = u.astype(BF16)
        if explicit_prev:
            u_ref[...] = u
        else:
            u_ref[...] = u[tm - 8:, :]

    w = w_ref[...]
    h = _dot(ub_scr[...], w)
    if explicit_prev:
        hp = _dot(prev_ref[...].astype(BF16), w)
    else:
        @pl.when(i == 0)
        def _():
            carry_scr[j] = jnp.zeros(carry_scr.shape[1:], F32)

        first = jnp.where(i % tiles_per_seq == 0, 0.0, carry_scr[j, 0:1, :])
        row0 = lax.broadcasted_iota(jnp.int32, h.shape, 0) == 0
        hp = jnp.where(row0, first, pltpu.roll(h, 1, axis=0))
        carry_scr[j, 0:1, :] = h[tm - 1:tm, :]
    hx_ref[...] = h + (hp - h) * mu_ref[...]


def _in_proj(x, gamma, sc, sh, w_bf16, mu_pad, *, tm, tiles_per_seq, prev=None):
    m, d = x.shape
    n = w_bf16.shape[1]
    tn = 512
    nt = n // tn
    explicit_prev = prev is not None
    if explicit_prev:
        mod_spec = pl.BlockSpec((1, tm, d), lambda i, j: (i, 0, 0))
    else:
        mod_spec = pl.BlockSpec((1, 1, d), lambda i, j: (i // tiles_per_seq, 0, 0))
    in_specs = [pl.BlockSpec((tm, d), lambda i, j: (i, 0)),
                pl.BlockSpec((1, d), lambda i, j: (0, 0)),
                mod_spec, mod_spec]
    args = [x, gamma.reshape(1, d), sc, sh]
    if explicit_prev:
        in_specs.append(pl.BlockSpec((tm, d), lambda i, j: (i, 0)))
        args.append(prev)
    in_specs += [pl.BlockSpec((d, tn), lambda i, j: (0, j)),
                 pl.BlockSpec((1, tn), lambda i, j: (0, j))]
    args += [w_bf16, mu_pad.reshape(1, n)]
    scratch = [pltpu.VMEM((tm, d), BF16)]
    if explicit_prev:
        u_shape, u_spec = (m, d), pl.BlockSpec((tm, d), lambda i, j: (i, 0))
    else:
        u_shape, u_spec = (m // tm * 8, d), pl.BlockSpec((8, d), lambda i, j: (i, 0))
        scratch.append(pltpu.VMEM((nt, 8, tn), F32))
    return pl.pallas_call(
        functools.partial(_in_proj_kernel, tiles_per_seq=tiles_per_seq, explicit_prev=explicit_prev, tm=tm),
        grid=(m // tm, nt),
        in_specs=in_specs,
        out_specs=[pl.BlockSpec((tm, tn), lambda i, j: (i, j)), u_spec],
        out_shape=[jax.ShapeDtypeStruct((m, n), F32), jax.ShapeDtypeStruct(u_shape, F32)],
        scratch_shapes=scratch,
        compiler_params=pltpu.CompilerParams(dimension_semantics=("arbitrary", "arbitrary"),
                                             vmem_limit_bytes=VMEM_LIMIT),
    )(*args)


POOL_HALO = 16


def _pool_kernel(p_ref, pw_ref, ps_ref, o_ref, ext_scr, *, tiles_per_seq, full_count, tc):
    i = pl.program_id(0)

    @pl.when(i % tiles_per_seq == 0)
    def _():
        ext_scr[0:POOL_HALO, :] = jnp.zeros((POOL_HALO, D_POOL), F32)

    p = p_ref[...]
    ext_scr[POOL_HALO:POOL_HALO + tc, :] = p
    pos = (i % tiles_per_seq) * tc + lax.broadcasted_iota(jnp.int32, (tc, 1), 0)
    outs = []
    for gi, win in enumerate(POOL_WINDOWS):
        lo = gi * POOL_CH
        pg = p[:, lo:lo + POOL_CH]
        acc = pg
        for s in range(1, win):
            acc = acc + ext_scr[POOL_HALO - s:POOL_HALO - s + tc, lo:lo + POOL_CH]
        if full_count:
            dgi = acc / float(win) - pg
        else:
            cnt = jnp.minimum(pos + 1, win).astype(F32)
            dgi = acc / cnt - pg
        outs.append(_dot(dgi.astype(BF16), pw_ref[gi]))
    y = jnp.concatenate(outs, axis=-1) * ps_ref[...]
    o_ref[...] = y.astype(BF16)
    ext_scr[0:POOL_HALO, :] = ext_scr[tc:tc + POOL_HALO, :]


def _pool(src, col_block, pool_w_bf16, pool_scale, *, tc, tiles_per_seq, full_count):
    m = src.shape[0]
    return pl.pallas_call(
        functools.partial(_pool_kernel, tiles_per_seq=tiles_per_seq, full_count=full_count, tc=tc),
        grid=(m // tc,),
        in_specs=[pl.BlockSpec((tc, D_POOL), lambda i: (i, col_block)),
                  pl.BlockSpec((len(POOL_WINDOWS), POOL_CH, POOL_CH), lambda i: (0, 0, 0)),
                  pl.BlockSpec((1, D_POOL), lambda i: (0, 0))],
        out_specs=pl.BlockSpec((tc, D_POOL), lambda i: (i, 0)),
        out_shape=jax.ShapeDtypeStruct((m, D_POOL), BF16),
        scratch_shapes=[pltpu.VMEM((POOL_HALO + tc, D_POOL), F32)],
        compiler_params=pltpu.CompilerParams(dimension_semantics=("arbitrary",),
                                             vmem_limit_bytes=VMEM_LIMIT),
    )(src, pool_w_bf16, pool_scale.reshape(1, D_POOL))


def _softplus(z):
    return jnp.maximum(z, 0.0) + jnp.log1p(jnp.exp(-jnp.abs(z)))


def _rwkv_prep(r, k, v, lora, prm, seg, segt):
    (w0, wd, a0, wa, wg, k_k, k_a, r_k) = prm
    wl = lora[:, 0:LORA_PAD]
    al = lora[:, LORA_PAD:2 * LORA_PAD]
    gl = lora[:, 2 * LORA_PAD:]
    logw = -_softplus(-(w0 + _dot_x3(jnp.tanh(wl), wd))) - 0.5
    decay = jnp.exp(-jnp.exp(logw))
    a = _sigmoid(a0 + _dot_x3(al, wa))
    gate = _dot(_sigmoid(gl).astype(BF16), wg.astype(BF16))
    kk = k * k_k
    ss = _dot_exact_rhs(kk * kk, seg)
    inv = 1.0 / jnp.maximum(jnp.sqrt(ss), 1e-12)
    kk = kk * _dot_exact_rhs(inv, segt)
    k_mod = k * (1.0 + (a - 1.0) * k_a)
    bonus = _dot_exact_rhs(_dot_exact_rhs(r * k_mod * r_k, seg), segt) * v
    return decay, k_mod, -kk, kk * a, gate, bonus


def _wkv_step(s, eye, r_t, w_t, k_t, v_t, na_t, b_t):
    sa = jnp.sum(s * na_t, axis=1, keepdims=True)
    v_col = jnp.sum(jnp.where(eye, v_t, 0.0), axis=1, keepdims=True)
    s = s * w_t + sa * b_t + v_col * k_t
    y_col = jnp.sum(s * r_t, axis=1, keepdims=True)
    y_row = jnp.sum(jnp.where(eye, y_col, 0.0), axis=0, keepdims=True)
    return s, y_row


def _group_norm_head(y):
    mu = jnp.mean(y, axis=-1, keepdims=True)
    yc = y - mu
    var = jnp.mean(yc * yc, axis=-1, keepdims=True)
    return yc * lax.rsqrt(var + GN_EPS)


HEADS_PER_LOOP = 4


def _rwkv_seq_kernel(r_ref, k_ref, v_ref, lora_ref, w0_ref, wd_ref, a0_ref, wa_ref, wg_ref, kk_ref, ka_ref,
                     rk_ref, lnw_ref, lnb_ref, seg_ref, segt_ref, o_ref, s_ref,
                     r_scr, w_scr, k_scr, v_scr, a_scr, b_scr, y_scr, yn_scr, *, tc):
    c = pl.program_id(1)

    @pl.when(c == 0)
    def _():
        s_ref[...] = jnp.zeros(s_ref.shape, F32)

    r = r_ref[...]
    v = v_ref[...]
    prm = (w0_ref[...], wd_ref[...], a0_ref[...], wa_ref[...], wg_ref[...], kk_ref[...], ka_ref[...], rk_ref[...])
    decay, k_mod, neg_kk, b, gate, bonus = _rwkv_prep(r, k_ref[...], v, lora_ref[...], prm,
                                                       seg_ref[...], segt_ref[...])
    for h in range(RWKV_HEADS):
        sl = slice(h * HEAD_DIM, (h + 1) * HEAD_DIM)
        r_scr[h] = r[:, sl]
        w_scr[h] = decay[:, sl]
        k_scr[h] = k_mod[:, sl]
        v_scr[h] = v[:, sl]
        a_scr[h] = neg_kk[:, sl]
        b_scr[h] = b[:, sl]

    eye = (lax.broadcasted_iota(jnp.int32, (HEAD_DIM, HEAD_DIM), 0)
           == lax.broadcasted_iota(jnp.int32, (HEAD_DIM, HEAD_DIM), 1))
    for hg in range(RWKV_HEADS // HEADS_PER_LOOP):
        heads = range(hg * HEADS_PER_LOOP, (hg + 1) * HEADS_PER_LOOP)

        def step(t, states, heads=heads):
            new = []
            for s, h in zip(states, heads):
                row = lambda scr: scr[h, pl.ds(t, 1), :]
                s, y_row = _wkv_step(s, eye, row(r_scr), row(w_scr), row(k_scr), row(v_scr), row(a_scr),
                                     row(b_scr))
                y_scr[h, pl.ds(t, 1), :] = y_row
                new.append(s)
            return tuple(new)

        states = lax.fori_loop(0, tc, step, tuple(s_ref[0, h] for h in heads))
        for s, h in zip(states, heads):
            s_ref[0, h] = s

    for h in range(RWKV_HEADS):
        yn_scr[:, h * HEAD_DIM:(h + 1) * HEAD_DIM] = _group_norm_head(y_scr[h])
    out = (yn_scr[...] * lnw_ref[...] + lnb_ref[...] + bonus) * gate
    o_ref[...] = out.astype(BF16)


def _rwkv_weights(w0, w_decay_up, a0, w_aaa_up, w_gate_up, k_k, k_a, r_k, ln_x_w, ln_x_b):
    row = lambda z: z.reshape(1, D_RWKV)
    pad = lambda z: jnp.pad(z, ((0, LORA_PAD - z.shape[0]), (0, 0)))
    head_of_col = jnp.arange(D_RWKV, dtype=jnp.int32) // HEAD_DIM
    seg = (head_of_col[:, None] == jnp.arange(LANES, dtype=jnp.int32)[None, :]).astype(BF16)
    return (row(w0), pad(w_decay_up), row(a0), pad(w_aaa_up), w_gate_up, row(k_k), row(k_a), row(r_k),
            row(ln_x_w), row(ln_x_b), seg, seg.T)


def _const_spec(shape, grid_rank):
    zeros = (0,) * len(shape)
    if grid_rank == 1:
        return pl.BlockSpec(shape, lambda i: zeros)
    return pl.BlockSpec(shape, lambda i, j: zeros)


def _rwkv_seq(hx, weights, *, batch, seq, tc):
    nchunk = seq // tc
    rowmap = lambda cb: (lambda b, c: (b * nchunk + c, cb))
    in_specs = [pl.BlockSpec((tc, D_RWKV), rowmap(0)),
                pl.BlockSpec((tc, D_RWKV), rowmap(1)),
                pl.BlockSpec((tc, D_RWKV), rowmap(2)),
                pl.BlockSpec((tc, 2 * LORA_PAD + GATE_LORA), rowmap(COL_LORA // (2 * LORA_PAD + GATE_LORA)))]
    in_specs += [_const_spec(w.shape, 2) for w in weights]
    head_scr = pltpu.VMEM((RWKV_HEADS, tc, HEAD_DIM), F32)
    return pl.pallas_call(
        functools.partial(_rwkv_seq_kernel, tc=tc),
        grid=(batch, nchunk),
        in_specs=in_specs,
        out_specs=[pl.BlockSpec((tc, D_RWKV), lambda b, c: (b * nchunk + c, 0)),
                   pl.BlockSpec((1, RWKV_HEADS, HEAD_DIM, HEAD_DIM), lambda b, c: (b, 0, 0, 0))],
        out_shape=[jax.ShapeDtypeStruct((batch * seq, D_RWKV), BF16),
                   jax.ShapeDtypeStruct((batch, RWKV_HEADS, HEAD_DIM, HEAD_DIM), F32)],
        scratch_shapes=[head_scr] * 7 + [pltpu.VMEM((tc, D_RWKV), F32)],
        compiler_params=pltpu.CompilerParams(dimension_semantics=("arbitrary", "arbitrary"),
                                             vmem_limit_bytes=VMEM_LIMIT),
    )(hx, hx, hx, hx, *weights)


def _rwkv_tok_kernel(r_ref, k_ref, v_ref, lora_ref, w0_ref, wd_ref, a0_ref, wa_ref, wg_ref, kk_ref, ka_ref,
                     rk_ref, lnw_ref, lnb_ref, seg_ref, segt_ref, s0_ref, o_ref, s_ref, yn_scr, *, tb):
    r = r_ref[...]
    v = v_ref[...]
    prm = (w0_ref[...], wd_ref[...], a0_ref[...], wa_ref[...], wg_ref[...], kk_ref[...], ka_ref[...], rk_ref[...])
    decay, k_mod, neg_kk, b, gate, bonus = _rwkv_prep(r, k_ref[...], v, lora_ref[...], prm,
                                                       seg_ref[...], segt_ref[...])
    eye = (lax.broadcasted_iota(jnp.int32, (HEAD_DIM, HEAD_DIM), 0)
           == lax.broadcasted_iota(jnp.int32, (HEAD_DIM, HEAD_DIM), 1))
    for bi in range(tb):
        for h in range(RWKV_HEADS):
            row = lambda z: z[bi:bi + 1, h * HEAD_DIM:(h + 1) * HEAD_DIM]
            s, y_row = _wkv_step(s0_ref[bi, h], eye, row(r), row(decay), row(k_mod), row(v), row(neg_kk), row(b))
            s_ref[bi, h] = s
            yn_scr[bi:bi + 1, h * HEAD_DIM:(h + 1) * HEAD_DIM] = _group_norm_head(y_row)
    out = (yn_scr[...] * lnw_ref[...] + lnb_ref[...] + bonus) * gate
    o_ref[...] = out.astype(BF16)


def _rwkv_tok(hx, state, weights, *, tb):
    nb = hx.shape[0]
    rowmap = lambda cb: (lambda i: (i, cb))
    in_specs = [pl.BlockSpec((tb, D_RWKV), rowmap(0)),
                pl.BlockSpec((tb, D_RWKV), rowmap(1)),
                pl.BlockSpec((tb, D_RWKV), rowmap(2)),
                pl.BlockSpec((tb, 2 * LORA_PAD + GATE_LORA), rowmap(COL_LORA // (2 * LORA_PAD + GATE_LORA)))]
    in_specs += [_const_spec(w.shape, 1) for w in weights]
    state_spec = pl.BlockSpec((tb, RWKV_HEADS, HEAD_DIM, HEAD_DIM), lambda i: (i, 0, 0, 0))
    in_specs.append(state_spec)
    return pl.pallas_call(
        functools.partial(_rwkv_tok_kernel, tb=tb),
        grid=(nb // tb,),
        in_specs=in_specs,
        out_specs=[pl.BlockSpec((tb, D_RWKV), lambda i: (i, 0)), state_spec],
        out_shape=[jax.ShapeDtypeStruct((nb, D_RWKV), BF16),
                   jax.ShapeDtypeStruct(state.shape, F32)],
        scratch_shapes=[pltpu.VMEM((tb, D_RWKV), F32)],
        compiler_params=pltpu.CompilerParams(dimension_semantics=("arbitrary",),
                                             vmem_limit_bytes=VMEM_LIMIT),
    )(hx, hx, hx, hx, *weights, state)


def _route(logits, bias):
    scores = _sigmoid(logits)
    sel = scores + bias
    rows = logits.shape[0]
    lane_i = lax.broadcasted_iota(jnp.int32, (rows, N_EXPERTS), 1)
    grp = lane_i // GROUP_SIZE
    lane = lane_i.astype(F32)
    neg_inf = -jnp.inf

    def take_max(x):
        m = jnp.max(x, axis=1, keepdims=True)
        idx = jnp.min(jnp.where(x == m, lane, float(N_EXPERTS)), axis=1, keepdims=True)
        return m, lane == idx

    gscore = []
    for g in range(N_GROUPS):
        x = jnp.where(grp == g, sel, neg_inf)
        m1, hit = take_max(x)
        m2 = jnp.max(jnp.where(hit, neg_inf, x), axis=1, keepdims=True)
        gscore.append(m1 + m2)
    emask = jnp.zeros((rows, N_EXPERTS), jnp.bool_)
    for g in range(N_GROUPS):
        ahead = jnp.zeros((rows, 1), jnp.int32)
        for g2 in range(N_GROUPS):
            if g2 == g:
                continue
            beats = (gscore[g2] >= gscore[g]) if g2 < g else (gscore[g2] > gscore[g])
            ahead = ahead + beats.astype(jnp.int32)
        emask = emask | ((grp == g) & (ahead < TOPK_GROUPS))
    x = jnp.where(emask, sel, neg_inf)
    chosen = jnp.zeros((rows, N_EXPERTS), jnp.bool_)
    for _ in range(TOP_K):
        _, hit = take_max(x)
        chosen = chosen | hit
        x = jnp.where(hit, neg_inf, x)
    top = jnp.where(chosen, scores, 0.0)
    gates = top / jnp.sum(top, axis=1, keepdims=True) * ROUTE_SCALE
    return gates, chosen.astype(F32)


def _out_proj_kernel(pool_ref, rwkv_ref, x_ref, g1_ref, sc_ref, sh_ref, wp_ref, wr_ref, npost_ref, npre_ref,
                     wrt_ref, rb_ref, x1_ref, u2_ref, gates_ref, chosen_ref):
    m = _dot(pool_ref[...], wp_ref[...]) + _dot(rwkv_ref[...], wr_ref[...])
    x1 = x_ref[...] + g1_ref[0] * _rms(m, npost_ref[...])
    x1_ref[...] = x1
    u2 = _rms(x1, npre_ref[...]) * (1.0 + sc_ref[0]) + sh_ref[0]
    u2_ref[...] = u2.astype(BF16)
    gates, chosen = _route(_dot_x3(u2, wrt_ref[...]), rb_ref[...])
    gates_ref[...] = gates
    chosen_ref[...] = chosen


def _out_proj(pool_out, rwkv_out, x, g1, sc2, sh2, w_out_bf16, n_post, n_pre, w_router, router_bias, *, tm,
              tiles_per_seq):
    m, d = x.shape
    if tiles_per_seq is None:
        mod_spec = pl.BlockSpec((1, tm, d), lambda i: (i, 0, 0))
    else:
        mod_spec = pl.BlockSpec((1, 1, d), lambda i: (i // tiles_per_seq, 0, 0))
    row = lambda width: pl.BlockSpec((tm, width), lambda i: (i, 0))
    return pl.pallas_call(
        _out_proj_kernel,
        grid=(m // tm,),
        in_specs=[row(D_POOL), row(D_RWKV), row(d), mod_spec, mod_spec, mod_spec,
                  _const_spec((D_POOL, d), 1), _const_spec((D_RWKV, d), 1),
                  _const_spec((1, d), 1), _const_spec((1, d), 1),
                  _const_spec((d, N_EXPERTS), 1), _const_spec((1, N_EXPERTS), 1)],
        out_specs=[row(d), row(d), row(N_EXPERTS), row(N_EXPERTS)],
        out_shape=[jax.ShapeDtypeStruct((m, d), F32), jax.ShapeDtypeStruct((m, d), BF16),
                   jax.ShapeDtypeStruct((m, N_EXPERTS), F32), jax.ShapeDtypeStruct((m, N_EXPERTS), F32)],
        compiler_params=pltpu.CompilerParams(dimension_semantics=("arbitrary",),
                                             vmem_limit_bytes=VMEM_LIMIT),
    )(pool_out, rwkv_out, x, g1, sc2, sh2, w_out_bf16[:D_POOL], w_out_bf16[D_POOL:],
      n_post.reshape(1, d), n_pre.reshape(1, d), w_router, router_bias.reshape(1, N_EXPERTS))


MOE_TM = 128


def _experts_kernel(te_ref, nused_ref, xs_ref, ws_ref, wg_ref, wu_ref, wd_ref, ys_ref, g_scr, u_scr, d_scr):
    t = pl.program_id(0)
    fresh = jnp.logical_or(t == 0, te_ref[t] != te_ref[jnp.maximum(t - 1, 0)])

    @pl.when(fresh)
    def _():
        g_scr[...] = wg_ref[0].astype(BF16)
        u_scr[...] = wu_ref[0].astype(BF16)
        d_scr[...] = wd_ref[0].astype(BF16)

    @pl.when(t < nused_ref[0])
    def _():
        x = xs_ref[...]
        h = _silu(_dot(x, g_scr[...])) * _dot(x, u_scr[...])
        ys_ref[...] = _dot(h.astype(BF16), d_scr[...]) * ws_ref[...]

    @pl.when(t >= nused_ref[0])
    def _():
        ys_ref[...] = jnp.zeros(ys_ref.shape, F32)


def _experts(tile_expert, n_used, xs, ws, exp_gate, exp_up, exp_down):
    n_slots, d = xs.shape
    n_tiles = n_slots // MOE_TM
    grid_spec = pltpu.PrefetchScalarGridSpec(
        num_scalar_prefetch=2,
        grid=(n_tiles,),
        in_specs=[pl.BlockSpec((MOE_TM, d), lambda t, te, nu: (t, 0)),
                  pl.BlockSpec((MOE_TM, 1), lambda t, te, nu: (t, 0)),
                  pl.BlockSpec((1, d, D_EXPERT), lambda t, te, nu: (te[t], 0, 0)),
                  pl.BlockSpec((1, d, D_EXPERT), lambda t, te, nu: (te[t], 0, 0)),
                  pl.BlockSpec((1, D_EXPERT, d), lambda t, te, nu: (te[t], 0, 0))],
        out_specs=pl.BlockSpec((MOE_TM, d), lambda t, te, nu: (t, 0)),
        scratch_shapes=[pltpu.VMEM((d, D_EXPERT), BF16), pltpu.VMEM((d, D_EXPERT), BF16),
                        pltpu.VMEM((D_EXPERT, d), BF16)],
    )
    return pl.pallas_call(
        _experts_kernel,
        grid_spec=grid_spec,
        out_shape=jax.ShapeDtypeStruct((n_slots, d), F32),
        compiler_params=pltpu.CompilerParams(dimension_semantics=("arbitrary",),
                                             vmem_limit_bytes=VMEM_LIMIT),
    )(tile_expert, n_used, xs, ws, exp_gate, exp_up, exp_down)


def _dispatch(chosen, gates):
    n = chosen.shape[0]
    n_assign = n * TOP_K
    n_tiles = -(-n_assign // MOE_TM) + N_EXPERTS
    n_slots = n_tiles * MOE_TM
    sel = chosen.astype(jnp.int32)
    counts = jnp.sum(sel, axis=0)
    rank = jnp.cumsum(sel, axis=0) - sel
    tiles_e = (counts + MOE_TM - 1) // MOE_TM
    tile_end = jnp.cumsum(tiles_e)
    slot = (tile_end - tiles_e)[None, :] * MOE_TM + rank
    slot = jnp.where(sel > 0, slot, n_slots)
    tok = jnp.broadcast_to(jnp.arange(n, dtype=jnp.int32)[:, None], slot.shape)
    slot_tok = jnp.zeros((n_slots,), jnp.int32).at[slot.reshape(-1)].set(tok.reshape(-1), mode="drop")
    slot_w = jnp.zeros((n_slots,), F32).at[slot.reshape(-1)].set(gates.reshape(-1), mode="drop")
    n_used = tile_end[-1]
    tile_ids = jnp.minimum(jnp.arange(n_tiles, dtype=jnp.int32), n_used - 1)
    tile_expert = jnp.minimum(jnp.searchsorted(tile_end, tile_ids, side="right"), N_EXPERTS - 1).astype(jnp.int32)
    return slot, slot_tok, slot_w, tile_expert, n_used.reshape(1).astype(jnp.int32)


def _final_kernel(u2_ref, routed_ref, x1_ref, g2_ref, sg_ref, su_ref, sd_ref, npost_ref, o_ref):
    u2 = u2_ref[...]
    h = _silu(_dot(u2, sg_ref[...])) * _dot(u2, su_ref[...])
    f = routed_ref[...] + _dot(h.astype(BF16), sd_ref[...])
    o_ref[...] = x1_ref[...] + g2_ref[0] * _rms(f, npost_ref[...])


def _final(u2, routed, x1, g2, sg, su, sd, n_post, *, tm, tiles_per_seq):
    m, d = x1.shape
    if tiles_per_seq is None:
        mod_spec = pl.BlockSpec((1, tm, d), lambda i: (i, 0, 0))
    else:
        mod_spec = pl.BlockSpec((1, 1, d), lambda i: (i // tiles_per_seq, 0, 0))
    row = pl.BlockSpec((tm, d), lambda i: (i, 0))
    return pl.pallas_call(
        _final_kernel,
        grid=(m // tm,),
        in_specs=[row, row, row, mod_spec,
                  _const_spec(sg.shape, 1), _const_spec(su.shape, 1), _const_spec(sd.shape, 1),
                  _const_spec((1, d), 1)],
        out_specs=row,
        out_shape=jax.ShapeDtypeStruct((m, d), F32),
        compiler_params=pltpu.CompilerParams(dimension_semantics=("arbitrary",),
                                             vmem_limit_bytes=VMEM_LIMIT),
    )(u2, routed, x1, g2, sg, su, sd, n_post.reshape(1, d))


def _pad_cols_in(w_in, mu_shift):
    d = w_in.shape[0]
    hp = w_in[:, :D_POOL]
    hr = w_in[:, D_POOL:]
    cuts = [3 * D_RWKV, 3 * D_RWKV + DECAY_LORA, 3 * D_RWKV + DECAY_LORA + AAA_LORA]
    rkv, wl, al, gl = jnp.split(hr, cuts, axis=1)
    zpad = lambda z: jnp.pad(z, ((0, 0), (0, LORA_PAD - z.shape[1])))
    w = jnp.concatenate([rkv, hp, zpad(wl), zpad(al), gl], axis=1).astype(BF16)
    mu_rkv, mu_wl, mu_al, mu_gl = jnp.split(mu_shift, cuts)
    zp1 = lambda z: jnp.pad(z, (0, LORA_PAD - z.shape[0]))
    mu = jnp.concatenate([mu_rkv, jnp.zeros((D_POOL,), F32), zp1(mu_wl), zp1(mu_al), mu_gl])
    assert w.shape == (d, N_IN_PAD) and mu.shape == (N_IN_PAD,)
    return w, mu


def kernel(x_prompt, x_sample, c_prompt, c_sample, state_shift, state_pool, state_wkv, w_ada, b_ada, norm_pre_mix, norm_post_mix, norm_pre_ffn, norm_post_ffn, w_in, mu_shift, pool_w, pool_scale, w0, w_decay_up, a0, w_aaa_up, w_gate_up, k_k, k_a, r_k, ln_x_w, ln_x_b, w_out, w_router, router_bias, exp_gate, exp_up, exp_down, sh_gate, sh_up, sh_down):
    bp, seq, d = x_prompt.shape
    bs = x_sample.shape[0]
    assert w_ada.shape[0] == 1 and x_sample.shape[1] == 1
    np_rows = bp * seq

    c_all = jnp.concatenate([c_prompt, c_sample], axis=0)
    pad_rows = (-c_all.shape[0]) % 16
    ada = _ada(jnp.pad(c_all, ((0, pad_rows), (0, 0))), w_ada[0], b_ada[0])
    sh1, sc1, g1, sh2, sc2, g2 = jnp.split(ada, 6, axis=-1)
    pr = lambda z: z[:bp].reshape(bp, 1, d)
    TS = 128
    sr = lambda z: z[bp:bp + bs].reshape(bs // TS, TS, d)

    w_in_p, mu_p = _pad_cols_in(w_in[0], mu_shift[0])
    xp = x_prompt.reshape(np_rows, d)
    xs = x_sample.reshape(bs, d)

    TM_IN = 1024
    hx_p, u_tail = _in_proj(xp, norm_pre_mix[0], pr(sc1), pr(sh1), w_in_p, mu_p, tm=TM_IN,
                            tiles_per_seq=seq // TM_IN)
    hx_s, u_s = _in_proj(xs, norm_pre_mix[0], sr(sc1), sr(sh1), w_in_p, mu_p, tm=TS, tiles_per_seq=1,
                         prev=state_shift[0])
    new_shift_prompt = u_tail.reshape(bp, seq // TM_IN, 8, d)[:, -1, -1][None]
    new_shift_sample = u_s[None]

    pool_w_b = pool_w[0].astype(BF16)
    pool_p = _pool(hx_p, COL_POOL // D_POOL, pool_w_b, pool_scale[0], tc=512, tiles_per_seq=seq // 512,
                   full_count=False)
    hp_p = hx_p[:, COL_POOL:COL_POOL + D_POOL].reshape(bp, seq, D_POOL)
    new_pool_prompt = hp_p[:, seq - POOL_BUF:][None]
    hp_s = hx_s[:, COL_POOL:COL_POOL + D_POOL]
    ext_s = jnp.concatenate([state_pool[0], hp_s[:, None, :]], axis=1)
    assert PAST_LEN + 1 >= max(POOL_WINDOWS)
    pool_s = _pool(ext_s.reshape(bs * (POOL_BUF + 1), D_POOL), 0, pool_w_b, pool_scale[0], tc=POOL_BUF + 1,
                   tiles_per_seq=1, full_count=True)
    pool_s = pool_s.reshape(bs, POOL_BUF + 1, D_POOL)[:, -1]
    new_pool_sample = ext_s[:, 1:][None]

    rw = _rwkv_weights(w0[0], w_decay_up[0], a0[0], w_aaa_up[0], w_gate_up[0], k_k[0], k_a[0], r_k[0],
                       ln_x_w[0], ln_x_b[0])
    rwkv_p, wkv_p = _rwkv_seq(hx_p, rw, batch=bp, seq=seq, tc=128)
    rwkv_s, wkv_s = _rwkv_tok(hx_s, state_wkv[0], rw, tb=8)

    w_out_b = w_out[0].astype(BF16)
    TM_OUT = 256
    x1_p, u2_p, gates_p, chosen_p = _out_proj(pool_p, rwkv_p, xp, pr(g1), pr(sc2), pr(sh2), w_out_b,
                                              norm_post_mix[0], norm_pre_ffn[0], w_router[0], router_bias[0],
                                              tm=TM_OUT, tiles_per_seq=seq // TM_OUT)
    x1_s, u2_s, gates_s, chosen_s = _out_proj(pool_s, rwkv_s, xs, sr(g1), sr(sc2), sr(sh2), w_out_b,
                                              norm_post_mix[0], norm_pre_ffn[0], w_router[0], router_bias[0],
                                              tm=TS, tiles_per_seq=None)

    u2_all = jnp.concatenate([u2_p, u2_s], axis=0)
    gates_all = jnp.concatenate([gates_p, gates_s], axis=0)
    chosen_all = jnp.concatenate([chosen_p, chosen_s], axis=0)
    slot, slot_tok, slot_w, tile_expert, n_used = _dispatch(chosen_all, gates_all)
    ys = _experts(tile_expert, n_used, u2_all[slot_tok], slot_w[:, None], exp_gate[0], exp_up[0], exp_down[0])
    ys = jnp.concatenate([ys, jnp.zeros((1, d), F32)], axis=0)
    _, top_idx = lax.top_k(chosen_all, TOP_K)
    slot8 = jnp.take_along_axis(slot, top_idx, axis=1)
    routed = jnp.sum(ys[slot8], axis=1)

    sg, su, sd = sh_gate[0].astype(BF16), sh_up[0].astype(BF16), sh_down[0].astype(BF16)
    y_p = _final(u2_p, routed[:np_rows], x1_p, pr(g2), sg, su, sd, norm_post_ffn[0], tm=TM_OUT,
                 tiles_per_seq=seq // TM_OUT)
    y_s = _final(u2_s, routed[np_rows:], x1_s, sr(g2), sg, su, sd, norm_post_ffn[0], tm=TS, tiles_per_seq=None)

    return (y_p.reshape(bp, seq, d), y_s.reshape(bs, 1, d), new_shift_prompt, new_pool_prompt, wkv_p[None],
            new_shift_sample, new_pool_sample, wkv_s[None])
```

```python
import functools

import jax
import jax.numpy as jnp
from jax import lax
from jax.experimental import pallas as pl
from jax.experimental.pallas import tpu as pltpu

F32 = jnp.float32
BF16 = jnp.bfloat16

D_MODEL = 2048
D_POOL = 512
POOL_WINDOWS = (2, 4, 8, 16)
POOL_CH = 128
POOL_BUF = 15
D_RWKV = 1536
HEAD_DIM = 64
RWKV_HEADS = 24
DECAY_LORA = 96
AAA_LORA = 96
GATE_LORA = 256
LORA_PAD = 128
GN_EPS = 64e-5
N_EXPERTS = 256
TOP_K = 8
N_GROUPS = 8
GROUP_SIZE = N_EXPERTS // N_GROUPS
TOPK_GROUPS = 4
D_EXPERT = 512
ROUTE_SCALE = 2.5
EPS = 1e-6
PAST_LEN = 16384

COL_R, COL_K, COL_V = 0, D_RWKV, 2 * D_RWKV
COL_POOL = 3 * D_RWKV
COL_LORA = COL_POOL + D_POOL
N_IN_PAD = COL_LORA + 2 * LORA_PAD + GATE_LORA
LANES = 128
VMEM_LIMIT = 56 * 1024 * 1024


def _dot(a, b):
    return jnp.dot(a, b, preferred_element_type=F32)


def _split2(x):
    hi = x.astype(BF16)
    lo = (x - hi.astype(F32)).astype(BF16)
    return hi, lo


def _split3(x):
    hi = x.astype(BF16)
    r = x - hi.astype(F32)
    mid = r.astype(BF16)
    lo = (r - mid.astype(F32)).astype(BF16)
    return hi, mid, lo


def _dot_x3(a, b):
    ah, al = _split2(a)
    bh, bl = _split2(b)
    return _dot(ah, bh) + (_dot(ah, bl) + _dot(al, bh))


def _dot_exact_rhs(a, b_bf16):
    h, m, l = _split3(a)
    return _dot(h, b_bf16) + (_dot(m, b_bf16) + _dot(l, b_bf16))


def _sigmoid(x):
    return 1.0 / (1.0 + jnp.exp(-x))


def _silu(x):
    return x * _sigmoid(x)


def _rms(x, g):
    return x * lax.rsqrt(jnp.mean(x * x, axis=-1, keepdims=True) + EPS) * g


def _ada_kernel(c_ref, w_ref, b_ref, o_ref):
    o_ref[...] = _dot_x3(_silu(c_ref[...]), w_ref[...]) + b_ref[...]


def _ada(c_all, w_ada, b_ada):
    m, d = c_all.shape
    n = w_ada.shape[1]
    tn = 512
    return pl.pallas_call(
        _ada_kernel,
        grid=(n // tn,),
        in_specs=[pl.BlockSpec((m, d), lambda j: (0, 0)),
                  pl.BlockSpec((d, tn), lambda j: (0, j)),
                  pl.BlockSpec((1, tn), lambda j: (0, j))],
        out_specs=pl.BlockSpec((m, tn), lambda j: (0, j)),
        out_shape=jax.ShapeDtypeStruct((m, n), F32),
        compiler_params=pltpu.CompilerParams(dimension_semantics=("arbitrary",),
                                             vmem_limit_bytes=VMEM_LIMIT),
    )(c_all, w_ada, b_ada.reshape(1, n))


def _in_proj_kernel(*refs, tiles_per_seq, explicit_prev, tm):
    if explicit_prev:
        x_ref, g_ref, sc_ref, sh_ref, prev_ref, w_ref, mu_ref, hx_ref, u_ref, ub_scr = refs
    else:
        x_ref, g_ref, sc_ref, sh_ref, w_ref, mu_ref, hx_ref, u_ref, ub_scr, carry_scr = refs
    i = pl.program_id(0)
    j = pl.program_id(1)

    @pl.when(j == 0)
    def _():
        u = _rms(x_ref[...], g_ref[...])
        u = u * (1.0 + sc_ref[0]) + sh_ref[0]
        ub_scr[...] = u.astype(BF16)
        if explicit_prev:
            u_ref[...] = u
        else:
            u_ref[...] = u[tm - 8:, :]

    w = w_ref[...]
    h = _dot(ub_scr[...], w)
    if explicit_prev:
        hp = _dot(prev_ref[...].astype(BF16), w)
    else:
        @pl.when(i == 0)
        def _():
            carry_scr[j] = jnp.zeros(carry_scr.shape[1:], F32)

        first = jnp.where(i % tiles_per_seq == 0, 0.0, carry_scr[j, 0:1, :])
        row0 = lax.broadcasted_iota(jnp.int32, h.shape, 0) == 0
        hp = jnp.where(row0, first, pltpu.roll(h, 1, axis=0))
        carry_scr[j, 0:1, :] = h[tm - 1:tm, :]
    hx_ref[...] = h + (hp - h) * mu_ref[...]


def _in_proj(x, gamma, sc, sh, w_bf16, mu_pad, *, tm, tiles_per_seq, prev=None):
    m, d = x.shape
    n = w_bf16.shape[1]
    tn = 512
    nt = n // tn
    explicit_prev = prev is not None
    if explicit_prev:
        mod_spec = pl.BlockSpec((1, tm, d), lambda i, j: (i, 0, 0))
    else:
        mod_spec = pl.BlockSpec((1, 1, d), lambda i, j: (i // tiles_per_seq, 0, 0))
    in_specs = [pl.BlockSpec((tm, d), lambda i, j: (i, 0)),
                pl.BlockSpec((1, d), lambda i, j: (0, 0)),
                mod_spec, mod_spec]
    args = [x, gamma.reshape(1, d), sc, sh]
    if explicit_prev:
        in_specs.append(pl.BlockSpec((tm, d), lambda i, j: (i, 0)))
        args.append(prev)
    in_specs += [pl.BlockSpec((d, tn), lambda i, j: (0, j)),
                 pl.BlockSpec((1, tn), lambda i, j: (0, j))]
    args += [w_bf16, mu_pad.reshape(1, n)]
    scratch = [pltpu.VMEM((tm, d), BF16)]
    if explicit_prev:
        u_shape, u_spec = (m, d), pl.BlockSpec((tm, d), lambda i, j: (i, 0))
    else:
        u_shape, u_spec = (m // tm * 8, d), pl.BlockSpec((8, d), lambda i, j: (i, 0))
        scratch.append(pltpu.VMEM((nt, 8, tn), F32))
    return pl.pallas_call(
        functools.partial(_in_proj_kernel, tiles_per_seq=tiles_per_seq, explicit_prev=explicit_prev, tm=tm),
        grid=(m // tm, nt),
        in_specs=in_specs,
        out_specs=[pl.BlockSpec((tm, tn), lambda i, j: (i, j)), u_spec],
        out_shape=[jax.ShapeDtypeStruct((m, n), F32), jax.ShapeDtypeStruct(u_shape, F32)],
        scratch_shapes=scratch,
        compiler_params=pltpu.CompilerParams(dimension_semantics=("arbitrary", "arbitrary"),
                                             vmem_limit_bytes=VMEM_LIMIT),
    )(*args)


POOL_HALO = 16


def _pool_kernel(p_ref, pw_ref, ps_ref, o_ref, ext_scr, *, tiles_per_seq, full_count, tc):
    i = pl.program_id(0)

    @pl.when(i % tiles_per_seq == 0)
    def _():
        ext_scr[0:POOL_HALO, :] = jnp.zeros((POOL_HALO, D_POOL), F32)

    p = p_ref[...]
    ext_scr[POOL_HALO:POOL_HALO + tc, :] = p
    pos = (i % tiles_per_seq) * tc + lax.broadcasted_iota(jnp.int32, (tc, 1), 0)
    outs = []
    for gi, win in enumerate(POOL_WINDOWS):
        lo = gi * POOL_CH
        pg = p[:, lo:lo + POOL_CH]
        acc = pg
        for s in range(1, win):
            acc = acc + ext_scr[POOL_HALO - s:POOL_HALO - s + tc, lo:lo + POOL_CH]
        if full_count:
            dgi = acc / float(win) - pg
        else:
            cnt = jnp.minimum(pos + 1, win).astype(F32)
            dgi = acc / cnt - pg
        outs.append(_dot(dgi.astype(BF16), pw_ref[gi]))
    y = jnp.concatenate(outs, axis=-1) * ps_ref[...]
    o_ref[...] = y.astype(BF16)
    ext_scr[0:POOL_HALO, :] = ext_scr[tc:tc + POOL_HALO, :]


def _pool(src, col_block, pool_w_bf16, pool_scale, *, tc, tiles_per_seq, full_count):
    m = src.shape[0]
    return pl.pallas_call(
        functools.partial(_pool_kernel, tiles_per_seq=tiles_per_seq, full_count=full_count, tc=tc),
        grid=(m // tc,),
        in_specs=[pl.BlockSpec((tc, D_POOL), lambda i: (i, col_block)),
                  pl.BlockSpec((len(POOL_WINDOWS), POOL_CH, POOL_CH), lambda i: (0, 0, 0)),
                  pl.BlockSpec((1, D_POOL), lambda i: (0, 0))],
        out_specs=pl.BlockSpec((tc, D_POOL), lambda i: (i, 0)),
        out_shape=jax.ShapeDtypeStruct((m, D_POOL), BF16),
        scratch_shapes=[pltpu.VMEM((POOL_HALO + tc, D_POOL), F32)],
        compiler_params=pltpu.CompilerParams(dimension_semantics=("arbitrary",),
                                             vmem_limit_bytes=VMEM_LIMIT),
    )(src, pool_w_bf16, pool_scale.reshape(1, D_POOL))


def _softplus(z):
    return jnp.maximum(z, 0.0) + jnp.log1p(jnp.exp(-jnp.abs(z)))


def _rwkv_prep(r, k, v, lora, prm, seg, segt):
    (w0, wd, a0, wa, wg, k_k, k_a, r_k) = prm
    wl = lora[:, 0:LORA_PAD]
    al = lora[:, LORA_PAD:2 * LORA_PAD]
    gl = lora[:, 2 * LORA_PAD:]
    logw = -_softplus(-(w0 + _dot_x3(jnp.tanh(wl), wd))) - 0.5
    decay = jnp.exp(-jnp.exp(logw))
    a = _sigmoid(a0 + _dot_x3(al, wa))
    gate = _dot(_sigmoid(gl).astype(BF16), wg.astype(BF16))
    kk = k * k_k
    ss = _dot_exact_rhs(kk * kk, seg)
    inv = 1.0 / jnp.maximum(jnp.sqrt(ss), 1e-12)
    kk = kk * _dot_exact_rhs(inv, segt)
    k_mod = k * (1.0 + (a - 1.0) * k_a)
    bonus = _dot_exact_rhs(_dot_exact_rhs(r * k_mod * r_k, seg), segt) * v
    return decay, k_mod, -kk, kk * a, gate, bonus


def _wkv_step(s, eye, r_t, w_t, k_t, v_t, na_t, b_t):
    sa = jnp.sum(s * na_t, axis=1, keepdims=True)
    v_col = jnp.sum(jnp.where(eye, v_t, 0.0), axis=1, keepdims=True)
    s = s * w_t + sa * b_t + v_col * k_t
    y_col = jnp.sum(s * r_t, axis=1, keepdims=True)
    y_row = jnp.sum(jnp.where(eye, y_col, 0.0), axis=0, keepdims=True)
    return s, y_row


def _group_norm_head(y):
    mu = jnp.mean(y, axis=-1, keepdims=True)
    yc = y - mu
    var = jnp.mean(yc * yc, axis=-1, keepdims=True)
    return yc * lax.rsqrt(var + GN_EPS)


HEADS_PER_LOOP = 4


def _rwkv_seq_kernel(r_ref, k_ref, v_ref, lora_ref, w0_ref, wd_ref, a0_ref, wa_ref, wg_ref, kk_ref, ka_ref,
                     rk_ref, lnw_ref, lnb_ref, seg_ref, segt_ref, o_ref, s_ref,
                     r_scr, w_scr, k_scr, v_scr, a_scr, b_scr, y_scr, yn_scr, *, tc):
    c = pl.program_id(1)

    @pl.when(c == 0)
    def _():
        s_ref[...] = jnp.zeros(s_ref.shape, F32)

    r = r_ref[...]
    v = v_ref[...]
    prm = (w0_ref[...], wd_ref[...], a0_ref[...], wa_ref[...], wg_ref[...], kk_ref[...], ka_ref[...], rk_ref[...])
    decay, k_mod, neg_kk, b, gate, bonus = _rwkv_prep(r, k_ref[...], v, lora_ref[...], prm,
                                                       seg_ref[...], segt_ref[...])
    for h in range(RWKV_HEADS):
        sl = slice(h * HEAD_DIM, (h + 1) * HEAD_DIM)
        r_scr[h] = r[:, sl]
        w_scr[h] = decay[:, sl]
        k_scr[h] = k_mod[:, sl]
        v_scr[h] = v[:, sl]
        a_scr[h] = neg_kk[:, sl]
        b_scr[h] = b[:, sl]

    eye = (lax.broadcasted_iota(jnp.int32, (HEAD_DIM, HEAD_DIM), 0)
           == lax.broadcasted_iota(jnp.int32, (HEAD_DIM, HEAD_DIM), 1))
    for hg in range(RWKV_HEADS // HEADS_PER_LOOP):
        heads = range(hg * HEADS_PER_LOOP, (hg + 1) * HEADS_PER_LOOP)

        def step(t, states, heads=heads):
            new = []
            for s, h in zip(states, heads):
                row = lambda scr: scr[h, pl.ds(t, 1), :]
                s, y_row = _wkv_step(s, eye, row(r_scr), row(w_scr), row(k_scr), row(v_scr), row(a_scr),
                                     row(b_scr))
                y_scr[h, pl.ds(t, 1), :] = y_row
                new.append(s)
            return tuple(new)

        states = lax.fori_loop(0, tc, step, tuple(s_ref[0, h] for h in heads))
        for s, h in zip(states, heads):
            s_ref[0, h] = s

    for h in range(RWKV_HEADS):
        yn_scr[:, h * HEAD_DIM:(h + 1) * HEAD_DIM] = _group_norm_head(y_scr[h])
    out = (yn_scr[...] * lnw_ref[...] + lnb_ref[...] + bonus) * gate
    o_ref[...] = out.astype(BF16)


def _rwkv_weights(w0, w_decay_up, a0, w_aaa_up, w_gate_up, k_k, k_a, r_k, ln_x_w, ln_x_b):
    row = lambda z: z.reshape(1, D_RWKV)
    pad = lambda z: jnp.pad(z, ((0, LORA_PAD - z.shape[0]), (0, 0)))
    head_of_col = jnp.arange(D_RWKV, dtype=jnp.int32) // HEAD_DIM
    seg = (head_of_col[:, None] == jnp.arange(LANES, dtype=jnp.int32)[None, :]).astype(BF16)
    return (row(w0), pad(w_decay_up), row(a0), pad(w_aaa_up), w_gate_up, row(k_k), row(k_a), row(r_k),
            row(ln_x_w), row(ln_x_b), seg, seg.T)


def _const_spec(shape, grid_rank):
    zeros = (0,) * len(shape)
    if grid_rank == 1:
        return pl.BlockSpec(shape, lambda i: zeros)
    return pl.BlockSpec(shape, lambda i, j: zeros)


def _rwkv_seq(hx, weights, *, batch, seq, tc):
    nchunk = seq // tc
    rowmap = lambda cb: (lambda b, c: (b * nchunk + c, cb))
    in_specs = [pl.BlockSpec((tc, D_RWKV), rowmap(0)),
                pl.BlockSpec((tc, D_RWKV), rowmap(1)),
                pl.BlockSpec((tc, D_RWKV), rowmap(2)),
                pl.BlockSpec((tc, 2 * LORA_PAD + GATE_LORA), rowmap(COL_LORA // (2 * LORA_PAD + GATE_LORA)))]
    in_specs += [_const_spec(w.shape, 2) for w in weights]
    head_scr = pltpu.VMEM((RWKV_HEADS, tc, HEAD_DIM), F32)
    return pl.pallas_call(
        functools.partial(_rwkv_seq_kernel, tc=tc),
        grid=(batch, nchunk),
        in_specs=in_specs,
        out_specs=[pl.BlockSpec((tc, D_RWKV), lambda b, c: (b * nchunk + c, 0)),
                   pl.BlockSpec((1, RWKV_HEADS, HEAD_DIM, HEAD_DIM), lambda b, c: (b, 0, 0, 0))],
        out_shape=[jax.ShapeDtypeStruct((batch * seq, D_RWKV), BF16),
                   jax.ShapeDtypeStruct((batch, RWKV_HEADS, HEAD_DIM, HEAD_DIM), F32)],
        scratch_shapes=[head_scr] * 7 + [pltpu.VMEM((tc, D_RWKV), F32)],
        compiler_params=pltpu.CompilerParams(dimension_semantics=("arbitrary", "arbitrary"),
                                             vmem_limit_bytes=VMEM_LIMIT),
    )(hx, hx, hx, hx, *weights)


def _rwkv_tok_prep_kernel(r_ref, k_ref, v_ref, lora_ref, w0_ref, wd_ref, a0_ref, wa_ref, wg_ref, kk_ref, ka_ref,
                          rk_ref, seg_ref, segt_ref, w_o, k_o, a_o, b_o, g_o, bonus_o):
    prm = (w0_ref[...], wd_ref[...], a0_ref[...], wa_ref[...], wg_ref[...], kk_ref[...], ka_ref[...], rk_ref[...])
    decay, k_mod, neg_kk, b, gate, bonus = _rwkv_prep(r_ref[...], k_ref[...], v_ref[...], lora_ref[...], prm,
                                                       seg_ref[...], segt_ref[...])
    w_o[...] = decay
    k_o[...] = k_mod
    a_o[...] = neg_kk
    b_o[...] = b
    g_o[...] = gate
    bonus_o[...] = bonus


def _rwkv_tok_step_kernel(r_ref, w_ref, k_ref, v_ref, a_ref, b_ref, g_ref, bonus_ref, lnw_ref, lnb_ref, s0_ref,
                          o_ref, s_ref, *, tb):
    eye = (lax.broadcasted_iota(jnp.int32, (HEAD_DIM, HEAD_DIM), 0)
           == lax.broadcasted_iota(jnp.int32, (HEAD_DIM, HEAD_DIM), 1))

    def body(bi, carry):
        s = s0_ref[bi]
        sa = jnp.sum(s * a_ref[bi], axis=2, keepdims=True)
        v_col = jnp.sum(jnp.where(eye, v_ref[bi], 0.0), axis=2, keepdims=True)
        s = s * w_ref[bi] + sa * b_ref[bi] + v_col * k_ref[bi]
        s_ref[bi] = s
        y_col = jnp.sum(s * r_ref[bi], axis=2, keepdims=True)
        y = jnp.sum(jnp.where(eye, y_col, 0.0), axis=1, keepdims=True)
        o_ref[bi] = (_group_norm_head(y) * lnw_ref[0] + lnb_ref[0] + bonus_ref[bi]) * g_ref[bi]
        return carry

    lax.fori_loop(0, tb, body, 0)


def _rwkv_tok(hx, state, weights, *, tb):
    nb = hx.shape[0]
    (w0, wd, a0, wa, wg, k_k, k_a, r_k, ln_w, ln_b, seg, segt) = weights
    prep_w = (w0, wd, a0, wa, wg, k_k, k_a, r_k, seg, segt)
    colmap = lambda cb: (lambda i: (0, cb))
    in_specs = [pl.BlockSpec((nb, D_RWKV), colmap(0)),
                pl.BlockSpec((nb, D_RWKV), colmap(1)),
                pl.BlockSpec((nb, D_RWKV), colmap(2)),
                pl.BlockSpec((nb, 2 * LORA_PAD + GATE_LORA), colmap(COL_LORA // (2 * LORA_PAD + GATE_LORA)))]
    in_specs += [_const_spec(w.shape, 1) for w in prep_w]
    rows = jax.ShapeDtypeStruct((nb, D_RWKV), F32)
    prepped = pl.pallas_call(
        _rwkv_tok_prep_kernel,
        grid=(1,),
        in_specs=in_specs,
        out_specs=[pl.BlockSpec((nb, D_RWKV), lambda i: (0, 0))] * 6,
        out_shape=[rows] * 6,
        compiler_params=pltpu.CompilerParams(dimension_semantics=("arbitrary",),
                                             vmem_limit_bytes=VMEM_LIMIT),
    )(hx, hx, hx, hx, *prep_w)
    decay, k_mod, neg_kk, b, gate, bonus = prepped
    heads = lambda z: z.reshape(nb, RWKV_HEADS, 1, HEAD_DIM)
    vecs = [heads(z) for z in (hx[:, COL_R:COL_R + D_RWKV], decay, k_mod, hx[:, COL_V:COL_V + D_RWKV], neg_kk, b,
                               gate, bonus)]
    vec_spec = pl.BlockSpec((tb, RWKV_HEADS, 1, HEAD_DIM), lambda i: (i, 0, 0, 0))
    ln_spec = pl.BlockSpec((1, RWKV_HEADS, 1, HEAD_DIM), lambda i: (0, 0, 0, 0))
    state_spec = pl.BlockSpec((tb, RWKV_HEADS, HEAD_DIM, HEAD_DIM), lambda i: (i, 0, 0, 0))
    out, new_state = pl.pallas_call(
        functools.partial(_rwkv_tok_step_kernel, tb=tb),
        grid=(nb // tb,),
        in_specs=[vec_spec] * 8 + [ln_spec, ln_spec, state_spec],
        out_specs=[vec_spec, state_spec],
        out_shape=[jax.ShapeDtypeStruct((nb, RWKV_HEADS, 1, HEAD_DIM), F32),
                   jax.ShapeDtypeStruct(state.shape, F32)],
        compiler_params=pltpu.CompilerParams(dimension_semantics=("arbitrary",),
                                             vmem_limit_bytes=VMEM_LIMIT),
    )(*vecs, ln_w.reshape(1, RWKV_HEADS, 1, HEAD_DIM), ln_b.reshape(1, RWKV_HEADS, 1, HEAD_DIM), state)
    return out.reshape(nb, D_RWKV).astype(BF16), new_state


def _route(logits, bias):
    scores = _sigmoid(logits)
    sel = scores + bias
    rows = logits.shape[0]
    lane_i = lax.broadcasted_iota(jnp.int32, (rows, N_EXPERTS), 1)
    grp = lane_i // GROUP_SIZE
    lane = lane_i.astype(F32)
    neg_inf = -jnp.inf

    def take_max(x):
        m = jnp.max(x, axis=1, keepdims=True)
        idx = jnp.min(jnp.where(x == m, lane, float(N_EXPERTS)), axis=1, keepdims=True)
        return m, lane == idx

    gscore = []
    for g in range(N_GROUPS):
        x = jnp.where(grp == g, sel, neg_inf)
        m1, hit = take_max(x)
        m2 = jnp.max(jnp.where(hit, neg_inf, x), axis=1, keepdims=True)
        gscore.append(m1 + m2)
    emask = jnp.zeros((rows, N_EXPERTS), jnp.bool_)
    for g in range(N_GROUPS):
        ahead = jnp.zeros((rows, 1), jnp.int32)
        for g2 in range(N_GROUPS):
            if g2 == g:
                continue
            beats = (gscore[g2] >= gscore[g]) if g2 < g else (gscore[g2] > gscore[g])
            ahead = ahead + beats.astype(jnp.int32)
        emask = emask | ((grp == g) & (ahead < TOPK_GROUPS))
    x = jnp.where(emask, sel, neg_inf)
    chosen = jnp.zeros((rows, N_EXPERTS), jnp.bool_)
    slot_lane = lax.broadcasted_iota(jnp.int32, (rows, LANES), 1)
    top_idx = jnp.zeros((rows, LANES), F32)
    top_w = jnp.zeros((rows, LANES), F32)
    for kth in range(TOP_K):
        m = jnp.max(x, axis=1, keepdims=True)
        idx = jnp.min(jnp.where(x == m, lane, float(N_EXPERTS)), axis=1, keepdims=True)
        hit = lane == idx
        chosen = chosen | hit
        x = jnp.where(hit, neg_inf, x)
        top_idx = jnp.where(slot_lane == kth, idx, top_idx)
        top_w = jnp.where(slot_lane == kth, jnp.sum(jnp.where(hit, scores, 0.0), axis=1, keepdims=True), top_w)
    top_w = top_w / jnp.sum(top_w, axis=1, keepdims=True) * ROUTE_SCALE
    return top_idx.astype(jnp.int32), top_w, chosen.astype(F32)


def _out_proj_kernel(pool_ref, rwkv_ref, x_ref, g1_ref, sc_ref, sh_ref, wp_ref, wr_ref, npost_ref, npre_ref,
                     wrt_ref, rb_ref, x1_ref, u2_ref, idx_ref, topw_ref, chosen_ref):
    m = _dot(pool_ref[...], wp_ref[...]) + _dot(rwkv_ref[...], wr_ref[...])
    x1 = x_ref[...] + g1_ref[0] * _rms(m, npost_ref[...])
    x1_ref[...] = x1
    u2 = _rms(x1, npre_ref[...]) * (1.0 + sc_ref[0]) + sh_ref[0]
    u2_ref[...] = u2.astype(BF16)
    top_idx, top_w, chosen = _route(_dot_x3(u2, wrt_ref[...]), rb_ref[...])
    idx_ref[...] = top_idx
    topw_ref[...] = top_w
    chosen_ref[...] = chosen


def _out_proj(pool_out, rwkv_out, x, g1, sc2, sh2, w_out_bf16, n_post, n_pre, w_router, router_bias, *, tm,
              tiles_per_seq):
    m, d = x.shape
    if tiles_per_seq is None:
        mod_spec = pl.BlockSpec((1, tm, d), lambda i: (i, 0, 0))
    else:
        mod_spec = pl.BlockSpec((1, 1, d), lambda i: (i // tiles_per_seq, 0, 0))
    row = lambda width: pl.BlockSpec((tm, width), lambda i: (i, 0))
    return pl.pallas_call(
        _out_proj_kernel,
        grid=(m // tm,),
        in_specs=[row(D_POOL), row(D_RWKV), row(d), mod_spec, mod_spec, mod_spec,
                  _const_spec((D_POOL, d), 1), _const_spec((D_RWKV, d), 1),
                  _const_spec((1, d), 1), _const_spec((1, d), 1),
                  _const_spec((d, N_EXPERTS), 1), _const_spec((1, N_EXPERTS), 1)],
        out_specs=[row(d), row(d), row(LANES), row(LANES), row(N_EXPERTS)],
        out_shape=[jax.ShapeDtypeStruct((m, d), F32), jax.ShapeDtypeStruct((m, d), BF16),
                   jax.ShapeDtypeStruct((m, LANES), jnp.int32), jax.ShapeDtypeStruct((m, LANES), F32),
                   jax.ShapeDtypeStruct((m, N_EXPERTS), F32)],
        compiler_params=pltpu.CompilerParams(dimension_semantics=("arbitrary",),
                                             vmem_limit_bytes=VMEM_LIMIT),
    )(pool_out, rwkv_out, x, g1, sc2, sh2, w_out_bf16[:D_POOL], w_out_bf16[D_POOL:],
      n_post.reshape(1, d), n_pre.reshape(1, d), w_router, router_bias.reshape(1, N_EXPERTS))


MOE_TM = 128


def _experts_kernel(te_ref, nused_ref, xs_ref, ws_ref, wg_ref, wu_ref, wd_ref, ys_ref, g_scr, u_scr, d_scr):
    t = pl.program_id(0)
    fresh = jnp.logical_or(t == 0, te_ref[t] != te_ref[jnp.maximum(t - 1, 0)])

    @pl.when(fresh)
    def _():
        g_scr[...] = wg_ref[0].astype(BF16)
        u_scr[...] = wu_ref[0].astype(BF16)
        d_scr[...] = wd_ref[0].astype(BF16)

    @pl.when(t < nused_ref[0])
    def _():
        x = xs_ref[...]
        h = _silu(_dot(x, g_scr[...])) * _dot(x, u_scr[...])
        ys_ref[...] = _dot(h.astype(BF16), d_scr[...]) * ws_ref[...]

    @pl.when(t >= nused_ref[0])
    def _():
        ys_ref[...] = jnp.zeros(ys_ref.shape, F32)


def _experts(tile_expert, n_used, xs, ws, exp_gate, exp_up, exp_down):
    n_slots, d = xs.shape
    n_tiles = n_slots // MOE_TM
    grid_spec = pltpu.PrefetchScalarGridSpec(
        num_scalar_prefetch=2,
        grid=(n_tiles,),
        in_specs=[pl.BlockSpec((MOE_TM, d), lambda t, te, nu: (t, 0)),
                  pl.BlockSpec((MOE_TM, 1), lambda t, te, nu: (t, 0)),
                  pl.BlockSpec((1, d, D_EXPERT), lambda t, te, nu: (te[t], 0, 0)),
                  pl.BlockSpec((1, d, D_EXPERT), lambda t, te, nu: (te[t], 0, 0)),
                  pl.BlockSpec((1, D_EXPERT, d), lambda t, te, nu: (te[t], 0, 0))],
        out_specs=pl.BlockSpec((MOE_TM, d), lambda t, te, nu: (t, 0)),
        scratch_shapes=[pltpu.VMEM((d, D_EXPERT), BF16), pltpu.VMEM((d, D_EXPERT), BF16),
                        pltpu.VMEM((D_EXPERT, d), BF16)],
    )
    return pl.pallas_call(
        _experts_kernel,
        grid_spec=grid_spec,
        out_shape=jax.ShapeDtypeStruct((n_slots, d), F32),
        compiler_params=pltpu.CompilerParams(dimension_semantics=("arbitrary",),
                                             vmem_limit_bytes=VMEM_LIMIT),
    )(tile_expert, n_used, xs, ws, exp_gate, exp_up, exp_down)


ASSIGN_BITS = 17


def _dispatch(chosen, top_idx, top_w):
    n = chosen.shape[0]
    n_assign = n * TOP_K
    assert n_assign < (1 << ASSIGN_BITS) and N_EXPERTS << ASSIGN_BITS < 2 ** 31
    n_tiles = -(-n_assign // MOE_TM) + N_EXPERTS
    n_slots = n_tiles * MOE_TM
    sel = chosen.astype(jnp.int32)
    counts = jnp.sum(sel, axis=0)
    rank = jnp.cumsum(sel, axis=0) - sel
    row_start = jnp.cumsum(counts) - counts
    tiles_e = (counts + MOE_TM - 1) // MOE_TM
    tile_end = jnp.cumsum(tiles_e)
    tile_start = tile_end - tiles_e
    slot = tile_start[None, :] * MOE_TM + rank
    slot_of_pair = jnp.take_along_axis(slot, top_idx, axis=1)
    key = top_idx.reshape(-1) * (1 << ASSIGN_BITS) + jnp.arange(n_assign, dtype=jnp.int32)
    order = lax.sort(key) & ((1 << ASSIGN_BITS) - 1)
    n_used = tile_end[-1]
    tile_ids = jnp.minimum(jnp.arange(n_tiles, dtype=jnp.int32), n_used - 1)
    tile_expert = jnp.minimum(jnp.searchsorted(tile_end, tile_ids, side="right"), N_EXPERTS - 1).astype(jnp.int32)
    e_s = jnp.repeat(tile_expert, MOE_TM)
    r_s = jnp.arange(n_slots, dtype=jnp.int32) - tile_start[e_s] * MOE_TM
    valid = r_s < counts[e_s]
    src = order[jnp.clip(row_start[e_s] + r_s, 0, n_assign - 1)]
    slot_tok = jnp.where(valid, src // TOP_K, 0)
    slot_w = jnp.where(valid, top_w.reshape(-1)[src], 0.0)
    return slot_of_pair, slot_tok, slot_w, tile_expert, n_used.reshape(1).astype(jnp.int32)


def _final_kernel(u2_ref, ys_ref, x1_ref, g2_ref, sg_ref, su_ref, sd_ref, npost_ref, o_ref):
    u2 = u2_ref[...]
    h = _silu(_dot(u2, sg_ref[...])) * _dot(u2, su_ref[...])
    routed = ys_ref[0]
    for kth in range(1, TOP_K):
        routed = routed + ys_ref[kth]
    f = routed + _dot(h.astype(BF16), sd_ref[...])
    o_ref[...] = x1_ref[...] + g2_ref[0] * _rms(f, npost_ref[...])


def _final(u2, ys_pairs, x1, g2, sg, su, sd, n_post, *, tm, tiles_per_seq, row_offset):
    m, d = x1.shape
    if tiles_per_seq is None:
        mod_spec = pl.BlockSpec((1, tm, d), lambda i: (i, 0, 0))
    else:
        mod_spec = pl.BlockSpec((1, 1, d), lambda i: (i // tiles_per_seq, 0, 0))
    row = pl.BlockSpec((tm, d), lambda i: (i, 0))
    off = row_offset // tm
    assert off * tm == row_offset
    return pl.pallas_call(
        _final_kernel,
        grid=(m // tm,),
        in_specs=[row, pl.BlockSpec((TOP_K, tm, d), lambda i: (0, i + off, 0)), row, mod_spec,
                  _const_spec(sg.shape, 1), _const_spec(su.shape, 1), _const_spec(sd.shape, 1),
                  _const_spec((1, d), 1)],
        out_specs=row,
        out_shape=jax.ShapeDtypeStruct((m, d), F32),
        compiler_params=pltpu.CompilerParams(dimension_semantics=("arbitrary",),
                                             vmem_limit_bytes=VMEM_LIMIT),
    )(u2, ys_pairs, x1, g2, sg, su, sd, n_post.reshape(1, d))


def _pad_cols_in(w_in, mu_shift):
    d = w_in.shape[0]
    hp = w_in[:, :D_POOL]
    hr = w_in[:, D_POOL:]
    cuts = [3 * D_RWKV, 3 * D_RWKV + DECAY_LORA, 3 * D_RWKV + DECAY_LORA + AAA_LORA]
    rkv, wl, al, gl = jnp.split(hr, cuts, axis=1)
    zpad = lambda z: jnp.pad(z, ((0, 0), (0, LORA_PAD - z.shape[1])))
    w = jnp.concatenate([rkv, hp, zpad(wl), zpad(al), gl], axis=1).astype(BF16)
    mu_rkv, mu_wl, mu_al, mu_gl = jnp.split(mu_shift, cuts)
    zp1 = lambda z: jnp.pad(z, (0, LORA_PAD - z.shape[0]))
    mu = jnp.concatenate([mu_rkv, jnp.zeros((D_POOL,), F32), zp1(mu_wl), zp1(mu_al), mu_gl])
    assert w.shape == (d, N_IN_PAD) and mu.shape == (N_IN_PAD,)
    return w, mu


def kernel(x_prompt, x_sample, c_prompt, c_sample, state_shift, state_pool, state_wkv, w_ada, b_ada, norm_pre_mix, norm_post_mix, norm_pre_ffn, norm_post_ffn, w_in, mu_shift, pool_w, pool_scale, w0, w_decay_up, a0, w_aaa_up, w_gate_up, k_k, k_a, r_k, ln_x_w, ln_x_b, w_out, w_router, router_bias, exp_gate, exp_up, exp_down, sh_gate, sh_up, sh_down):
    bp, seq, d = x_prompt.shape
    bs = x_sample.shape[0]
    assert w_ada.shape[0] == 1 and x_sample.shape[1] == 1
    np_rows = bp * seq

    c_all = jnp.concatenate([c_prompt, c_sample], axis=0)
    pad_rows = (-c_all.shape[0]) % 16
    ada = _ada(jnp.pad(c_all, ((0, pad_rows), (0, 0))), w_ada[0], b_ada[0])
    sh1, sc1, g1, sh2, sc2, g2 = jnp.split(ada, 6, axis=-1)
    pr = lambda z: z[:bp].reshape(bp, 1, d)
    TS = min(128, bs)
    sr = lambda z: z[bp:bp + bs].reshape(bs // TS, TS, d)

    w_in_p, mu_p = _pad_cols_in(w_in[0], mu_shift[0])
    xp = x_prompt.reshape(np_rows, d)
    xs = x_sample.reshape(bs, d)

    TM_IN = min(1024, seq)
    hx_p, u_tail = _in_proj(xp, norm_pre_mix[0], pr(sc1), pr(sh1), w_in_p, mu_p, tm=TM_IN,
                            tiles_per_seq=seq // TM_IN)
    hx_s, u_s = _in_proj(xs, norm_pre_mix[0], sr(sc1), sr(sh1), w_in_p, mu_p, tm=TS, tiles_per_seq=1,
                         prev=state_shift[0])
    new_shift_prompt = u_tail.reshape(bp, seq // TM_IN, 8, d)[:, -1, -1][None]
    new_shift_sample = u_s[None]

    pool_w_b = pool_w[0].astype(BF16)
    TC_POOL = min(512, seq)
    pool_p = _pool(hx_p, COL_POOL // D_POOL, pool_w_b, pool_scale[0], tc=TC_POOL, tiles_per_seq=seq // TC_POOL,
                   full_count=False)
    hp_p = hx_p[:, COL_POOL:COL_POOL + D_POOL].reshape(bp, seq, D_POOL)
    new_pool_prompt = hp_p[:, seq - POOL_BUF:][None]
    hp_s = hx_s[:, COL_POOL:COL_POOL + D_POOL]
    ext_s = jnp.concatenate([state_pool[0], hp_s[:, None, :]], axis=1)
    assert PAST_LEN + 1 >= max(POOL_WINDOWS)
    pool_s = _pool(ext_s.reshape(bs * (POOL_BUF + 1), D_POOL), 0, pool_w_b, pool_scale[0], tc=POOL_BUF + 1,
                   tiles_per_seq=1, full_count=True)
    pool_s = pool_s.reshape(bs, POOL_BUF + 1, D_POOL)[:, -1]
    new_pool_sample = ext_s[:, 1:][None]

    rw = _rwkv_weights(w0[0], w_decay_up[0], a0[0], w_aaa_up[0], w_gate_up[0], k_k[0], k_a[0], r_k[0],
                       ln_x_w[0], ln_x_b[0])
    rwkv_p, wkv_p = _rwkv_seq(hx_p, rw, batch=bp, seq=seq, tc=128)
    rwkv_s, wkv_s = _rwkv_tok(hx_s, state_wkv[0], rw, tb=8)

    w_out_b = w_out[0].astype(BF16)
    TM_OUT = min(256, seq)
    x1_p, u2_p, idx_p, topw_p, chosen_p = _out_proj(pool_p, rwkv_p, xp, pr(g1), pr(sc2), pr(sh2), w_out_b,
                                                    norm_post_mix[0], norm_pre_ffn[0], w_router[0],
                                                    router_bias[0], tm=TM_OUT, tiles_per_seq=seq // TM_OUT)
    x1_s, u2_s, idx_s, topw_s, chosen_s = _out_proj(pool_s, rwkv_s, xs, sr(g1), sr(sc2), sr(sh2), w_out_b,
                                                    norm_post_mix[0], norm_pre_ffn[0], w_router[0],
                                                    router_bias[0], tm=TS, tiles_per_seq=None)

    n_all = np_rows + bs
    u2_all = jnp.concatenate([u2_p, u2_s], axis=0)
    idx_all = jnp.concatenate([idx_p[:, :TOP_K], idx_s[:, :TOP_K]], axis=0)
    topw_all = jnp.concatenate([topw_p[:, :TOP_K], topw_s[:, :TOP_K]], axis=0)
    chosen_all = jnp.concatenate([chosen_p, chosen_s], axis=0)
    slot_of_pair, slot_tok, slot_w, tile_expert, n_used = _dispatch(chosen_all, idx_all, topw_all)
    ys = _experts(tile_expert, n_used, u2_all[slot_tok], slot_w[:, None], exp_gate[0], exp_up[0], exp_down[0])
    ys_pairs = ys[slot_of_pair.T.reshape(-1)].reshape(TOP_K, n_all, d)

    sg, su, sd = sh_gate[0].astype(BF16), sh_up[0].astype(BF16), sh_down[0].astype(BF16)
    y_p = _final(u2_p, ys_pairs, x1_p, pr(g2), sg, su, sd, norm_post_ffn[0], tm=TM_OUT,
                 tiles_per_seq=seq // TM_OUT, row_offset=0)
    y_s = _final(u2_s, ys_pairs, x1_s, sr(g2), sg, su, sd, norm_post_ffn[0], tm=TS, tiles_per_seq=None,
                 row_offset=np_rows)

    return (y_p.reshape(bp, seq, d), y_s.reshape(bs, 1, d), new_shift_prompt, new_pool_prompt, wkv_p[None],
            new_shift_sample, new_pool_sample, wkv_s[None])
```

```python
import functools

import jax
import jax.numpy as jnp
from jax import lax
from jax.experimental import pallas as pl
from jax.experimental.pallas import tpu as pltpu

F32 = jnp.float32
BF16 = jnp.bfloat16

D_MODEL = 2048
D_POOL = 512
POOL_WINDOWS = (2, 4, 8, 16)
POOL_CH = 128
POOL_BUF = 15
D_RWKV = 1536
HEAD_DIM = 64
RWKV_HEADS = 24
DECAY_LORA = 96
AAA_LORA = 96
GATE_LORA = 256
LORA_PAD = 128
GN_EPS = 64e-5
N_EXPERTS = 256
TOP_K = 8
N_GROUPS = 8
GROUP_SIZE = N_EXPERTS // N_GROUPS
TOPK_GROUPS = 4
D_EXPERT = 512
ROUTE_SCALE = 2.5
EPS = 1e-6
PAST_LEN = 16384

COL_R, COL_K, COL_V = 0, D_RWKV, 2 * D_RWKV
COL_POOL = 3 * D_RWKV
COL_LORA = COL_POOL + D_POOL
N_IN_PAD = COL_LORA + 2 * LORA_PAD + GATE_LORA
LANES = 128
VMEM_LIMIT = 56 * 1024 * 1024


def _dot(a, b):
    return jnp.dot(a, b, preferred_element_type=F32)


def _split2(x):
    hi = x.astype(BF16)
    lo = (x - hi.astype(F32)).astype(BF16)
    return hi, lo


def _split3(x):
    hi = x.astype(BF16)
    r = x - hi.astype(F32)
    mid = r.astype(BF16)
    lo = (r - mid.astype(F32)).astype(BF16)
    return hi, mid, lo


def _dot_x3(a, b):
    ah, al = _split2(a)
    bh, bl = _split2(b)
    return _dot(ah, bh) + (_dot(ah, bl) + _dot(al, bh))


def _dot_exact_rhs(a, b_bf16):
    h, m, l = _split3(a)
    return _dot(h, b_bf16) + (_dot(m, b_bf16) + _dot(l, b_bf16))


def _sigmoid(x):
    return 1.0 / (1.0 + jnp.exp(-x))


def _silu(x):
    return x * _sigmoid(x)


def _rms(x, g):
    return x * lax.rsqrt(jnp.mean(x * x, axis=-1, keepdims=True) + EPS) * g


def _ada_kernel(c_ref, w_ref, b_ref, o_ref):
    o_ref[...] = _dot_x3(_silu(c_ref[...]), w_ref[...]) + b_ref[...]


def _ada(c_all, w_ada, b_ada):
    m, d = c_all.shape
    n = w_ada.shape[1]
    tn = 512
    return pl.pallas_call(
        _ada_kernel,
        grid=(n // tn,),
        in_specs=[pl.BlockSpec((m, d), lambda j: (0, 0)),
                  pl.BlockSpec((d, tn), lambda j: (0, j)),
                  pl.BlockSpec((1, tn), lambda j: (0, j))],
        out_specs=pl.BlockSpec((m, tn), lambda j: (0, j)),
        out_shape=jax.ShapeDtypeStruct((m, n), F32),
        compiler_params=pltpu.CompilerParams(dimension_semantics=("arbitrary",),
                                             vmem_limit_bytes=VMEM_LIMIT),
    )(c_all, w_ada, b_ada.reshape(1, n))


def _in_proj_kernel(*refs, tiles_per_seq, explicit_prev, tm):
    if explicit_prev:
        x_ref, g_ref, sc_ref, sh_ref, prev_ref, w_ref, mu_ref, hx_ref, u_ref, ub_scr = refs
    else:
        x_ref, g_ref, sc_ref, sh_ref, w_ref, mu_ref, hx_ref, u_ref, ub_scr, carry_scr = refs
    i = pl.program_id(0)
    j = pl.program_id(1)

    @pl.when(j == 0)
    def _():
        u = _rms(x_ref[...], g_ref[...])
        u = u * (1.0 + sc_ref[0]) + sh_ref[0]
        ub_scr[...] = u.astype(BF16)
        if explicit_prev:
            u_ref[...] = u
        else:
            u_ref[...] = u[tm - 8:, :]

    w = w_ref[...]
    h = _dot(ub_scr[...], w)
    if explicit_prev:
        hp = _dot(prev_ref[...].astype(BF16), w)
    else:
        @pl.when(i == 0)
        def _():
            carry_scr[j] = jnp.zeros(carry_scr.shape[1:], F32)

        first = jnp.where(i % tiles_per_seq == 0, 0.0, carry_scr[j, 0:1, :])
        row0 = lax.broadcasted_iota(jnp.int32, h.shape, 0) == 0
        hp = jnp.where(row0, first, pltpu.roll(h, 1, axis=0))
        carry_scr[j, 0:1, :] = h[tm - 1:tm, :]
    hx_ref[...] = h + (hp - h) * mu_ref[...]


def _in_proj(x, gamma, sc, sh, w_bf16, mu_pad, *, tm, tiles_per_seq, prev=None):
    m, d = x.shape
    n = w_bf16.shape[1]
    tn = 512
    nt = n // tn
    explicit_prev = prev is not None
    if explicit_prev:
        mod_spec = pl.BlockSpec((1, tm, d), lambda i, j: (i, 0, 0))
    else:
        mod_spec = pl.BlockSpec((1, 1, d), lambda i, j: (i // tiles_per_seq, 0, 0))
    in_specs = [pl.BlockSpec((tm, d), lambda i, j: (i, 0)),
                pl.BlockSpec((1, d), lambda i, j: (0, 0)),
                mod_spec, mod_spec]
    args = [x, gamma.reshape(1, d), sc, sh]
    if explicit_prev:
        in_specs.append(pl.BlockSpec((tm, d), lambda i, j: (i, 0)))
        args.append(prev)
    in_specs += [pl.BlockSpec((d, tn), lambda i, j: (0, j)),
                 pl.BlockSpec((1, tn), lambda i, j: (0, j))]
    args += [w_bf16, mu_pad.reshape(1, n)]
    scratch = [pltpu.VMEM((tm, d), BF16)]
    if explicit_prev:
        u_shape, u_spec = (m, d), pl.BlockSpec((tm, d), lambda i, j: (i, 0))
    else:
        u_shape, u_spec = (m // tm * 8, d), pl.BlockSpec((8, d), lambda i, j: (i, 0))
        scratch.append(pltpu.VMEM((nt, 8, tn), F32))
    return pl.pallas_call(
        functools.partial(_in_proj_kernel, tiles_per_seq=tiles_per_seq, explicit_prev=explicit_prev, tm=tm),
        grid=(m // tm, nt),
        in_specs=in_specs,
        out_specs=[pl.BlockSpec((tm, tn), lambda i, j: (i, j)), u_spec],
        out_shape=[jax.ShapeDtypeStruct((m, n), F32), jax.ShapeDtypeStruct(u_shape, F32)],
        scratch_shapes=scratch,
        compiler_params=pltpu.CompilerParams(dimension_semantics=("arbitrary", "arbitrary"),
                                             vmem_limit_bytes=VMEM_LIMIT),
    )(*args)


POOL_HALO = 16


def _pool_kernel(p_ref, pw_ref, ps_ref, o_ref, ext_scr, *, tiles_per_seq, full_count, tc):
    i = pl.program_id(0)

    @pl.when(i % tiles_per_seq == 0)
    def _():
        ext_scr[0:POOL_HALO, :] = jnp.zeros((POOL_HALO, D_POOL), F32)

    p = p_ref[...]
    ext_scr[POOL_HALO:POOL_HALO + tc, :] = p
    pos = (i % tiles_per_seq) * tc + lax.broadcasted_iota(jnp.int32, (tc, 1), 0)
    outs = []
    for gi, win in enumerate(POOL_WINDOWS):
        lo = gi * POOL_CH
        pg = p[:, lo:lo + POOL_CH]
        acc = pg
        for s in range(1, win):
            acc = acc + ext_scr[POOL_HALO - s:POOL_HALO - s + tc, lo:lo + POOL_CH]
        if full_count:
            dgi = acc / float(win) - pg
        else:
            cnt = jnp.minimum(pos + 1, win).astype(F32)
            dgi = acc / cnt - pg
        outs.append(_dot(dgi.astype(BF16), pw_ref[gi]))
    y = jnp.concatenate(outs, axis=-1) * ps_ref[...]
    o_ref[...] = y.astype(BF16)
    ext_scr[0:POOL_HALO, :] = ext_scr[tc:tc + POOL_HALO, :]


def _pool(src, col_block, pool_w_bf16, pool_scale, *, tc, tiles_per_seq, full_count):
    m = src.shape[0]
    return pl.pallas_call(
        functools.partial(_pool_kernel, tiles_per_seq=tiles_per_seq, full_count=full_count, tc=tc),
        grid=(m // tc,),
        in_specs=[pl.BlockSpec((tc, D_POOL), lambda i: (i, col_block)),
                  pl.BlockSpec((len(POOL_WINDOWS), POOL_CH, POOL_CH), lambda i: (0, 0, 0)),
                  pl.BlockSpec((1, D_POOL), lambda i: (0, 0))],
        out_specs=pl.BlockSpec((tc, D_POOL), lambda i: (i, 0)),
        out_shape=jax.ShapeDtypeStruct((m, D_POOL), BF16),
        scratch_shapes=[pltpu.VMEM((POOL_HALO + tc, D_POOL), F32)],
        compiler_params=pltpu.CompilerParams(dimension_semantics=("arbitrary",),
                                             vmem_limit_bytes=VMEM_LIMIT),
    )(src, pool_w_bf16, pool_scale.reshape(1, D_POOL))


def _softplus(z):
    return jnp.maximum(z, 0.0) + jnp.log1p(jnp.exp(-jnp.abs(z)))


def _rwkv_prep(r, k, v, lora, prm, seg, segt):
    (w0, wd, a0, wa, wg, k_k, k_a, r_k) = prm
    wl = lora[:, 0:LORA_PAD]
    al = lora[:, LORA_PAD:2 * LORA_PAD]
    gl = lora[:, 2 * LORA_PAD:]
    logw = -_softplus(-(w0 + _dot_x3(jnp.tanh(wl), wd))) - 0.5
    log_decay = -jnp.exp(logw)
    decay = jnp.exp(log_decay)
    a = _sigmoid(a0 + _dot_x3(al, wa))
    gate = _dot(_sigmoid(gl).astype(BF16), wg.astype(BF16))
    kk = k * k_k
    ss = _dot_exact_rhs(kk * kk, seg)
    inv = 1.0 / jnp.maximum(jnp.sqrt(ss), 1e-12)
    kk = kk * _dot_exact_rhs(inv, segt)
    k_mod = k * (1.0 + (a - 1.0) * k_a)
    bonus = _dot_exact_rhs(_dot_exact_rhs(r * k_mod * r_k, seg), segt) * v
    return decay, log_decay, k_mod, -kk, kk * a, gate, bonus


def _group_norm_head(y):
    mu = jnp.mean(y, axis=-1, keepdims=True)
    yc = y - mu
    var = jnp.mean(yc * yc, axis=-1, keepdims=True)
    return yc * lax.rsqrt(var + GN_EPS)


def _group_norm_rows(y, seg, segt):
    inv_n = 1.0 / HEAD_DIM
    mu = _dot_exact_rhs(_dot_exact_rhs(y, seg) * inv_n, segt)
    yc = y - mu
    var = _dot_exact_rhs(yc * yc, seg) * inv_n
    return yc * _dot_exact_rhs(lax.rsqrt(var + GN_EPS), segt)


WKV_CHUNK = HEAD_DIM
PAIR_LANES = 2 * HEAD_DIM
N_PAIRS = RWKV_HEADS // 2
_NT = (((1,), (1,)), ((), ()))
_TN = (((0,), (0,)), ((), ()))


def _dg_x3(a, b, dims):
    ah, al = _split2(a)
    bh, bl = _split2(b)
    dg = lambda x, y: lax.dot_general(x, y, dims, preferred_element_type=F32)
    return dg(ah, bh) + (dg(ah, bl) + dg(al, bh))


def _wkv_masks():
    n = PAIR_LANES
    ri = lax.broadcasted_iota(jnp.int32, (n, n), 0)
    ci = lax.broadcasted_iota(jnp.int32, (n, n), 1)
    same_head = (ri // HEAD_DIM) == (ci // HEAD_DIM)
    same16 = (ri // 16) == (ci // 16)
    same32 = (ri // 32) == (ci // 32)
    return dict(same_head=same_head, strict=same_head & (ci < ri), incl=same_head & (ci <= ri), eye=ri == ci,
                same16=same16, only32=same32 & jnp.logical_not(same16), not32=jnp.logical_not(same32),
                head0=lax.broadcasted_iota(jnp.int32, (1, n), 1) < HEAD_DIM)


def _unit_lower_inverse(lab, mk):
    d = jnp.where(mk["same16"], lab, 0.0)
    x = jnp.where(mk["eye"], 1.0, 0.0) + d
    for _ in range(3):
        d = _dot_x3(d, d)
        x = x + _dot_x3(x, d)
    for level in ("only32", "not32"):
        e = jnp.where(mk[level], lab, 0.0)
        x = x + _dot_x3(x, _dot_x3(e, x))
    return x


def _wkv_chunk_pair(at, rt, bt, kt, bh, kh, v, w_last, s_bd, mk):
    c = WKV_CHUNK
    stack = lambda x: jnp.concatenate([jnp.where(mk["head0"], x, 0.0), jnp.where(mk["head0"], 0.0, x)], axis=0)
    dup = lambda x: jnp.concatenate([x, x], axis=0)
    a_s, r_s, v_s = stack(at), stack(rt), stack(v)
    b2, k2, bh2, kh2 = dup(bt), dup(kt), dup(bh), dup(kh)
    lab = jnp.where(mk["strict"], _dg_x3(a_s, b2, _NT), 0.0)
    lak = jnp.where(mk["strict"], _dg_x3(a_s, k2, _NT), 0.0)
    lrb = jnp.where(mk["incl"], _dg_x3(r_s, b2, _NT), 0.0)
    lrk = jnp.where(mk["incl"], _dg_x3(r_s, k2, _NT), 0.0)
    t_inv = _unit_lower_inverse(lab, mk)
    ta_s = _dot_x3(t_inv, a_s)
    u_s = _dot_x3(t_inv, _dot_x3(lak, v_s))
    m_bd = jnp.where(mk["same_head"], _dg_x3(ta_s, bh2, _TN), 0.0) + jnp.where(mk["eye"], w_last, 0.0)
    g_bd = jnp.where(mk["same_head"], _dg_x3(u_s, bh2, _TN) + _dg_x3(v_s, kh2, _TN), 0.0)
    q_s = _dot_x3(lrb, ta_s)
    y0_s = _dot_x3(lrb, u_s) + _dot_x3(lrk, v_s)
    q = rt + (q_s[:c] + q_s[c:])
    y = _dg_x3(q, s_bd, _NT) + (y0_s[:c] + y0_s[c:])
    return y, _dot_x3(s_bd, m_bd) + g_bd


def _rwkv_seq_kernel(r_ref, k_ref, v_ref, lora_ref, w0_ref, wd_ref, a0_ref, wa_ref, wg_ref, kk_ref, ka_ref,
                     rk_ref, lnw_ref, lnb_ref, seg_ref, segt_ref, o_ref, sout_ref, s_scr, y_scr):
    c = pl.program_id(1)

    @pl.when(c == 0)
    def _():
        s_scr[...] = jnp.zeros(s_scr.shape, F32)

    r = r_ref[...]
    v = v_ref[...]
    seg = seg_ref[...]
    segt = segt_ref[...]
    prm = (w0_ref[...], wd_ref[...], a0_ref[...], wa_ref[...], wg_ref[...], kk_ref[...], ka_ref[...], rk_ref[...])
    decay, log_decay, k_mod, neg_kk, b, gate, bonus = _rwkv_prep(r, k_ref[...], v, lora_ref[...], prm, seg, segt)

    n = WKV_CHUNK
    tri = (lax.broadcasted_iota(jnp.int32, (n, n), 1) <= lax.broadcasted_iota(jnp.int32, (n, n), 0)).astype(BF16)
    ld_h, ld_m, ld_l = _split3(log_decay)
    cw = _dot(tri, ld_h) + (_dot(tri, ld_m) + _dot(tri, ld_l))
    cw_prev = cw - log_decay
    cw_last = cw[n - 1:n, :]
    e_prev = jnp.exp(cw_prev)
    e_inv = jnp.exp(-cw)
    e_rem = jnp.exp(cw_last - cw)
    at = neg_kk * e_prev
    rt = r * (e_prev * decay)
    bt = b * e_inv
    kt = k_mod * e_inv
    bh = b * e_rem
    kh = k_mod * e_rem
    w_last = jnp.exp(cw_last)

    mk = _wkv_masks()
    for p in range(N_PAIRS):
        sl = slice(p * PAIR_LANES, (p + 1) * PAIR_LANES)
        y, s_new = _wkv_chunk_pair(at[:, sl], rt[:, sl], bt[:, sl], kt[:, sl], bh[:, sl], kh[:, sl], v[:, sl],
                                   w_last[:, sl], s_scr[p], mk)
        s_scr[p] = s_new
        y_scr[:, sl] = y

    out = (_group_norm_rows(y_scr[...], seg, segt) * lnw_ref[...] + lnb_ref[...] + bonus) * gate
    o_ref[...] = out.astype(BF16)

    @pl.when(c == pl.num_programs(1) - 1)
    def _():
        for p in range(N_PAIRS):
            sout_ref[0, 2 * p] = s_scr[p, 0:HEAD_DIM, 0:HEAD_DIM]
            sout_ref[0, 2 * p + 1] = s_scr[p, HEAD_DIM:, HEAD_DIM:]


def _rwkv_weights(w0, w_decay_up, a0, w_aaa_up, w_gate_up, k_k, k_a, r_k, ln_x_w, ln_x_b):
    row = lambda z: z.reshape(1, D_RWKV)
    pad = lambda z: jnp.pad(z, ((0, LORA_PAD - z.shape[0]), (0, 0)))
    head_of_col = jnp.arange(D_RWKV, dtype=jnp.int32) // HEAD_DIM
    seg = (head_of_col[:, None] == jnp.arange(LANES, dtype=jnp.int32)[None, :]).astype(BF16)
    return (row(w0), pad(w_decay_up), row(a0), pad(w_aaa_up), w_gate_up, row(k_k), row(k_a), row(r_k),
            row(ln_x_w), row(ln_x_b), seg, seg.T)


def _const_spec(shape, grid_rank):
    zeros = (0,) * len(shape)
    if grid_rank == 1:
        return pl.BlockSpec(shape, lambda i: zeros)
    return pl.BlockSpec(shape, lambda i, j: zeros)


def _rwkv_seq(hx, weights, *, batch, seq):
    tc = WKV_CHUNK
    nchunk = seq // tc
    rowmap = lambda cb: (lambda b, c: (b * nchunk + c, cb))
    in_specs = [pl.BlockSpec((tc, D_RWKV), rowmap(0)),
                pl.BlockSpec((tc, D_RWKV), rowmap(1)),
                pl.BlockSpec((tc, D_RWKV), rowmap(2)),
                pl.BlockSpec((tc, 2 * LORA_PAD + GATE_LORA), rowmap(COL_LORA // (2 * LORA_PAD + GATE_LORA)))]
    in_specs += [_const_spec(w.shape, 2) for w in weights]
    return pl.pallas_call(
        _rwkv_seq_kernel,
        grid=(batch, nchunk),
        in_specs=in_specs,
        out_specs=[pl.BlockSpec((tc, D_RWKV), lambda b, c: (b * nchunk + c, 0)),
                   pl.BlockSpec((1, RWKV_HEADS, HEAD_DIM, HEAD_DIM), lambda b, c: (b, 0, 0, 0))],
        out_shape=[jax.ShapeDtypeStruct((batch * seq, D_RWKV), BF16),
                   jax.ShapeDtypeStruct((batch, RWKV_HEADS, HEAD_DIM, HEAD_DIM), F32)],
        scratch_shapes=[pltpu.VMEM((N_PAIRS, PAIR_LANES, PAIR_LANES), F32), pltpu.VMEM((tc, D_RWKV), F32)],
        compiler_params=pltpu.CompilerParams(dimension_semantics=("arbitrary", "arbitrary"),
                                             vmem_limit_bytes=VMEM_LIMIT),
    )(hx, hx, hx, hx, *weights)


def _rwkv_tok_prep_kernel(r_ref, k_ref, v_ref, lora_ref, w0_ref, wd_ref, a0_ref, wa_ref, wg_ref, kk_ref, ka_ref,
                          rk_ref, seg_ref, segt_ref, w_o, k_o, a_o, b_o, g_o, bonus_o):
    prm = (w0_ref[...], wd_ref[...], a0_ref[...], wa_ref[...], wg_ref[...], kk_ref[...], ka_ref[...], rk_ref[...])
    decay, _, k_mod, neg_kk, b, gate, bonus = _rwkv_prep(r_ref[...], k_ref[...], v_ref[...], lora_ref[...], prm,
                                                          seg_ref[...], segt_ref[...])
    w_o[...] = decay
    k_o[...] = k_mod
    a_o[...] = neg_kk
    b_o[...] = b
    g_o[...] = gate
    bonus_o[...] = bonus


def _rwkv_tok_step_kernel(r_ref, w_ref, k_ref, v_ref, a_ref, b_ref, g_ref, bonus_ref, lnw_ref, lnb_ref, s0_ref,
                          o_ref, s_ref, *, tb):
    eye = (lax.broadcasted_iota(jnp.int32, (HEAD_DIM, HEAD_DIM), 0)
           == lax.broadcasted_iota(jnp.int32, (HEAD_DIM, HEAD_DIM), 1))

    def body(bi, carry):
        s = s0_ref[bi]
        sa = jnp.sum(s * a_ref[bi], axis=2, keepdims=True)
        v_col = jnp.sum(jnp.where(eye, v_ref[bi], 0.0), axis=2, keepdims=True)
        s = s * w_ref[bi] + sa * b_ref[bi] + v_col * k_ref[bi]
        s_ref[bi] = s
        y_col = jnp.sum(s * r_ref[bi], axis=2, keepdims=True)
        y = jnp.sum(jnp.where(eye, y_col, 0.0), axis=1, keepdims=True)
        o_ref[bi] = (_group_norm_head(y) * lnw_ref[0] + lnb_ref[0] + bonus_ref[bi]) * g_ref[bi]
        return carry

    lax.fori_loop(0, tb, body, 0)


def _rwkv_tok(hx, state, weights, *, tb):
    nb = hx.shape[0]
    (w0, wd, a0, wa, wg, k_k, k_a, r_k, ln_w, ln_b, seg, segt) = weights
    prep_w = (w0, wd, a0, wa, wg, k_k, k_a, r_k, seg, segt)
    colmap = lambda cb: (lambda i: (0, cb))
    in_specs = [pl.BlockSpec((nb, D_RWKV), colmap(0)),
                pl.BlockSpec((nb, D_RWKV), colmap(1)),
                pl.BlockSpec((nb, D_RWKV), colmap(2)),
                pl.BlockSpec((nb, 2 * LORA_PAD + GATE_LORA), colmap(COL_LORA // (2 * LORA_PAD + GATE_LORA)))]
    in_specs += [_const_spec(w.shape, 1) for w in prep_w]
    rows = jax.ShapeDtypeStruct((nb, D_RWKV), F32)
    prepped = pl.pallas_call(
        _rwkv_tok_prep_kernel,
        grid=(1,),
        in_specs=in_specs,
        out_specs=[pl.BlockSpec((nb, D_RWKV), lambda i: (0, 0))] * 6,
        out_shape=[rows] * 6,
        compiler_params=pltpu.CompilerParams(dimension_semantics=("arbitrary",),
                                             vmem_limit_bytes=VMEM_LIMIT),
    )(hx, hx, hx, hx, *prep_w)
    decay, k_mod, neg_kk, b, gate, bonus = prepped
    heads = lambda z: z.reshape(nb, RWKV_HEADS, 1, HEAD_DIM)
    vecs = [heads(z) for z in (hx[:, COL_R:COL_R + D_RWKV], decay, k_mod, hx[:, COL_V:COL_V + D_RWKV], neg_kk, b,
                               gate, bonus)]
    vec_spec = pl.BlockSpec((tb, RWKV_HEADS, 1, HEAD_DIM), lambda i: (i, 0, 0, 0))
    ln_spec = pl.BlockSpec((1, RWKV_HEADS, 1, HEAD_DIM), lambda i: (0, 0, 0, 0))
    state_spec = pl.BlockSpec((tb, RWKV_HEADS, HEAD_DIM, HEAD_DIM), lambda i: (i, 0, 0, 0))
    out, new_state = pl.pallas_call(
        functools.partial(_rwkv_tok_step_kernel, tb=tb),
        grid=(nb // tb,),
        in_specs=[vec_spec] * 8 + [ln_spec, ln_spec, state_spec],
        out_specs=[vec_spec, state_spec],
        out_shape=[jax.ShapeDtypeStruct((nb, RWKV_HEADS, 1, HEAD_DIM), F32),
                   jax.ShapeDtypeStruct(state.shape, F32)],
        compiler_params=pltpu.CompilerParams(dimension_semantics=("arbitrary",),
                                             vmem_limit_bytes=VMEM_LIMIT),
    )(*vecs, ln_w.reshape(1, RWKV_HEADS, 1, HEAD_DIM), ln_b.reshape(1, RWKV_HEADS, 1, HEAD_DIM), state)
    return out.reshape(nb, D_RWKV).astype(BF16), new_state


def _route(logits, bias):
    scores = _sigmoid(logits)
    sel = scores + bias
    rows = logits.shape[0]
    lane_i = lax.broadcasted_iota(jnp.int32, (rows, N_EXPERTS), 1)
    grp = lane_i // GROUP_SIZE
    lane = lane_i.astype(F32)
    neg_inf = -jnp.inf

    def take_max(x):
        m = jnp.max(x, axis=1, keepdims=True)
        idx = jnp.min(jnp.where(x == m, lane, float(N_EXPERTS)), axis=1, keepdims=True)
        return m, lane == idx

    gscore = []
    for g in range(N_GROUPS):
        x = jnp.where(grp == g, sel, neg_inf)
        m1, hit = take_max(x)
        m2 = jnp.max(jnp.where(hit, neg_inf, x), axis=1, keepdims=True)
        gscore.append(m1 + m2)
    emask = jnp.zeros((rows, N_EXPERTS), jnp.bool_)
    for g in range(N_GROUPS):
        ahead = jnp.zeros((rows, 1), jnp.int32)
        for g2 in range(N_GROUPS):
            if g2 == g:
                continue
            beats = (gscore[g2] >= gscore[g]) if g2 < g else (gscore[g2] > gscore[g])
            ahead = ahead + beats.astype(jnp.int32)
        emask = emask | ((grp == g) & (ahead < TOPK_GROUPS))
    x = jnp.where(emask, sel, neg_inf)
    chosen = jnp.zeros((rows, N_EXPERTS), jnp.bool_)
    slot_lane = lax.broadcasted_iota(jnp.int32, (rows, LANES), 1)
    top_idx = jnp.zeros((rows, LANES), F32)
    top_w = jnp.zeros((rows, LANES), F32)
    for kth in range(TOP_K):
        m = jnp.max(x, axis=1, keepdims=True)
        idx = jnp.min(jnp.where(x == m, lane, float(N_EXPERTS)), axis=1, keepdims=True)
        hit = lane == idx
        chosen = chosen | hit
        x = jnp.where(hit, neg_inf, x)
        top_idx = jnp.where(slot_lane == kth, idx, top_idx)
        top_w = jnp.where(slot_lane == kth, jnp.sum(jnp.where(hit, scores, 0.0), axis=1, keepdims=True), top_w)
    top_w = top_w / jnp.sum(top_w, axis=1, keepdims=True) * ROUTE_SCALE
    return top_idx.astype(jnp.int32), top_w, chosen.astype(F32)


def _out_proj_kernel(pool_ref, rwkv_ref, x_ref, g1_ref, sc_ref, sh_ref, wp_ref, wr_ref, npost_ref, npre_ref,
                     wrt_ref, rb_ref, x1_ref, u2_ref, idx_ref, topw_ref, chosen_ref):
    m = _dot(pool_ref[...], wp_ref[...]) + _dot(rwkv_ref[...], wr_ref[...])
    x1 = x_ref[...] + g1_ref[0] * _rms(m, npost_ref[...])
    x1_ref[...] = x1
    u2 = _rms(x1, npre_ref[...]) * (1.0 + sc_ref[0]) + sh_ref[0]
    u2_ref[...] = u2.astype(BF16)
    top_idx, top_w, chosen = _route(_dot_x3(u2, wrt_ref[...]), rb_ref[...])
    idx_ref[...] = top_idx
    topw_ref[...] = top_w
    chosen_ref[...] = chosen


def _out_proj(pool_out, rwkv_out, x, g1, sc2, sh2, w_out_bf16, n_post, n_pre, w_router, router_bias, *, tm,
              tiles_per_seq):
    m, d = x.shape
    if tiles_per_seq is None:
        mod_spec = pl.BlockSpec((1, tm, d), lambda i: (i, 0, 0))
    else:
        mod_spec = pl.BlockSpec((1, 1, d), lambda i: (i // tiles_per_seq, 0, 0))
    row = lambda width: pl.BlockSpec((tm, width), lambda i: (i, 0))
    return pl.pallas_call(
        _out_proj_kernel,
        grid=(m // tm,),
        in_specs=[row(D_POOL), row(D_RWKV), row(d), mod_spec, mod_spec, mod_spec,
                  _const_spec((D_POOL, d), 1), _const_spec((D_RWKV, d), 1),
                  _const_spec((1, d), 1), _const_spec((1, d), 1),
                  _const_spec((d, N_EXPERTS), 1), _const_spec((1, N_EXPERTS), 1)],
        out_specs=[row(d), row(d), row(LANES), row(LANES), row(N_EXPERTS)],
        out_shape=[jax.ShapeDtypeStruct((m, d), F32), jax.ShapeDtypeStruct((m, d), BF16),
                   jax.ShapeDtypeStruct((m, LANES), jnp.int32), jax.ShapeDtypeStruct((m, LANES), F32),
                   jax.ShapeDtypeStruct((m, N_EXPERTS), F32)],
        compiler_params=pltpu.CompilerParams(dimension_semantics=("arbitrary",),
                                             vmem_limit_bytes=VMEM_LIMIT),
    )(pool_out, rwkv_out, x, g1, sc2, sh2, w_out_bf16[:D_POOL], w_out_bf16[D_POOL:],
      n_post.reshape(1, d), n_pre.reshape(1, d), w_router, router_bias.reshape(1, N_EXPERTS))


MOE_TM = 128


def _experts_kernel(te_ref, nused_ref, xs_ref, ws_ref, wg_ref, wu_ref, wd_ref, ys_ref, g_scr, u_scr, d_scr):
    t = pl.program_id(0)
    fresh = jnp.logical_or(t == 0, te_ref[t] != te_ref[jnp.maximum(t - 1, 0)])

    @pl.when(fresh)
    def _():
        g_scr[...] = wg_ref[0].astype(BF16)
        u_scr[...] = wu_ref[0].astype(BF16)
        d_scr[...] = wd_ref[0].astype(BF16)

    @pl.when(t < nused_ref[0])
    def _():
        x = xs_ref[...]
        h = _silu(_dot(x, g_scr[...])) * _dot(x, u_scr[...])
        ys_ref[...] = _dot(h.astype(BF16), d_scr[...]) * ws_ref[...]

    @pl.when(t >= nused_ref[0])
    def _():
        ys_ref[...] = jnp.zeros(ys_ref.shape, F32)


def _experts(tile_expert, n_used, xs, ws, exp_gate, exp_up, exp_down):
    n_slots, d = xs.shape
    n_tiles = n_slots // MOE_TM
    grid_spec = pltpu.PrefetchScalarGridSpec(
        num_scalar_prefetch=2,
        grid=(n_tiles,),
        in_specs=[pl.BlockSpec((MOE_TM, d), lambda t, te, nu: (t, 0)),
                  pl.BlockSpec((MOE_TM, 1), lambda t, te, nu: (t, 0)),
                  pl.BlockSpec((1, d, D_EXPERT), lambda t, te, nu: (te[t], 0, 0)),
                  pl.BlockSpec((1, d, D_EXPERT), lambda t, te, nu: (te[t], 0, 0)),
                  pl.BlockSpec((1, D_EXPERT, d), lambda t, te, nu: (te[t], 0, 0))],
        out_specs=pl.BlockSpec((MOE_TM, d), lambda t, te, nu: (t, 0)),
        scratch_shapes=[pltpu.VMEM((d, D_EXPERT), BF16), pltpu.VMEM((d, D_EXPERT), BF16),
                        pltpu.VMEM((D_EXPERT, d), BF16)],
    )
    return pl.pallas_call(
        _experts_kernel,
        grid_spec=grid_spec,
        out_shape=jax.ShapeDtypeStruct((n_slots, d), F32),
        compiler_params=pltpu.CompilerParams(dimension_semantics=("arbitrary",),
                                             vmem_limit_bytes=VMEM_LIMIT),
    )(tile_expert, n_used, xs, ws, exp_gate, exp_up, exp_down)


ASSIGN_BITS = 17


def _dispatch(chosen, top_idx, top_w):
    n = chosen.shape[0]
    n_assign = n * TOP_K
    assert n_assign < (1 << ASSIGN_BITS) and N_EXPERTS << ASSIGN_BITS < 2 ** 31
    n_tiles = -(-n_assign // MOE_TM) + N_EXPERTS
    n_slots = n_tiles * MOE_TM
    sel = chosen.astype(jnp.int32)
    counts = jnp.sum(sel, axis=0)
    rank = jnp.cumsum(sel, axis=0) - sel
    row_start = jnp.cumsum(counts) - counts
    tiles_e = (counts + MOE_TM - 1) // MOE_TM
    tile_end = jnp.cumsum(tiles_e)
    tile_start = tile_end - tiles_e
    slot = tile_start[None, :] * MOE_TM + rank
    slot_of_pair = jnp.take_along_axis(slot, top_idx, axis=1)
    key = top_idx.reshape(-1) * (1 << ASSIGN_BITS) + jnp.arange(n_assign, dtype=jnp.int32)
    order = lax.sort(key) & ((1 << ASSIGN_BITS) - 1)
    n_used = tile_end[-1]
    tile_ids = jnp.minimum(jnp.arange(n_tiles, dtype=jnp.int32), n_used - 1)
    tile_expert = jnp.minimum(jnp.searchsorted(tile_end, tile_ids, side="right"), N_EXPERTS - 1).astype(jnp.int32)
    e_s = jnp.repeat(tile_expert, MOE_TM)
    r_s = jnp.arange(n_slots, dtype=jnp.int32) - tile_start[e_s] * MOE_TM
    valid = r_s < counts[e_s]
    src = order[jnp.clip(row_start[e_s] + r_s, 0, n_assign - 1)]
    slot_tok = jnp.where(valid, src // TOP_K, 0)
    slot_w = jnp.where(valid, top_w.reshape(-1)[src], 0.0)
    return slot_of_pair, slot_tok, slot_w, tile_expert, n_used.reshape(1).astype(jnp.int32)


def _final_kernel(u2_ref, ys_ref, x1_ref, g2_ref, sg_ref, su_ref, sd_ref, npost_ref, o_ref):
    u2 = u2_ref[...]
    h = _silu(_dot(u2, sg_ref[...])) * _dot(u2, su_ref[...])
    routed = ys_ref[0]
    for kth in range(1, TOP_K):
        routed = routed + ys_ref[kth]
    f = routed + _dot(h.astype(BF16), sd_ref[...])
    o_ref[...] = x1_ref[...] + g2_ref[0] * _rms(f, npost_ref[...])


def _final(u2, ys_pairs, x1, g2, sg, su, sd, n_post, *, tm, tiles_per_seq, row_offset):
    m, d = x1.shape
    if tiles_per_seq is None:
        mod_spec = pl.BlockSpec((1, tm, d), lambda i: (i, 0, 0))
    else:
        mod_spec = pl.BlockSpec((1, 1, d), lambda i: (i // tiles_per_seq, 0, 0))
    row = pl.BlockSpec((tm, d), lambda i: (i, 0))
    off = row_offset // tm
    assert off * tm == row_offset
    return pl.pallas_call(
        _final_kernel,
        grid=(m // tm,),
        in_specs=[row, pl.BlockSpec((TOP_K, tm, d), lambda i: (0, i + off, 0)), row, mod_spec,
                  _const_spec(sg.shape, 1), _const_spec(su.shape, 1), _const_spec(sd.shape, 1),
                  _const_spec((1, d), 1)],
        out_specs=row,
        out_shape=jax.ShapeDtypeStruct((m, d), F32),
        compiler_params=pltpu.CompilerParams(dimension_semantics=("arbitrary",),
                                             vmem_limit_bytes=VMEM_LIMIT),
    )(u2, ys_pairs, x1, g2, sg, su, sd, n_post.reshape(1, d))


def _pad_cols_in(w_in, mu_shift):
    d = w_in.shape[0]
    hp = w_in[:, :D_POOL]
    hr = w_in[:, D_POOL:]
    cuts = [3 * D_RWKV, 3 * D_RWKV + DECAY_LORA, 3 * D_RWKV + DECAY_LORA + AAA_LORA]
    rkv, wl, al, gl = jnp.split(hr, cuts, axis=1)
    zpad = lambda z: jnp.pad(z, ((0, 0), (0, LORA_PAD - z.shape[1])))
    w = jnp.concatenate([rkv, hp, zpad(wl), zpad(al), gl], axis=1).astype(BF16)
    mu_rkv, mu_wl, mu_al, mu_gl = jnp.split(mu_shift, cuts)
    zp1 = lambda z: jnp.pad(z, (0, LORA_PAD - z.shape[0]))
    mu = jnp.concatenate([mu_rkv, jnp.zeros((D_POOL,), F32), zp1(mu_wl), zp1(mu_al), mu_gl])
    assert w.shape == (d, N_IN_PAD) and mu.shape == (N_IN_PAD,)
    return w, mu


def kernel(x_prompt, x_sample, c_prompt, c_sample, state_shift, state_pool, state_wkv, w_ada, b_ada, norm_pre_mix, norm_post_mix, norm_pre_ffn, norm_post_ffn, w_in, mu_shift, pool_w, pool_scale, w0, w_decay_up, a0, w_aaa_up, w_gate_up, k_k, k_a, r_k, ln_x_w, ln_x_b, w_out, w_router, router_bias, exp_gate, exp_up, exp_down, sh_gate, sh_up, sh_down):
    bp, seq, d = x_prompt.shape
    bs = x_sample.shape[0]
    assert w_ada.shape[0] == 1 and x_sample.shape[1] == 1
    np_rows = bp * seq

    c_all = jnp.concatenate([c_prompt, c_sample], axis=0)
    pad_rows = (-c_all.shape[0]) % 16
    ada = _ada(jnp.pad(c_all, ((0, pad_rows), (0, 0))), w_ada[0], b_ada[0])
    sh1, sc1, g1, sh2, sc2, g2 = jnp.split(ada, 6, axis=-1)
    pr = lambda z: z[:bp].reshape(bp, 1, d)
    TS = min(128, bs)
    sr = lambda z: z[bp:bp + bs].reshape(bs // TS, TS, d)

    w_in_p, mu_p = _pad_cols_in(w_in[0], mu_shift[0])
    xp = x_prompt.reshape(np_rows, d)
    xs = x_sample.reshape(bs, d)

    TM_IN = min(1024, seq)
    hx_p, u_tail = _in_proj(xp, norm_pre_mix[0], pr(sc1), pr(sh1), w_in_p, mu_p, tm=TM_IN,
                            tiles_per_seq=seq // TM_IN)
    hx_s, u_s = _in_proj(xs, norm_pre_mix[0], sr(sc1), sr(sh1), w_in_p, mu_p, tm=TS, tiles_per_seq=1,
                         prev=state_shift[0])
    new_shift_prompt = u_tail.reshape(bp, seq // TM_IN, 8, d)[:, -1, -1][None]
    new_shift_sample = u_s[None]

    pool_w_b = pool_w[0].astype(BF16)
    TC_POOL = min(512, seq)
    pool_p = _pool(hx_p, COL_POOL // D_POOL, pool_w_b, pool_scale[0], tc=TC_POOL, tiles_per_seq=seq // TC_POOL,
                   full_count=False)
    hp_p = hx_p[:, COL_POOL:COL_POOL + D_POOL].reshape(bp, seq, D_POOL)
    new_pool_prompt = hp_p[:, seq - POOL_BUF:][None]
    hp_s = hx_s[:, COL_POOL:COL_POOL + D_POOL]
    ext_s = jnp.concatenate([state_pool[0], hp_s[:, None, :]], axis=1)
    assert PAST_LEN + 1 >= max(POOL_WINDOWS)
    pool_s = _pool(ext_s.reshape(bs * (POOL_BUF + 1), D_POOL), 0, pool_w_b, pool_scale[0], tc=POOL_BUF + 1,
                   tiles_per_seq=1, full_count=True)
    pool_s = pool_s.reshape(bs, POOL_BUF + 1, D_POOL)[:, -1]
    new_pool_sample = ext_s[:, 1:][None]

    rw = _rwkv_weights(w0[0], w_decay_up[0], a0[0], w_aaa_up[0], w_gate_up[0], k_k[0], k_a[0], r_k[0],
                       ln_x_w[0], ln_x_b[0])
    rwkv_p, wkv_p = _rwkv_seq(hx_p, rw, batch=bp, seq=seq)
    rwkv_s, wkv_s = _rwkv_tok(hx_s, state_wkv[0], rw, tb=8)

    w_out_b = w_out[0].astype(BF16)
    TM_OUT = min(256, seq)
    x1_p, u2_p, idx_p, topw_p, chosen_p = _out_proj(pool_p, rwkv_p, xp, pr(g1), pr(sc2), pr(sh2), w_out_b,
                                                    norm_post_mix[0], norm_pre_ffn[0], w_router[0],
                                                    router_bias[0], tm=TM_OUT, tiles_per_seq=seq // TM_OUT)
    x1_s, u2_s, idx_s, topw_s, chosen_s = _out_proj(pool_s, rwkv_s, xs, sr(g1), sr(sc2), sr(sh2), w_out_b,
                                                    norm_post_mix[0], norm_pre_ffn[0], w_router[0],
                                                    router_bias[0], tm=TS, tiles_per_seq=None)

    n_all = np_rows + bs
    u2_all = jnp.concatenate([u2_p, u2_s], axis=0)
    idx_all = jnp.concatenate([idx_p[:, :TOP_K], idx_s[:, :TOP_K]], axis=0)
    topw_all = jnp.concatenate([topw_p[:, :TOP_K], topw_s[:, :TOP_K]], axis=0)
    chosen_all = jnp.concatenate([chosen_p, chosen_s], axis=0)
    slot_of_pair, slot_tok, slot_w, tile_expert, n_used = _dispatch(chosen_all, idx_all, topw_all)
    ys = _experts(tile_expert, n_used, u2_all[slot_tok], slot_w[:, None], exp_gate[0], exp_up[0], exp_down[0])
    ys_pairs = ys[slot_of_pair.T.reshape(-1)].reshape(TOP_K, n_all, d)

    sg, su, sd = sh_gate[0].astype(BF16), sh_up[0].astype(BF16), sh_down[0].astype(BF16)
    y_p = _final(u2_p, ys_pairs, x1_p, pr(g2), sg, su, sd, norm_post_ffn[0], tm=TM_OUT,
                 tiles_per_seq=seq // TM_OUT, row_offset=0)
    y_s = _final(u2_s, ys_pairs, x1_s, sr(g2), sg, su, sd, norm_post_ffn[0], tm=TS, tiles_per_seq=None,
                 row_offset=np_rows)

    return (y_p.reshape(bp, seq, d), y_s.reshape(bs, 1, d), new_shift_prompt, new_pool_prompt, wkv_p[None],
            new_shift_sample, new_pool_sample, wkv_s[None])
```

```python
import functools

import jax
import jax.numpy as jnp
from jax import lax
from jax.experimental import pallas as pl
from jax.experimental.pallas import tpu as pltpu

F32 = jnp.float32
BF16 = jnp.bfloat16

D_MODEL = 2048
D_POOL = 512
POOL_WINDOWS = (2, 4, 8, 16)
POOL_CH = 128
POOL_BUF = 15
D_RWKV = 1536
HEAD_DIM = 64
RWKV_HEADS = 24
DECAY_LORA = 96
AAA_LORA = 96
GATE_LORA = 256
LORA_PAD = 128
GN_EPS = 64e-5
N_EXPERTS = 256
TOP_K = 8
N_GROUPS = 8
GROUP_SIZE = N_EXPERTS // N_GROUPS
TOPK_GROUPS = 4
D_EXPERT = 512
ROUTE_SCALE = 2.5
EPS = 1e-6
PAST_LEN = 16384

COL_R, COL_K, COL_V = 0, D_RWKV, 2 * D_RWKV
COL_POOL = 3 * D_RWKV
COL_LORA = COL_POOL + D_POOL
N_IN_PAD = COL_LORA + 2 * LORA_PAD + GATE_LORA
LANES = 128
VMEM_LIMIT = 56 * 1024 * 1024


def _dot(a, b):
    return jnp.dot(a, b, preferred_element_type=F32)


def _split2(x):
    hi = x.astype(BF16)
    lo = (x - hi.astype(F32)).astype(BF16)
    return hi, lo


def _split3(x):
    hi = x.astype(BF16)
    r = x - hi.astype(F32)
    mid = r.astype(BF16)
    lo = (r - mid.astype(F32)).astype(BF16)
    return hi, mid, lo


def _dot_x3(a, b):
    ah, al = _split2(a)
    bh, bl = _split2(b)
    return _dot(ah, bh) + (_dot(ah, bl) + _dot(al, bh))


def _dot_exact_rhs(a, b_bf16):
    h, m, l = _split3(a)
    return _dot(h, b_bf16) + (_dot(m, b_bf16) + _dot(l, b_bf16))


def _sigmoid(x):
    return 1.0 / (1.0 + jnp.exp(-x))


def _silu(x):
    return x * _sigmoid(x)


def _rms(x, g):
    return x * lax.rsqrt(jnp.mean(x * x, axis=-1, keepdims=True) + EPS) * g


def _ada_kernel(c_ref, w_ref, b_ref, o_ref):
    o_ref[...] = _dot_x3(_silu(c_ref[...]), w_ref[...]) + b_ref[...]


def _ada(c_all, w_ada, b_ada):
    m, d = c_all.shape
    n = w_ada.shape[1]
    tn = 512
    return pl.pallas_call(
        _ada_kernel,
        grid=(n // tn,),
        in_specs=[pl.BlockSpec((m, d), lambda j: (0, 0)),
                  pl.BlockSpec((d, tn), lambda j: (0, j)),
                  pl.BlockSpec((1, tn), lambda j: (0, j))],
        out_specs=pl.BlockSpec((m, tn), lambda j: (0, j)),
        out_shape=jax.ShapeDtypeStruct((m, n), F32),
        compiler_params=pltpu.CompilerParams(dimension_semantics=("arbitrary",),
                                             vmem_limit_bytes=VMEM_LIMIT),
    )(c_all, w_ada, b_ada.reshape(1, n))


def _in_proj_kernel(*refs, tiles_per_seq, explicit_prev, tm):
    if explicit_prev:
        x_ref, g_ref, sc_ref, sh_ref, prev_ref, w_ref, mu_ref, hx_ref, u_ref, ub_scr = refs
    else:
        x_ref, g_ref, sc_ref, sh_ref, w_ref, mu_ref, hx_ref, u_ref, ub_scr, carry_scr = refs
    i = pl.program_id(0)
    j = pl.program_id(1)

    @pl.when(j == 0)
    def _():
        u = _rms(x_ref[...], g_ref[...])
        u = u * (1.0 + sc_ref[0]) + sh_ref[0]
        ub_scr[...] = u.astype(BF16)
        if explicit_prev:
            u_ref[...] = u
        else:
            u_ref[...] = u[tm - 8:, :]

    w = w_ref[...]
    h = _dot(ub_scr[...], w)
    if explicit_prev:
        hp = _dot(prev_ref[...].astype(BF16), w)
    else:
        @pl.when(i == 0)
        def _():
            carry_scr[j] = jnp.zeros(carry_scr.shape[1:], F32)

        first = jnp.where(i % tiles_per_seq == 0, 0.0, carry_scr[j, 0:1, :])
        row0 = lax.broadcasted_iota(jnp.int32, h.shape, 0) == 0
        hp = jnp.where(row0, first, pltpu.roll(h, 1, axis=0))
        carry_scr[j, 0:1, :] = h[tm - 1:tm, :]
    hx_ref[...] = h + (hp - h) * mu_ref[...]


def _in_proj(x, gamma, sc, sh, w_bf16, mu_pad, *, tm, tiles_per_seq, prev=None):
    m, d = x.shape
    n = w_bf16.shape[1]
    tn = 512
    nt = n // tn
    explicit_prev = prev is not None
    if explicit_prev:
        mod_spec = pl.BlockSpec((1, tm, d), lambda i, j: (i, 0, 0))
    else:
        mod_spec = pl.BlockSpec((1, 1, d), lambda i, j: (i // tiles_per_seq, 0, 0))
    in_specs = [pl.BlockSpec((tm, d), lambda i, j: (i, 0)),
                pl.BlockSpec((1, d), lambda i, j: (0, 0)),
                mod_spec, mod_spec]
    args = [x, gamma.reshape(1, d), sc, sh]
    if explicit_prev:
        in_specs.append(pl.BlockSpec((tm, d), lambda i, j: (i, 0)))
        args.append(prev)
    in_specs += [pl.BlockSpec((d, tn), lambda i, j: (0, j)),
                 pl.BlockSpec((1, tn), lambda i, j: (0, j))]
    args += [w_bf16, mu_pad.reshape(1, n)]
    scratch = [pltpu.VMEM((tm, d), BF16)]
    if explicit_prev:
        u_shape, u_spec = (m, d), pl.BlockSpec((tm, d), lambda i, j: (i, 0))
    else:
        u_shape, u_spec = (m // tm * 8, d), pl.BlockSpec((8, d), lambda i, j: (i, 0))
        scratch.append(pltpu.VMEM((nt, 8, tn), F32))
    return pl.pallas_call(
        functools.partial(_in_proj_kernel, tiles_per_seq=tiles_per_seq, explicit_prev=explicit_prev, tm=tm),
        grid=(m // tm, nt),
        in_specs=in_specs,
        out_specs=[pl.BlockSpec((tm, tn), lambda i, j: (i, j)), u_spec],
        out_shape=[jax.ShapeDtypeStruct((m, n), F32), jax.ShapeDtypeStruct(u_shape, F32)],
        scratch_shapes=scratch,
        compiler_params=pltpu.CompilerParams(dimension_semantics=("arbitrary", "arbitrary"),
                                             vmem_limit_bytes=VMEM_LIMIT),
    )(*args)


POOL_HALO = 16


def _pool_kernel(p_ref, pw_ref, ps_ref, o_ref, ext_scr, *, tiles_per_seq, full_count, tc):
    i = pl.program_id(0)

    @pl.when(i % tiles_per_seq == 0)
    def _():
        ext_scr[0:POOL_HALO, :] = jnp.zeros((POOL_HALO, D_POOL), F32)

    p = p_ref[...]
    ext_scr[POOL_HALO:POOL_HALO + tc, :] = p
    pos = (i % tiles_per_seq) * tc + lax.broadcasted_iota(jnp.int32, (tc, 1), 0)
    outs = []
    for gi, win in enumerate(POOL_WINDOWS):
        lo = gi * POOL_CH
        pg = p[:, lo:lo + POOL_CH]
        acc = pg
        for s in range(1, win):
            acc = acc + ext_scr[POOL_HALO - s:POOL_HALO - s + tc, lo:lo + POOL_CH]
        if full_count:
            dgi = acc / float(win) - pg
        else:
            cnt = jnp.minimum(pos + 1, win).astype(F32)
            dgi = acc / cnt - pg
        outs.append(_dot(dgi.astype(BF16), pw_ref[gi]))
    y = jnp.concatenate(outs, axis=-1) * ps_ref[...]
    o_ref[...] = y.astype(BF16)
    ext_scr[0:POOL_HALO, :] = ext_scr[tc:tc + POOL_HALO, :]


def _pool(src, col_block, pool_w_bf16, pool_scale, *, tc, tiles_per_seq, full_count):
    m = src.shape[0]
    return pl.pallas_call(
        functools.partial(_pool_kernel, tiles_per_seq=tiles_per_seq, full_count=full_count, tc=tc),
        grid=(m // tc,),
        in_specs=[pl.BlockSpec((tc, D_POOL), lambda i: (i, col_block)),
                  pl.BlockSpec((len(POOL_WINDOWS), POOL_CH, POOL_CH), lambda i: (0, 0, 0)),
                  pl.BlockSpec((1, D_POOL), lambda i: (0, 0))],
        out_specs=pl.BlockSpec((tc, D_POOL), lambda i: (i, 0)),
        out_shape=jax.ShapeDtypeStruct((m, D_POOL), BF16),
        scratch_shapes=[pltpu.VMEM((POOL_HALO + tc, D_POOL), F32)],
        compiler_params=pltpu.CompilerParams(dimension_semantics=("arbitrary",),
                                             vmem_limit_bytes=VMEM_LIMIT),
    )(src, pool_w_bf16, pool_scale.reshape(1, D_POOL))


def _softplus(z):
    return jnp.maximum(z, 0.0) + jnp.log1p(jnp.exp(-jnp.abs(z)))


def _rwkv_prep(r, k, v, lora, prm, seg, segt):
    (w0, wd, a0, wa, wg, k_k, k_a, r_k) = prm
    wl = lora[:, 0:LORA_PAD]
    al = lora[:, LORA_PAD:2 * LORA_PAD]
    gl = lora[:, 2 * LORA_PAD:]
    logw = -_softplus(-(w0 + _dot_x3(jnp.tanh(wl), wd))) - 0.5
    log_decay = -jnp.exp(logw)
    decay = jnp.exp(log_decay)
    a = _sigmoid(a0 + _dot_x3(al, wa))
    gate = _dot(_sigmoid(gl).astype(BF16), wg.astype(BF16))
    kk = k * k_k
    ss = _dot_exact_rhs(kk * kk, seg)
    inv = 1.0 / jnp.maximum(jnp.sqrt(ss), 1e-12)
    kk = kk * _dot_exact_rhs(inv, segt)
    k_mod = k * (1.0 + (a - 1.0) * k_a)
    bonus = _dot_exact_rhs(_dot_exact_rhs(r * k_mod * r_k, seg), segt) * v
    return decay, log_decay, k_mod, -kk, kk * a, gate, bonus


def _group_norm_head(y):
    mu = jnp.mean(y, axis=-1, keepdims=True)
    yc = y - mu
    var = jnp.mean(yc * yc, axis=-1, keepdims=True)
    return yc * lax.rsqrt(var + GN_EPS)


def _group_norm_rows(y, seg, segt):
    inv_n = 1.0 / HEAD_DIM
    mu = _dot_exact_rhs(_dot_exact_rhs(y, seg) * inv_n, segt)
    yc = y - mu
    var = _dot_exact_rhs(yc * yc, seg) * inv_n
    return yc * _dot_exact_rhs(lax.rsqrt(var + GN_EPS), segt)


WKV_CHUNK = HEAD_DIM
PAIR_LANES = 2 * HEAD_DIM
N_PAIRS = RWKV_HEADS // 2
_NT = (((1,), (1,)), ((), ()))
_TN = (((0,), (0,)), ((), ()))


def _dg_x3(a, b, dims):
    ah, al = _split2(a)
    bh, bl = _split2(b)
    dg = lambda x, y: lax.dot_general(x, y, dims, preferred_element_type=F32)
    return dg(ah, bh) + (dg(ah, bl) + dg(al, bh))


def _wkv_masks():
    n = PAIR_LANES
    ri = lax.broadcasted_iota(jnp.int32, (n, n), 0)
    ci = lax.broadcasted_iota(jnp.int32, (n, n), 1)
    same_head = (ri // HEAD_DIM) == (ci // HEAD_DIM)
    same16 = (ri // 16) == (ci // 16)
    same32 = (ri // 32) == (ci // 32)
    return dict(same_head=same_head, strict=same_head & (ci < ri), incl=same_head & (ci <= ri), eye=ri == ci,
                same16=same16, only32=same32 & jnp.logical_not(same16), not32=jnp.logical_not(same32),
                head0=lax.broadcasted_iota(jnp.int32, (1, n), 1) < HEAD_DIM)


def _unit_lower_inverse(lab, mk):
    d = jnp.where(mk["same16"], lab, 0.0)
    x = jnp.where(mk["eye"], 1.0, 0.0) + d
    for _ in range(3):
        d = _dot_x3(d, d)
        x = x + _dot_x3(x, d)
    for level in ("only32", "not32"):
        e = jnp.where(mk[level], lab, 0.0)
        x = x + _dot_x3(x, _dot_x3(e, x))
    return x


def _wkv_chunk_pair(at, rt, bt, kt, bh, kh, v, w_last, s_bd, mk):
    c = WKV_CHUNK
    stack = lambda x: jnp.concatenate([jnp.where(mk["head0"], x, 0.0), jnp.where(mk["head0"], 0.0, x)], axis=0)
    dup = lambda x: jnp.concatenate([x, x], axis=0)
    a_s, r_s, v_s = stack(at), stack(rt), stack(v)
    b2, k2, bh2, kh2 = dup(bt), dup(kt), dup(bh), dup(kh)
    lab = jnp.where(mk["strict"], _dg_x3(a_s, b2, _NT), 0.0)
    lak = jnp.where(mk["strict"], _dg_x3(a_s, k2, _NT), 0.0)
    lrb = jnp.where(mk["incl"], _dg_x3(r_s, b2, _NT), 0.0)
    lrk = jnp.where(mk["incl"], _dg_x3(r_s, k2, _NT), 0.0)
    t_inv = _unit_lower_inverse(lab, mk)
    ta_s = _dot_x3(t_inv, a_s)
    u_s = _dot_x3(t_inv, _dot_x3(lak, v_s))
    m_bd = jnp.where(mk["same_head"], _dg_x3(ta_s, bh2, _TN), 0.0) + jnp.where(mk["eye"], w_last, 0.0)
    g_bd = jnp.where(mk["same_head"], _dg_x3(u_s, bh2, _TN) + _dg_x3(v_s, kh2, _TN), 0.0)
    q_s = _dot_x3(lrb, ta_s)
    y0_s = _dot_x3(lrb, u_s) + _dot_x3(lrk, v_s)
    q = rt + (q_s[:c] + q_s[c:])
    y = _dg_x3(q, s_bd, _NT) + (y0_s[:c] + y0_s[c:])
    return y, _dot_x3(s_bd, m_bd) + g_bd


def _rwkv_seq_kernel(r_ref, k_ref, v_ref, lora_ref, w0_ref, wd_ref, a0_ref, wa_ref, wg_ref, kk_ref, ka_ref,
                     rk_ref, lnw_ref, lnb_ref, seg_ref, segt_ref, o_ref, sout_ref, s_scr, y_scr):
    c = pl.program_id(1)

    @pl.when(c == 0)
    def _():
        s_scr[...] = jnp.zeros(s_scr.shape, F32)

    r = r_ref[...]
    v = v_ref[...]
    seg = seg_ref[...]
    segt = segt_ref[...]
    prm = (w0_ref[...], wd_ref[...], a0_ref[...], wa_ref[...], wg_ref[...], kk_ref[...], ka_ref[...], rk_ref[...])
    decay, log_decay, k_mod, neg_kk, b, gate, bonus = _rwkv_prep(r, k_ref[...], v, lora_ref[...], prm, seg, segt)

    n = WKV_CHUNK
    tri = (lax.broadcasted_iota(jnp.int32, (n, n), 1) <= lax.broadcasted_iota(jnp.int32, (n, n), 0)).astype(BF16)
    ld_h, ld_m, ld_l = _split3(log_decay)
    cw = _dot(tri, ld_h) + (_dot(tri, ld_m) + _dot(tri, ld_l))
    cw_prev = cw - log_decay
    cw_last = cw[n - 1:n, :]
    e_prev = jnp.exp(cw_prev)
    e_inv = jnp.exp(-cw)
    e_rem = jnp.exp(cw_last - cw)
    at = neg_kk * e_prev
    rt = r * (e_prev * decay)
    bt = b * e_inv
    kt = k_mod * e_inv
    bh = b * e_rem
    kh = k_mod * e_rem
    w_last = jnp.exp(cw_last)

    mk = _wkv_masks()
    for p in range(N_PAIRS):
        sl = slice(p * PAIR_LANES, (p + 1) * PAIR_LANES)
        y, s_new = _wkv_chunk_pair(at[:, sl], rt[:, sl], bt[:, sl], kt[:, sl], bh[:, sl], kh[:, sl], v[:, sl],
                                   w_last[:, sl], s_scr[p], mk)
        s_scr[p] = s_new
        y_scr[:, sl] = y

    out = (_group_norm_rows(y_scr[...], seg, segt) * lnw_ref[...] + lnb_ref[...] + bonus) * gate
    o_ref[...] = out.astype(BF16)

    @pl.when(c == pl.num_programs(1) - 1)
    def _():
        for p in range(N_PAIRS):
            sout_ref[0, 2 * p] = s_scr[p, 0:HEAD_DIM, 0:HEAD_DIM]
            sout_ref[0, 2 * p + 1] = s_scr[p, HEAD_DIM:, HEAD_DIM:]


def _rwkv_weights(w0, w_decay_up, a0, w_aaa_up, w_gate_up, k_k, k_a, r_k, ln_x_w, ln_x_b):
    row = lambda z: z.reshape(1, D_RWKV)
    pad = lambda z: jnp.pad(z, ((0, LORA_PAD - z.shape[0]), (0, 0)))
    head_of_col = jnp.arange(D_RWKV, dtype=jnp.int32) // HEAD_DIM
    seg = (head_of_col[:, None] == jnp.arange(LANES, dtype=jnp.int32)[None, :]).astype(BF16)
    return (row(w0), pad(w_decay_up), row(a0), pad(w_aaa_up), w_gate_up, row(k_k), row(k_a), row(r_k),
            row(ln_x_w), row(ln_x_b), seg, seg.T)


def _const_spec(shape, grid_rank):
    zeros = (0,) * len(shape)
    if grid_rank == 1:
        return pl.BlockSpec(shape, lambda i: zeros)
    return pl.BlockSpec(shape, lambda i, j: zeros)


def _rwkv_seq(hx, weights, *, batch, seq):
    tc = WKV_CHUNK
    nchunk = seq // tc
    rowmap = lambda cb: (lambda b, c: (b * nchunk + c, cb))
    in_specs = [pl.BlockSpec((tc, D_RWKV), rowmap(0)),
                pl.BlockSpec((tc, D_RWKV), rowmap(1)),
                pl.BlockSpec((tc, D_RWKV), rowmap(2)),
                pl.BlockSpec((tc, 2 * LORA_PAD + GATE_LORA), rowmap(COL_LORA // (2 * LORA_PAD + GATE_LORA)))]
    in_specs += [_const_spec(w.shape, 2) for w in weights]
    return pl.pallas_call(
        _rwkv_seq_kernel,
        grid=(batch, nchunk),
        in_specs=in_specs,
        out_specs=[pl.BlockSpec((tc, D_RWKV), lambda b, c: (b * nchunk + c, 0)),
                   pl.BlockSpec((1, RWKV_HEADS, HEAD_DIM, HEAD_DIM), lambda b, c: (b, 0, 0, 0))],
        out_shape=[jax.ShapeDtypeStruct((batch * seq, D_RWKV), BF16),
                   jax.ShapeDtypeStruct((batch, RWKV_HEADS, HEAD_DIM, HEAD_DIM), F32)],
        scratch_shapes=[pltpu.VMEM((N_PAIRS, PAIR_LANES, PAIR_LANES), F32), pltpu.VMEM((tc, D_RWKV), F32)],
        compiler_params=pltpu.CompilerParams(dimension_semantics=("arbitrary", "arbitrary"),
                                             vmem_limit_bytes=VMEM_LIMIT),
    )(hx, hx, hx, hx, *weights)


def _rwkv_tok_prep_kernel(r_ref, k_ref, v_ref, lora_ref, w0_ref, wd_ref, a0_ref, wa_ref, wg_ref, kk_ref, ka_ref,
                          rk_ref, seg_ref, segt_ref, w_o, k_o, a_o, b_o, g_o, bonus_o):
    prm = (w0_ref[...], wd_ref[...], a0_ref[...], wa_ref[...], wg_ref[...], kk_ref[...], ka_ref[...], rk_ref[...])
    decay, _, k_mod, neg_kk, b, gate, bonus = _rwkv_prep(r_ref[...], k_ref[...], v_ref[...], lora_ref[...], prm,
                                                          seg_ref[...], segt_ref[...])
    w_o[...] = decay
    k_o[...] = k_mod
    a_o[...] = neg_kk
    b_o[...] = b
    g_o[...] = gate
    bonus_o[...] = bonus


def _rwkv_tok_step_kernel(r_ref, w_ref, k_ref, v_ref, a_ref, b_ref, g_ref, bonus_ref, lnw_ref, lnb_ref, s0_ref,
                          o_ref, s_ref, *, tb):
    eye = (lax.broadcasted_iota(jnp.int32, (HEAD_DIM, HEAD_DIM), 0)
           == lax.broadcasted_iota(jnp.int32, (HEAD_DIM, HEAD_DIM), 1))

    def body(bi, carry):
        s = s0_ref[bi]
        sa = jnp.sum(s * a_ref[bi], axis=2, keepdims=True)
        v_col = jnp.sum(jnp.where(eye, v_ref[bi], 0.0), axis=2, keepdims=True)
        s = s * w_ref[bi] + sa * b_ref[bi] + v_col * k_ref[bi]
        s_ref[bi] = s
        y_col = jnp.sum(s * r_ref[bi], axis=2, keepdims=True)
        y = jnp.sum(jnp.where(eye, y_col, 0.0), axis=1, keepdims=True)
        o_ref[bi] = (_group_norm_head(y) * lnw_ref[0] + lnb_ref[0] + bonus_ref[bi]) * g_ref[bi]
        return carry

    lax.fori_loop(0, tb, body, 0)


def _rwkv_tok(hx, state, weights, *, tb):
    nb = hx.shape[0]
    (w0, wd, a0, wa, wg, k_k, k_a, r_k, ln_w, ln_b, seg, segt) = weights
    prep_w = (w0, wd, a0, wa, wg, k_k, k_a, r_k, seg, segt)
    colmap = lambda cb: (lambda i: (0, cb))
    in_specs = [pl.BlockSpec((nb, D_RWKV), colmap(0)),
                pl.BlockSpec((nb, D_RWKV), colmap(1)),
                pl.BlockSpec((nb, D_RWKV), colmap(2)),
                pl.BlockSpec((nb, 2 * LORA_PAD + GATE_LORA), colmap(COL_LORA // (2 * LORA_PAD + GATE_LORA)))]
    in_specs += [_const_spec(w.shape, 1) for w in prep_w]
    rows = jax.ShapeDtypeStruct((nb, D_RWKV), F32)
    prepped = pl.pallas_call(
        _rwkv_tok_prep_kernel,
        grid=(1,),
        in_specs=in_specs,
        out_specs=[pl.BlockSpec((nb, D_RWKV), lambda i: (0, 0))] * 6,
        out_shape=[rows] * 6,
        compiler_params=pltpu.CompilerParams(dimension_semantics=("arbitrary",),
                                             vmem_limit_bytes=VMEM_LIMIT),
    )(hx, hx, hx, hx, *prep_w)
    decay, k_mod, neg_kk, b, gate, bonus = prepped
    heads = lambda z: z.reshape(nb, RWKV_HEADS, 1, HEAD_DIM)
    vecs = [heads(z) for z in (hx[:, COL_R:COL_R + D_RWKV], decay, k_mod, hx[:, COL_V:COL_V + D_RWKV], neg_kk, b,
                               gate, bonus)]
    vec_spec = pl.BlockSpec((tb, RWKV_HEADS, 1, HEAD_DIM), lambda i: (i, 0, 0, 0))
    ln_spec = pl.BlockSpec((1, RWKV_HEADS, 1, HEAD_DIM), lambda i: (0, 0, 0, 0))
    state_spec = pl.BlockSpec((tb, RWKV_HEADS, HEAD_DIM, HEAD_DIM), lambda i: (i, 0, 0, 0))
    out, new_state = pl.pallas_call(
        functools.partial(_rwkv_tok_step_kernel, tb=tb),
        grid=(nb // tb,),
        in_specs=[vec_spec] * 8 + [ln_spec, ln_spec, state_spec],
        out_specs=[vec_spec, state_spec],
        out_shape=[jax.ShapeDtypeStruct((nb, RWKV_HEADS, 1, HEAD_DIM), F32),
                   jax.ShapeDtypeStruct(state.shape, F32)],
        compiler_params=pltpu.CompilerParams(dimension_semantics=("arbitrary",),
                                             vmem_limit_bytes=VMEM_LIMIT),
    )(*vecs, ln_w.reshape(1, RWKV_HEADS, 1, HEAD_DIM), ln_b.reshape(1, RWKV_HEADS, 1, HEAD_DIM), state)
    return out.reshape(nb, D_RWKV).astype(BF16), new_state


def _route(logits, bias):
    scores = _sigmoid(logits)
    sel = scores + bias
    rows = logits.shape[0]
    lane_i = lax.broadcasted_iota(jnp.int32, (rows, N_EXPERTS), 1)
    grp = lane_i // GROUP_SIZE
    lane = lane_i.astype(F32)
    neg_inf = -jnp.inf

    def take_max(x):
        m = jnp.max(x, axis=1, keepdims=True)
        idx = jnp.min(jnp.where(x == m, lane, float(N_EXPERTS)), axis=1, keepdims=True)
        return m, lane == idx

    gscore = []
    for g in range(N_GROUPS):
        x = jnp.where(grp == g, sel, neg_inf)
        m1, hit = take_max(x)
        m2 = jnp.max(jnp.where(hit, neg_inf, x), axis=1, keepdims=True)
        gscore.append(m1 + m2)
    emask = jnp.zeros((rows, N_EXPERTS), jnp.bool_)
    for g in range(N_GROUPS):
        ahead = jnp.zeros((rows, 1), jnp.int32)
        for g2 in range(N_GROUPS):
            if g2 == g:
                continue
            beats = (gscore[g2] >= gscore[g]) if g2 < g else (gscore[g2] > gscore[g])
            ahead = ahead + beats.astype(jnp.int32)
        emask = emask | ((grp == g) & (ahead < TOPK_GROUPS))
    x = jnp.where(emask, sel, neg_inf)
    chosen = jnp.zeros((rows, N_EXPERTS), jnp.bool_)
    slot_lane = lax.broadcasted_iota(jnp.int32, (rows, LANES), 1)
    top_idx = jnp.zeros((rows, LANES), F32)
    top_w = jnp.zeros((rows, LANES), F32)
    for kth in range(TOP_K):
        m = jnp.max(x, axis=1, keepdims=True)
        idx = jnp.min(jnp.where(x == m, lane, float(N_EXPERTS)), axis=1, keepdims=True)
        hit = lane == idx
        chosen = chosen | hit
        x = jnp.where(hit, neg_inf, x)
        top_idx = jnp.where(slot_lane == kth, idx, top_idx)
        top_w = jnp.where(slot_lane == kth, jnp.sum(jnp.where(hit, scores, 0.0), axis=1, keepdims=True), top_w)
    top_w = top_w / jnp.sum(top_w, axis=1, keepdims=True) * ROUTE_SCALE
    return top_idx.astype(jnp.int32), top_w, chosen.astype(F32)


def _out_proj_kernel(pool_ref, rwkv_ref, x_ref, g1_ref, sc_ref, sh_ref, wp_ref, wr_ref, npost_ref, npre_ref,
                     wrt_ref, rb_ref, x1_ref, u2_ref, idx_ref, topw_ref, chosen_ref):
    m = _dot(pool_ref[...], wp_ref[...]) + _dot(rwkv_ref[...], wr_ref[...])
    x1 = x_ref[...] + g1_ref[0] * _rms(m, npost_ref[...])
    x1_ref[...] = x1
    u2 = _rms(x1, npre_ref[...]) * (1.0 + sc_ref[0]) + sh_ref[0]
    u2_ref[...] = u2.astype(BF16)
    top_idx, top_w, chosen = _route(_dot_x3(u2, wrt_ref[...]), rb_ref[...])
    idx_ref[...] = top_idx
    topw_ref[...] = top_w
    chosen_ref[...] = chosen


def _out_proj(pool_out, rwkv_out, x, g1, sc2, sh2, w_out_bf16, n_post, n_pre, w_router, router_bias, *, tm,
              tiles_per_seq):
    m, d = x.shape
    if tiles_per_seq is None:
        mod_spec = pl.BlockSpec((1, tm, d), lambda i: (i, 0, 0))
    else:
        mod_spec = pl.BlockSpec((1, 1, d), lambda i: (i // tiles_per_seq, 0, 0))
    row = lambda width: pl.BlockSpec((tm, width), lambda i: (i, 0))
    return pl.pallas_call(
        _out_proj_kernel,
        grid=(m // tm,),
        in_specs=[row(D_POOL), row(D_RWKV), row(d), mod_spec, mod_spec, mod_spec,
                  _const_spec((D_POOL, d), 1), _const_spec((D_RWKV, d), 1),
                  _const_spec((1, d), 1), _const_spec((1, d), 1),
                  _const_spec((d, N_EXPERTS), 1), _const_spec((1, N_EXPERTS), 1)],
        out_specs=[row(d), row(d), row(LANES), row(LANES), row(N_EXPERTS)],
        out_shape=[jax.ShapeDtypeStruct((m, d), F32), jax.ShapeDtypeStruct((m, d), BF16),
                   jax.ShapeDtypeStruct((m, LANES), jnp.int32), jax.ShapeDtypeStruct((m, LANES), F32),
                   jax.ShapeDtypeStruct((m, N_EXPERTS), F32)],
        compiler_params=pltpu.CompilerParams(dimension_semantics=("arbitrary",),
                                             vmem_limit_bytes=VMEM_LIMIT),
    )(pool_out, rwkv_out, x, g1, sc2, sh2, w_out_bf16[:D_POOL], w_out_bf16[D_POOL:],
      n_post.reshape(1, d), n_pre.reshape(1, d), w_router, router_bias.reshape(1, N_EXPERTS))


MOE_TM = 128


def _experts_kernel(te_ref, nused_ref, xs_ref, ws_ref, wg_ref, wu_ref, wd_ref, ys_ref, g_scr, u_scr, d_scr):
    t = pl.program_id(0)
    fresh = jnp.logical_or(t == 0, te_ref[t] != te_ref[jnp.maximum(t - 1, 0)])

    @pl.when(fresh)
    def _():
        g_scr[...] = wg_ref[0].astype(BF16)
        u_scr[...] = wu_ref[0].astype(BF16)
        d_scr[...] = wd_ref[0].astype(BF16)

    @pl.when(t < nused_ref[0])
    def _():
        x = xs_ref[...]
        h = _silu(_dot(x, g_scr[...])) * _dot(x, u_scr[...])
        ys_ref[...] = _dot(h.astype(BF16), d_scr[...]) * ws_ref[...]

    @pl.when(t >= nused_ref[0])
    def _():
        ys_ref[...] = jnp.zeros(ys_ref.shape, F32)


def _experts(tile_expert, n_used, xs, ws, exp_gate, exp_up, exp_down):
    n_slots, d = xs.shape
    n_tiles = n_slots // MOE_TM
    grid_spec = pltpu.PrefetchScalarGridSpec(
        num_scalar_prefetch=2,
        grid=(n_tiles,),
        in_specs=[pl.BlockSpec((MOE_TM, d), lambda t, te, nu: (t, 0)),
                  pl.BlockSpec((MOE_TM, 1), lambda t, te, nu: (t, 0)),
                  pl.BlockSpec((1, d, D_EXPERT), lambda t, te, nu: (te[t], 0, 0)),
                  pl.BlockSpec((1, d, D_EXPERT), lambda t, te, nu: (te[t], 0, 0)),
                  pl.BlockSpec((1, D_EXPERT, d), lambda t, te, nu: (te[t], 0, 0))],
        out_specs=pl.BlockSpec((MOE_TM, d), lambda t, te, nu: (t, 0)),
        scratch_shapes=[pltpu.VMEM((d, D_EXPERT), BF16), pltpu.VMEM((d, D_EXPERT), BF16),
                        pltpu.VMEM((D_EXPERT, d), BF16)],
    )
    return pl.pallas_call(
        _experts_kernel,
        grid_spec=grid_spec,
        out_shape=jax.ShapeDtypeStruct((n_slots, d), F32),
        compiler_params=pltpu.CompilerParams(dimension_semantics=("arbitrary",),
                                             vmem_limit_bytes=VMEM_LIMIT),
    )(tile_expert, n_used, xs, ws, exp_gate, exp_up, exp_down)


ASSIGN_BITS = 17


def _dispatch(chosen, top_idx, top_w):
    n = chosen.shape[0]
    n_assign = n * TOP_K
    low_mask = (1 << ASSIGN_BITS) - 1
    assert n_assign + MOE_TM <= low_mask and N_EXPERTS << ASSIGN_BITS < 2 ** 31
    n_tiles = -(-n_assign // MOE_TM) + N_EXPERTS
    n_slots = n_tiles * MOE_TM
    sel = chosen.astype(jnp.int32)
    counts = jnp.sum(sel, axis=0)
    rank = jnp.cumsum(sel, axis=0) - sel
    tiles_e = (counts + MOE_TM - 1) // MOE_TM
    tile_end = jnp.cumsum(tiles_e)
    slot = (tile_end - tiles_e)[None, :] * MOE_TM + rank
    expert_ids = jnp.arange(N_EXPERTS, dtype=jnp.int32)
    slot_of_pair = jnp.stack([jnp.sum(jnp.where(top_idx[:, k:k + 1] == expert_ids[None, :], slot, 0), axis=1)
                              for k in range(TOP_K)])
    shift = 1 << ASSIGN_BITS
    int_max = jnp.iinfo(jnp.int32).max
    real_key = top_idx.reshape(-1) * shift + jnp.arange(n_assign, dtype=jnp.int32)
    pad_j = jnp.arange(MOE_TM - 1, dtype=jnp.int32)[None, :]
    pad_key = jnp.where(pad_j < (tiles_e * MOE_TM - counts)[:, None],
                        expert_ids[:, None] * shift + n_assign + pad_j, int_max)
    n_rest = n_slots - n_assign
    keys = jnp.concatenate([real_key, pad_key.reshape(-1),
                            jnp.full((n_rest - pad_key.size,), int_max, jnp.int32)])
    weights = jnp.concatenate([top_w.reshape(-1), jnp.zeros((n_rest,), F32)])
    sorted_key, sorted_w = lax.sort((keys, weights), num_keys=1)
    pair = sorted_key & low_mask
    is_real = pair < n_assign
    slot_tok = jnp.where(is_real, pair // TOP_K, 0)
    slot_w = jnp.where(is_real, sorted_w, 0.0)
    n_used = tile_end[-1]
    tile_ids = jnp.minimum(jnp.arange(n_tiles, dtype=jnp.int32), n_used - 1)
    tile_expert = jnp.minimum(jnp.searchsorted(tile_end, tile_ids, side="right"), N_EXPERTS - 1).astype(jnp.int32)
    return slot_of_pair, slot_tok, slot_w, tile_expert, n_used.reshape(1).astype(jnp.int32)


def _final_kernel(idx_ref, idx_next_ref, u2_ref, ys_hbm, x1_ref, g2_ref, sg_ref, su_ref, sd_ref, npost_ref, o_ref,
                  rows_buf, sems, *, tm):
    i = pl.program_id(0)
    n_rows = TOP_K * tm

    def start_gather(table_ref, buf_slot):
        def body(j, carry):
            pltpu.make_async_copy(ys_hbm.at[pl.ds(table_ref[0, 0, j], 1)], rows_buf.at[buf_slot, pl.ds(j, 1)],
                                  sems.at[buf_slot]).start()
            return carry
        lax.fori_loop(0, n_rows, body, 0, unroll=8)

    @pl.when(i == 0)
    def _():
        start_gather(idx_ref, 0)

    @pl.when(i + 1 < pl.num_programs(0))
    def _():
        start_gather(idx_next_ref, (i + 1) % 2)

    u2 = u2_ref[...]
    h = _silu(_dot(u2, sg_ref[...])) * _dot(u2, su_ref[...])
    shared = _dot(h.astype(BF16), sd_ref[...])
    cur = i % 2
    pltpu.make_async_copy(ys_hbm.at[pl.ds(0, n_rows)], rows_buf.at[cur], sems.at[cur]).wait()
    routed = rows_buf[cur, 0:tm]
    for kth in range(1, TOP_K):
        routed = routed + rows_buf[cur, kth * tm:(kth + 1) * tm]
    o_ref[...] = x1_ref[...] + g2_ref[0] * _rms(routed + shared, npost_ref[...])


def _final(u2, ys, slot_tiles, x1, g2, sg, su, sd, n_post, *, tm, tiles_per_seq, tile_offset):
    m, d = x1.shape
    nt = m // tm
    if tiles_per_seq is None:
        mod_spec = pl.BlockSpec((1, tm, d), lambda i: (i, 0, 0))
    else:
        mod_spec = pl.BlockSpec((1, 1, d), lambda i: (i // tiles_per_seq, 0, 0))
    row = pl.BlockSpec((tm, d), lambda i: (i, 0))
    idx_block = (1, 1, TOP_K * tm)
    return pl.pallas_call(
        functools.partial(_final_kernel, tm=tm),
        grid=(nt,),
        in_specs=[pl.BlockSpec(idx_block, lambda i: (i + tile_offset, 0, 0), memory_space=pltpu.SMEM),
                  pl.BlockSpec(idx_block, lambda i: (jnp.minimum(i + 1, nt - 1) + tile_offset, 0, 0),
                               memory_space=pltpu.SMEM),
                  row, pl.BlockSpec(memory_space=pl.ANY), row, mod_spec,
                  _const_spec(sg.shape, 1), _const_spec(su.shape, 1), _const_spec(sd.shape, 1),
                  _const_spec((1, d), 1)],
        out_specs=row,
        out_shape=jax.ShapeDtypeStruct((m, d), F32),
        scratch_shapes=[pltpu.VMEM((2, TOP_K * tm, d), F32), pltpu.SemaphoreType.DMA((2,))],
        compiler_params=pltpu.CompilerParams(dimension_semantics=("arbitrary",),
                                             vmem_limit_bytes=VMEM_LIMIT),
    )(slot_tiles, slot_tiles, u2, ys, x1, g2, sg, su, sd, n_post.reshape(1, d))


def _pad_cols_in(w_in, mu_shift):
    d = w_in.shape[0]
    hp = w_in[:, :D_POOL]
    hr = w_in[:, D_POOL:]
    cuts = [3 * D_RWKV, 3 * D_RWKV + DECAY_LORA, 3 * D_RWKV + DECAY_LORA + AAA_LORA]
    rkv, wl, al, gl = jnp.split(hr, cuts, axis=1)
    zpad = lambda z: jnp.pad(z, ((0, 0), (0, LORA_PAD - z.shape[1])))
    w = jnp.concatenate([rkv, hp, zpad(wl), zpad(al), gl], axis=1).astype(BF16)
    mu_rkv, mu_wl, mu_al, mu_gl = jnp.split(mu_shift, cuts)
    zp1 = lambda z: jnp.pad(z, (0, LORA_PAD - z.shape[0]))
    mu = jnp.concatenate([mu_rkv, jnp.zeros((D_POOL,), F32), zp1(mu_wl), zp1(mu_al), mu_gl])
    assert w.shape == (d, N_IN_PAD) and mu.shape == (N_IN_PAD,)
    return w, mu


def kernel(x_prompt, x_sample, c_prompt, c_sample, state_shift, state_pool, state_wkv, w_ada, b_ada, norm_pre_mix, norm_post_mix, norm_pre_ffn, norm_post_ffn, w_in, mu_shift, pool_w, pool_scale, w0, w_decay_up, a0, w_aaa_up, w_gate_up, k_k, k_a, r_k, ln_x_w, ln_x_b, w_out, w_router, router_bias, exp_gate, exp_up, exp_down, sh_gate, sh_up, sh_down):
    bp, seq, d = x_prompt.shape
    bs = x_sample.shape[0]
    assert w_ada.shape[0] == 1 and x_sample.shape[1] == 1
    np_rows = bp * seq

    c_all = jnp.concatenate([c_prompt, c_sample], axis=0)
    pad_rows = (-c_all.shape[0]) % 16
    ada = _ada(jnp.pad(c_all, ((0, pad_rows), (0, 0))), w_ada[0], b_ada[0])
    sh1, sc1, g1, sh2, sc2, g2 = jnp.split(ada, 6, axis=-1)
    pr = lambda z: z[:bp].reshape(bp, 1, d)
    TS = min(128, bs)
    sr = lambda z: z[bp:bp + bs].reshape(bs // TS, TS, d)

    w_in_p, mu_p = _pad_cols_in(w_in[0], mu_shift[0])
    xp = x_prompt.reshape(np_rows, d)
    xs = x_sample.reshape(bs, d)

    TM_IN = min(1024, seq)
    hx_p, u_tail = _in_proj(xp, norm_pre_mix[0], pr(sc1), pr(sh1), w_in_p, mu_p, tm=TM_IN,
                            tiles_per_seq=seq // TM_IN)
    hx_s, u_s = _in_proj(xs, norm_pre_mix[0], sr(sc1), sr(sh1), w_in_p, mu_p, tm=TS, tiles_per_seq=1,
                         prev=state_shift[0])
    new_shift_prompt = u_tail.reshape(bp, seq // TM_IN, 8, d)[:, -1, -1][None]
    new_shift_sample = u_s[None]

    pool_w_b = pool_w[0].astype(BF16)
    TC_POOL = min(512, seq)
    pool_p = _pool(hx_p, COL_POOL // D_POOL, pool_w_b, pool_scale[0], tc=TC_POOL, tiles_per_seq=seq // TC_POOL,
                   full_count=False)
    hp_p = hx_p[:, COL_POOL:COL_POOL + D_POOL].reshape(bp, seq, D_POOL)
    new_pool_prompt = hp_p[:, seq - POOL_BUF:][None]
    hp_s = hx_s[:, COL_POOL:COL_POOL + D_POOL]
    ext_s = jnp.concatenate([state_pool[0], hp_s[:, None, :]], axis=1)
    assert PAST_LEN + 1 >= max(POOL_WINDOWS)
    pool_s = _pool(ext_s.reshape(bs * (POOL_BUF + 1), D_POOL), 0, pool_w_b, pool_scale[0], tc=POOL_BUF + 1,
                   tiles_per_seq=1, full_count=True)
    pool_s = pool_s.reshape(bs, POOL_BUF + 1, D_POOL)[:, -1]
    new_pool_sample = ext_s[:, 1:][None]

    rw = _rwkv_weights(w0[0], w_decay_up[0], a0[0], w_aaa_up[0], w_gate_up[0], k_k[0], k_a[0], r_k[0],
                       ln_x_w[0], ln_x_b[0])
    rwkv_p, wkv_p = _rwkv_seq(hx_p, rw, batch=bp, seq=seq)
    rwkv_s, wkv_s = _rwkv_tok(hx_s, state_wkv[0], rw, tb=8)

    w_out_b = w_out[0].astype(BF16)
    TM_OUT = min(256, seq)
    x1_p, u2_p, idx_p, topw_p, chosen_p = _out_proj(pool_p, rwkv_p, xp, pr(g1), pr(sc2), pr(sh2), w_out_b,
                                                    norm_post_mix[0], norm_pre_ffn[0], w_router[0],
                                                    router_bias[0], tm=TM_OUT, tiles_per_seq=seq // TM_OUT)
    x1_s, u2_s, idx_s, topw_s, chosen_s = _out_proj(pool_s, rwkv_s, xs, sr(g1), sr(sc2), sr(sh2), w_out_b,
                                                    norm_post_mix[0], norm_pre_ffn[0], w_router[0],
                                                    router_bias[0], tm=TS, tiles_per_seq=None)

    n_all = np_rows + bs
    u2_all = jnp.concatenate([u2_p, u2_s], axis=0)
    idx_all = jnp.concatenate([idx_p[:, :TOP_K], idx_s[:, :TOP_K]], axis=0)
    topw_all = jnp.concatenate([topw_p[:, :TOP_K], topw_s[:, :TOP_K]], axis=0)
    chosen_all = jnp.concatenate([chosen_p, chosen_s], axis=0)
    slot_of_pair, slot_tok, slot_w, tile_expert, n_used = _dispatch(chosen_all, idx_all, topw_all)
    ys = _experts(tile_expert, n_used, u2_all[slot_tok], slot_w[:, None], exp_gate[0], exp_up[0], exp_down[0])

    assert np_rows % TS == 0 and seq % TS == 0
    slot_tiles = slot_of_pair.reshape(TOP_K, n_all // TS, TS).transpose(1, 0, 2).reshape(n_all // TS, 1, TOP_K * TS)
    sg, su, sd = sh_gate[0].astype(BF16), sh_up[0].astype(BF16), sh_down[0].astype(BF16)
    y_p = _final(u2_p, ys, slot_tiles, x1_p, pr(g2), sg, su, sd, norm_post_ffn[0], tm=TS,
                 tiles_per_seq=seq // TS, tile_offset=0)
    y_s = _final(u2_s, ys, slot_tiles, x1_s, sr(g2), sg, su, sd, norm_post_ffn[0], tm=TS, tiles_per_seq=None,
                 tile_offset=np_rows // TS)

    return (y_p.reshape(bp, seq, d), y_s.reshape(bs, 1, d), new_shift_prompt, new_pool_prompt, wkv_p[None],
            new_shift_sample, new_pool_sample, wkv_s[None])
```

```python
import functools

import jax
import jax.numpy as jnp
from jax import lax
from jax.experimental import pallas as pl
from jax.experimental.pallas import tpu as pltpu

F32 = jnp.float32
BF16 = jnp.bfloat16

D_MODEL = 2048
D_POOL = 512
POOL_WINDOWS = (2, 4, 8, 16)
POOL_CH = 128
POOL_BUF = 15
D_RWKV = 1536
HEAD_DIM = 64
RWKV_HEADS = 24
DECAY_LORA = 96
AAA_LORA = 96
GATE_LORA = 256
LORA_PAD = 128
GN_EPS = 64e-5
N_EXPERTS = 256
TOP_K = 8
N_GROUPS = 8
GROUP_SIZE = N_EXPERTS // N_GROUPS
TOPK_GROUPS = 4
D_EXPERT = 512
ROUTE_SCALE = 2.5
EPS = 1e-6
PAST_LEN = 16384

COL_R, COL_K, COL_V = 0, D_RWKV, 2 * D_RWKV
COL_POOL = 3 * D_RWKV
COL_LORA = COL_POOL + D_POOL
N_IN_PAD = COL_LORA + 2 * LORA_PAD + GATE_LORA
LANES = 128
VMEM_LIMIT = 56 * 1024 * 1024


def _dot(a, b):
    return jnp.dot(a, b, preferred_element_type=F32)


def _split2(x):
    hi = x.astype(BF16)
    lo = (x - hi.astype(F32)).astype(BF16)
    return hi, lo


def _split3(x):
    hi = x.astype(BF16)
    r = x - hi.astype(F32)
    mid = r.astype(BF16)
    lo = (r - mid.astype(F32)).astype(BF16)
    return hi, mid, lo


def _dot_x3(a, b):
    ah, al = _split2(a)
    bh, bl = _split2(b)
    return _dot(ah, bh) + (_dot(ah, bl) + _dot(al, bh))


def _dot_exact_rhs(a, b_bf16):
    h, m, l = _split3(a)
    return _dot(h, b_bf16) + (_dot(m, b_bf16) + _dot(l, b_bf16))


def _sigmoid(x):
    return 1.0 / (1.0 + jnp.exp(-x))


def _silu(x):
    return x * _sigmoid(x)


def _rms(x, g):
    return x * lax.rsqrt(jnp.mean(x * x, axis=-1, keepdims=True) + EPS) * g


def _ada_kernel(c_ref, w_ref, b_ref, o_ref):
    o_ref[...] = _dot_x3(_silu(c_ref[...]), w_ref[...]) + b_ref[...]


def _ada(c_all, w_ada, b_ada):
    m, d = c_all.shape
    n = w_ada.shape[1]
    tn = 512
    return pl.pallas_call(
        _ada_kernel,
        grid=(n // tn,),
        in_specs=[pl.BlockSpec((m, d), lambda j: (0, 0)),
                  pl.BlockSpec((d, tn), lambda j: (0, j)),
                  pl.BlockSpec((1, tn), lambda j: (0, j))],
        out_specs=pl.BlockSpec((m, tn), lambda j: (0, j)),
        out_shape=jax.ShapeDtypeStruct((m, n), F32),
        compiler_params=pltpu.CompilerParams(dimension_semantics=("arbitrary",),
                                             vmem_limit_bytes=VMEM_LIMIT),
    )(c_all, w_ada, b_ada.reshape(1, n))


def _in_proj_kernel(*refs, tiles_per_seq, explicit_prev, tm):
    if explicit_prev:
        x_ref, g_ref, sc_ref, sh_ref, prev_ref, w_ref, mu_ref, hx_ref, u_ref, ub_scr = refs
    else:
        x_ref, g_ref, sc_ref, sh_ref, w_ref, mu_ref, hx_ref, u_ref, ub_scr, carry_scr = refs
    i = pl.program_id(0)
    j = pl.program_id(1)

    @pl.when(j == 0)
    def _():
        u = _rms(x_ref[...], g_ref[...])
        u = u * (1.0 + sc_ref[0]) + sh_ref[0]
        ub_scr[...] = u.astype(BF16)
        if explicit_prev:
            u_ref[...] = u
        else:
            u_ref[...] = u[tm - 8:, :]

    w = w_ref[...]
    h = _dot(ub_scr[...], w)
    if explicit_prev:
        hp = _dot(prev_ref[...].astype(BF16), w)
    else:
        @pl.when(i == 0)
        def _():
            carry_scr[j] = jnp.zeros(carry_scr.shape[1:], F32)

        first = jnp.where(i % tiles_per_seq == 0, 0.0, carry_scr[j, 0:1, :])
        row0 = lax.broadcasted_iota(jnp.int32, h.shape, 0) == 0
        hp = jnp.where(row0, first, pltpu.roll(h, 1, axis=0))
        carry_scr[j, 0:1, :] = h[tm - 1:tm, :]
    hx_ref[...] = h + (hp - h) * mu_ref[...]


def _in_proj(x, gamma, sc, sh, w_bf16, mu_pad, *, tm, tiles_per_seq, prev=None):
    m, d = x.shape
    n = w_bf16.shape[1]
    tn = 512
    nt = n // tn
    explicit_prev = prev is not None
    if explicit_prev:
        mod_spec = pl.BlockSpec((1, tm, d), lambda i, j: (i, 0, 0))
    else:
        mod_spec = pl.BlockSpec((1, 1, d), lambda i, j: (i // tiles_per_seq, 0, 0))
    in_specs = [pl.BlockSpec((tm, d), lambda i, j: (i, 0)),
                pl.BlockSpec((1, d), lambda i, j: (0, 0)),
                mod_spec, mod_spec]
    args = [x, gamma.reshape(1, d), sc, sh]
    if explicit_prev:
        in_specs.append(pl.BlockSpec((tm, d), lambda i, j: (i, 0)))
        args.append(prev)
    in_specs += [pl.BlockSpec((d, tn), lambda i, j: (0, j)),
                 pl.BlockSpec((1, tn), lambda i, j: (0, j))]
    args += [w_bf16, mu_pad.reshape(1, n)]
    scratch = [pltpu.VMEM((tm, d), BF16)]
    if explicit_prev:
        u_shape, u_spec = (m, d), pl.BlockSpec((tm, d), lambda i, j: (i, 0))
    else:
        u_shape, u_spec = (m // tm * 8, d), pl.BlockSpec((8, d), lambda i, j: (i, 0))
        scratch.append(pltpu.VMEM((nt, 8, tn), F32))
    return pl.pallas_call(
        functools.partial(_in_proj_kernel, tiles_per_seq=tiles_per_seq, explicit_prev=explicit_prev, tm=tm),
        grid=(m // tm, nt),
        in_specs=in_specs,
        out_specs=[pl.BlockSpec((tm, tn), lambda i, j: (i, j)), u_spec],
        out_shape=[jax.ShapeDtypeStruct((m, n), F32), jax.ShapeDtypeStruct(u_shape, F32)],
        scratch_shapes=scratch,
        compiler_params=pltpu.CompilerParams(dimension_semantics=("arbitrary", "arbitrary"),
                                             vmem_limit_bytes=VMEM_LIMIT),
    )(*args)


POOL_HALO = 16


def _pool_kernel(p_ref, pw_ref, ps_ref, o_ref, ext_scr, *, tiles_per_seq, full_count, tc):
    i = pl.program_id(0)

    @pl.when(i % tiles_per_seq == 0)
    def _():
        ext_scr[0:POOL_HALO, :] = jnp.zeros((POOL_HALO, D_POOL), F32)

    p = p_ref[...]
    ext_scr[POOL_HALO:POOL_HALO + tc, :] = p
    pos = (i % tiles_per_seq) * tc + lax.broadcasted_iota(jnp.int32, (tc, 1), 0)
    outs = []
    for gi, win in enumerate(POOL_WINDOWS):
        lo = gi * POOL_CH
        pg = p[:, lo:lo + POOL_CH]
        acc = pg
        for s in range(1, win):
            acc = acc + ext_scr[POOL_HALO - s:POOL_HALO - s + tc, lo:lo + POOL_CH]
        if full_count:
            dgi = acc / float(win) - pg
        else:
            cnt = jnp.minimum(pos + 1, win).astype(F32)
            dgi = acc / cnt - pg
        outs.append(_dot(dgi.astype(BF16), pw_ref[gi]))
    y = jnp.concatenate(outs, axis=-1) * ps_ref[...]
    o_ref[...] = y.astype(BF16)
    ext_scr[0:POOL_HALO, :] = ext_scr[tc:tc + POOL_HALO, :]


def _pool(src, col_block, pool_w_bf16, pool_scale, *, tc, tiles_per_seq, full_count):
    m = src.shape[0]
    return pl.pallas_call(
        functools.partial(_pool_kernel, tiles_per_seq=tiles_per_seq, full_count=full_count, tc=tc),
        grid=(m // tc,),
        in_specs=[pl.BlockSpec((tc, D_POOL), lambda i: (i, col_block)),
                  pl.BlockSpec((len(POOL_WINDOWS), POOL_CH, POOL_CH), lambda i: (0, 0, 0)),
                  pl.BlockSpec((1, D_POOL), lambda i: (0, 0))],
        out_specs=pl.BlockSpec((tc, D_POOL), lambda i: (i, 0)),
        out_shape=jax.ShapeDtypeStruct((m, D_POOL), BF16),
        scratch_shapes=[pltpu.VMEM((POOL_HALO + tc, D_POOL), F32)],
        compiler_params=pltpu.CompilerParams(dimension_semantics=("arbitrary",),
                                             vmem_limit_bytes=VMEM_LIMIT),
    )(src, pool_w_bf16, pool_scale.reshape(1, D_POOL))


def _softplus(z):
    return jnp.maximum(z, 0.0) + jnp.log1p(jnp.exp(-jnp.abs(z)))


def _rwkv_prep(r, k, v, lora, prm, seg, segt):
    (w0, wd, a0, wa, wg, k_k, k_a, r_k) = prm
    wl = lora[:, 0:LORA_PAD]
    al = lora[:, LORA_PAD:2 * LORA_PAD]
    gl = lora[:, 2 * LORA_PAD:]
    logw = -_softplus(-(w0 + _dot_x3(jnp.tanh(wl), wd))) - 0.5
    log_decay = -jnp.exp(logw)
    decay = jnp.exp(log_decay)
    a = _sigmoid(a0 + _dot_x3(al, wa))
    gate = _dot(_sigmoid(gl).astype(BF16), wg.astype(BF16))
    kk = k * k_k
    ss = _dot_exact_rhs(kk * kk, seg)
    inv = 1.0 / jnp.maximum(jnp.sqrt(ss), 1e-12)
    kk = kk * _dot_exact_rhs(inv, segt)
    k_mod = k * (1.0 + (a - 1.0) * k_a)
    bonus = _dot_exact_rhs(_dot_exact_rhs(r * k_mod * r_k, seg), segt) * v
    return decay, log_decay, k_mod, -kk, kk * a, gate, bonus


def _group_norm_head(y):
    mu = jnp.mean(y, axis=-1, keepdims=True)
    yc = y - mu
    var = jnp.mean(yc * yc, axis=-1, keepdims=True)
    return yc * lax.rsqrt(var + GN_EPS)


def _group_norm_rows(y, seg, segt):
    inv_n = 1.0 / HEAD_DIM
    mu = _dot_exact_rhs(_dot_exact_rhs(y, seg) * inv_n, segt)
    yc = y - mu
    var = _dot_exact_rhs(yc * yc, seg) * inv_n
    return yc * _dot_exact_rhs(lax.rsqrt(var + GN_EPS), segt)


WKV_CHUNK = HEAD_DIM
PAIR_LANES = 2 * HEAD_DIM
N_PAIRS = RWKV_HEADS // 2
_NT = (((1,), (1,)), ((), ()))
_TN = (((0,), (0,)), ((), ()))


def _dg_x3(a, b, dims):
    ah, al = _split2(a)
    bh, bl = _split2(b)
    dg = lambda x, y: lax.dot_general(x, y, dims, preferred_element_type=F32)
    return dg(ah, bh) + (dg(ah, bl) + dg(al, bh))


def _wkv_masks():
    n = PAIR_LANES
    ri = lax.broadcasted_iota(jnp.int32, (n, n), 0)
    ci = lax.broadcasted_iota(jnp.int32, (n, n), 1)
    same_head = (ri // HEAD_DIM) == (ci // HEAD_DIM)
    same16 = (ri // 16) == (ci // 16)
    same32 = (ri // 32) == (ci // 32)
    return dict(same_head=same_head, strict=same_head & (ci < ri), incl=same_head & (ci <= ri), eye=ri == ci,
                same16=same16, only32=same32 & jnp.logical_not(same16), not32=jnp.logical_not(same32),
                head0=lax.broadcasted_iota(jnp.int32, (1, n), 1) < HEAD_DIM)


def _each(fn, *lists):
    return [fn(*xs) for xs in zip(*lists)]


def _unit_lower_inverse(labs, mk):
    ds = [jnp.where(mk["same16"], lab, 0.0) for lab in labs]
    xs = [jnp.where(mk["eye"], 1.0, 0.0) + d for d in ds]
    for _ in range(3):
        ds = _each(_dot_x3, ds, ds)
        xs = _each(lambda x, d: x + _dot_x3(x, d), xs, ds)
    for level in ("only32", "not32"):
        es = [jnp.where(mk[level], lab, 0.0) for lab in labs]
        exs = _each(_dot_x3, es, xs)
        xs = _each(lambda x, ex: x + _dot_x3(x, ex), xs, exs)
    return xs


WKV_GROUP = 4


def _wkv_chunk_group(rows, s_bds, mk):
    c = WKV_CHUNK
    stack = lambda x: jnp.concatenate([jnp.where(mk["head0"], x, 0.0), jnp.where(mk["head0"], 0.0, x)], axis=0)
    dup = lambda x: jnp.concatenate([x, x], axis=0)
    col = lambda i: [row[i] for row in rows]
    a_s, r_s, v_s = [_each(stack, col(i)) for i in (0, 1, 6)]
    b2, k2, bh2, kh2 = [_each(dup, col(i)) for i in (2, 3, 4, 5)]
    nt = lambda x, y: _dg_x3(x, y, _NT)
    tn = lambda x, y: _dg_x3(x, y, _TN)
    below = lambda x: jnp.where(mk["strict"], x, 0.0)
    upto = lambda x: jnp.where(mk["incl"], x, 0.0)
    heads = lambda x: jnp.where(mk["same_head"], x, 0.0)
    lab = _each(below, _each(nt, a_s, b2))
    lak = _each(below, _each(nt, a_s, k2))
    lrb = _each(upto, _each(nt, r_s, b2))
    lrk = _each(upto, _each(nt, r_s, k2))
    t_inv = _unit_lower_inverse(lab, mk)
    ta_s = _each(_dot_x3, t_inv, a_s)
    u_s = _each(_dot_x3, t_inv, _each(_dot_x3, lak, v_s))
    m_bd = _each(lambda m, w: heads(m) + jnp.where(mk["eye"], w, 0.0), _each(tn, ta_s, bh2), col(7))
    g_bd = _each(lambda g1, g2: heads(g1 + g2), _each(tn, u_s, bh2), _each(tn, v_s, kh2))
    q_s = _each(_dot_x3, lrb, ta_s)
    y0_s = _each(lambda p1, p2: p1 + p2, _each(_dot_x3, lrb, u_s), _each(_dot_x3, lrk, v_s))
    q = _each(lambda rt, qs: rt + (qs[:c] + qs[c:]), col(1), q_s)
    y = _each(lambda yq, y0: yq + (y0[:c] + y0[c:]), _each(nt, q, s_bds), y0_s)
    s_new = _each(lambda sm, g: sm + g, _each(_dot_x3, s_bds, m_bd), g_bd)
    return y, s_new


def _rwkv_seq_kernel(r_ref, k_ref, v_ref, lora_ref, w0_ref, wd_ref, a0_ref, wa_ref, wg_ref, kk_ref, ka_ref,
                     rk_ref, lnw_ref, lnb_ref, seg_ref, segt_ref, o_ref, sout_ref, s_scr, y_scr):
    c = pl.program_id(1)

    @pl.when(c == 0)
    def _():
        s_scr[...] = jnp.zeros(s_scr.shape, F32)

    r = r_ref[...]
    v = v_ref[...]
    seg = seg_ref[...]
    segt = segt_ref[...]
    prm = (w0_ref[...], wd_ref[...], a0_ref[...], wa_ref[...], wg_ref[...], kk_ref[...], ka_ref[...], rk_ref[...])
    decay, log_decay, k_mod, neg_kk, b, gate, bonus = _rwkv_prep(r, k_ref[...], v, lora_ref[...], prm, seg, segt)

    n = WKV_CHUNK
    tri = (lax.broadcasted_iota(jnp.int32, (n, n), 1) <= lax.broadcasted_iota(jnp.int32, (n, n), 0)).astype(BF16)
    ld_h, ld_m, ld_l = _split3(log_decay)
    cw = _dot(tri, ld_h) + (_dot(tri, ld_m) + _dot(tri, ld_l))
    cw_prev = cw - log_decay
    cw_last = cw[n - 1:n, :]
    e_prev = jnp.exp(cw_prev)
    e_inv = jnp.exp(-cw)
    e_rem = jnp.exp(cw_last - cw)
    at = neg_kk * e_prev
    rt = r * (e_prev * decay)
    bt = b * e_inv
    kt = k_mod * e_inv
    bh = b * e_rem
    kh = k_mod * e_rem
    w_last = jnp.exp(cw_last)

    mk = _wkv_masks()
    for p0 in range(0, N_PAIRS, WKV_GROUP):
        pairs = range(p0, p0 + WKV_GROUP)
        lanes = [slice(p * PAIR_LANES, (p + 1) * PAIR_LANES) for p in pairs]
        rows = [tuple(z[:, sl] for z in (at, rt, bt, kt, bh, kh, v, w_last)) for sl in lanes]
        ys, s_new = _wkv_chunk_group(rows, [s_scr[p] for p in pairs], mk)
        for p, sl, y, s in zip(pairs, lanes, ys, s_new):
            s_scr[p] = s
            y_scr[:, sl] = y

    out = (_group_norm_rows(y_scr[...], seg, segt) * lnw_ref[...] + lnb_ref[...] + bonus) * gate
    o_ref[...] = out.astype(BF16)

    @pl.when(c == pl.num_programs(1) - 1)
    def _():
        for p in range(N_PAIRS):
            sout_ref[0, 2 * p] = s_scr[p, 0:HEAD_DIM, 0:HEAD_DIM]
            sout_ref[0, 2 * p + 1] = s_scr[p, HEAD_DIM:, HEAD_DIM:]


def _rwkv_weights(w0, w_decay_up, a0, w_aaa_up, w_gate_up, k_k, k_a, r_k, ln_x_w, ln_x_b):
    row = lambda z: z.reshape(1, D_RWKV)
    pad = lambda z: jnp.pad(z, ((0, LORA_PAD - z.shape[0]), (0, 0)))
    head_of_col = jnp.arange(D_RWKV, dtype=jnp.int32) // HEAD_DIM
    seg = (head_of_col[:, None] == jnp.arange(LANES, dtype=jnp.int32)[None, :]).astype(BF16)
    return (row(w0), pad(w_decay_up), row(a0), pad(w_aaa_up), w_gate_up, row(k_k), row(k_a), row(r_k),
            row(ln_x_w), row(ln_x_b), seg, seg.T)


def _const_spec(shape, grid_rank):
    zeros = (0,) * len(shape)
    if grid_rank == 1:
        return pl.BlockSpec(shape, lambda i: zeros)
    return pl.BlockSpec(shape, lambda i, j: zeros)


def _rwkv_seq(hx, weights, *, batch, seq):
    tc = WKV_CHUNK
    nchunk = seq // tc
    rowmap = lambda cb: (lambda b, c: (b * nchunk + c, cb))
    in_specs = [pl.BlockSpec((tc, D_RWKV), rowmap(0)),
                pl.BlockSpec((tc, D_RWKV), rowmap(1)),
                pl.BlockSpec((tc, D_RWKV), rowmap(2)),
                pl.BlockSpec((tc, 2 * LORA_PAD + GATE_LORA), rowmap(COL_LORA // (2 * LORA_PAD + GATE_LORA)))]
    in_specs += [_const_spec(w.shape, 2) for w in weights]
    return pl.pallas_call(
        _rwkv_seq_kernel,
        grid=(batch, nchunk),
        in_specs=in_specs,
        out_specs=[pl.BlockSpec((tc, D_RWKV), lambda b, c: (b * nchunk + c, 0)),
                   pl.BlockSpec((1, RWKV_HEADS, HEAD_DIM, HEAD_DIM), lambda b, c: (b, 0, 0, 0))],
        out_shape=[jax.ShapeDtypeStruct((batch * seq, D_RWKV), BF16),
                   jax.ShapeDtypeStruct((batch, RWKV_HEADS, HEAD_DIM, HEAD_DIM), F32)],
        scratch_shapes=[pltpu.VMEM((N_PAIRS, PAIR_LANES, PAIR_LANES), F32), pltpu.VMEM((tc, D_RWKV), F32)],
        compiler_params=pltpu.CompilerParams(dimension_semantics=("arbitrary", "arbitrary"),
                                             vmem_limit_bytes=VMEM_LIMIT),
    )(hx, hx, hx, hx, *weights)


def _rwkv_tok_prep_kernel(r_ref, k_ref, v_ref, lora_ref, w0_ref, wd_ref, a0_ref, wa_ref, wg_ref, kk_ref, ka_ref,
                          rk_ref, seg_ref, segt_ref, w_o, k_o, a_o, b_o, g_o, bonus_o):
    prm = (w0_ref[...], wd_ref[...], a0_ref[...], wa_ref[...], wg_ref[...], kk_ref[...], ka_ref[...], rk_ref[...])
    decay, _, k_mod, neg_kk, b, gate, bonus = _rwkv_prep(r_ref[...], k_ref[...], v_ref[...], lora_ref[...], prm,
                                                          seg_ref[...], segt_ref[...])
    w_o[...] = decay
    k_o[...] = k_mod
    a_o[...] = neg_kk
    b_o[...] = b
    g_o[...] = gate
    bonus_o[...] = bonus


def _rwkv_tok_step_kernel(r_ref, w_ref, k_ref, v_ref, a_ref, b_ref, g_ref, bonus_ref, lnw_ref, lnb_ref, s0_ref,
                          o_ref, s_ref, *, tb):
    eye = (lax.broadcasted_iota(jnp.int32, (HEAD_DIM, HEAD_DIM), 0)
           == lax.broadcasted_iota(jnp.int32, (HEAD_DIM, HEAD_DIM), 1))

    def body(bi, carry):
        s = s0_ref[bi]
        sa = jnp.sum(s * a_ref[bi], axis=2, keepdims=True)
        v_col = jnp.sum(jnp.where(eye, v_ref[bi], 0.0), axis=2, keepdims=True)
        s = s * w_ref[bi] + sa * b_ref[bi] + v_col * k_ref[bi]
        s_ref[bi] = s
        y_col = jnp.sum(s * r_ref[bi], axis=2, keepdims=True)
        y = jnp.sum(jnp.where(eye, y_col, 0.0), axis=1, keepdims=True)
        o_ref[bi] = (_group_norm_head(y) * lnw_ref[0] + lnb_ref[0] + bonus_ref[bi]) * g_ref[bi]
        return carry

    lax.fori_loop(0, tb, body, 0)


def _rwkv_tok(hx, state, weights, *, tb):
    nb = hx.shape[0]
    (w0, wd, a0, wa, wg, k_k, k_a, r_k, ln_w, ln_b, seg, segt) = weights
    prep_w = (w0, wd, a0, wa, wg, k_k, k_a, r_k, seg, segt)
    colmap = lambda cb: (lambda i: (0, cb))
    in_specs = [pl.BlockSpec((nb, D_RWKV), colmap(0)),
                pl.BlockSpec((nb, D_RWKV), colmap(1)),
                pl.BlockSpec((nb, D_RWKV), colmap(2)),
                pl.BlockSpec((nb, 2 * LORA_PAD + GATE_LORA), colmap(COL_LORA // (2 * LORA_PAD + GATE_LORA)))]
    in_specs += [_const_spec(w.shape, 1) for w in prep_w]
    rows = jax.ShapeDtypeStruct((nb, D_RWKV), F32)
    prepped = pl.pallas_call(
        _rwkv_tok_prep_kernel,
        grid=(1,),
        in_specs=in_specs,
        out_specs=[pl.BlockSpec((nb, D_RWKV), lambda i: (0, 0))] * 6,
        out_shape=[rows] * 6,
        compiler_params=pltpu.CompilerParams(dimension_semantics=("arbitrary",),
                                             vmem_limit_bytes=VMEM_LIMIT),
    )(hx, hx, hx, hx, *prep_w)
    decay, k_mod, neg_kk, b, gate, bonus = prepped
    heads = lambda z: z.reshape(nb, RWKV_HEADS, 1, HEAD_DIM)
    vecs = [heads(z) for z in (hx[:, COL_R:COL_R + D_RWKV], decay, k_mod, hx[:, COL_V:COL_V + D_RWKV], neg_kk, b,
                               gate, bonus)]
    vec_spec = pl.BlockSpec((tb, RWKV_HEADS, 1, HEAD_DIM), lambda i: (i, 0, 0, 0))
    ln_spec = pl.BlockSpec((1, RWKV_HEADS, 1, HEAD_DIM), lambda i: (0, 0, 0, 0))
    state_spec = pl.BlockSpec((tb, RWKV_HEADS, HEAD_DIM, HEAD_DIM), lambda i: (i, 0, 0, 0))
    out, new_state = pl.pallas_call(
        functools.partial(_rwkv_tok_step_kernel, tb=tb),
        grid=(nb // tb,),
        in_specs=[vec_spec] * 8 + [ln_spec, ln_spec, state_spec],
        out_specs=[vec_spec, state_spec],
        out_shape=[jax.ShapeDtypeStruct((nb, RWKV_HEADS, 1, HEAD_DIM), F32),
                   jax.ShapeDtypeStruct(state.shape, F32)],
        compiler_params=pltpu.CompilerParams(dimension_semantics=("arbitrary",),
                                             vmem_limit_bytes=VMEM_LIMIT),
    )(*vecs, ln_w.reshape(1, RWKV_HEADS, 1, HEAD_DIM), ln_b.reshape(1, RWKV_HEADS, 1, HEAD_DIM), state)
    return out.reshape(nb, D_RWKV).astype(BF16), new_state


def _route(logits, bias):
    scores = _sigmoid(logits)
    sel = scores + bias
    rows = logits.shape[0]
    lane_i = lax.broadcasted_iota(jnp.int32, (rows, N_EXPERTS), 1)
    grp = lane_i // GROUP_SIZE
    lane = lane_i.astype(F32)
    neg_inf = -jnp.inf

    def take_max(x):
        m = jnp.max(x, axis=1, keepdims=True)
        idx = jnp.min(jnp.where(x == m, lane, float(N_EXPERTS)), axis=1, keepdims=True)
        return m, lane == idx

    gscore = []
    for g in range(N_GROUPS):
        x = jnp.where(grp == g, sel, neg_inf)
        m1, hit = take_max(x)
        m2 = jnp.max(jnp.where(hit, neg_inf, x), axis=1, keepdims=True)
        gscore.append(m1 + m2)
    emask = jnp.zeros((rows, N_EXPERTS), jnp.bool_)
    for g in range(N_GROUPS):
        ahead = jnp.zeros((rows, 1), jnp.int32)
        for g2 in range(N_GROUPS):
            if g2 == g:
                continue
            beats = (gscore[g2] >= gscore[g]) if g2 < g else (gscore[g2] > gscore[g])
            ahead = ahead + beats.astype(jnp.int32)
        emask = emask | ((grp == g) & (ahead < TOPK_GROUPS))
    x = jnp.where(emask, sel, neg_inf)
    chosen = jnp.zeros((rows, N_EXPERTS), jnp.bool_)
    slot_lane = lax.broadcasted_iota(jnp.int32, (rows, LANES), 1)
    top_idx = jnp.zeros((rows, LANES), F32)
    top_w = jnp.zeros((rows, LANES), F32)
    for kth in range(TOP_K):
        m = jnp.max(x, axis=1, keepdims=True)
        idx = jnp.min(jnp.where(x == m, lane, float(N_EXPERTS)), axis=1, keepdims=True)
        hit = lane == idx
        chosen = chosen | hit
        x = jnp.where(hit, neg_inf, x)
        top_idx = jnp.where(slot_lane == kth, idx, top_idx)
        top_w = jnp.where(slot_lane == kth, jnp.sum(jnp.where(hit, scores, 0.0), axis=1, keepdims=True), top_w)
    top_w = top_w / jnp.sum(top_w, axis=1, keepdims=True) * ROUTE_SCALE
    return top_idx.astype(jnp.int32), top_w, chosen.astype(F32)


def _out_proj_kernel(pool_ref, rwkv_ref, x_ref, g1_ref, sc_ref, sh_ref, wp_ref, wr_ref, npost_ref, npre_ref,
                     wrt_ref, rb_ref, x1_ref, u2_ref, idx_ref, topw_ref, chosen_ref):
    m = _dot(pool_ref[...], wp_ref[...]) + _dot(rwkv_ref[...], wr_ref[...])
    x1 = x_ref[...] + g1_ref[0] * _rms(m, npost_ref[...])
    x1_ref[...] = x1
    u2 = _rms(x1, npre_ref[...]) * (1.0 + sc_ref[0]) + sh_ref[0]
    u2_ref[...] = u2
    top_idx, top_w, chosen = _route(_dot_x3(u2, wrt_ref[...]), rb_ref[...])
    idx_ref[...] = top_idx
    topw_ref[...] = top_w
    chosen_ref[...] = chosen


def _out_proj(pool_out, rwkv_out, x, g1, sc2, sh2, w_out_bf16, n_post, n_pre, w_router, router_bias, *, tm,
              tiles_per_seq):
    m, d = x.shape
    if tiles_per_seq is None:
        mod_spec = pl.BlockSpec((1, tm, d), lambda i: (i, 0, 0))
    else:
        mod_spec = pl.BlockSpec((1, 1, d), lambda i: (i // tiles_per_seq, 0, 0))
    row = lambda width: pl.BlockSpec((tm, width), lambda i: (i, 0))
    return pl.pallas_call(
        _out_proj_kernel,
        grid=(m // tm,),
        in_specs=[row(D_POOL), row(D_RWKV), row(d), mod_spec, mod_spec, mod_spec,
                  _const_spec((D_POOL, d), 1), _const_spec((D_RWKV, d), 1),
                  _const_spec((1, d), 1), _const_spec((1, d), 1),
                  _const_spec((d, N_EXPERTS), 1), _const_spec((1, N_EXPERTS), 1)],
        out_specs=[row(d), row(d), row(LANES), row(LANES), row(N_EXPERTS)],
        out_shape=[jax.ShapeDtypeStruct((m, d), F32), jax.ShapeDtypeStruct((m, d), F32),
                   jax.ShapeDtypeStruct((m, LANES), jnp.int32), jax.ShapeDtypeStruct((m, LANES), F32),
                   jax.ShapeDtypeStruct((m, N_EXPERTS), F32)],
        compiler_params=pltpu.CompilerParams(dimension_semantics=("arbitrary",),
                                             vmem_limit_bytes=VMEM_LIMIT),
    )(pool_out, rwkv_out, x, g1, sc2, sh2, w_out_bf16[:D_POOL], w_out_bf16[D_POOL:],
      n_post.reshape(1, d), n_pre.reshape(1, d), w_router, router_bias.reshape(1, N_EXPERTS))


MOE_TM = 128


def _experts_kernel(te_ref, fresh_ref, wslot_ref, nexte_ref, nused_ref, idx_ref, idx_next_ref, ws_ref, x_hbm,
                    wg_hbm, wu_hbm, wd_hbm, ys_ref, wg_buf, wu_buf, wd_buf, g_bf, u_bf, d_bf, x_buf, w_sems, x_sems):
    t = pl.program_id(0)
    n_used = nused_ref[0]

    def weight_copies(e, slot):
        return (pltpu.make_async_copy(wg_hbm.at[e], wg_buf.at[slot], w_sems.at[slot]),
                pltpu.make_async_copy(wu_hbm.at[e], wu_buf.at[slot], w_sems.at[slot]),
                pltpu.make_async_copy(wd_hbm.at[e], wd_buf.at[slot], w_sems.at[slot]))

    def start_rows(table_ref, slot):
        def body(j, carry):
            pltpu.make_async_copy(x_hbm.at[pl.ds(table_ref[0, 0, j], 1)], x_buf.at[slot, pl.ds(j, 1)],
                                  x_sems.at[slot]).start()
            return carry
        lax.fori_loop(0, MOE_TM, body, 0, unroll=8)

    @pl.when(t == 0)
    def _():
        for cp in weight_copies(te_ref[0], 0):
            cp.start()
        start_rows(idx_ref, 0)

    @pl.when(jnp.logical_and(t < n_used, fresh_ref[t] == 1))
    def _():
        slot = wslot_ref[t]
        for cp in weight_copies(te_ref[t], slot):
            cp.wait()

        @pl.when(nexte_ref[t] >= 0)
        def _():
            for cp in weight_copies(nexte_ref[t], 1 - slot):
                cp.start()

        g_bf[...] = wg_buf[slot].astype(BF16)
        u_bf[...] = wu_buf[slot].astype(BF16)
        d_bf[...] = wd_buf[slot].astype(BF16)

    @pl.when(t + 1 < n_used)
    def _():
        start_rows(idx_next_ref, (t + 1) % 2)

    @pl.when(t < n_used)
    def _():
        cur = t % 2
        pltpu.make_async_copy(x_hbm.at[pl.ds(0, MOE_TM)], x_buf.at[cur], x_sems.at[cur]).wait()
        x = x_buf[cur].astype(BF16)
        h = _silu(_dot(x, g_bf[...])) * _dot(x, u_bf[...])
        ys_ref[...] = _dot(h.astype(BF16), d_bf[...]) * ws_ref[...]

    @pl.when(t >= n_used)
    def _():
        ys_ref[...] = jnp.zeros(ys_ref.shape, F32)


def _experts(tables, x_all, exp_gate, exp_up, exp_down):
    tile_expert, fresh, wslot, next_e, n_used, slot_tok, slot_w = tables
    d = x_all.shape[1]
    n_tiles = tile_expert.shape[0]
    n_slots = n_tiles * MOE_TM
    idx_block = (1, 1, MOE_TM)
    any_spec = pl.BlockSpec(memory_space=pl.ANY)
    grid_spec = pltpu.PrefetchScalarGridSpec(
        num_scalar_prefetch=5,
        grid=(n_tiles,),
        in_specs=[pl.BlockSpec(idx_block, lambda t, *_: (t, 0, 0), memory_space=pltpu.SMEM),
                  pl.BlockSpec(idx_block, lambda t, *_: (jnp.minimum(t + 1, n_tiles - 1), 0, 0),
                               memory_space=pltpu.SMEM),
                  pl.BlockSpec((MOE_TM, 1), lambda t, *_: (t, 0)),
                  any_spec, any_spec, any_spec, any_spec],
        out_specs=pl.BlockSpec((MOE_TM, d), lambda t, *_: (t, 0)),
        scratch_shapes=[pltpu.VMEM((2, d, D_EXPERT), F32), pltpu.VMEM((2, d, D_EXPERT), F32),
                        pltpu.VMEM((2, D_EXPERT, d), F32),
                        pltpu.VMEM((d, D_EXPERT), BF16), pltpu.VMEM((d, D_EXPERT), BF16),
                        pltpu.VMEM((D_EXPERT, d), BF16),
                        pltpu.VMEM((2, MOE_TM, d), F32),
                        pltpu.SemaphoreType.DMA((2,)), pltpu.SemaphoreType.DMA((2,))],
    )
    idx_tiles = slot_tok.reshape(n_tiles, 1, MOE_TM)
    return pl.pallas_call(
        _experts_kernel,
        grid_spec=grid_spec,
        out_shape=jax.ShapeDtypeStruct((n_slots, d), F32),
        compiler_params=pltpu.CompilerParams(dimension_semantics=("arbitrary",),
                                             vmem_limit_bytes=VMEM_LIMIT),
    )(tile_expert, fresh, wslot, next_e, n_used, idx_tiles, idx_tiles, slot_w.reshape(n_slots, 1), x_all,
      exp_gate, exp_up, exp_down)


ASSIGN_BITS = 17


def _dispatch(chosen, top_idx, top_w):
    n = chosen.shape[0]
    n_assign = n * TOP_K
    low_mask = (1 << ASSIGN_BITS) - 1
    assert n_assign + MOE_TM <= low_mask and N_EXPERTS << ASSIGN_BITS < 2 ** 31
    n_tiles = -(-n_assign // MOE_TM) + N_EXPERTS
    n_slots = n_tiles * MOE_TM
    sel = chosen.astype(jnp.int32)
    counts = jnp.sum(sel, axis=0)
    rank = jnp.cumsum(sel, axis=0) - sel
    tiles_e = (counts + MOE_TM - 1) // MOE_TM
    tile_end = jnp.cumsum(tiles_e)
    slot = (tile_end - tiles_e)[None, :] * MOE_TM + rank
    expert_ids = jnp.arange(N_EXPERTS, dtype=jnp.int32)
    slot_of_pair = jnp.stack([jnp.sum(jnp.where(top_idx[:, k:k + 1] == expert_ids[None, :], slot, 0), axis=1)
                              for k in range(TOP_K)])
    shift = 1 << ASSIGN_BITS
    int_max = jnp.iinfo(jnp.int32).max
    real_key = top_idx.reshape(-1) * shift + jnp.arange(n_assign, dtype=jnp.int32)
    pad_j = jnp.arange(MOE_TM - 1, dtype=jnp.int32)[None, :]
    pad_key = jnp.where(pad_j < (tiles_e * MOE_TM - counts)[:, None],
                        expert_ids[:, None] * shift + n_assign + pad_j, int_max)
    n_rest = n_slots - n_assign
    keys = jnp.concatenate([real_key, pad_key.reshape(-1),
                            jnp.full((n_rest - pad_key.size,), int_max, jnp.int32)])
    weights = jnp.concatenate([top_w.reshape(-1), jnp.zeros((n_rest,), F32)])
    sorted_key, sorted_w = lax.sort((keys, weights), num_keys=1)
    pair = sorted_key & low_mask
    is_real = pair < n_assign
    slot_tok = jnp.where(is_real, pair // TOP_K, 0)
    slot_w = jnp.where(is_real, sorted_w, 0.0)
    n_used = tile_end[-1]
    tile_ids = jnp.minimum(jnp.arange(n_tiles, dtype=jnp.int32), n_used - 1)
    tile_expert = jnp.minimum(jnp.searchsorted(tile_end, tile_ids, side="right"), N_EXPERTS - 1).astype(jnp.int32)
    in_use = jnp.arange(n_tiles, dtype=jnp.int32) < n_used
    prev_expert = jnp.concatenate([jnp.full((1,), -1, jnp.int32), tile_expert[:-1]])
    fresh = jnp.logical_and(in_use, tile_expert != prev_expert).astype(jnp.int32)
    wslot = (jnp.cumsum(fresh) - 1) % 2
    later = jnp.where(counts > 0, expert_ids, N_EXPERTS)
    next_used = jnp.concatenate([lax.cummin(later, reverse=True)[1:], jnp.full((1,), N_EXPERTS, jnp.int32)])
    next_used = jnp.where(next_used >= N_EXPERTS, -1, next_used)
    tables = (tile_expert, fresh, wslot.astype(jnp.int32), next_used[tile_expert].astype(jnp.int32),
              n_used.reshape(1).astype(jnp.int32), slot_tok.astype(jnp.int32), slot_w)
    return slot_of_pair, tables


def _final_kernel(idx_ref, idx_next_ref, u2_ref, ys_hbm, x1_ref, g2_ref, sg_ref, su_ref, sd_ref, npost_ref, o_ref,
                  rows_buf, sems, *, tm):
    i = pl.program_id(0)
    n_rows = TOP_K * tm

    def start_gather(table_ref, buf_slot):
        def body(j, carry):
            pltpu.make_async_copy(ys_hbm.at[pl.ds(table_ref[0, 0, j], 1)], rows_buf.at[buf_slot, pl.ds(j, 1)],
                                  sems.at[buf_slot]).start()
            return carry
        lax.fori_loop(0, n_rows, body, 0, unroll=8)

    @pl.when(i == 0)
    def _():
        start_gather(idx_ref, 0)

    @pl.when(i + 1 < pl.num_programs(0))
    def _():
        start_gather(idx_next_ref, (i + 1) % 2)

    u2 = u2_ref[...].astype(BF16)
    h = _silu(_dot(u2, sg_ref[...])) * _dot(u2, su_ref[...])
    shared =_dot(h.astype(BF16), sd_ref[...])
    cur = i % 2
    pltpu.make_async_copy(ys_hbm.at[pl.ds(0, n_rows)], rows_buf.at[cur], sems.at[cur]).wait()
    routed = rows_buf[cur, 0:tm]
    for kth in range(1, TOP_K):
        routed = routed + rows_buf[cur, kth * tm:(kth + 1) * tm]
    o_ref[...] = x1_ref[...] + g2_ref[0] * _rms(routed + shared, npost_ref[...])


def _final(u2, ys, slot_tiles, x1, g2, sg, su, sd, n_post, *, tm, tiles_per_seq, tile_offset):
    m, d = x1.shape
    nt = m // tm
    if tiles_per_seq is None:
        mod_spec = pl.BlockSpec((1, tm, d), lambda i: (i, 0, 0))
    else:
        mod_spec = pl.BlockSpec((1, 1, d), lambda i: (i // tiles_per_seq, 0, 0))
    row = pl.BlockSpec((tm, d), lambda i: (i, 0))
    idx_block = (1, 1, TOP_K * tm)
    return pl.pallas_call(
        functools.partial(_final_kernel, tm=tm),
        grid=(nt,),
        in_specs=[pl.BlockSpec(idx_block, lambda i: (i + tile_offset, 0, 0), memory_space=pltpu.SMEM),
                  pl.BlockSpec(idx_block, lambda i: (jnp.minimum(i + 1, nt - 1) + tile_offset, 0, 0),
                               memory_space=pltpu.SMEM),
                  row, pl.BlockSpec(memory_space=pl.ANY), row, mod_spec,
                  _const_spec(sg.shape, 1), _const_spec(su.shape, 1), _const_spec(sd.shape, 1),
                  _const_spec((1, d), 1)],
        out_specs=row,
        out_shape=jax.ShapeDtypeStruct((m, d), F32),
        scratch_shapes=[pltpu.VMEM((2, TOP_K * tm, d), F32), pltpu.SemaphoreType.DMA((2,))],
        compiler_params=pltpu.CompilerParams(dimension_semantics=("arbitrary",),
                                             vmem_limit_bytes=VMEM_LIMIT),
    )(slot_tiles, slot_tiles, u2, ys, x1, g2, sg, su, sd, n_post.reshape(1, d))


def _pad_cols_in(w_in, mu_shift):
    d = w_in.shape[0]
    hp = w_in[:, :D_POOL]
    hr = w_in[:, D_POOL:]
    cuts = [3 * D_RWKV, 3 * D_RWKV + DECAY_LORA, 3 * D_RWKV + DECAY_LORA + AAA_LORA]
    rkv, wl, al, gl = jnp.split(hr, cuts, axis=1)
    zpad = lambda z: jnp.pad(z, ((0, 0), (0, LORA_PAD - z.shape[1])))
    w = jnp.concatenate([rkv, hp, zpad(wl), zpad(al), gl], axis=1).astype(BF16)
    mu_rkv, mu_wl, mu_al, mu_gl = jnp.split(mu_shift, cuts)
    zp1 = lambda z: jnp.pad(z, (0, LORA_PAD - z.shape[0]))
    mu = jnp.concatenate([mu_rkv, jnp.zeros((D_POOL,), F32), zp1(mu_wl), zp1(mu_al), mu_gl])
    assert w.shape == (d, N_IN_PAD) and mu.shape == (N_IN_PAD,)
    return w, mu


def kernel(x_prompt, x_sample, c_prompt, c_sample, state_shift, state_pool, state_wkv, w_ada, b_ada, norm_pre_mix, norm_post_mix, norm_pre_ffn, norm_post_ffn, w_in, mu_shift, pool_w, pool_scale, w0, w_decay_up, a0, w_aaa_up, w_gate_up, k_k, k_a, r_k, ln_x_w, ln_x_b, w_out, w_router, router_bias, exp_gate, exp_up, exp_down, sh_gate, sh_up, sh_down):
    bp, seq, d = x_prompt.shape
    bs = x_sample.shape[0]
    assert w_ada.shape[0] == 1 and x_sample.shape[1] == 1
    np_rows = bp * seq

    c_all = jnp.concatenate([c_prompt, c_sample], axis=0)
    pad_rows = (-c_all.shape[0]) % 16
    ada = _ada(jnp.pad(c_all, ((0, pad_rows), (0, 0))), w_ada[0], b_ada[0])
    sh1, sc1, g1, sh2, sc2, g2 = jnp.split(ada, 6, axis=-1)
    pr = lambda z: z[:bp].reshape(bp, 1, d)
    TS = min(128, bs)
    sr = lambda z: z[bp:bp + bs].reshape(bs // TS, TS, d)

    w_in_p, mu_p = _pad_cols_in(w_in[0], mu_shift[0])
    xp = x_prompt.reshape(np_rows, d)
    xs = x_sample.reshape(bs, d)

    TM_IN = min(1024, seq)
    hx_p, u_tail = _in_proj(xp, norm_pre_mix[0], pr(sc1), pr(sh1), w_in_p, mu_p, tm=TM_IN,
                            tiles_per_seq=seq // TM_IN)
    hx_s, u_s = _in_proj(xs, norm_pre_mix[0], sr(sc1), sr(sh1), w_in_p, mu_p, tm=TS, tiles_per_seq=1,
                         prev=state_shift[0])
    new_shift_prompt = u_tail.reshape(bp, seq // TM_IN, 8, d)[:, -1, -1][None]
    new_shift_sample = u_s[None]

    pool_w_b = pool_w[0].astype(BF16)
    TC_POOL = min(512, seq)
    pool_p = _pool(hx_p, COL_POOL // D_POOL, pool_w_b, pool_scale[0], tc=TC_POOL, tiles_per_seq=seq // TC_POOL,
                   full_count=False)
    hp_p = hx_p[:, COL_POOL:COL_POOL + D_POOL].reshape(bp, seq, D_POOL)
    new_pool_prompt = hp_p[:, seq - POOL_BUF:][None]
    hp_s = hx_s[:, COL_POOL:COL_POOL + D_POOL]
    ext_s = jnp.concatenate([state_pool[0], hp_s[:, None, :]], axis=1)
    assert PAST_LEN + 1 >= max(POOL_WINDOWS)
    pool_s = _pool(ext_s.reshape(bs * (POOL_BUF + 1), D_POOL), 0, pool_w_b, pool_scale[0], tc=POOL_BUF + 1,
                   tiles_per_seq=1, full_count=True)
    pool_s = pool_s.reshape(bs, POOL_BUF + 1, D_POOL)[:, -1]
    new_pool_sample = ext_s[:, 1:][None]

    rw = _rwkv_weights(w0[0], w_decay_up[0], a0[0], w_aaa_up[0], w_gate_up[0], k_k[0], k_a[0], r_k[0],
                       ln_x_w[0], ln_x_b[0])
    rwkv_p, wkv_p = _rwkv_seq(hx_p, rw, batch=bp, seq=seq)
    rwkv_s, wkv_s = _rwkv_tok(hx_s, state_wkv[0], rw, tb=8)

    w_out_b = w_out[0].astype(BF16)
    TM_OUT = min(256, seq)
    x1_p, u2_p, idx_p, topw_p, chosen_p = _out_proj(pool_p, rwkv_p, xp, pr(g1), pr(sc2), pr(sh2), w_out_b,
                                                    norm_post_mix[0], norm_pre_ffn[0], w_router[0],
                                                    router_bias[0], tm=TM_OUT, tiles_per_seq=seq // TM_OUT)
    x1_s, u2_s, idx_s, topw_s, chosen_s = _out_proj(pool_s, rwkv_s, xs, sr(g1), sr(sc2), sr(sh2), w_out_b,
                                                    norm_post_mix[0], norm_pre_ffn[0], w_router[0],
                                                    router_bias[0], tm=TS, tiles_per_seq=None)

    n_all = np_rows + bs
    u2_all = jnp.concatenate([u2_p, u2_s], axis=0)
    idx_all = jnp.concatenate([idx_p[:, :TOP_K], idx_s[:, :TOP_K]], axis=0)
    topw_all = jnp.concatenate([topw_p[:, :TOP_K], topw_s[:, :TOP_K]], axis=0)
    chosen_all = jnp.concatenate([chosen_p, chosen_s], axis=0)
    slot_of_pair, tables = _dispatch(chosen_all, idx_all, topw_all)
    ys = _experts(tables, u2_all, exp_gate[0], exp_up[0], exp_down[0])

    assert np_rows % TS == 0 and seq % TS == 0
    slot_tiles = slot_of_pair.reshape(TOP_K, n_all // TS, TS).transpose(1, 0, 2).reshape(n_all // TS, 1, TOP_K * TS)
    sg, su, sd = sh_gate[0].astype(BF16), sh_up[0].astype(BF16), sh_down[0].astype(BF16)
    y_p = _final(u2_p, ys, slot_tiles, x1_p, pr(g2), sg, su, sd, norm_post_ffn[0], tm=TS,
                 tiles_per_seq=seq // TS, tile_offset=0)
    y_s = _final(u2_s, ys, slot_tiles, x1_s, sr(g2), sg, su, sd, norm_post_ffn[0], tm=TS, tiles_per_seq=None,
                 tile_offset=np_rows // TS)

    return (y_p.reshape(bp, seq, d), y_s.reshape(bs, 1, d), new_shift_prompt, new_pool_prompt, wkv_p[None],
            new_shift_sample, new_pool_sample, wkv_s[None])
```

```python
import functools

import jax
import jax.numpy as jnp
from jax import lax
from jax.experimental import pallas as pl
from jax.experimental.pallas import tpu as pltpu

F32 = jnp.float32
BF16 = jnp.bfloat16

D_MODEL = 2048
D_POOL = 512
POOL_WINDOWS = (2, 4, 8, 16)
POOL_CH = 128
POOL_BUF = 15
D_RWKV = 1536
HEAD_DIM = 64
RWKV_HEADS = 24
DECAY_LORA = 96
AAA_LORA = 96
GATE_LORA = 256
LORA_PAD = 128
GN_EPS = 64e-5
N_EXPERTS = 256
TOP_K = 8
N_GROUPS = 8
GROUP_SIZE = N_EXPERTS // N_GROUPS
TOPK_GROUPS = 4
D_EXPERT = 512
ROUTE_SCALE = 2.5
EPS = 1e-6
PAST_LEN = 16384

COL_R, COL_K, COL_V = 0, D_RWKV, 2 * D_RWKV
COL_POOL = 3 * D_RWKV
COL_LORA = COL_POOL + D_POOL
N_IN_PAD = COL_LORA + 2 * LORA_PAD + GATE_LORA
LANES = 128
VMEM_LIMIT = 56 * 1024 * 1024


def _dot(a, b):
    return jnp.dot(a, b, preferred_element_type=F32)


def _split2(x):
    hi = x.astype(BF16)
    lo = (x - hi.astype(F32)).astype(BF16)
    return hi, lo


def _split3(x):
    hi = x.astype(BF16)
    r = x - hi.astype(F32)
    mid = r.astype(BF16)
    lo = (r - mid.astype(F32)).astype(BF16)
    return hi, mid, lo


def _dot_x3(a, b):
    ah, al = _split2(a)
    bh, bl = _split2(b)
    return _dot(ah, bh) + (_dot(ah, bl) + _dot(al, bh))


def _dot_exact_rhs(a, b_bf16):
    h, m, l = _split3(a)
    return _dot(h, b_bf16) + (_dot(m, b_bf16) + _dot(l, b_bf16))


def _sigmoid(x):
    return 1.0 / (1.0 + jnp.exp(-x))


def _silu(x):
    return x * _sigmoid(x)


def _rms(x, g):
    return x * lax.rsqrt(jnp.mean(x * x, axis=-1, keepdims=True) + EPS) * g


def _ada_kernel(c_ref, w_ref, b_ref, o_ref):
    o_ref[...] = _dot_x3(_silu(c_ref[...]), w_ref[...]) + b_ref[...]


def _ada(c_all, w_ada, b_ada):
    m, d = c_all.shape
    n = w_ada.shape[1]
    tn = 512
    return pl.pallas_call(
        _ada_kernel,
        grid=(n // tn,),
        in_specs=[pl.BlockSpec((m, d), lambda j: (0, 0)),
                  pl.BlockSpec((d, tn), lambda j: (0, j)),
                  pl.BlockSpec((1, tn), lambda j: (0, j))],
        out_specs=pl.BlockSpec((m, tn), lambda j: (0, j)),
        out_shape=jax.ShapeDtypeStruct((m, n), F32),
        compiler_params=pltpu.CompilerParams(dimension_semantics=("arbitrary",),
                                             vmem_limit_bytes=VMEM_LIMIT),
    )(c_all, w_ada, b_ada.reshape(1, n))


def _in_proj_kernel(*refs, tiles_per_seq, explicit_prev, tm):
    if explicit_prev:
        x_ref, g_ref, sc_ref, sh_ref, prev_ref, w_ref, mu_ref, hx_ref, u_ref, ub_scr = refs
    else:
        x_ref, g_ref, sc_ref, sh_ref, w_ref, mu_ref, hx_ref, u_ref, ub_scr, carry_scr = refs
    i = pl.program_id(0)
    j = pl.program_id(1)

    @pl.when(j == 0)
    def _():
        u = _rms(x_ref[...], g_ref[...])
        u = u * (1.0 + sc_ref[0]) + sh_ref[0]
        ub_scr[...] = u.astype(BF16)
        if explicit_prev:
            u_ref[...] = u
        else:
            u_ref[...] = u[tm - 8:, :]

    w = w_ref[...]
    h = _dot(ub_scr[...], w)
    if explicit_prev:
        hp = _dot(prev_ref[...].astype(BF16), w)
    else:
        @pl.when(i == 0)
        def _():
            carry_scr[j] = jnp.zeros(carry_scr.shape[1:], F32)

        first = jnp.where(i % tiles_per_seq == 0, 0.0, carry_scr[j, 0:1, :])
        row0 = lax.broadcasted_iota(jnp.int32, h.shape, 0) == 0
        hp = jnp.where(row0, first, pltpu.roll(h, 1, axis=0))
        carry_scr[j, 0:1, :] = h[tm - 1:tm, :]
    hx_ref[...] = h + (hp - h) * mu_ref[...]


def _in_proj(x, gamma, sc, sh, w_bf16, mu_pad, *, tm, tiles_per_seq, prev=None):
    m, d = x.shape
    n = w_bf16.shape[1]
    tn = 512
    nt = n // tn
    explicit_prev = prev is not None
    if explicit_prev:
        mod_spec = pl.BlockSpec((1, tm, d), lambda i, j: (i, 0, 0))
    else:
        mod_spec = pl.BlockSpec((1, 1, d), lambda i, j: (i // tiles_per_seq, 0, 0))
    in_specs = [pl.BlockSpec((tm, d), lambda i, j: (i, 0)),
                pl.BlockSpec((1, d), lambda i, j: (0, 0)),
                mod_spec, mod_spec]
    args = [x, gamma.reshape(1, d), sc, sh]
    if explicit_prev:
        in_specs.append(pl.BlockSpec((tm, d), lambda i, j: (i, 0)))
        args.append(prev)
    in_specs += [pl.BlockSpec((d, tn), lambda i, j: (0, j)),
                 pl.BlockSpec((1, tn), lambda i, j: (0, j))]
    args += [w_bf16, mu_pad.reshape(1, n)]
    scratch = [pltpu.VMEM((tm, d), BF16)]
    if explicit_prev:
        u_shape, u_spec = (m, d), pl.BlockSpec((tm, d), lambda i, j: (i, 0))
    else:
        u_shape, u_spec = (m // tm * 8, d), pl.BlockSpec((8, d), lambda i, j: (i, 0))
        scratch.append(pltpu.VMEM((nt, 8, tn), F32))
    return pl.pallas_call(
        functools.partial(_in_proj_kernel, tiles_per_seq=tiles_per_seq, explicit_prev=explicit_prev, tm=tm),
        grid=(m // tm, nt),
        in_specs=in_specs,
        out_specs=[pl.BlockSpec((tm, tn), lambda i, j: (i, j)), u_spec],
        out_shape=[jax.ShapeDtypeStruct((m, n), F32), jax.ShapeDtypeStruct(u_shape, F32)],
        scratch_shapes=scratch,
        compiler_params=pltpu.CompilerParams(dimension_semantics=("arbitrary", "arbitrary"),
                                             vmem_limit_bytes=VMEM_LIMIT),
    )(*args)


POOL_HALO = 16


def _pool_kernel(p_ref, pw_ref, ps_ref, o_ref, ext_scr, *, tiles_per_seq, full_count, tc):
    i = pl.program_id(0)

    @pl.when(i % tiles_per_seq == 0)
    def _():
        ext_scr[0:POOL_HALO, :] = jnp.zeros((POOL_HALO, D_POOL), F32)

    p = p_ref[...]
    ext_scr[POOL_HALO:POOL_HALO + tc, :] = p
    pos = (i % tiles_per_seq) * tc + lax.broadcasted_iota(jnp.int32, (tc, 1), 0)
    outs = []
    for gi, win in enumerate(POOL_WINDOWS):
        lo = gi * POOL_CH
        pg = p[:, lo:lo + POOL_CH]
        acc = pg
        for s in range(1, win):
            acc = acc + ext_scr[POOL_HALO - s:POOL_HALO - s + tc, lo:lo + POOL_CH]
        if full_count:
            dgi = acc / float(win) - pg
        else:
            cnt = jnp.minimum(pos + 1, win).astype(F32)
            dgi = acc / cnt - pg
        outs.append(_dot(dgi.astype(BF16), pw_ref[gi]))
    y = jnp.concatenate(outs, axis=-1) * ps_ref[...]
    o_ref[...] = y.astype(BF16)
    ext_scr[0:POOL_HALO, :] = ext_scr[tc:tc + POOL_HALO, :]


def _pool(src, col_block, pool_w_bf16, pool_scale, *, tc, tiles_per_seq, full_count):
    m = src.shape[0]
    return pl.pallas_call(
        functools.partial(_pool_kernel, tiles_per_seq=tiles_per_seq, full_count=full_count, tc=tc),
        grid=(m // tc,),
        in_specs=[pl.BlockSpec((tc, D_POOL), lambda i: (i, col_block)),
                  pl.BlockSpec((len(POOL_WINDOWS), POOL_CH, POOL_CH), lambda i: (0, 0, 0)),
                  pl.BlockSpec((1, D_POOL), lambda i: (0, 0))],
        out_specs=pl.BlockSpec((tc, D_POOL), lambda i: (i, 0)),
        out_shape=jax.ShapeDtypeStruct((m, D_POOL), BF16),
        scratch_shapes=[pltpu.VMEM((POOL_HALO + tc, D_POOL), F32)],
        compiler_params=pltpu.CompilerParams(dimension_semantics=("arbitrary",),
                                             vmem_limit_bytes=VMEM_LIMIT),
    )(src, pool_w_bf16, pool_scale.reshape(1, D_POOL))


def _softplus(z):
    return jnp.maximum(z, 0.0) + jnp.log1p(jnp.exp(-jnp.abs(z)))


def _rwkv_prep(r, k, v, lora, prm, seg, segt):
    (w0, wd, a0, wa, wg, k_k, k_a, r_k) = prm
    wl = lora[:, 0:LORA_PAD]
    al = lora[:, LORA_PAD:2 * LORA_PAD]
    gl = lora[:, 2 * LORA_PAD:]
    logw = -_softplus(-(w0 + _dot_x3(jnp.tanh(wl), wd))) - 0.5
    log_decay = -jnp.exp(logw)
    decay = jnp.exp(log_decay)
    a = _sigmoid(a0 + _dot_x3(al, wa))
    gate = _dot(_sigmoid(gl).astype(BF16), wg.astype(BF16))
    kk = k * k_k
    ss = _dot_exact_rhs(kk * kk, seg)
    inv = 1.0 / jnp.maximum(jnp.sqrt(ss), 1e-12)
    kk = kk * _dot_exact_rhs(inv, segt)
    k_mod = k * (1.0 + (a - 1.0) * k_a)
    bonus = _dot_exact_rhs(_dot_exact_rhs(r * k_mod * r_k, seg), segt) * v
    return decay, log_decay, k_mod, -kk, kk * a, gate, bonus


def _group_norm_head(y):
    mu = jnp.mean(y, axis=-1, keepdims=True)
    yc = y - mu
    var = jnp.mean(yc * yc, axis=-1, keepdims=True)
    return yc * lax.rsqrt(var + GN_EPS)


def _group_norm_rows(y, seg, segt):
    inv_n = 1.0 / HEAD_DIM
    mu = _dot_exact_rhs(_dot_exact_rhs(y, seg) * inv_n, segt)
    yc = y - mu
    var = _dot_exact_rhs(yc * yc, seg) * inv_n
    return yc * _dot_exact_rhs(lax.rsqrt(var + GN_EPS), segt)


WKV_CHUNK = HEAD_DIM
PAIR_LANES = 2 * HEAD_DIM
N_PAIRS = RWKV_HEADS // 2
_NT = (((1,), (1,)), ((), ()))
_TN = (((0,), (0,)), ((), ()))


def _dg_x3(a, b, dims):
    ah, al = _split2(a)
    bh, bl = _split2(b)
    dg = lambda x, y: lax.dot_general(x, y, dims, preferred_element_type=F32)
    return dg(ah, bh) + (dg(ah, bl) + dg(al, bh))


def _wkv_masks():
    n = PAIR_LANES
    ri = lax.broadcasted_iota(jnp.int32, (n, n), 0)
    ci = lax.broadcasted_iota(jnp.int32, (n, n), 1)
    same_head = (ri // HEAD_DIM) == (ci // HEAD_DIM)
    same16 = (ri // 16) == (ci // 16)
    same32 = (ri // 32) == (ci // 32)
    return dict(same_head=same_head, strict=same_head & (ci < ri), incl=same_head & (ci <= ri), eye=ri == ci,
                same16=same16, only32=same32 & jnp.logical_not(same16), not32=jnp.logical_not(same32),
                head0=lax.broadcasted_iota(jnp.int32, (1, n), 1) < HEAD_DIM)


def _each(fn, *lists):
    return [fn(*xs) for xs in zip(*lists)]


def _unit_lower_inverse(labs, mk):
    ds = [jnp.where(mk["same16"], lab, 0.0) for lab in labs]
    xs = [jnp.where(mk["eye"], 1.0, 0.0) + d for d in ds]
    for _ in range(3):
        ds = _each(_dot_x3, ds, ds)
        xs = _each(lambda x, d: x + _dot_x3(x, d), xs, ds)
    for level in ("only32", "not32"):
        es = [jnp.where(mk[level], lab, 0.0) for lab in labs]
        exs = _each(_dot_x3, es, xs)
        xs = _each(lambda x, ex: x + _dot_x3(x, ex), xs, exs)
    return xs


WKV_GROUP = 4


def _wkv_chunk_group(rows, s_bds, mk):
    c = WKV_CHUNK
    stack = lambda x: jnp.concatenate([jnp.where(mk["head0"], x, 0.0), jnp.where(mk["head0"], 0.0, x)], axis=0)
    dup = lambda x: jnp.concatenate([x, x], axis=0)
    col = lambda i: [row[i] for row in rows]
    a_s, r_s, v_s = [_each(stack, col(i)) for i in (0, 1, 6)]
    b2, k2, bh2, kh2 = [_each(dup, col(i)) for i in (2, 3, 4, 5)]
    nt = lambda x, y: _dg_x3(x, y, _NT)
    tn = lambda x, y: _dg_x3(x, y, _TN)
    below = lambda x: jnp.where(mk["strict"], x, 0.0)
    upto = lambda x: jnp.where(mk["incl"], x, 0.0)
    heads = lambda x: jnp.where(mk["same_head"], x, 0.0)
    lab = _each(below, _each(nt, a_s, b2))
    lak = _each(below, _each(nt, a_s, k2))
    nt1 = lambda x, y: lax.dot_general(x.astype(BF16), y.astype(BF16), _NT, preferred_element_type=F32)
    mm1 = lambda x, y: _dot(x.astype(BF16), y.astype(BF16))
    lrb = _each(upto, _each(nt1, r_s, b2))
    lrk = _each(upto, _each(nt1, r_s, k2))
    t_inv = _unit_lower_inverse(lab, mk)
    ta_s = _each(_dot_x3, t_inv, a_s)
    u_s = _each(_dot_x3, t_inv, _each(_dot_x3, lak, v_s))
    m_bd = _each(lambda m, w: heads(m) + jnp.where(mk["eye"], w, 0.0), _each(tn, ta_s, bh2), col(7))
    g_bd = _each(lambda g1, g2: heads(g1 + g2), _each(tn, u_s, bh2), _each(tn, v_s, kh2))
    q_s = _each(mm1, lrb, ta_s)
    y0_s = _each(lambda p1, p2: p1 + p2, _each(mm1, lrb, u_s), _each(mm1, lrk, v_s))
    q = _each(lambda rt, qs: rt + (qs[:c] + qs[c:]), col(1), q_s)
    y = _each(lambda yq, y0: yq + (y0[:c] + y0[c:]), _each(nt1, q, s_bds), y0_s)
    s_new = _each(lambda sm, g: sm + g, _each(_dot_x3, s_bds, m_bd), g_bd)
    return y, s_new


def _rwkv_seq_kernel(r_ref, k_ref, v_ref, lora_ref, w0_ref, wd_ref, a0_ref, wa_ref, wg_ref, kk_ref, ka_ref,
                     rk_ref, lnw_ref, lnb_ref, seg_ref, segt_ref, o_ref, sout_ref, s_scr, y_scr):
    c = pl.program_id(1)

    @pl.when(c == 0)
    def _():
        s_scr[...] = jnp.zeros(s_scr.shape, F32)

    r = r_ref[...]
    v = v_ref[...]
    seg = seg_ref[...]
    segt = segt_ref[...]
    prm = (w0_ref[...], wd_ref[...], a0_ref[...], wa_ref[...], wg_ref[...], kk_ref[...], ka_ref[...], rk_ref[...])
    decay, log_decay, k_mod, neg_kk, b, gate, bonus = _rwkv_prep(r, k_ref[...], v, lora_ref[...], prm, seg, segt)

    n = WKV_CHUNK
    tri = (lax.broadcasted_iota(jnp.int32, (n, n), 1) <= lax.broadcasted_iota(jnp.int32, (n, n), 0)).astype(BF16)
    ld_h, ld_m, ld_l = _split3(log_decay)
    cw = _dot(tri, ld_h) + (_dot(tri, ld_m) + _dot(tri, ld_l))
    cw_prev = cw - log_decay
    cw_last = cw[n - 1:n, :]
    e_prev = jnp.exp(cw_prev)
    e_inv = jnp.exp(-cw)
    e_rem = jnp.exp(cw_last - cw)
    at = neg_kk * e_prev
    rt = r * (e_prev * decay)
    bt = b * e_inv
    kt = k_mod * e_inv
    bh = b * e_rem
    kh = k_mod * e_rem
    w_last = jnp.exp(cw_last)

    mk = _wkv_masks()
    for p0 in range(0, N_PAIRS, WKV_GROUP):
        pairs = range(p0, p0 + WKV_GROUP)
        lanes = [slice(p * PAIR_LANES, (p + 1) * PAIR_LANES) for p in pairs]
        rows = [tuple(z[:, sl] for z in (at, rt, bt, kt, bh, kh, v, w_last)) for sl in lanes]
        ys, s_new = _wkv_chunk_group(rows, [s_scr[p] for p in pairs], mk)
        for p, sl, y, s in zip(pairs, lanes, ys, s_new):
            s_scr[p] = s
            y_scr[:, sl] = y

    out = (_group_norm_rows(y_scr[...], seg, segt) * lnw_ref[...] + lnb_ref[...] + bonus) * gate
    o_ref[...] = out.astype(BF16)

    @pl.when(c == pl.num_programs(1) - 1)
    def _():
        for p in range(N_PAIRS):
            sout_ref[0, 2 * p] = s_scr[p, 0:HEAD_DIM, 0:HEAD_DIM]
            sout_ref[0, 2 * p + 1] = s_scr[p, HEAD_DIM:, HEAD_DIM:]


def _rwkv_weights(w0, w_decay_up, a0, w_aaa_up, w_gate_up, k_k, k_a, r_k, ln_x_w, ln_x_b):
    row = lambda z: z.reshape(1, D_RWKV)
    pad = lambda z: jnp.pad(z, ((0, LORA_PAD - z.shape[0]), (0, 0)))
    head_of_col = jnp.arange(D_RWKV, dtype=jnp.int32) // HEAD_DIM
    seg = (head_of_col[:, None] == jnp.arange(LANES, dtype=jnp.int32)[None, :]).astype(BF16)
    return (row(w0), pad(w_decay_up), row(a0), pad(w_aaa_up), w_gate_up, row(k_k), row(k_a), row(r_k),
            row(ln_x_w), row(ln_x_b), seg, seg.T)


def _const_spec(shape, grid_rank):
    zeros = (0,) * len(shape)
    if grid_rank == 1:
        return pl.BlockSpec(shape, lambda i: zeros)
    return pl.BlockSpec(shape, lambda i, j: zeros)


def _rwkv_seq(hx, weights, *, batch, seq):
    tc = WKV_CHUNK
    nchunk = seq // tc
    rowmap = lambda cb: (lambda b, c: (b * nchunk + c, cb))
    in_specs = [pl.BlockSpec((tc, D_RWKV), rowmap(0)),
                pl.BlockSpec((tc, D_RWKV), rowmap(1)),
                pl.BlockSpec((tc, D_RWKV), rowmap(2)),
                pl.BlockSpec((tc, 2 * LORA_PAD + GATE_LORA), rowmap(COL_LORA // (2 * LORA_PAD + GATE_LORA)))]
    in_specs += [_const_spec(w.shape, 2) for w in weights]
    return pl.pallas_call(
        _rwkv_seq_kernel,
        grid=(batch, nchunk),
        in_specs=in_specs,
        out_specs=[pl.BlockSpec((tc, D_RWKV), lambda b, c: (b * nchunk + c, 0)),
                   pl.BlockSpec((1, RWKV_HEADS, HEAD_DIM, HEAD_DIM), lambda b, c: (b, 0, 0, 0))],
        out_shape=[jax.ShapeDtypeStruct((batch * seq, D_RWKV), BF16),
                   jax.ShapeDtypeStruct((batch, RWKV_HEADS, HEAD_DIM, HEAD_DIM), F32)],
        scratch_shapes=[pltpu.VMEM((N_PAIRS, PAIR_LANES, PAIR_LANES), F32), pltpu.VMEM((tc, D_RWKV), F32)],
        compiler_params=pltpu.CompilerParams(dimension_semantics=("arbitrary", "arbitrary"),
                                             vmem_limit_bytes=VMEM_LIMIT),
    )(hx, hx, hx, hx, *weights)


def _rwkv_tok_prep_kernel(r_ref, k_ref, v_ref, lora_ref, w0_ref, wd_ref, a0_ref, wa_ref, wg_ref, kk_ref, ka_ref,
                          rk_ref, seg_ref, segt_ref, w_o, k_o, a_o, b_o, g_o, bonus_o):
    prm = (w0_ref[...], wd_ref[...], a0_ref[...], wa_ref[...], wg_ref[...], kk_ref[...], ka_ref[...], rk_ref[...])
    decay, _, k_mod, neg_kk, b, gate, bonus = _rwkv_prep(r_ref[...], k_ref[...], v_ref[...], lora_ref[...], prm,
                                                          seg_ref[...], segt_ref[...])
    w_o[...] = decay
    k_o[...] = k_mod
    a_o[...] = neg_kk
    b_o[...] = b
    g_o[...] = gate
    bonus_o[...] = bonus


def _rwkv_tok_step_kernel(r_ref, w_ref, k_ref, v_ref, a_ref, b_ref, g_ref, bonus_ref, lnw_ref, lnb_ref, s0_ref,
                          o_ref, s_ref, *, tb):
    eye = (lax.broadcasted_iota(jnp.int32, (HEAD_DIM, HEAD_DIM), 0)
           == lax.broadcasted_iota(jnp.int32, (HEAD_DIM, HEAD_DIM), 1))

    def body(bi, carry):
        s = s0_ref[bi]
        sa = jnp.sum(s * a_ref[bi], axis=2, keepdims=True)
        v_col = jnp.sum(jnp.where(eye, v_ref[bi], 0.0), axis=2, keepdims=True)
        s = s * w_ref[bi] + sa * b_ref[bi] + v_col * k_ref[bi]
        s_ref[bi] = s
        y_col = jnp.sum(s * r_ref[bi], axis=2, keepdims=True)
        y = jnp.sum(jnp.where(eye, y_col, 0.0), axis=1, keepdims=True)
        o_ref[bi] = (_group_norm_head(y) * lnw_ref[0] + lnb_ref[0] + bonus_ref[bi]) * g_ref[bi]
        return carry

    lax.fori_loop(0, tb, body, 0)


def _rwkv_tok(hx, state, weights, *, tb):
    nb = hx.shape[0]
    (w0, wd, a0, wa, wg, k_k, k_a, r_k, ln_w, ln_b, seg, segt) = weights
    prep_w = (w0, wd, a0, wa, wg, k_k, k_a, r_k, seg, segt)
    colmap = lambda cb: (lambda i: (0, cb))
    in_specs = [pl.BlockSpec((nb, D_RWKV), colmap(0)),
                pl.BlockSpec((nb, D_RWKV), colmap(1)),
                pl.BlockSpec((nb, D_RWKV), colmap(2)),
                pl.BlockSpec((nb, 2 * LORA_PAD + GATE_LORA), colmap(COL_LORA // (2 * LORA_PAD + GATE_LORA)))]
    in_specs += [_const_spec(w.shape, 1) for w in prep_w]
    rows = jax.ShapeDtypeStruct((nb, D_RWKV), F32)
    prepped = pl.pallas_call(
        _rwkv_tok_prep_kernel,
        grid=(1,),
        in_specs=in_specs,
        out_specs=[pl.BlockSpec((nb, D_RWKV), lambda i: (0, 0))] * 6,
        out_shape=[rows] * 6,
        compiler_params=pltpu.CompilerParams(dimension_semantics=("arbitrary",),
                                             vmem_limit_bytes=VMEM_LIMIT),
    )(hx, hx, hx, hx, *prep_w)
    decay, k_mod, neg_kk, b, gate, bonus = prepped
    heads = lambda z: z.reshape(nb, RWKV_HEADS, 1, HEAD_DIM)
    vecs = [heads(z) for z in (hx[:, COL_R:COL_R + D_RWKV], decay, k_mod, hx[:, COL_V:COL_V + D_RWKV], neg_kk, b,
                               gate, bonus)]
    vec_spec = pl.BlockSpec((tb, RWKV_HEADS, 1, HEAD_DIM), lambda i: (i, 0, 0, 0))
    ln_spec = pl.BlockSpec((1, RWKV_HEADS, 1, HEAD_DIM), lambda i: (0, 0, 0, 0))
    state_spec = pl.BlockSpec((tb, RWKV_HEADS, HEAD_DIM, HEAD_DIM), lambda i: (i, 0, 0, 0))
    out, new_state = pl.pallas_call(
        functools.partial(_rwkv_tok_step_kernel, tb=tb),
        grid=(nb // tb,),
        in_specs=[vec_spec] * 8 + [ln_spec, ln_spec, state_spec],
        out_specs=[vec_spec, state_spec],
        out_shape=[jax.ShapeDtypeStruct((nb, RWKV_HEADS, 1, HEAD_DIM), F32),
                   jax.ShapeDtypeStruct(state.shape, F32)],
        compiler_params=pltpu.CompilerParams(dimension_semantics=("arbitrary",),
                                             vmem_limit_bytes=VMEM_LIMIT),
    )(*vecs, ln_w.reshape(1, RWKV_HEADS, 1, HEAD_DIM), ln_b.reshape(1, RWKV_HEADS, 1, HEAD_DIM), state)
    return out.reshape(nb, D_RWKV).astype(BF16), new_state


def _route(logits, bias):
    scores = _sigmoid(logits)
    sel = scores + bias
    rows = logits.shape[0]
    lane_i = lax.broadcasted_iota(jnp.int32, (rows, N_EXPERTS), 1)
    grp = lane_i // GROUP_SIZE
    lane = lane_i.astype(F32)
    neg_inf = -jnp.inf

    def take_max(x):
        m = jnp.max(x, axis=1, keepdims=True)
        idx = jnp.min(jnp.where(x == m, lane, float(N_EXPERTS)), axis=1, keepdims=True)
        return m, lane == idx

    gscore = []
    for g in range(N_GROUPS):
        x = jnp.where(grp == g, sel, neg_inf)
        m1, hit = take_max(x)
        m2 = jnp.max(jnp.where(hit, neg_inf, x), axis=1, keepdims=True)
        gscore.append(m1 + m2)
    emask = jnp.zeros((rows, N_EXPERTS), jnp.bool_)
    for g in range(N_GROUPS):
        ahead = jnp.zeros((rows, 1), jnp.int32)
        for g2 in range(N_GROUPS):
            if g2 == g:
                continue
            beats = (gscore[g2] >= gscore[g]) if g2 < g else (gscore[g2] > gscore[g])
            ahead = ahead + beats.astype(jnp.int32)
        emask = emask | ((grp == g) & (ahead < TOPK_GROUPS))
    x = jnp.where(emask, sel, neg_inf)
    chosen = jnp.zeros((rows, N_EXPERTS), jnp.bool_)
    slot_lane = lax.broadcasted_iota(jnp.int32, (rows, LANES), 1)
    top_idx = jnp.zeros((rows, LANES), F32)
    top_w = jnp.zeros((rows, LANES), F32)
    for kth in range(TOP_K):
        m = jnp.max(x, axis=1, keepdims=True)
        idx = jnp.min(jnp.where(x == m, lane, float(N_EXPERTS)), axis=1, keepdims=True)
        hit = lane == idx
        chosen = chosen | hit
        x = jnp.where(hit, neg_inf, x)
        top_idx = jnp.where(slot_lane == kth, idx, top_idx)
        top_w = jnp.where(slot_lane == kth, jnp.sum(jnp.where(hit, scores, 0.0), axis=1, keepdims=True), top_w)
    top_w = top_w / jnp.sum(top_w, axis=1, keepdims=True) * ROUTE_SCALE
    return top_idx.astype(jnp.int32), top_w, chosen.astype(F32)


def _out_proj_kernel(pool_ref, rwkv_ref, x_ref, g1_ref, sc_ref, sh_ref, wp_ref, wr_ref, npost_ref, npre_ref,
                     wrt_ref, rb_ref, x1_ref, u2_ref, idx_ref, topw_ref, chosen_ref):
    m = _dot(pool_ref[...], wp_ref[...]) + _dot(rwkv_ref[...], wr_ref[...])
    x1 = x_ref[...] + g1_ref[0] * _rms(m, npost_ref[...])
    x1_ref[...] = x1
    u2 = _rms(x1, npre_ref[...]) * (1.0 + sc_ref[0]) + sh_ref[0]
    u2_ref[...] = u2
    top_idx, top_w, chosen = _route(_dot_x3(u2, wrt_ref[...]), rb_ref[...])
    idx_ref[...] = top_idx
    topw_ref[...] = top_w
    chosen_ref[...] = chosen


def _out_proj(pool_out, rwkv_out, x, g1, sc2, sh2, w_out_bf16, n_post, n_pre, w_router, router_bias, *, tm,
              tiles_per_seq):
    m, d = x.shape
    if tiles_per_seq is None:
        mod_spec = pl.BlockSpec((1, tm, d), lambda i: (i, 0, 0))
    else:
        mod_spec = pl.BlockSpec((1, 1, d), lambda i: (i // tiles_per_seq, 0, 0))
    row = lambda width: pl.BlockSpec((tm, width), lambda i: (i, 0))
    return pl.pallas_call(
        _out_proj_kernel,
        grid=(m // tm,),
        in_specs=[row(D_POOL), row(D_RWKV), row(d), mod_spec, mod_spec, mod_spec,
                  _const_spec((D_POOL, d), 1), _const_spec((D_RWKV, d), 1),
                  _const_spec((1, d), 1), _const_spec((1, d), 1),
                  _const_spec((d, N_EXPERTS), 1), _const_spec((1, N_EXPERTS), 1)],
        out_specs=[row(d), row(d), row(LANES), row(LANES), row(N_EXPERTS)],
        out_shape=[jax.ShapeDtypeStruct((m, d), F32), jax.ShapeDtypeStruct((m, d), F32),
                   jax.ShapeDtypeStruct((m, LANES), jnp.int32), jax.ShapeDtypeStruct((m, LANES), F32),
                   jax.ShapeDtypeStruct((m, N_EXPERTS), F32)],
        compiler_params=pltpu.CompilerParams(dimension_semantics=("arbitrary",),
                                             vmem_limit_bytes=VMEM_LIMIT),
    )(pool_out, rwkv_out, x, g1, sc2, sh2, w_out_bf16[:D_POOL], w_out_bf16[D_POOL:],
      n_post.reshape(1, d), n_pre.reshape(1, d), w_router, router_bias.reshape(1, N_EXPERTS))


MOE_TM = 384


ROW_GROUP = 8


def _experts_kernel(te_ref, fresh_ref, wslot_ref, nexte_ref, groups_ref, nused_ref, idx_ref, idx_next_ref, ws_ref,
                    x_hbm, wg_hbm, wu_hbm, wd_hbm, ys_ref, wg_buf, wu_buf, wd_buf, g_bf, u_bf, d_bf, x_buf,
                    w_sems, x_sems):
    t = pl.program_id(0)
    n_used = nused_ref[0]

    def weight_copies(e, slot):
        return (pltpu.make_async_copy(wg_hbm.at[e], wg_buf.at[slot], w_sems.at[slot]),
                pltpu.make_async_copy(wu_hbm.at[e], wu_buf.at[slot], w_sems.at[slot]),
                pltpu.make_async_copy(wd_hbm.at[e], wd_buf.at[slot], w_sems.at[slot]))

    def start_rows(table_ref, slot, n_groups):
        def body(g, carry):
            for r in range(ROW_GROUP):
                j = g * ROW_GROUP + r
                pltpu.make_async_copy(x_hbm.at[pl.ds(table_ref[0, 0, j], 1)], x_buf.at[slot, pl.ds(j, 1)],
                                      x_sems.at[slot]).start()
            return carry
        lax.fori_loop(0, n_groups, body, 0)

    @pl.when(t == 0)
    def _():
        for cp in weight_copies(te_ref[0], 0):
            cp.start(priority=1)
        x_buf[...] = jnp.zeros(x_buf.shape, F32)
        start_rows(idx_ref, 0, groups_ref[0])

    @pl.when(t + 1 < n_used)
    def _():
        start_rows(idx_next_ref, (t + 1) % 2, groups_ref[t + 1])

    @pl.when(jnp.logical_and(t < n_used, fresh_ref[t] == 1))
    def _():
        slot = wslot_ref[t]
        for cp in weight_copies(te_ref[t], slot):
            cp.wait()

        @pl.when(nexte_ref[t] >= 0)
        def _():
            for cp in weight_copies(nexte_ref[t], 1 - slot):
                cp.start(priority=1)

        g_bf[...] = wg_buf[slot].astype(BF16)
        u_bf[...] = wu_buf[slot].astype(BF16)
        d_bf[...] = wd_buf[slot].astype(BF16)

    @pl.when(t < n_used)
    def _():
        cur = t % 2
        n_rows = pl.multiple_of(groups_ref[t] * ROW_GROUP, ROW_GROUP)
        pltpu.make_async_copy(x_hbm.at[pl.ds(0, n_rows)], x_buf.at[cur, pl.ds(0, n_rows)], x_sems.at[cur]).wait()
        x = x_buf[cur].astype(BF16)
        h = _silu(_dot(x, g_bf[...])) * _dot(x, u_bf[...])
        ys_ref[...] = _dot(h.astype(BF16), d_bf[...]) * ws_ref[...]

    @pl.when(t >= n_used)
    def _():
        ys_ref[...] = jnp.zeros(ys_ref.shape, F32)


def _experts(tables, x_all, exp_gate, exp_up, exp_down):
    tile_expert, fresh, wslot, next_e, row_groups, n_used, slot_tok, slot_w = tables
    d = x_all.shape[1]
    n_tiles = tile_expert.shape[0]
    n_slots = n_tiles * MOE_TM
    idx_block = (1, 1, MOE_TM)
    any_spec = pl.BlockSpec(memory_space=pl.ANY)
    live = lambda t, nused: jnp.where(t < nused[0], t, n_tiles - 1)
    grid_spec = pltpu.PrefetchScalarGridSpec(
        num_scalar_prefetch=6,
        grid=(n_tiles,),
        in_specs=[pl.BlockSpec(idx_block, lambda t, *s: (live(t, s[5]), 0, 0), memory_space=pltpu.SMEM),
                  pl.BlockSpec(idx_block, lambda t, *s: (live(t + 1, s[5]), 0, 0), memory_space=pltpu.SMEM),
                  pl.BlockSpec((MOE_TM, 1), lambda t, *s: (live(t, s[5]), 0)),
                  any_spec, any_spec, any_spec, any_spec],
        out_specs=pl.BlockSpec((MOE_TM, d), lambda t, *s: (live(t, s[5]), 0)),
        scratch_shapes=[pltpu.VMEM((2, d, D_EXPERT), F32), pltpu.VMEM((2, d, D_EXPERT), F32),
                        pltpu.VMEM((2, D_EXPERT, d), F32),
                        pltpu.VMEM((d, D_EXPERT), BF16), pltpu.VMEM((d, D_EXPERT), BF16),
                        pltpu.VMEM((D_EXPERT, d), BF16),
                        pltpu.VMEM((2, MOE_TM, d), F32),
                        pltpu.SemaphoreType.DMA((2,)), pltpu.SemaphoreType.DMA((2,))],
    )
    idx_tiles = slot_tok.reshape(n_tiles, 1, MOE_TM)
    return pl.pallas_call(
        _experts_kernel,
        grid_spec=grid_spec,
        out_shape=jax.ShapeDtypeStruct((n_slots, d), F32),
        compiler_params=pltpu.CompilerParams(dimension_semantics=("arbitrary",),
                                             vmem_limit_bytes=VMEM_LIMIT),
    )(tile_expert, fresh, wslot, next_e, row_groups, n_used, idx_tiles, idx_tiles, slot_w.reshape(n_slots, 1),
      x_all, exp_gate, exp_up, exp_down)


ASSIGN_BITS = 17


def _dispatch(chosen, top_idx, top_w):
    n = chosen.shape[0]
    n_assign = n * TOP_K
    low_mask = (1 << ASSIGN_BITS) - 1
    assert n_assign + MOE_TM <= low_mask and N_EXPERTS << ASSIGN_BITS < 2 ** 31
    n_tiles = -(-n_assign // MOE_TM) + N_EXPERTS
    n_slots = n_tiles * MOE_TM
    sel = chosen.astype(jnp.int32)
    counts = jnp.sum(sel, axis=0)
    rank = jnp.cumsum(sel, axis=0) - sel
    tiles_e = (counts + MOE_TM - 1) // MOE_TM
    tile_end = jnp.cumsum(tiles_e)
    slot = (tile_end - tiles_e)[None, :] * MOE_TM + rank
    expert_ids = jnp.arange(N_EXPERTS, dtype=jnp.int32)
    slot_of_pair = jnp.stack([jnp.sum(jnp.where(top_idx[:, k:k + 1] == expert_ids[None, :], slot, 0), axis=1)
                              for k in range(TOP_K)])
    shift = 1 << ASSIGN_BITS
    int_max = jnp.iinfo(jnp.int32).max
    real_key = top_idx.reshape(-1) * shift + jnp.arange(n_assign, dtype=jnp.int32)
    pad_j = jnp.arange(MOE_TM - 1, dtype=jnp.int32)[None, :]
    pad_key = jnp.where(pad_j < (tiles_e * MOE_TM - counts)[:, None],
                        expert_ids[:, None] * shift + n_assign + pad_j, int_max)
    n_rest = n_slots - n_assign
    keys = jnp.concatenate([real_key, pad_key.reshape(-1),
                            jnp.full((n_rest - pad_key.size,), int_max, jnp.int32)])
    weights = jnp.concatenate([top_w.reshape(-1), jnp.zeros((n_rest,), F32)])
    sorted_key, sorted_w = lax.sort((keys, weights), num_keys=1)
    pair = sorted_key & low_mask
    is_real = pair < n_assign
    slot_tok = jnp.where(is_real, pair // TOP_K, 0)
    slot_w = jnp.where(is_real, sorted_w, 0.0)
    n_used = tile_end[-1]
    tile_ids = jnp.minimum(jnp.arange(n_tiles, dtype=jnp.int32), n_used - 1)
    tile_expert = jnp.minimum(jnp.searchsorted(tile_end, tile_ids, side="right"), N_EXPERTS - 1).astype(jnp.int32)
    in_use = jnp.arange(n_tiles, dtype=jnp.int32) < n_used
    prev_expert = jnp.concatenate([jnp.full((1,), -1, jnp.int32), tile_expert[:-1]])
    fresh = jnp.logical_and(in_use, tile_expert != prev_expert).astype(jnp.int32)
    wslot = (jnp.cumsum(fresh) - 1) % 2
    later = jnp.where(counts > 0, expert_ids, N_EXPERTS)
    next_used = jnp.concatenate([lax.cummin(later, reverse=True)[1:], jnp.full((1,), N_EXPERTS, jnp.int32)])
    next_used = jnp.where(next_used >= N_EXPERTS, -1, next_used)
    tile_in_expert = jnp.arange(n_tiles, dtype=jnp.int32) - (tile_end - tiles_e)[tile_expert]
    rows_in_tile = jnp.clip(counts[tile_expert] - tile_in_expert * MOE_TM, 0, MOE_TM)
    row_groups = jnp.where(in_use, (rows_in_tile + ROW_GROUP - 1) // ROW_GROUP, 0)
    tables = (tile_expert, fresh, wslot.astype(jnp.int32), next_used[tile_expert].astype(jnp.int32),
              row_groups.astype(jnp.int32), n_used.reshape(1).astype(jnp.int32), slot_tok.astype(jnp.int32), slot_w)
    return slot_of_pair, tables


def _final_kernel(idx_ref, idx_next_ref, u2_ref, ys_hbm, x1_ref, g2_ref, sg_ref, su_ref, sd_ref, npost_ref, o_ref,
                  rows_buf, sems, *, tm):
    i = pl.program_id(0)
    n_rows = TOP_K * tm

    def start_gather(table_ref, buf_slot):
        def body(j, carry):
            pltpu.make_async_copy(ys_hbm.at[pl.ds(table_ref[0, 0, j], 1)], rows_buf.at[buf_slot, pl.ds(j, 1)],
                                  sems.at[buf_slot]).start()
            return carry
        lax.fori_loop(0, n_rows, body, 0, unroll=8)

    @pl.when(i == 0)
    def _():
        start_gather(idx_ref, 0)

    @pl.when(i + 1 < pl.num_programs(0))
    def _():
        start_gather(idx_next_ref, (i + 1) % 2)

    u2 = u2_ref[...].astype(BF16)
    h = _silu(_dot(u2, sg_ref[...])) * _dot(u2, su_ref[...])
    shared =_dot(h.astype(BF16), sd_ref[...])
    cur = i % 2
    pltpu.make_async_copy(ys_hbm.at[pl.ds(0, n_rows)], rows_buf.at[cur], sems.at[cur]).wait()
    routed = rows_buf[cur, 0:tm]
    for kth in range(1, TOP_K):
        routed = routed + rows_buf[cur, kth * tm:(kth + 1) * tm]
    o_ref[...] = x1_ref[...] + g2_ref[0] * _rms(routed + shared, npost_ref[...])


def _final(u2, ys, slot_tiles, x1, g2, sg, su, sd, n_post, *, tm, tiles_per_seq, tile_offset):
    m, d = x1.shape
    nt = m // tm
    if tiles_per_seq is None:
        mod_spec = pl.BlockSpec((1, tm, d), lambda i: (i, 0, 0))
    else:
        mod_spec = pl.BlockSpec((1, 1, d), lambda i: (i // tiles_per_seq, 0, 0))
    row = pl.BlockSpec((tm, d), lambda i: (i, 0))
    idx_block = (1, 1, TOP_K * tm)
    return pl.pallas_call(
        functools.partial(_final_kernel, tm=tm),
        grid=(nt,),
        in_specs=[pl.BlockSpec(idx_block, lambda i: (i + tile_offset, 0, 0), memory_space=pltpu.SMEM),
                  pl.BlockSpec(idx_block, lambda i: (jnp.minimum(i + 1, nt - 1) + tile_offset, 0, 0),
                               memory_space=pltpu.SMEM),
                  row, pl.BlockSpec(memory_space=pl.ANY), row, mod_spec,
                  _const_spec(sg.shape, 1), _const_spec(su.shape, 1), _const_spec(sd.shape, 1),
                  _const_spec((1, d), 1)],
        out_specs=row,
        out_shape=jax.ShapeDtypeStruct((m, d), F32),
        scratch_shapes=[pltpu.VMEM((2, TOP_K * tm, d), F32), pltpu.SemaphoreType.DMA((2,))],
        compiler_params=pltpu.CompilerParams(dimension_semantics=("arbitrary",),
                                             vmem_limit_bytes=VMEM_LIMIT),
    )(slot_tiles, slot_tiles, u2, ys, x1, g2, sg, su, sd, n_post.reshape(1, d))


def _pad_cols_in(w_in, mu_shift):
    d = w_in.shape[0]
    hp = w_in[:, :D_POOL]
    hr = w_in[:, D_POOL:]
    cuts = [3 * D_RWKV, 3 * D_RWKV + DECAY_LORA, 3 * D_RWKV + DECAY_LORA + AAA_LORA]
    rkv, wl, al, gl = jnp.split(hr, cuts, axis=1)
    zpad = lambda z: jnp.pad(z, ((0, 0), (0, LORA_PAD - z.shape[1])))
    w = jnp.concatenate([rkv, hp, zpad(wl), zpad(al), gl], axis=1).astype(BF16)
    mu_rkv, mu_wl, mu_al, mu_gl = jnp.split(mu_shift, cuts)
    zp1 = lambda z: jnp.pad(z, (0, LORA_PAD - z.shape[0]))
    mu = jnp.concatenate([mu_rkv, jnp.zeros((D_POOL,), F32), zp1(mu_wl), zp1(mu_al), mu_gl])
    assert w.shape == (d, N_IN_PAD) and mu.shape == (N_IN_PAD,)
    return w, mu


def kernel(x_prompt, x_sample, c_prompt, c_sample, state_shift, state_pool, state_wkv, w_ada, b_ada, norm_pre_mix, norm_post_mix, norm_pre_ffn, norm_post_ffn, w_in, mu_shift, pool_w, pool_scale, w0, w_decay_up, a0, w_aaa_up, w_gate_up, k_k, k_a, r_k, ln_x_w, ln_x_b, w_out, w_router, router_bias, exp_gate, exp_up, exp_down, sh_gate, sh_up, sh_down):
    bp, seq, d = x_prompt.shape
    bs = x_sample.shape[0]
    assert w_ada.shape[0] == 1 and x_sample.shape[1] == 1
    np_rows = bp * seq

    c_all = jnp.concatenate([c_prompt, c_sample], axis=0)
    pad_rows = (-c_all.shape[0]) % 16
    ada = _ada(jnp.pad(c_all, ((0, pad_rows), (0, 0))), w_ada[0], b_ada[0])
    sh1, sc1, g1, sh2, sc2, g2 = jnp.split(ada, 6, axis=-1)
    pr = lambda z: z[:bp].reshape(bp, 1, d)
    TS = min(128, bs)
    sr = lambda z: z[bp:bp + bs].reshape(bs // TS, TS, d)

    w_in_p, mu_p = _pad_cols_in(w_in[0], mu_shift[0])
    xp = x_prompt.reshape(np_rows, d)
    xs = x_sample.reshape(bs, d)

    TM_IN = min(1024, seq)
    hx_p, u_tail = _in_proj(xp, norm_pre_mix[0], pr(sc1), pr(sh1), w_in_p, mu_p, tm=TM_IN,
                            tiles_per_seq=seq // TM_IN)
    hx_s, u_s = _in_proj(xs, norm_pre_mix[0], sr(sc1), sr(sh1), w_in_p, mu_p, tm=TS, tiles_per_seq=1,
                         prev=state_shift[0])
    new_shift_prompt = u_tail.reshape(bp, seq // TM_IN, 8, d)[:, -1, -1][None]
    new_shift_sample = u_s[None]

    pool_w_b = pool_w[0].astype(BF16)
    TC_POOL = min(512, seq)
    pool_p = _pool(hx_p, COL_POOL // D_POOL, pool_w_b, pool_scale[0], tc=TC_POOL, tiles_per_seq=seq // TC_POOL,
                   full_count=False)
    hp_p = hx_p[:, COL_POOL:COL_POOL + D_POOL].reshape(bp, seq, D_POOL)
    new_pool_prompt = hp_p[:, seq - POOL_BUF:][None]
    hp_s = hx_s[:, COL_POOL:COL_POOL + D_POOL]
    ext_s = jnp.concatenate([state_pool[0], hp_s[:, None, :]], axis=1)
    assert PAST_LEN + 1 >= max(POOL_WINDOWS)
    pool_s = _pool(ext_s.reshape(bs * (POOL_BUF + 1), D_POOL), 0, pool_w_b, pool_scale[0], tc=POOL_BUF + 1,
                   tiles_per_seq=1, full_count=True)
    pool_s = pool_s.reshape(bs, POOL_BUF + 1, D_POOL)[:, -1]
    new_pool_sample = ext_s[:, 1:][None]

    rw = _rwkv_weights(w0[0], w_decay_up[0], a0[0], w_aaa_up[0], w_gate_up[0], k_k[0], k_a[0], r_k[0],
                       ln_x_w[0], ln_x_b[0])
    rwkv_p, wkv_p = _rwkv_seq(hx_p, rw, batch=bp, seq=seq)
    rwkv_s, wkv_s = _rwkv_tok(hx_s, state_wkv[0], rw, tb=8)

    w_out_b = w_out[0].astype(BF16)
    TM_OUT = min(256, seq)
    x1_p, u2_p, idx_p, topw_p, chosen_p = _out_proj(pool_p, rwkv_p, xp, pr(g1), pr(sc2), pr(sh2), w_out_b,
                                                    norm_post_mix[0], norm_pre_ffn[0], w_router[0],
                                                    router_bias[0], tm=TM_OUT, tiles_per_seq=seq // TM_OUT)
    x1_s, u2_s, idx_s, topw_s, chosen_s = _out_proj(pool_s, rwkv_s, xs, sr(g1), sr(sc2), sr(sh2), w_out_b,
                                                    norm_post_mix[0], norm_pre_ffn[0], w_router[0],
                                                    router_bias[0], tm=TS, tiles_per_seq=None)

    n_all = np_rows + bs
    u2_all = jnp.concatenate([u2_p, u2_s], axis=0)
    idx_all = jnp.concatenate([idx_p[:, :TOP_K], idx_s[:, :TOP_K]], axis=0)
    topw_all = jnp.concatenate([topw_p[:, :TOP_K], topw_s[:, :TOP_K]], axis=0)
    chosen_all = jnp.concatenate([chosen_p, chosen_s], axis=0)
    slot_of_pair, tables = _dispatch(chosen_all, idx_all, topw_all)
    ys = _experts(tables, u2_all, exp_gate[0], exp_up[0], exp_down[0])

    assert np_rows % TS == 0 and seq % TS == 0
    slot_tiles = slot_of_pair.reshape(TOP_K, n_all // TS, TS).transpose(1, 0, 2).reshape(n_all // TS, 1, TOP_K * TS)
    sg, su, sd = sh_gate[0].astype(BF16), sh_up[0].astype(BF16), sh_down[0].astype(BF16)
    y_p = _final(u2_p, ys, slot_tiles, x1_p, pr(g2), sg, su, sd, norm_post_ffn[0], tm=TS,
                 tiles_per_seq=seq // TS, tile_offset=0)
    y_s = _final(u2_s, ys, slot_tiles, x1_s, sr(g2), sg, su, sd, norm_post_ffn[0], tm=TS, tiles_per_seq=None,
                 tile_offset=np_rows // TS)

    return (y_p.reshape(bp, seq, d), y_s.reshape(bs, 1, d), new_shift_prompt, new_pool_prompt, wkv_p[None],
            new_shift_sample, new_pool_sample, wkv_s[None])
```

```python
import functools

import jax
import jax.numpy as jnp
from jax import lax
from jax.experimental import pallas as pl
from jax.experimental.pallas import tpu as pltpu

F32 = jnp.float32
BF16 = jnp.bfloat16

D_MODEL = 2048
D_POOL = 512
POOL_WINDOWS = (2, 4, 8, 16)
POOL_CH = 128
POOL_BUF = 15
D_RWKV = 1536
HEAD_DIM = 64
RWKV_HEADS = 24
DECAY_LORA = 96
AAA_LORA = 96
GATE_LORA = 256
LORA_PAD = 128
GN_EPS = 64e-5
N_EXPERTS = 256
TOP_K = 8
N_GROUPS = 8
GROUP_SIZE = N_EXPERTS // N_GROUPS
TOPK_GROUPS = 4
D_EXPERT = 512
ROUTE_SCALE = 2.5
EPS = 1e-6
PAST_LEN = 16384

COL_R, COL_K, COL_V = 0, D_RWKV, 2 * D_RWKV
COL_POOL = 3 * D_RWKV
COL_LORA = COL_POOL + D_POOL
N_IN_PAD = COL_LORA + 2 * LORA_PAD + GATE_LORA
LANES = 128
VMEM_LIMIT = 56 * 1024 * 1024


def _dot(a, b):
    return jnp.dot(a, b, preferred_element_type=F32)


def _split2(x):
    hi = x.astype(BF16)
    lo = (x - hi.astype(F32)).astype(BF16)
    return hi, lo


def _split3(x):
    hi = x.astype(BF16)
    r = x - hi.astype(F32)
    mid = r.astype(BF16)
    lo = (r - mid.astype(F32)).astype(BF16)
    return hi, mid, lo


def _dot_x3(a, b):
    ah, al = _split2(a)
    bh, bl = _split2(b)
    return _dot(ah, bh) + (_dot(ah, bl) + _dot(al, bh))


def _dot_exact_rhs(a, b_bf16):
    h, m, l = _split3(a)
    return _dot(h, b_bf16) + (_dot(m, b_bf16) + _dot(l, b_bf16))


def _sigmoid(x):
    return 1.0 / (1.0 + jnp.exp(-x))


def _silu(x):
    return x * _sigmoid(x)


def _rms(x, g):
    return x * lax.rsqrt(jnp.mean(x * x, axis=-1, keepdims=True) + EPS) * g


def _ada_kernel(c_ref, w_ref, b_ref, o_ref):
    o_ref[...] = _dot_x3(_silu(c_ref[...]), w_ref[...]) + b_ref[...]


def _ada(c_all, w_ada, b_ada):
    m, d = c_all.shape
    n = w_ada.shape[1]
    tn = 512
    return pl.pallas_call(
        _ada_kernel,
        grid=(n // tn,),
        in_specs=[pl.BlockSpec((m, d), lambda j: (0, 0)),
                  pl.BlockSpec((d, tn), lambda j: (0, j)),
                  pl.BlockSpec((1, tn), lambda j: (0, j))],
        out_specs=pl.BlockSpec((m, tn), lambda j: (0, j)),
        out_shape=jax.ShapeDtypeStruct((m, n), F32),
        compiler_params=pltpu.CompilerParams(dimension_semantics=("arbitrary",),
                                             vmem_limit_bytes=VMEM_LIMIT),
    )(c_all, w_ada, b_ada.reshape(1, n))


def _in_proj_kernel(*refs, tiles_per_seq, explicit_prev, tm):
    if explicit_prev:
        x_ref, g_ref, sc_ref, sh_ref, prev_ref, w_ref, mu_ref, hx_ref, u_ref, ub_scr = refs
    else:
        x_ref, g_ref, sc_ref, sh_ref, w_ref, mu_ref, hx_ref, u_ref, ub_scr, carry_scr = refs
    i = pl.program_id(0)
    j = pl.program_id(1)

    @pl.when(j == 0)
    def _():
        u = _rms(x_ref[...], g_ref[...])
        u = u * (1.0 + sc_ref[0]) + sh_ref[0]
        ub_scr[...] = u.astype(BF16)
        if explicit_prev:
            u_ref[...] = u
        else:
            u_ref[...] = u[tm - 8:, :]

    w = w_ref[...]
    h = _dot(ub_scr[...], w)
    if explicit_prev:
        hp = _dot(prev_ref[...].astype(BF16), w)
    else:
        @pl.when(i == 0)
        def _():
            carry_scr[j] = jnp.zeros(carry_scr.shape[1:], F32)

        first = jnp.where(i % tiles_per_seq == 0, 0.0, carry_scr[j, 0:1, :])
        row0 = lax.broadcasted_iota(jnp.int32, h.shape, 0) == 0
        hp = jnp.where(row0, first, pltpu.roll(h, 1, axis=0))
        carry_scr[j, 0:1, :] = h[tm - 1:tm, :]
    hx_ref[...] = h + (hp - h) * mu_ref[...]


def _in_proj(x, gamma, sc, sh, w_bf16, mu_pad, *, tm, tiles_per_seq, prev=None):
    m, d = x.shape
    n = w_bf16.shape[1]
    tn = 512
    nt = n // tn
    explicit_prev = prev is not None
    if explicit_prev:
        mod_spec = pl.BlockSpec((1, tm, d), lambda i, j: (i, 0, 0))
    else:
        mod_spec = pl.BlockSpec((1, 1, d), lambda i, j: (i // tiles_per_seq, 0, 0))
    in_specs = [pl.BlockSpec((tm, d), lambda i, j: (i, 0)),
                pl.BlockSpec((1, d), lambda i, j: (0, 0)),
                mod_spec, mod_spec]
    args = [x, gamma.reshape(1, d), sc, sh]
    if explicit_prev:
        in_specs.append(pl.BlockSpec((tm, d), lambda i, j: (i, 0)))
        args.append(prev)
    in_specs += [pl.BlockSpec((d, tn), lambda i, j: (0, j)),
                 pl.BlockSpec((1, tn), lambda i, j: (0, j))]
    args += [w_bf16, mu_pad.reshape(1, n)]
    scratch = [pltpu.VMEM((tm, d), BF16)]
    if explicit_prev:
        u_shape, u_spec = (m, d), pl.BlockSpec((tm, d), lambda i, j: (i, 0))
    else:
        u_shape, u_spec = (m // tm * 8, d), pl.BlockSpec((8, d), lambda i, j: (i, 0))
        scratch.append(pltpu.VMEM((nt, 8, tn), F32))
    return pl.pallas_call(
        functools.partial(_in_proj_kernel, tiles_per_seq=tiles_per_seq, explicit_prev=explicit_prev, tm=tm),
        grid=(m // tm, nt),
        in_specs=in_specs,
        out_specs=[pl.BlockSpec((tm, tn), lambda i, j: (i, j)), u_spec],
        out_shape=[jax.ShapeDtypeStruct((m, n), F32), jax.ShapeDtypeStruct(u_shape, F32)],
        scratch_shapes=scratch,
        compiler_params=pltpu.CompilerParams(dimension_semantics=("arbitrary", "arbitrary"),
                                             vmem_limit_bytes=VMEM_LIMIT),
    )(*args)


POOL_HALO = 16


def _pool_kernel(p_ref, pw_ref, ps_ref, o_ref, ext_scr, *, tiles_per_seq, full_count, tc):
    i = pl.program_id(0)

    @pl.when(i % tiles_per_seq == 0)
    def _():
        ext_scr[0:POOL_HALO, :] = jnp.zeros((POOL_HALO, D_POOL), F32)

    p = p_ref[...]
    ext_scr[POOL_HALO:POOL_HALO + tc, :] = p
    pos = (i % tiles_per_seq) * tc + lax.broadcasted_iota(jnp.int32, (tc, 1), 0)
    outs = []
    for gi, win in enumerate(POOL_WINDOWS):
        lo = gi * POOL_CH
        pg = p[:, lo:lo + POOL_CH]
        acc = pg
        for s in range(1, win):
            acc = acc + ext_scr[POOL_HALO - s:POOL_HALO - s + tc, lo:lo + POOL_CH]
        if full_count:
            dgi = acc / float(win) - pg
        else:
            cnt = jnp.minimum(pos + 1, win).astype(F32)
            dgi = acc / cnt - pg
        outs.append(_dot(dgi.astype(BF16), pw_ref[gi]))
    y = jnp.concatenate(outs, axis=-1) * ps_ref[...]
    o_ref[...] = y.astype(BF16)
    ext_scr[0:POOL_HALO, :] = ext_scr[tc:tc + POOL_HALO, :]


def _pool(src, col_block, pool_w_bf16, pool_scale, *, tc, tiles_per_seq, full_count):
    m = src.shape[0]
    return pl.pallas_call(
        functools.partial(_pool_kernel, tiles_per_seq=tiles_per_seq, full_count=full_count, tc=tc),
        grid=(m // tc,),
        in_specs=[pl.BlockSpec((tc, D_POOL), lambda i: (i, col_block)),
                  pl.BlockSpec((len(POOL_WINDOWS), POOL_CH, POOL_CH), lambda i: (0, 0, 0)),
                  pl.BlockSpec((1, D_POOL), lambda i: (0, 0))],
        out_specs=pl.BlockSpec((tc, D_POOL), lambda i: (i, 0)),
        out_shape=jax.ShapeDtypeStruct((m, D_POOL), BF16),
        scratch_shapes=[pltpu.VMEM((POOL_HALO + tc, D_POOL), F32)],
        compiler_params=pltpu.CompilerParams(dimension_semantics=("arbitrary",),
                                             vmem_limit_bytes=VMEM_LIMIT),
    )(src, pool_w_bf16, pool_scale.reshape(1, D_POOL))


def _softplus(z):
    return jnp.maximum(z, 0.0) + jnp.log1p(jnp.exp(-jnp.abs(z)))


def _rwkv_prep(r, k, v, lora, prm, seg, segt):
    (w0, wd, a0, wa, wg, k_k, k_a, r_k) = prm
    wl = lora[:, 0:LORA_PAD]
    al = lora[:, LORA_PAD:2 * LORA_PAD]
    gl = lora[:, 2 * LORA_PAD:]
    logw = -_softplus(-(w0 + _dot_x3(jnp.tanh(wl), wd))) - 0.5
    log_decay = -jnp.exp(logw)
    decay = jnp.exp(log_decay)
    a = _sigmoid(a0 + _dot_x3(al, wa))
    gate = _dot(_sigmoid(gl).astype(BF16), wg.astype(BF16))
    kk = k * k_k
    ss = _dot_exact_rhs(kk * kk, seg)
    inv = 1.0 / jnp.maximum(jnp.sqrt(ss), 1e-12)
    kk = kk * _dot_exact_rhs(inv, segt)
    k_mod = k * (1.0 + (a - 1.0) * k_a)
    bonus = _dot_exact_rhs(_dot_exact_rhs(r * k_mod * r_k, seg), segt) * v
    return decay, log_decay, k_mod, -kk, kk * a, gate, bonus


def _group_norm_head(y):
    mu = jnp.mean(y, axis=-1, keepdims=True)
    yc = y - mu
    var = jnp.mean(yc * yc, axis=-1, keepdims=True)
    return yc * lax.rsqrt(var + GN_EPS)


def _group_norm_rows(y, seg, segt):
    inv_n = 1.0 / HEAD_DIM
    mu = _dot_exact_rhs(_dot_exact_rhs(y, seg) * inv_n, segt)
    yc = y - mu
    var = _dot_exact_rhs(yc * yc, seg) * inv_n
    return yc * _dot_exact_rhs(lax.rsqrt(var + GN_EPS), segt)


WKV_CHUNK = HEAD_DIM
PAIR_LANES = 2 * HEAD_DIM
N_PAIRS = RWKV_HEADS // 2
_NT = (((1,), (1,)), ((), ()))
_TN = (((0,), (0,)), ((), ()))


def _dg_x3(a, b, dims):
    ah, al = _split2(a)
    bh, bl = _split2(b)
    dg = lambda x, y: lax.dot_general(x, y, dims, preferred_element_type=F32)
    return dg(ah, bh) + (dg(ah, bl) + dg(al, bh))


def _wkv_masks():
    n = PAIR_LANES
    ri = lax.broadcasted_iota(jnp.int32, (n, n), 0)
    ci = lax.broadcasted_iota(jnp.int32, (n, n), 1)
    same_head = (ri // HEAD_DIM) == (ci // HEAD_DIM)
    same16 = (ri // 16) == (ci // 16)
    same32 = (ri // 32) == (ci // 32)
    return dict(same_head=same_head, strict=same_head & (ci < ri), incl=same_head & (ci <= ri), eye=ri == ci,
                same16=same16, only32=same32 & jnp.logical_not(same16), not32=jnp.logical_not(same32),
                head0=lax.broadcasted_iota(jnp.int32, (1, n), 1) < HEAD_DIM)


def _each(fn, *lists):
    return [fn(*xs) for xs in zip(*lists)]


def _unit_lower_inverse(labs, mk):
    ds = [jnp.where(mk["same16"], lab, 0.0) for lab in labs]
    xs = [jnp.where(mk["eye"], 1.0, 0.0) + d for d in ds]
    for _ in range(3):
        ds = _each(_dot_x3, ds, ds)
        xs = _each(lambda x, d: x + _dot_x3(x, d), xs, ds)
    for level in ("only32", "not32"):
        es = [jnp.where(mk[level], lab, 0.0) for lab in labs]
        exs = _each(_dot_x3, es, xs)
        xs = _each(lambda x, ex: x + _dot_x3(x, ex), xs, exs)
    return xs


WKV_GROUP = 4


def _wkv_chunk_group(rows, s_bds, mk):
    c = WKV_CHUNK
    stack = lambda x: jnp.concatenate([jnp.where(mk["head0"], x, 0.0), jnp.where(mk["head0"], 0.0, x)], axis=0)
    dup = lambda x: jnp.concatenate([x, x], axis=0)
    col = lambda i: [row[i] for row in rows]
    n = PAIR_LANES
    a_s, r_s, v_s = [_each(stack, col(i)) for i in (0, 1, 6)]
    bh2, kh2 = [_each(dup, col(i)) for i in (4, 5)]
    bk2 = _each(lambda bt, kt: jnp.concatenate([dup(bt), dup(kt)], axis=0), col(2), col(3))
    nt = lambda x, y: _dg_x3(x, y, _NT)
    tn = lambda x, y: _dg_x3(x, y, _TN)
    below = lambda x: jnp.where(mk["strict"], x, 0.0)
    upto = lambda x: jnp.where(mk["incl"], x, 0.0)
    heads = lambda x: jnp.where(mk["same_head"], x, 0.0)
    a_bk = _each(nt, a_s, bk2)
    lab = [below(x[:, :n]) for x in a_bk]
    lak = [below(x[:, n:]) for x in a_bk]
    nt1 = lambda x, y: lax.dot_general(x.astype(BF16), y.astype(BF16), _NT, preferred_element_type=F32)
    mm1 = lambda x, y: _dot(x.astype(BF16), y.astype(BF16))
    r_bk = _each(nt1, r_s, bk2)
    lrb = [upto(x[:, :n]) for x in r_bk]
    lrk = [upto(x[:, n:]) for x in r_bk]
    t_inv = _unit_lower_inverse(lab, mk)
    lakv = _each(_dot_x3, lak, v_s)
    ta_u = _each(lambda t, a, w: _dot_x3(t, jnp.concatenate([a, w], axis=1)), t_inv, a_s, lakv)
    ta_s = [x[:, :n] for x in ta_u]
    u_s = [x[:, n:] for x in ta_u]
    m_bd = _each(lambda m, w: heads(m) + jnp.where(mk["eye"], w, 0.0), _each(tn, ta_s, bh2), col(7))
    g_bd = _each(lambda g1, g2: heads(g1 + g2), _each(tn, u_s, bh2), _each(tn, v_s, kh2))
    lrb_tau = _each(mm1, lrb, ta_u)
    q_s = [x[:, :n] for x in lrb_tau]
    y0_s = _each(lambda x, p2: x[:, n:] + p2, lrb_tau, _each(mm1, lrk, v_s))
    q = _each(lambda rt, qs: rt + (qs[:c] + qs[c:]), col(1), q_s)
    y = _each(lambda yq, y0: yq + (y0[:c] + y0[c:]), _each(nt1, q, s_bds), y0_s)
    s_new = _each(lambda sm, g: sm + g, _each(_dot_x3, s_bds, m_bd), g_bd)
    return y, s_new


def _rwkv_seq_kernel(r_ref, k_ref, v_ref, lora_ref, w0_ref, wd_ref, a0_ref, wa_ref, wg_ref, kk_ref, ka_ref,
                     rk_ref, lnw_ref, lnb_ref, seg_ref, segt_ref, o_ref, sout_ref, s_scr, y_scr):
    c = pl.program_id(1)

    @pl.when(c == 0)
    def _():
        s_scr[...] = jnp.zeros(s_scr.shape, F32)

    r = r_ref[...]
    v = v_ref[...]
    seg = seg_ref[...]
    segt = segt_ref[...]
    prm = (w0_ref[...], wd_ref[...], a0_ref[...], wa_ref[...], wg_ref[...], kk_ref[...], ka_ref[...], rk_ref[...])
    decay, log_decay, k_mod, neg_kk, b, gate, bonus = _rwkv_prep(r, k_ref[...], v, lora_ref[...], prm, seg, segt)

    n = WKV_CHUNK
    tri = (lax.broadcasted_iota(jnp.int32, (n, n), 1) <= lax.broadcasted_iota(jnp.int32, (n, n), 0)).astype(BF16)
    ld_h, ld_m, ld_l = _split3(log_decay)
    cw = _dot(tri, ld_h) + (_dot(tri, ld_m) + _dot(tri, ld_l))
    cw_prev = cw - log_decay
    cw_last = cw[n - 1:n, :]
    e_prev = jnp.exp(cw_prev)
    e_inv = jnp.exp(-cw)
    e_rem = jnp.exp(cw_last - cw)
    at = neg_kk * e_prev
    rt = r * (e_prev * decay)
    bt = b * e_inv
    kt = k_mod * e_inv
    bh = b * e_rem
    kh = k_mod * e_rem
    w_last = jnp.exp(cw_last)

    mk = _wkv_masks()
    for p0 in range(0, N_PAIRS, WKV_GROUP):
        pairs = range(p0, p0 + WKV_GROUP)
        lanes = [slice(p * PAIR_LANES, (p + 1) * PAIR_LANES) for p in pairs]
        rows = [tuple(z[:, sl] for z in (at, rt, bt, kt, bh, kh, v, w_last)) for sl in lanes]
        ys, s_new = _wkv_chunk_group(rows, [s_scr[p] for p in pairs], mk)
        for p, sl, y, s in zip(pairs, lanes, ys, s_new):
            s_scr[p] = s
            y_scr[:, sl] = y

    out = (_group_norm_rows(y_scr[...], seg, segt) * lnw_ref[...] + lnb_ref[...] + bonus) * gate
    o_ref[...] = out.astype(BF16)

    @pl.when(c == pl.num_programs(1) - 1)
    def _():
        for p in range(N_PAIRS):
            sout_ref[0, 2 * p] = s_scr[p, 0:HEAD_DIM, 0:HEAD_DIM]
            sout_ref[0, 2 * p + 1] = s_scr[p, HEAD_DIM:, HEAD_DIM:]


def _rwkv_weights(w0, w_decay_up, a0, w_aaa_up, w_gate_up, k_k, k_a, r_k, ln_x_w, ln_x_b):
    row = lambda z: z.reshape(1, D_RWKV)
    pad = lambda z: jnp.pad(z, ((0, LORA_PAD - z.shape[0]), (0, 0)))
    head_of_col = jnp.arange(D_RWKV, dtype=jnp.int32) // HEAD_DIM
    seg = (head_of_col[:, None] == jnp.arange(LANES, dtype=jnp.int32)[None, :]).astype(BF16)
    return (row(w0), pad(w_decay_up), row(a0), pad(w_aaa_up), w_gate_up, row(k_k), row(k_a), row(r_k),
            row(ln_x_w), row(ln_x_b), seg, seg.T)


def _const_spec(shape, grid_rank):
    zeros = (0,) * len(shape)
    if grid_rank == 1:
        return pl.BlockSpec(shape, lambda i: zeros)
    return pl.BlockSpec(shape, lambda i, j: zeros)


def _rwkv_seq(hx, weights, *, batch, seq):
    tc = WKV_CHUNK
    nchunk = seq // tc
    rowmap = lambda cb: (lambda b, c: (b * nchunk + c, cb))
    in_specs = [pl.BlockSpec((tc, D_RWKV), rowmap(0)),
                pl.BlockSpec((tc, D_RWKV), rowmap(1)),
                pl.BlockSpec((tc, D_RWKV), rowmap(2)),
                pl.BlockSpec((tc, 2 * LORA_PAD + GATE_LORA), rowmap(COL_LORA // (2 * LORA_PAD + GATE_LORA)))]
    in_specs += [_const_spec(w.shape, 2) for w in weights]
    return pl.pallas_call(
        _rwkv_seq_kernel,
        grid=(batch, nchunk),
        in_specs=in_specs,
        out_specs=[pl.BlockSpec((tc, D_RWKV), lambda b, c: (b * nchunk + c, 0)),
                   pl.BlockSpec((1, RWKV_HEADS, HEAD_DIM, HEAD_DIM), lambda b, c: (b, 0, 0, 0))],
        out_shape=[jax.ShapeDtypeStruct((batch * seq, D_RWKV), BF16),
                   jax.ShapeDtypeStruct((batch, RWKV_HEADS, HEAD_DIM, HEAD_DIM), F32)],
        scratch_shapes=[pltpu.VMEM((N_PAIRS, PAIR_LANES, PAIR_LANES), F32), pltpu.VMEM((tc, D_RWKV), F32)],
        compiler_params=pltpu.CompilerParams(dimension_semantics=("arbitrary", "arbitrary"),
                                             vmem_limit_bytes=VMEM_LIMIT),
    )(hx, hx, hx, hx, *weights)


def _rwkv_tok_prep_kernel(r_ref, k_ref, v_ref, lora_ref, w0_ref, wd_ref, a0_ref, wa_ref, wg_ref, kk_ref, ka_ref,
                          rk_ref, seg_ref, segt_ref, w_o, k_o, a_o, b_o, g_o, bonus_o):
    prm = (w0_ref[...], wd_ref[...], a0_ref[...], wa_ref[...], wg_ref[...], kk_ref[...], ka_ref[...], rk_ref[...])
    decay, _, k_mod, neg_kk, b, gate, bonus = _rwkv_prep(r_ref[...], k_ref[...], v_ref[...], lora_ref[...], prm,
                                                          seg_ref[...], segt_ref[...])
    w_o[...] = decay
    k_o[...] = k_mod
    a_o[...] = neg_kk
    b_o[...] = b
    g_o[...] = gate
    bonus_o[...] = bonus


def _rwkv_tok_step_kernel(r_ref, w_ref, k_ref, v_ref, a_ref, b_ref, g_ref, bonus_ref, lnw_ref, lnb_ref, s0_ref,
                          o_ref, s_ref, *, tb):
    eye = (lax.broadcasted_iota(jnp.int32, (HEAD_DIM, HEAD_DIM), 0)
           == lax.broadcasted_iota(jnp.int32, (HEAD_DIM, HEAD_DIM), 1))

    def body(bi, carry):
        s = s0_ref[bi]
        sa = jnp.sum(s * a_ref[bi], axis=2, keepdims=True)
        v_col = jnp.sum(jnp.where(eye, v_ref[bi], 0.0), axis=2, keepdims=True)
        s = s * w_ref[bi] + sa * b_ref[bi] + v_col * k_ref[bi]
        s_ref[bi] = s
        y_col = jnp.sum(s * r_ref[bi], axis=2, keepdims=True)
        y = jnp.sum(jnp.where(eye, y_col, 0.0), axis=1, keepdims=True)
        o_ref[bi] = (_group_norm_head(y) * lnw_ref[0] + lnb_ref[0] + bonus_ref[bi]) * g_ref[bi]
        return carry

    lax.fori_loop(0, tb, body, 0)


def _rwkv_tok(hx, state, weights, *, tb):
    nb = hx.shape[0]
    (w0, wd, a0, wa, wg, k_k, k_a, r_k, ln_w, ln_b, seg, segt) = weights
    prep_w = (w0, wd, a0, wa, wg, k_k, k_a, r_k, seg, segt)
    colmap = lambda cb: (lambda i: (0, cb))
    in_specs = [pl.BlockSpec((nb, D_RWKV), colmap(0)),
                pl.BlockSpec((nb, D_RWKV), colmap(1)),
                pl.BlockSpec((nb, D_RWKV), colmap(2)),
                pl.BlockSpec((nb, 2 * LORA_PAD + GATE_LORA), colmap(COL_LORA // (2 * LORA_PAD + GATE_LORA)))]
    in_specs += [_const_spec(w.shape, 1) for w in prep_w]
    rows = jax.ShapeDtypeStruct((nb, D_RWKV), F32)
    prepped = pl.pallas_call(
        _rwkv_tok_prep_kernel,
        grid=(1,),
        in_specs=in_specs,
        out_specs=[pl.BlockSpec((nb, D_RWKV), lambda i: (0, 0))] * 6,
        out_shape=[rows] * 6,
        compiler_params=pltpu.CompilerParams(dimension_semantics=("arbitrary",),
                                             vmem_limit_bytes=VMEM_LIMIT),
    )(hx, hx, hx, hx, *prep_w)
    decay, k_mod, neg_kk, b, gate, bonus = prepped
    heads = lambda z: z.reshape(nb, RWKV_HEADS, 1, HEAD_DIM)
    vecs = [heads(z) for z in (hx[:, COL_R:COL_R + D_RWKV], decay, k_mod, hx[:, COL_V:COL_V + D_RWKV], neg_kk, b,
                               gate, bonus)]
    vec_spec = pl.BlockSpec((tb, RWKV_HEADS, 1, HEAD_DIM), lambda i: (i, 0, 0, 0))
    ln_spec = pl.BlockSpec((1, RWKV_HEADS, 1, HEAD_DIM), lambda i: (0, 0, 0, 0))
    state_spec = pl.BlockSpec((tb, RWKV_HEADS, HEAD_DIM, HEAD_DIM), lambda i: (i, 0, 0, 0))
    out, new_state = pl.pallas_call(
        functools.partial(_rwkv_tok_step_kernel, tb=tb),
        grid=(nb // tb,),
        in_specs=[vec_spec] * 8 + [ln_spec, ln_spec, state_spec],
        out_specs=[vec_spec, state_spec],
        out_shape=[jax.ShapeDtypeStruct((nb, RWKV_HEADS, 1, HEAD_DIM), F32),
                   jax.ShapeDtypeStruct(state.shape, F32)],
        compiler_params=pltpu.CompilerParams(dimension_semantics=("arbitrary",),
                                             vmem_limit_bytes=VMEM_LIMIT),
    )(*vecs, ln_w.reshape(1, RWKV_HEADS, 1, HEAD_DIM), ln_b.reshape(1, RWKV_HEADS, 1, HEAD_DIM), state)
    return out.reshape(nb, D_RWKV).astype(BF16), new_state


def _route(logits, bias):
    scores = _sigmoid(logits)
    sel = scores + bias
    rows = logits.shape[0]
    lane_i = lax.broadcasted_iota(jnp.int32, (rows, N_EXPERTS), 1)
    grp = lane_i // GROUP_SIZE
    lane = lane_i.astype(F32)
    neg_inf = -jnp.inf

    def take_max(x):
        m = jnp.max(x, axis=1, keepdims=True)
        idx = jnp.min(jnp.where(x == m, lane, float(N_EXPERTS)), axis=1, keepdims=True)
        return m, lane == idx

    gscore = []
    for g in range(N_GROUPS):
        x = jnp.where(grp == g, sel, neg_inf)
        m1, hit = take_max(x)
        m2 = jnp.max(jnp.where(hit, neg_inf, x), axis=1, keepdims=True)
        gscore.append(m1 + m2)
    emask = jnp.zeros((rows, N_EXPERTS), jnp.bool_)
    for g in range(N_GROUPS):
        ahead = jnp.zeros((rows, 1), jnp.int32)
        for g2 in range(N_GROUPS):
            if g2 == g:
                continue
            beats = (gscore[g2] >= gscore[g]) if g2 < g else (gscore[g2] > gscore[g])
            ahead = ahead + beats.astype(jnp.int32)
        emask = emask | ((grp == g) & (ahead < TOPK_GROUPS))
    x = jnp.where(emask, sel, neg_inf)
    chosen = jnp.zeros((rows, N_EXPERTS), jnp.bool_)
    slot_lane = lax.broadcasted_iota(jnp.int32, (rows, LANES), 1)
    top_idx = jnp.zeros((rows, LANES), F32)
    top_w = jnp.zeros((rows, LANES), F32)
    for kth in range(TOP_K):
        m = jnp.max(x, axis=1, keepdims=True)
        idx = jnp.min(jnp.where(x == m, lane, float(N_EXPERTS)), axis=1, keepdims=True)
        hit = lane == idx
        chosen = chosen | hit
        x = jnp.where(hit, neg_inf, x)
        top_idx = jnp.where(slot_lane == kth, idx, top_idx)
        top_w = jnp.where(slot_lane == kth, jnp.sum(jnp.where(hit, scores, 0.0), axis=1, keepdims=True), top_w)
    top_w = top_w / jnp.sum(top_w, axis=1, keepdims=True) * ROUTE_SCALE
    return top_idx.astype(jnp.int32), top_w, chosen.astype(F32)


def _out_proj_kernel(pool_p, rwkv_p, x_p, g1_p, sc_p, sh_p, pool_s, rwkv_s, x_s, g1_s, sc_s, sh_s, wp_ref, wr_ref,
                     npost_ref, npre_ref, wrt_ref, rb_ref, x1_ref, u2_ref, idx_ref, topw_ref, chosen_ref, *,
                     n_prompt_tiles):
    is_sample = pl.program_id(0) >= n_prompt_tiles
    pick = lambda s, p: jnp.where(is_sample, s, p)
    pool = pick(pool_s[...], pool_p[...])
    rwkv = pick(rwkv_s[...], rwkv_p[...])
    x = pick(x_s[...], x_p[...])
    m = _dot(pool, wp_ref[...]) + _dot(rwkv, wr_ref[...])
    x1 = x + pick(g1_s[0], g1_p[0]) * _rms(m, npost_ref[...])
    x1_ref[...] = x1
    u2 = _rms(x1, npre_ref[...]) * (1.0 + pick(sc_s[0], sc_p[0])) + pick(sh_s[0], sh_p[0])
    u2_ref[...] = u2
    top_idx, top_w, chosen = _route(_dot_x3(u2, wrt_ref[...]), rb_ref[...])
    idx_ref[...] = top_idx
    topw_ref[...] = top_w
    chosen_ref[...] = chosen


def _out_proj(prompt, sample, w_out_bf16, n_post, n_pre, w_router, router_bias, *, tm, tiles_per_seq):
    mp, d = prompt[2].shape
    ms = sample[2].shape[0]
    npt, nst = mp // tm, ms // tm
    p_tile = lambda i: jnp.minimum(i, npt - 1)
    s_tile = lambda i: jnp.maximum(i - npt, 0)
    p_row = lambda width: pl.BlockSpec((tm, width), lambda i: (p_tile(i), 0))
    s_row = lambda width: pl.BlockSpec((tm, width), lambda i: (s_tile(i), 0))
    p_mod = pl.BlockSpec((1, 1, d), lambda i: (p_tile(i) // tiles_per_seq, 0, 0))
    s_mod = pl.BlockSpec((1, tm, d), lambda i: (s_tile(i), 0, 0))
    row = lambda width: pl.BlockSpec((tm, width), lambda i: (i, 0))
    n_all = mp + ms
    return pl.pallas_call(
        functools.partial(_out_proj_kernel, n_prompt_tiles=npt),
        grid=(npt + nst,),
        in_specs=[p_row(D_POOL), p_row(D_RWKV), p_row(d), p_mod, p_mod, p_mod,
                  s_row(D_POOL), s_row(D_RWKV), s_row(d), s_mod, s_mod, s_mod,
                  _const_spec((D_POOL, d), 1), _const_spec((D_RWKV, d), 1),
                  _const_spec((1, d), 1), _const_spec((1, d), 1),
                  _const_spec((d, N_EXPERTS), 1), _const_spec((1, N_EXPERTS), 1)],
        out_specs=[row(d), row(d), row(LANES), row(LANES), row(N_EXPERTS)],
        out_shape=[jax.ShapeDtypeStruct((n_all, d), F32), jax.ShapeDtypeStruct((n_all, d), F32),
                   jax.ShapeDtypeStruct((n_all, LANES), jnp.int32), jax.ShapeDtypeStruct((n_all, LANES), F32),
                   jax.ShapeDtypeStruct((n_all, N_EXPERTS), F32)],
        compiler_params=pltpu.CompilerParams(dimension_semantics=("arbitrary",),
                                             vmem_limit_bytes=VMEM_LIMIT),
    )(*prompt, *sample, w_out_bf16[:D_POOL], w_out_bf16[D_POOL:],
      n_post.reshape(1, d), n_pre.reshape(1, d), w_router, router_bias.reshape(1, N_EXPERTS))


MOE_TM = 384


ROW_GROUP = 8


def _experts_kernel(te_ref, fresh_ref, wslot_ref, nexte_ref, groups_ref, nused_ref, idx_ref, idx_next_ref, ws_ref,
                    x_hbm, wg_hbm, wu_hbm, wd_hbm, ys_ref, wg_buf, wu_buf, wd_buf, g_bf, u_bf, d_bf, x_buf,
                    w_sems, x_sems):
    t = pl.program_id(0)
    n_used = nused_ref[0]

    def weight_copies(e, slot):
        return (pltpu.make_async_copy(wg_hbm.at[e], wg_buf.at[slot], w_sems.at[slot]),
                pltpu.make_async_copy(wu_hbm.at[e], wu_buf.at[slot], w_sems.at[slot]),
                pltpu.make_async_copy(wd_hbm.at[e], wd_buf.at[slot], w_sems.at[slot]))

    def start_rows(table_ref, slot, n_groups):
        def body(g, carry):
            for r in range(ROW_GROUP):
                j = g * ROW_GROUP + r
                pltpu.make_async_copy(x_hbm.at[pl.ds(table_ref[0, 0, j], 1)], x_buf.at[slot, pl.ds(j, 1)],
                                      x_sems.at[slot]).start()
            return carry
        lax.fori_loop(0, n_groups, body, 0)

    @pl.when(t == 0)
    def _():
        for cp in weight_copies(te_ref[0], 0):
            cp.start(priority=1)
        x_buf[...] = jnp.zeros(x_buf.shape, F32)
        start_rows(idx_ref, 0, groups_ref[0])

    @pl.when(t + 1 < n_used)
    def _():
        start_rows(idx_next_ref, (t + 1) % 2, groups_ref[t + 1])

    @pl.when(jnp.logical_and(t < n_used, fresh_ref[t] == 1))
    def _():
        slot = wslot_ref[t]
        for cp in weight_copies(te_ref[t], slot):
            cp.wait()

        @pl.when(nexte_ref[t] >= 0)
        def _():
            for cp in weight_copies(nexte_ref[t], 1 - slot):
                cp.start(priority=1)

        g_bf[...] = wg_buf[slot].astype(BF16)
        u_bf[...] = wu_buf[slot].astype(BF16)
        d_bf[...] = wd_buf[slot].astype(BF16)

    @pl.when(t < n_used)
    def _():
        cur = t % 2
        n_rows = pl.multiple_of(groups_ref[t] * ROW_GROUP, ROW_GROUP)
        pltpu.make_async_copy(x_hbm.at[pl.ds(0, n_rows)], x_buf.at[cur, pl.ds(0, n_rows)], x_sems.at[cur]).wait()
        x = x_buf[cur].astype(BF16)
        h = _silu(_dot(x, g_bf[...])) * _dot(x, u_bf[...])
        diag = (lax.broadcasted_iota(jnp.int32, (MOE_TM, MOE_TM), 0)
                == lax.broadcasted_iota(jnp.int32, (MOE_TM, MOE_TM), 1))
        w_col = jnp.sum(jnp.where(diag, ws_ref[0], 0.0), axis=1, keepdims=True)
        ys_ref[...] = _dot(h.astype(BF16), d_bf[...]) * w_col

    @pl.when(t >= n_used)
    def _():
        ys_ref[...] = jnp.zeros(ys_ref.shape, F32)


def _experts(tables, x_all, exp_gate, exp_up, exp_down):
    tile_expert, fresh, wslot, next_e, row_groups, n_used, slot_tok, slot_w = tables
    d = x_all.shape[1]
    n_tiles = tile_expert.shape[0]
    n_slots = n_tiles * MOE_TM
    idx_block = (1, 1, MOE_TM)
    any_spec = pl.BlockSpec(memory_space=pl.ANY)
    live = lambda t, nused: jnp.where(t < nused[0], t, n_tiles - 1)
    grid_spec = pltpu.PrefetchScalarGridSpec(
        num_scalar_prefetch=6,
        grid=(n_tiles,),
        in_specs=[pl.BlockSpec(idx_block, lambda t, *s: (live(t, s[5]), 0, 0), memory_space=pltpu.SMEM),
                  pl.BlockSpec(idx_block, lambda t, *s: (live(t + 1, s[5]), 0, 0), memory_space=pltpu.SMEM),
                  pl.BlockSpec(idx_block, lambda t, *s: (live(t, s[5]), 0, 0)),
                  any_spec, any_spec, any_spec, any_spec],
        out_specs=pl.BlockSpec((MOE_TM, d), lambda t, *s: (live(t, s[5]), 0)),
        scratch_shapes=[pltpu.VMEM((2, d, D_EXPERT), F32), pltpu.VMEM((2, d, D_EXPERT), F32),
                        pltpu.VMEM((2, D_EXPERT, d), F32),
                        pltpu.VMEM((d, D_EXPERT), BF16), pltpu.VMEM((d, D_EXPERT), BF16),
                        pltpu.VMEM((D_EXPERT, d), BF16),
                        pltpu.VMEM((2, MOE_TM, d), F32),
                        pltpu.SemaphoreType.DMA((2,)), pltpu.SemaphoreType.DMA((2,))],
    )
    idx_tiles = slot_tok.reshape(n_tiles, 1, MOE_TM)
    return pl.pallas_call(
        _experts_kernel,
        grid_spec=grid_spec,
        out_shape=jax.ShapeDtypeStruct((n_slots, d), F32),
        compiler_params=pltpu.CompilerParams(dimension_semantics=("arbitrary",),
                                             vmem_limit_bytes=VMEM_LIMIT),
    )(tile_expert, fresh, wslot, next_e, row_groups, n_used, idx_tiles, idx_tiles,
      slot_w.reshape(n_tiles, 1, MOE_TM), x_all, exp_gate, exp_up, exp_down)


ASSIGN_BITS = 17


def _dispatch(chosen, top_idx, top_w):
    n = chosen.shape[0]
    n_assign = n * TOP_K
    low_mask = (1 << ASSIGN_BITS) - 1
    assert n_assign + MOE_TM <= low_mask and N_EXPERTS << ASSIGN_BITS < 2 ** 31
    n_tiles = -(-n_assign // MOE_TM) + N_EXPERTS
    n_slots = n_tiles * MOE_TM
    sel = chosen.astype(jnp.int32)
    counts = jnp.sum(sel, axis=0)
    rank = jnp.cumsum(sel, axis=0) - sel
    tiles_e = (counts + MOE_TM - 1) // MOE_TM
    tile_end = jnp.cumsum(tiles_e)
    slot = (tile_end - tiles_e)[None, :] * MOE_TM + rank
    expert_ids = jnp.arange(N_EXPERTS, dtype=jnp.int32)
    slot_of_pair = jnp.stack([jnp.sum(jnp.where(top_idx[:, k:k + 1] == expert_ids[None, :], slot, 0), axis=1)
                              for k in range(TOP_K)])
    shift = 1 << ASSIGN_BITS
    int_max = jnp.iinfo(jnp.int32).max
    real_key = top_idx.reshape(-1) * shift + jnp.arange(n_assign, dtype=jnp.int32)
    pad_j = jnp.arange(MOE_TM - 1, dtype=jnp.int32)[None, :]
    pad_key = jnp.where(pad_j < (tiles_e * MOE_TM - counts)[:, None],
                        expert_ids[:, None] * shift + n_assign + pad_j, int_max)
    n_rest = n_slots - n_assign
    keys = jnp.concatenate([real_key, pad_key.reshape(-1),
                            jnp.full((n_rest - pad_key.size,), int_max, jnp.int32)])
    weights = jnp.concatenate([top_w.reshape(-1), jnp.zeros((n_rest,), F32)])
    sorted_key, sorted_w = lax.sort((keys, weights), num_keys=1)
    pair = sorted_key & low_mask
    is_real = pair < n_assign
    slot_tok = jnp.where(is_real, pair // TOP_K, 0)
    slot_w = jnp.where(is_real, sorted_w, 0.0)
    n_used = tile_end[-1]
    tile_ids = jnp.minimum(jnp.arange(n_tiles, dtype=jnp.int32), n_used - 1)
    tile_expert = jnp.minimum(jnp.searchsorted(tile_end, tile_ids, side="right"), N_EXPERTS - 1).astype(jnp.int32)
    in_use = jnp.arange(n_tiles, dtype=jnp.int32) < n_used
    prev_expert = jnp.concatenate([jnp.full((1,), -1, jnp.int32), tile_expert[:-1]])
    fresh = jnp.logical_and(in_use, tile_expert != prev_expert).astype(jnp.int32)
    wslot = (jnp.cumsum(fresh) - 1) % 2
    later = jnp.where(counts > 0, expert_ids, N_EXPERTS)
    next_used = jnp.concatenate([lax.cummin(later, reverse=True)[1:], jnp.full((1,), N_EXPERTS, jnp.int32)])
    next_used = jnp.where(next_used >= N_EXPERTS, -1, next_used)
    tile_in_expert = jnp.arange(n_tiles, dtype=jnp.int32) - (tile_end - tiles_e)[tile_expert]
    rows_in_tile = jnp.clip(counts[tile_expert] - tile_in_expert * MOE_TM, 0, MOE_TM)
    row_groups = jnp.where(in_use, (rows_in_tile + ROW_GROUP - 1) // ROW_GROUP, 0)
    tables = (tile_expert, fresh, wslot.astype(jnp.int32), next_used[tile_expert].astype(jnp.int32),
              row_groups.astype(jnp.int32), n_used.reshape(1).astype(jnp.int32), slot_tok.astype(jnp.int32), slot_w)
    return slot_of_pair, tables


def _final_kernel(idx_ref, idx_next_ref, u2_ref, ys_hbm, x1_ref, g2_ref, sg_ref, su_ref, sd_ref, npost_ref, o_ref,
                  rows_buf, sems, *, tm):
    i = pl.program_id(0)
    n_rows = TOP_K * tm

    def start_gather(table_ref, buf_slot):
        def body(j, carry):
            pltpu.make_async_copy(ys_hbm.at[pl.ds(table_ref[0, 0, j], 1)], rows_buf.at[buf_slot, pl.ds(j, 1)],
                                  sems.at[buf_slot]).start()
            return carry
        lax.fori_loop(0, n_rows, body, 0, unroll=8)

    @pl.when(i == 0)
    def _():
        start_gather(idx_ref, 0)

    @pl.when(i + 1 < pl.num_programs(0))
    def _():
        start_gather(idx_next_ref, (i + 1) % 2)

    u2 = u2_ref[...].astype(BF16)
    h = _silu(_dot(u2, sg_ref[...])) * _dot(u2, su_ref[...])
    shared =_dot(h.astype(BF16), sd_ref[...])
    cur = i % 2
    pltpu.make_async_copy(ys_hbm.at[pl.ds(0, n_rows)], rows_buf.at[cur], sems.at[cur]).wait()
    routed = rows_buf[cur, 0:tm]
    for kth in range(1, TOP_K):
        routed = routed + rows_buf[cur, kth * tm:(kth + 1) * tm]
    o_ref[...] = x1_ref[...] + g2_ref[0] * _rms(routed + shared, npost_ref[...])


def _final(u2, ys, slot_tiles, x1, g2, sg, su, sd, n_post, *, rows, tm, tiles_per_seq, tile_offset):
    m, d = rows, x1.shape[1]
    nt = m // tm
    if tiles_per_seq is None:
        mod_spec = pl.BlockSpec((1, tm, d), lambda i: (i, 0, 0))
    else:
        mod_spec = pl.BlockSpec((1, 1, d), lambda i: (i // tiles_per_seq, 0, 0))
    row_in = pl.BlockSpec((tm, d), lambda i: (i + tile_offset, 0))
    row = pl.BlockSpec((tm, d), lambda i: (i, 0))
    idx_block = (1, 1, TOP_K * tm)
    return pl.pallas_call(
        functools.partial(_final_kernel, tm=tm),
        grid=(nt,),
        in_specs=[pl.BlockSpec(idx_block, lambda i: (i + tile_offset, 0, 0), memory_space=pltpu.SMEM),
                  pl.BlockSpec(idx_block, lambda i: (jnp.minimum(i + 1, nt - 1) + tile_offset, 0, 0),
                               memory_space=pltpu.SMEM),
                  row_in, pl.BlockSpec(memory_space=pl.ANY), row_in, mod_spec,
                  _const_spec(sg.shape, 1), _const_spec(su.shape, 1), _const_spec(sd.shape, 1),
                  _const_spec((1, d), 1)],
        out_specs=row,
        out_shape=jax.ShapeDtypeStruct((m, d), F32),
        scratch_shapes=[pltpu.VMEM((2, TOP_K * tm, d), F32), pltpu.SemaphoreType.DMA((2,))],
        compiler_params=pltpu.CompilerParams(dimension_semantics=("arbitrary",),
                                             vmem_limit_bytes=VMEM_LIMIT),
    )(slot_tiles, slot_tiles, u2, ys, x1, g2, sg, su, sd, n_post.reshape(1, d))


def _pad_cols_in(w_in, mu_shift):
    d = w_in.shape[0]
    hp = w_in[:, :D_POOL]
    hr = w_in[:, D_POOL:]
    cuts = [3 * D_RWKV, 3 * D_RWKV + DECAY_LORA, 3 * D_RWKV + DECAY_LORA + AAA_LORA]
    rkv, wl, al, gl = jnp.split(hr, cuts, axis=1)
    zpad = lambda z: jnp.pad(z, ((0, 0), (0, LORA_PAD - z.shape[1])))
    w = jnp.concatenate([rkv, hp, zpad(wl), zpad(al), gl], axis=1).astype(BF16)
    mu_rkv, mu_wl, mu_al, mu_gl = jnp.split(mu_shift, cuts)
    zp1 = lambda z: jnp.pad(z, (0, LORA_PAD - z.shape[0]))
    mu = jnp.concatenate([mu_rkv, jnp.zeros((D_POOL,), F32), zp1(mu_wl), zp1(mu_al), mu_gl])
    assert w.shape == (d, N_IN_PAD) and mu.shape == (N_IN_PAD,)
    return w, mu


def kernel(x_prompt, x_sample, c_prompt, c_sample, state_shift, state_pool, state_wkv, w_ada, b_ada, norm_pre_mix, norm_post_mix, norm_pre_ffn, norm_post_ffn, w_in, mu_shift, pool_w, pool_scale, w0, w_decay_up, a0, w_aaa_up, w_gate_up, k_k, k_a, r_k, ln_x_w, ln_x_b, w_out, w_router, router_bias, exp_gate, exp_up, exp_down, sh_gate, sh_up, sh_down):
    bp, seq, d = x_prompt.shape
    bs = x_sample.shape[0]
    assert w_ada.shape[0] == 1 and x_sample.shape[1] == 1
    np_rows = bp * seq

    c_all = jnp.concatenate([c_prompt, c_sample], axis=0)
    pad_rows = (-c_all.shape[0]) % 16
    ada = _ada(jnp.pad(c_all, ((0, pad_rows), (0, 0))), w_ada[0], b_ada[0])
    sh1, sc1, g1, sh2, sc2, g2 = jnp.split(ada, 6, axis=-1)
    pr = lambda z: z[:bp].reshape(bp, 1, d)
    TS = min(128, bs)
    sr = lambda z: z[bp:bp + bs].reshape(bs // TS, TS, d)

    w_in_p, mu_p = _pad_cols_in(w_in[0], mu_shift[0])
    xp = x_prompt.reshape(np_rows, d)
    xs = x_sample.reshape(bs, d)

    TM_IN = min(1024, seq)
    hx_p, u_tail = _in_proj(xp, norm_pre_mix[0], pr(sc1), pr(sh1), w_in_p, mu_p, tm=TM_IN,
                            tiles_per_seq=seq // TM_IN)
    hx_s, u_s = _in_proj(xs, norm_pre_mix[0], sr(sc1), sr(sh1), w_in_p, mu_p, tm=TS, tiles_per_seq=1,
                         prev=state_shift[0])
    new_shift_prompt = u_tail.reshape(bp, seq // TM_IN, 8, d)[:, -1, -1][None]
    new_shift_sample = u_s[None]

    pool_w_b = pool_w[0].astype(BF16)
    TC_POOL = min(512, seq)
    pool_p = _pool(hx_p, COL_POOL // D_POOL, pool_w_b, pool_scale[0], tc=TC_POOL, tiles_per_seq=seq // TC_POOL,
                   full_count=False)
    hp_p = hx_p[:, COL_POOL:COL_POOL + D_POOL].reshape(bp, seq, D_POOL)
    new_pool_prompt = hp_p[:, seq - POOL_BUF:][None]
    hp_s = hx_s[:, COL_POOL:COL_POOL + D_POOL]
    ext_s = jnp.concatenate([state_pool[0], hp_s[:, None, :]], axis=1)
    assert PAST_LEN + 1 >= max(POOL_WINDOWS)
    pool_s = _pool(ext_s.reshape(bs * (POOL_BUF + 1), D_POOL), 0, pool_w_b, pool_scale[0],
                   tc=bs * (POOL_BUF + 1), tiles_per_seq=1, full_count=True)
    pool_s = pool_s.reshape(bs, POOL_BUF + 1, D_POOL)[:, -1]
    new_pool_sample = ext_s[:, 1:][None]

    rw = _rwkv_weights(w0[0], w_decay_up[0], a0[0], w_aaa_up[0], w_gate_up[0], k_k[0], k_a[0], r_k[0],
                       ln_x_w[0], ln_x_b[0])
    rwkv_p, wkv_p = _rwkv_seq(hx_p, rw, batch=bp, seq=seq)
    rwkv_s, wkv_s = _rwkv_tok(hx_s, state_wkv[0], rw, tb=8)

    w_out_b = w_out[0].astype(BF16)
    assert np_rows % TS == 0 and seq % TS == 0
    x1_all, u2_all, idx_all, topw_all, chosen_all = _out_proj(
        (pool_p, rwkv_p, xp, pr(g1), pr(sc2), pr(sh2)), (pool_s, rwkv_s, xs, sr(g1), sr(sc2), sr(sh2)), w_out_b,
        norm_post_mix[0], norm_pre_ffn[0], w_router[0], router_bias[0], tm=TS, tiles_per_seq=seq // TS)

    n_all = np_rows + bs
    slot_of_pair, tables = _dispatch(chosen_all, idx_all[:, :TOP_K], topw_all[:, :TOP_K])
    ys = _experts(tables, u2_all, exp_gate[0], exp_up[0], exp_down[0])

    slot_tiles = slot_of_pair.reshape(TOP_K, n_all // TS, TS).transpose(1, 0, 2).reshape(n_all // TS, 1, TOP_K * TS)
    sg, su, sd = sh_gate[0].astype(BF16), sh_up[0].astype(BF16), sh_down[0].astype(BF16)
    y_p = _final(u2_all, ys, slot_tiles, x1_all, pr(g2), sg, su, sd, norm_post_ffn[0], rows=np_rows, tm=TS,
                 tiles_per_seq=seq // TS, tile_offset=0)
    y_s = _final(u2_all, ys, slot_tiles, x1_all, sr(g2), sg, su, sd, norm_post_ffn[0], rows=bs, tm=TS,
                 tiles_per_seq=None, tile_offset=np_rows // TS)

    return (y_p.reshape(bp, seq, d), y_s.reshape(bs, 1, d), new_shift_prompt, new_pool_prompt, wkv_p[None],
            new_shift_sample, new_pool_sample, wkv_s[None])
```

```python
import functools

import jax
import jax.numpy as jnp
from jax import lax
from jax.experimental import pallas as pl
from jax.experimental.pallas import tpu as pltpu

F32 = jnp.float32
BF16 = jnp.bfloat16

D_MODEL = 2048
D_POOL = 512
POOL_WINDOWS = (2, 4, 8, 16)
POOL_CH = 128
POOL_BUF = 15
D_RWKV = 1536
HEAD_DIM = 64
RWKV_HEADS = 24
DECAY_LORA = 96
AAA_LORA = 96
GATE_LORA = 256
LORA_PAD = 128
GN_EPS = 64e-5
N_EXPERTS = 256
TOP_K = 8
N_GROUPS = 8
GROUP_SIZE = N_EXPERTS // N_GROUPS
TOPK_GROUPS = 4
D_EXPERT = 512
ROUTE_SCALE = 2.5
EPS = 1e-6
PAST_LEN = 16384

COL_R, COL_K, COL_V = 0, D_RWKV, 2 * D_RWKV
COL_POOL = 3 * D_RWKV
COL_LORA = COL_POOL + D_POOL
N_IN_PAD = COL_LORA + 2 * LORA_PAD + GATE_LORA
LANES = 128
VMEM_LIMIT = 56 * 1024 * 1024


def _dot(a, b):
    return jnp.dot(a, b, preferred_element_type=F32)


def _split2(x):
    hi = x.astype(BF16)
    lo = (x - hi.astype(F32)).astype(BF16)
    return hi, lo


def _split3(x):
    hi = x.astype(BF16)
    r = x - hi.astype(F32)
    mid = r.astype(BF16)
    lo = (r - mid.astype(F32)).astype(BF16)
    return hi, mid, lo


def _dot_x3(a, b):
    ah, al = _split2(a)
    bh, bl = _split2(b)
    return _dot(ah, bh) + (_dot(ah, bl) + _dot(al, bh))


def _dot_exact_rhs(a, b_bf16):
    h, m, l = _split3(a)
    return _dot(h, b_bf16) + (_dot(m, b_bf16) + _dot(l, b_bf16))


def _sigmoid(x):
    return 1.0 / (1.0 + jnp.exp(-x))


def _silu(x):
    return x * _sigmoid(x)


def _rms(x, g):
    return x * lax.rsqrt(jnp.mean(x * x, axis=-1, keepdims=True) + EPS) * g


def _ada_kernel(c_ref, w_ref, b_ref, o_ref):
    o_ref[...] = _dot_x3(_silu(c_ref[...]), w_ref[...]) + b_ref[...]


def _ada(c_all, w_ada, b_ada):
    m, d = c_all.shape
    n = w_ada.shape[1]
    tn = 512
    return pl.pallas_call(
        _ada_kernel,
        grid=(n // tn,),
        in_specs=[pl.BlockSpec((m, d), lambda j: (0, 0)),
                  pl.BlockSpec((d, tn), lambda j: (0, j)),
                  pl.BlockSpec((1, tn), lambda j: (0, j))],
        out_specs=pl.BlockSpec((m, tn), lambda j: (0, j)),
        out_shape=jax.ShapeDtypeStruct((m, n), F32),
        compiler_params=pltpu.CompilerParams(dimension_semantics=("arbitrary",),
                                             vmem_limit_bytes=VMEM_LIMIT),
    )(c_all, w_ada, b_ada.reshape(1, n))


def _in_proj_kernel(*refs, tiles_per_seq, explicit_prev, tm):
    if explicit_prev:
        x_ref, g_ref, sc_ref, sh_ref, prev_ref, w_ref, mu_ref, hx_ref, u_ref, ub_scr = refs
    else:
        x_ref, g_ref, sc_ref, sh_ref, w_ref, mu_ref, hx_ref, u_ref, ub_scr, carry_scr = refs
    i = pl.program_id(0)
    j = pl.program_id(1)

    @pl.when(j == 0)
    def _():
        u = _rms(x_ref[...], g_ref[...])
        u = u * (1.0 + sc_ref[0]) + sh_ref[0]
        ub_scr[...] = u.astype(BF16)
        if explicit_prev:
            u_ref[...] = u
        else:
            u_ref[...] = u[tm - 8:, :]

    w = w_ref[...]
    h = _dot(ub_scr[...], w)
    if explicit_prev:
        hp = _dot(prev_ref[...].astype(BF16), w)
    else:
        @pl.when(i == 0)
        def _():
            carry_scr[j] = jnp.zeros(carry_scr.shape[1:], F32)

        first = jnp.where(i % tiles_per_seq == 0, 0.0, carry_scr[j, 0:1, :])
        row0 = lax.broadcasted_iota(jnp.int32, h.shape, 0) == 0
        hp = jnp.where(row0, first, pltpu.roll(h, 1, axis=0))
        carry_scr[j, 0:1, :] = h[tm - 1:tm, :]
    hx_ref[...] = h + (hp - h) * mu_ref[...]


def _in_proj(x, gamma, sc, sh, w_bf16, mu_pad, *, tm, tiles_per_seq, prev=None):
    m, d = x.shape
    n = w_bf16.shape[1]
    tn = 512
    nt = n // tn
    explicit_prev = prev is not None
    if explicit_prev:
        mod_spec = pl.BlockSpec((1, tm, d), lambda i, j: (i, 0, 0))
    else:
        mod_spec = pl.BlockSpec((1, 1, d), lambda i, j: (i // tiles_per_seq, 0, 0))
    in_specs = [pl.BlockSpec((tm, d), lambda i, j: (i, 0)),
                pl.BlockSpec((1, d), lambda i, j: (0, 0)),
                mod_spec, mod_spec]
    args = [x, gamma.reshape(1, d), sc, sh]
    if explicit_prev:
        in_specs.append(pl.BlockSpec((tm, d), lambda i, j: (i, 0)))
        args.append(prev)
    in_specs += [pl.BlockSpec((d, tn), lambda i, j: (0, j)),
                 pl.BlockSpec((1, tn), lambda i, j: (0, j))]
    args += [w_bf16, mu_pad.reshape(1, n)]
    scratch = [pltpu.VMEM((tm, d), BF16)]
    if explicit_prev:
        u_shape, u_spec = (m, d), pl.BlockSpec((tm, d), lambda i, j: (i, 0))
    else:
        u_shape, u_spec = (m // tm * 8, d), pl.BlockSpec((8, d), lambda i, j: (i, 0))
        scratch.append(pltpu.VMEM((nt, 8, tn), F32))
    return pl.pallas_call(
        functools.partial(_in_proj_kernel, tiles_per_seq=tiles_per_seq, explicit_prev=explicit_prev, tm=tm),
        grid=(m // tm, nt),
        in_specs=in_specs,
        out_specs=[pl.BlockSpec((tm, tn), lambda i, j: (i, j)), u_spec],
        out_shape=[jax.ShapeDtypeStruct((m, n), F32), jax.ShapeDtypeStruct(u_shape, F32)],
        scratch_shapes=scratch,
        compiler_params=pltpu.CompilerParams(dimension_semantics=("arbitrary", "arbitrary"),
                                             vmem_limit_bytes=VMEM_LIMIT),
    )(*args)


POOL_HALO = 16


def _pool_kernel(p_ref, pw_ref, ps_ref, o_ref, ext_scr, *, tiles_per_seq, full_count, tc):
    i = pl.program_id(0)

    @pl.when(i % tiles_per_seq == 0)
    def _():
        ext_scr[0:POOL_HALO, :] = jnp.zeros((POOL_HALO, D_POOL), F32)

    p = p_ref[...]
    ext_scr[POOL_HALO:POOL_HALO + tc, :] = p
    pos = (i % tiles_per_seq) * tc + lax.broadcasted_iota(jnp.int32, (tc, 1), 0)
    outs = []
    for gi, win in enumerate(POOL_WINDOWS):
        lo = gi * POOL_CH
        pg = p[:, lo:lo + POOL_CH]
        acc = pg
        for s in range(1, win):
            acc = acc + ext_scr[POOL_HALO - s:POOL_HALO - s + tc, lo:lo + POOL_CH]
        if full_count:
            dgi = acc / float(win) - pg
        else:
            cnt = jnp.minimum(pos + 1, win).astype(F32)
            dgi = acc / cnt - pg
        outs.append(_dot(dgi.astype(BF16), pw_ref[gi]))
    y = jnp.concatenate(outs, axis=-1) * ps_ref[...]
    o_ref[...] = y.astype(BF16)
    ext_scr[0:POOL_HALO, :] = ext_scr[tc:tc + POOL_HALO, :]


def _pool(src, col_block, pool_w_bf16, pool_scale, *, tc, tiles_per_seq, full_count):
    m = src.shape[0]
    return pl.pallas_call(
        functools.partial(_pool_kernel, tiles_per_seq=tiles_per_seq, full_count=full_count, tc=tc),
        grid=(m // tc,),
        in_specs=[pl.BlockSpec((tc, D_POOL), lambda i: (i, col_block)),
                  pl.BlockSpec((len(POOL_WINDOWS), POOL_CH, POOL_CH), lambda i: (0, 0, 0)),
                  pl.BlockSpec((1, D_POOL), lambda i: (0, 0))],
        out_specs=pl.BlockSpec((tc, D_POOL), lambda i: (i, 0)),
        out_shape=jax.ShapeDtypeStruct((m, D_POOL), BF16),
        scratch_shapes=[pltpu.VMEM((POOL_HALO + tc, D_POOL), F32)],
        compiler_params=pltpu.CompilerParams(dimension_semantics=("arbitrary",),
                                             vmem_limit_bytes=VMEM_LIMIT),
    )(src, pool_w_bf16, pool_scale.reshape(1, D_POOL))


def _softplus(z):
    return jnp.maximum(z, 0.0) + jnp.log1p(jnp.exp(-jnp.abs(z)))


def _rwkv_prep(r, k, v, lora, prm, seg, segt):
    (w0, wd, a0, wa, wg, k_k, k_a, r_k) = prm
    wl = lora[:, 0:LORA_PAD]
    al = lora[:, LORA_PAD:2 * LORA_PAD]
    gl = lora[:, 2 * LORA_PAD:]
    logw = -_softplus(-(w0 + _dot_x3(jnp.tanh(wl), wd))) - 0.5
    log_decay = -jnp.exp(logw)
    decay = jnp.exp(log_decay)
    a = _sigmoid(a0 + _dot_x3(al, wa))
    gate = _dot(_sigmoid(gl).astype(BF16), wg.astype(BF16))
    kk = k * k_k
    ss = _dot_exact_rhs(kk * kk, seg)
    inv = 1.0 / jnp.maximum(jnp.sqrt(ss), 1e-12)
    kk = kk * _dot_exact_rhs(inv, segt)
    k_mod = k * (1.0 + (a - 1.0) * k_a)
    bonus = _dot_exact_rhs(_dot_exact_rhs(r * k_mod * r_k, seg), segt) * v
    return decay, log_decay, k_mod, -kk, kk * a, gate, bonus


def _group_norm_head(y):
    mu = jnp.mean(y, axis=-1, keepdims=True)
    yc = y - mu
    var = jnp.mean(yc * yc, axis=-1, keepdims=True)
    return yc * lax.rsqrt(var + GN_EPS)


def _group_norm_rows(y, seg, segt):
    inv_n = 1.0 / HEAD_DIM
    mu = _dot_exact_rhs(_dot_exact_rhs(y, seg) * inv_n, segt)
    yc = y - mu
    var = _dot_exact_rhs(yc * yc, seg) * inv_n
    return yc * _dot_exact_rhs(lax.rsqrt(var + GN_EPS), segt)


WKV_CHUNK = HEAD_DIM
PAIR_LANES = 2 * HEAD_DIM
N_PAIRS = RWKV_HEADS // 2
_NT = (((1,), (1,)), ((), ()))
_TN = (((0,), (0,)), ((), ()))


def _dg_x3(a, b, dims):
    ah, al = _split2(a)
    bh, bl = _split2(b)
    dg = lambda x, y: lax.dot_general(x, y, dims, preferred_element_type=F32)
    return dg(ah, bh) + (dg(ah, bl) + dg(al, bh))


def _wkv_masks():
    n = PAIR_LANES
    ri = lax.broadcasted_iota(jnp.int32, (n, n), 0)
    ci = lax.broadcasted_iota(jnp.int32, (n, n), 1)
    same_head = (ri // HEAD_DIM) == (ci // HEAD_DIM)
    same16 = (ri // 16) == (ci // 16)
    same32 = (ri // 32) == (ci // 32)
    return dict(same_head=same_head, strict=same_head & (ci < ri), incl=same_head & (ci <= ri), eye=ri == ci,
                same16=same16, only32=same32 & jnp.logical_not(same16), not32=jnp.logical_not(same32),
                head0=lax.broadcasted_iota(jnp.int32, (1, n), 1) < HEAD_DIM)


def _each(fn, *lists):
    return [fn(*xs) for xs in zip(*lists)]


def _unit_lower_inverse(labs, mk):
    ds = [jnp.where(mk["same16"], lab, 0.0) for lab in labs]
    xs = [jnp.where(mk["eye"], 1.0, 0.0) + d for d in ds]
    for _ in range(3):
        ds = _each(_dot_x3, ds, ds)
        xs = _each(lambda x, d: x + _dot_x3(x, d), xs, ds)
    for level in ("only32", "not32"):
        es = [jnp.where(mk[level], lab, 0.0) for lab in labs]
        exs = _each(_dot_x3, es, xs)
        xs = _each(lambda x, ex: x + _dot_x3(x, ex), xs, exs)
    return xs


WKV_GROUP = 4


def _wkv_chunk_group(rows, s_bds, mk):
    c = WKV_CHUNK
    stack = lambda x: jnp.concatenate([jnp.where(mk["head0"], x, 0.0), jnp.where(mk["head0"], 0.0, x)], axis=0)
    dup = lambda x: jnp.concatenate([x, x], axis=0)
    col = lambda i: [row[i] for row in rows]
    n = PAIR_LANES
    a_s, r_s, v_s = [_each(stack, col(i)) for i in (0, 1, 6)]
    bh2, kh2 = [_each(dup, col(i)) for i in (4, 5)]
    bk2 = _each(lambda bt, kt: jnp.concatenate([dup(bt), dup(kt)], axis=0), col(2), col(3))
    nt = lambda x, y: _dg_x3(x, y, _NT)
    tn = lambda x, y: _dg_x3(x, y, _TN)
    below = lambda x: jnp.where(mk["strict"], x, 0.0)
    upto = lambda x: jnp.where(mk["incl"], x, 0.0)
    heads = lambda x: jnp.where(mk["same_head"], x, 0.0)
    a_bk = _each(nt, a_s, bk2)
    lab = [below(x[:, :n]) for x in a_bk]
    lak = [below(x[:, n:]) for x in a_bk]
    nt1 = lambda x, y: lax.dot_general(x.astype(BF16), y.astype(BF16), _NT, preferred_element_type=F32)
    mm1 = lambda x, y: _dot(x.astype(BF16), y.astype(BF16))
    r_bk = _each(nt1, r_s, bk2)
    lrb = [upto(x[:, :n]) for x in r_bk]
    lrk = [upto(x[:, n:]) for x in r_bk]
    t_inv = _unit_lower_inverse(lab, mk)
    lakv = _each(_dot_x3, lak, v_s)
    ta_u = _each(lambda t, a, w: _dot_x3(t, jnp.concatenate([a, w], axis=1)), t_inv, a_s, lakv)
    ta_s = [x[:, :n] for x in ta_u]
    u_s = [x[:, n:] for x in ta_u]
    m_bd = _each(lambda m, w: heads(m) + jnp.where(mk["eye"], w, 0.0), _each(tn, ta_s, bh2), col(7))
    g_bd = _each(lambda g1, g2: heads(g1 + g2), _each(tn, u_s, bh2), _each(tn, v_s, kh2))
    lrb_tau = _each(mm1, lrb, ta_u)
    q_s = [x[:, :n] for x in lrb_tau]
    y0_s = _each(lambda x, p2: x[:, n:] + p2, lrb_tau, _each(mm1, lrk, v_s))
    q = _each(lambda rt, qs: rt + (qs[:c] + qs[c:]), col(1), q_s)
    y = _each(lambda yq, y0: yq + (y0[:c] + y0[c:]), _each(nt1, q, s_bds), y0_s)
    s_new = _each(lambda sm, g: sm + g, _each(_dot_x3, s_bds, m_bd), g_bd)
    return y, s_new


def _rwkv_seq_kernel(r_ref, k_ref, v_ref, lora_ref, w0_ref, wd_ref, a0_ref, wa_ref, wg_ref, kk_ref, ka_ref,
                     rk_ref, lnw_ref, lnb_ref, seg_ref, segt_ref, o_ref, sout_ref, s_scr, y_scr):
    c = pl.program_id(1)

    @pl.when(c == 0)
    def _():
        s_scr[...] = jnp.zeros(s_scr.shape, F32)

    r = r_ref[...]
    v = v_ref[...]
    seg = seg_ref[...]
    segt = segt_ref[...]
    prm = (w0_ref[...], wd_ref[...], a0_ref[...], wa_ref[...], wg_ref[...], kk_ref[...], ka_ref[...], rk_ref[...])
    decay, log_decay, k_mod, neg_kk, b, gate, bonus = _rwkv_prep(r, k_ref[...], v, lora_ref[...], prm, seg, segt)

    n = WKV_CHUNK
    tri = (lax.broadcasted_iota(jnp.int32, (n, n), 1) <= lax.broadcasted_iota(jnp.int32, (n, n), 0)).astype(BF16)
    ld_h, ld_m, ld_l = _split3(log_decay)
    cw = _dot(tri, ld_h) + (_dot(tri, ld_m) + _dot(tri, ld_l))
    cw_prev = cw - log_decay
    cw_last = cw[n - 1:n, :]
    e_prev = jnp.exp(cw_prev)
    e_inv = jnp.exp(-cw)
    e_rem = jnp.exp(cw_last - cw)
    at = neg_kk * e_prev
    rt = r * (e_prev * decay)
    bt = b * e_inv
    kt = k_mod * e_inv
    bh = b * e_rem
    kh = k_mod * e_rem
    w_last = jnp.exp(cw_last)

    mk = _wkv_masks()
    for p0 in range(0, N_PAIRS, WKV_GROUP):
        pairs = range(p0, p0 + WKV_GROUP)
        lanes = [slice(p * PAIR_LANES, (p + 1) * PAIR_LANES) for p in pairs]
        rows = [tuple(z[:, sl] for z in (at, rt, bt, kt, bh, kh, v, w_last)) for sl in lanes]
        ys, s_new = _wkv_chunk_group(rows, [s_scr[p] for p in pairs], mk)
        for p, sl, y, s in zip(pairs, lanes, ys, s_new):
            s_scr[p] = s
            y_scr[:, sl] = y

    out = (_group_norm_rows(y_scr[...], seg, segt) * lnw_ref[...] + lnb_ref[...] + bonus) * gate
    o_ref[...] = out.astype(BF16)

    @pl.when(c == pl.num_programs(1) - 1)
    def _():
        for p in range(N_PAIRS):
            sout_ref[0, 2 * p] = s_scr[p, 0:HEAD_DIM, 0:HEAD_DIM]
            sout_ref[0, 2 * p + 1] = s_scr[p, HEAD_DIM:, HEAD_DIM:]


def _rwkv_weights(w0, w_decay_up, a0, w_aaa_up, w_gate_up, k_k, k_a, r_k, ln_x_w, ln_x_b):
    row = lambda z: z.reshape(1, D_RWKV)
    pad = lambda z: jnp.pad(z, ((0, LORA_PAD - z.shape[0]), (0, 0)))
    head_of_col = jnp.arange(D_RWKV, dtype=jnp.int32) // HEAD_DIM
    seg = (head_of_col[:, None] == jnp.arange(LANES, dtype=jnp.int32)[None, :]).astype(BF16)
    return (row(w0), pad(w_decay_up), row(a0), pad(w_aaa_up), w_gate_up, row(k_k), row(k_a), row(r_k),
            row(ln_x_w), row(ln_x_b), seg, seg.T)


def _const_spec(shape, grid_rank):
    zeros = (0,) * len(shape)
    if grid_rank == 1:
        return pl.BlockSpec(shape, lambda i: zeros)
    return pl.BlockSpec(shape, lambda i, j: zeros)


def _rwkv_seq(hx, weights, *, batch, seq):
    tc = WKV_CHUNK
    nchunk = seq // tc
    rowmap = lambda cb: (lambda b, c: (b * nchunk + c, cb))
    in_specs = [pl.BlockSpec((tc, D_RWKV), rowmap(0)),
                pl.BlockSpec((tc, D_RWKV), rowmap(1)),
                pl.BlockSpec((tc, D_RWKV), rowmap(2)),
                pl.BlockSpec((tc, 2 * LORA_PAD + GATE_LORA), rowmap(COL_LORA // (2 * LORA_PAD + GATE_LORA)))]
    in_specs += [_const_spec(w.shape, 2) for w in weights]
    return pl.pallas_call(
        _rwkv_seq_kernel,
        grid=(batch, nchunk),
        in_specs=in_specs,
        out_specs=[pl.BlockSpec((tc, D_RWKV), lambda b, c: (b * nchunk + c, 0)),
                   pl.BlockSpec((1, RWKV_HEADS, HEAD_DIM, HEAD_DIM), lambda b, c: (b, 0, 0, 0))],
        out_shape=[jax.ShapeDtypeStruct((batch * seq, D_RWKV), BF16),
                   jax.ShapeDtypeStruct((batch, RWKV_HEADS, HEAD_DIM, HEAD_DIM), F32)],
        scratch_shapes=[pltpu.VMEM((N_PAIRS, PAIR_LANES, PAIR_LANES), F32), pltpu.VMEM((tc, D_RWKV), F32)],
        compiler_params=pltpu.CompilerParams(dimension_semantics=("arbitrary", "arbitrary"),
                                             vmem_limit_bytes=VMEM_LIMIT),
    )(hx, hx, hx, hx, *weights)


def _rwkv_tok_prep_kernel(r_ref, k_ref, v_ref, lora_ref, w0_ref, wd_ref, a0_ref, wa_ref, wg_ref, kk_ref, ka_ref,
                          rk_ref, seg_ref, segt_ref, w_o, k_o, a_o, b_o, g_o, bonus_o):
    prm = (w0_ref[...], wd_ref[...], a0_ref[...], wa_ref[...], wg_ref[...], kk_ref[...], ka_ref[...], rk_ref[...])
    decay, _, k_mod, neg_kk, b, gate, bonus = _rwkv_prep(r_ref[...], k_ref[...], v_ref[...], lora_ref[...], prm,
                                                          seg_ref[...], segt_ref[...])
    w_o[...] = decay
    k_o[...] = k_mod
    a_o[...] = neg_kk
    b_o[...] = b
    g_o[...] = gate
    bonus_o[...] = bonus


def _rwkv_tok_step_kernel(r_ref, w_ref, k_ref, v_ref, a_ref, b_ref, g_ref, bonus_ref, lnw_ref, lnb_ref, s0_ref,
                          o_ref, s_ref, *, tb):
    eye = (lax.broadcasted_iota(jnp.int32, (HEAD_DIM, HEAD_DIM), 0)
           == lax.broadcasted_iota(jnp.int32, (HEAD_DIM, HEAD_DIM), 1))

    def body(bi, carry):
        s = s0_ref[bi]
        sa = jnp.sum(s * a_ref[bi], axis=2, keepdims=True)
        v_col = jnp.sum(jnp.where(eye, v_ref[bi], 0.0), axis=2, keepdims=True)
        s = s * w_ref[bi] + sa * b_ref[bi] + v_col * k_ref[bi]
        s_ref[bi] = s
        y_col = jnp.sum(s * r_ref[bi], axis=2, keepdims=True)
        y = jnp.sum(jnp.where(eye, y_col, 0.0), axis=1, keepdims=True)
        o_ref[bi] = (_group_norm_head(y) * lnw_ref[0] + lnb_ref[0] + bonus_ref[bi]) * g_ref[bi]
        return carry

    lax.fori_loop(0, tb, body, 0)


def _rwkv_tok(hx, state, weights, *, tb):
    nb = hx.shape[0]
    (w0, wd, a0, wa, wg, k_k, k_a, r_k, ln_w, ln_b, seg, segt) = weights
    prep_w = (w0, wd, a0, wa, wg, k_k, k_a, r_k, seg, segt)
    colmap = lambda cb: (lambda i: (0, cb))
    in_specs = [pl.BlockSpec((nb, D_RWKV), colmap(0)),
                pl.BlockSpec((nb, D_RWKV), colmap(1)),
                pl.BlockSpec((nb, D_RWKV), colmap(2)),
                pl.BlockSpec((nb, 2 * LORA_PAD + GATE_LORA), colmap(COL_LORA // (2 * LORA_PAD + GATE_LORA)))]
    in_specs += [_const_spec(w.shape, 1) for w in prep_w]
    rows = jax.ShapeDtypeStruct((nb, D_RWKV), F32)
    prepped = pl.pallas_call(
        _rwkv_tok_prep_kernel,
        grid=(1,),
        in_specs=in_specs,
        out_specs=[pl.BlockSpec((nb, D_RWKV), lambda i: (0, 0))] * 6,
        out_shape=[rows] * 6,
        compiler_params=pltpu.CompilerParams(dimension_semantics=("arbitrary",),
                                             vmem_limit_bytes=VMEM_LIMIT),
    )(hx, hx, hx, hx, *prep_w)
    decay, k_mod, neg_kk, b, gate, bonus = prepped
    heads = lambda z: z.reshape(nb, RWKV_HEADS, 1, HEAD_DIM)
    vecs = [heads(z) for z in (hx[:, COL_R:COL_R + D_RWKV], decay, k_mod, hx[:, COL_V:COL_V + D_RWKV], neg_kk, b,
                               gate, bonus)]
    vec_spec = pl.BlockSpec((tb, RWKV_HEADS, 1, HEAD_DIM), lambda i: (i, 0, 0, 0))
    ln_spec = pl.BlockSpec((1, RWKV_HEADS, 1, HEAD_DIM), lambda i: (0, 0, 0, 0))
    state_spec = pl.BlockSpec((tb, RWKV_HEADS, HEAD_DIM, HEAD_DIM), lambda i: (i, 0, 0, 0))
    out, new_state = pl.pallas_call(
        functools.partial(_rwkv_tok_step_kernel, tb=tb),
        grid=(nb // tb,),
        in_specs=[vec_spec] * 8 + [ln_spec, ln_spec, state_spec],
        out_specs=[vec_spec, state_spec],
        out_shape=[jax.ShapeDtypeStruct((nb, RWKV_HEADS, 1, HEAD_DIM), F32),
                   jax.ShapeDtypeStruct(state.shape, F32)],
        compiler_params=pltpu.CompilerParams(dimension_semantics=("arbitrary",),
                                             vmem_limit_bytes=VMEM_LIMIT),
    )(*vecs, ln_w.reshape(1, RWKV_HEADS, 1, HEAD_DIM), ln_b.reshape(1, RWKV_HEADS, 1, HEAD_DIM), state)
    return out.reshape(nb, D_RWKV).astype(BF16), new_state


def _route(logits, bias):
    scores = _sigmoid(logits)
    sel = scores + bias
    rows = logits.shape[0]
    lane_i = lax.broadcasted_iota(jnp.int32, (rows, N_EXPERTS), 1)
    grp = lane_i // GROUP_SIZE
    lane = lane_i.astype(F32)
    neg_inf = -jnp.inf

    def take_max(x):
        m = jnp.max(x, axis=1, keepdims=True)
        idx = jnp.min(jnp.where(x == m, lane, float(N_EXPERTS)), axis=1, keepdims=True)
        return m, lane == idx

    gscore = []
    for g in range(N_GROUPS):
        x = jnp.where(grp == g, sel, neg_inf)
        m1, hit = take_max(x)
        m2 = jnp.max(jnp.where(hit, neg_inf, x), axis=1, keepdims=True)
        gscore.append(m1 + m2)
    emask = jnp.zeros((rows, N_EXPERTS), jnp.bool_)
    for g in range(N_GROUPS):
        ahead = jnp.zeros((rows, 1), jnp.int32)
        for g2 in range(N_GROUPS):
            if g2 == g:
                continue
            beats = (gscore[g2] >= gscore[g]) if g2 < g else (gscore[g2] > gscore[g])
            ahead = ahead + beats.astype(jnp.int32)
        emask = emask | ((grp == g) & (ahead < TOPK_GROUPS))
    x = jnp.where(emask, sel, neg_inf)
    chosen = jnp.zeros((rows, N_EXPERTS), jnp.bool_)
    slot_lane = lax.broadcasted_iota(jnp.int32, (rows, LANES), 1)
    top_idx = jnp.zeros((rows, LANES), F32)
    top_w = jnp.zeros((rows, LANES), F32)
    for kth in range(TOP_K):
        m = jnp.max(x, axis=1, keepdims=True)
        idx = jnp.min(jnp.where(x == m, lane, float(N_EXPERTS)), axis=1, keepdims=True)
        hit = lane == idx
        chosen = chosen | hit
        x = jnp.where(hit, neg_inf, x)
        top_idx = jnp.where(slot_lane == kth, idx, top_idx)
        top_w = jnp.where(slot_lane == kth, jnp.sum(jnp.where(hit, scores, 0.0), axis=1, keepdims=True), top_w)
    top_w = top_w / jnp.sum(top_w, axis=1, keepdims=True) * ROUTE_SCALE
    return top_idx.astype(jnp.int32), top_w, chosen.astype(F32)


def _out_proj_kernel(pool_ref, rwkv_ref, x_ref, g1_ref, sc_ref, sh_ref, wp_ref, wr_ref, npost_ref, npre_ref,
                     wrt_ref, rb_ref, sg_ref, su_ref, sd_ref, x1_ref, u2_ref, idx_ref, topw_ref, chosen_ref,
                     shared_ref):
    m = _dot(pool_ref[...], wp_ref[...]) + _dot(rwkv_ref[...], wr_ref[...])
    x1 = x_ref[...] + g1_ref[0] * _rms(m, npost_ref[...])
    x1_ref[...] = x1
    u2 = _rms(x1, npre_ref[...]) * (1.0 + sc_ref[0]) + sh_ref[0]
    u2_ref[...] = u2
    top_idx, top_w, chosen = _route(_dot_x3(u2, wrt_ref[...]), rb_ref[...])
    idx_ref[...] = top_idx
    topw_ref[...] = top_w
    chosen_ref[...] = chosen
    u2b = u2.astype(BF16)
    h = _silu(_dot(u2b, sg_ref[...])) * _dot(u2b, su_ref[...])
    shared_ref[...] = _dot(h.astype(BF16), sd_ref[...])


def _out_proj(pool_out, rwkv_out, x, g1, sc2, sh2, w_out_bf16, n_post, n_pre, w_router, router_bias, sg, su, sd, *,
              tm, tiles_per_seq):
    m, d = x.shape
    if tiles_per_seq is None:
        mod_spec = pl.BlockSpec((1, tm, d), lambda i: (i, 0, 0))
    else:
        mod_spec = pl.BlockSpec((1, 1, d), lambda i: (i // tiles_per_seq, 0, 0))
    row = lambda width: pl.BlockSpec((tm, width), lambda i: (i, 0))
    return pl.pallas_call(
        _out_proj_kernel,
        grid=(m // tm,),
        in_specs=[row(D_POOL), row(D_RWKV), row(d), mod_spec, mod_spec, mod_spec,
                  _const_spec((D_POOL, d), 1), _const_spec((D_RWKV, d), 1),
                  _const_spec((1, d), 1), _const_spec((1, d), 1),
                  _const_spec((d, N_EXPERTS), 1), _const_spec((1, N_EXPERTS), 1),
                  _const_spec(sg.shape, 1), _const_spec(su.shape, 1), _const_spec(sd.shape, 1)],
        out_specs=[row(d), row(d), row(LANES), row(LANES), row(N_EXPERTS), row(d)],
        out_shape=[jax.ShapeDtypeStruct((m, d), F32), jax.ShapeDtypeStruct((m, d), F32),
                   jax.ShapeDtypeStruct((m, LANES), jnp.int32), jax.ShapeDtypeStruct((m, LANES), F32),
                   jax.ShapeDtypeStruct((m, N_EXPERTS), F32), jax.ShapeDtypeStruct((m, d), F32)],
        compiler_params=pltpu.CompilerParams(dimension_semantics=("arbitrary",),
                                             vmem_limit_bytes=VMEM_LIMIT),
    )(pool_out, rwkv_out, x, g1, sc2, sh2, w_out_bf16[:D_POOL], w_out_bf16[D_POOL:],
      n_post.reshape(1, d), n_pre.reshape(1, d), w_router, router_bias.reshape(1, N_EXPERTS), sg, su, sd)


MOE_TM = 384


ROW_GROUP = 8


def _experts_kernel(te_ref, fresh_ref, wslot_ref, nexte_ref, groups_ref, nused_ref, idx_ref, idx_next_ref, ws_ref,
                    x_hbm, wg_hbm, wu_hbm, wd_hbm, ys_ref, wg_buf, wu_buf, wd_buf, g_bf, u_bf, d_bf, x_buf,
                    w_sems, x_sems):
    t = pl.program_id(0)
    n_used = nused_ref[0]

    def weight_copies(e, slot):
        return (pltpu.make_async_copy(wg_hbm.at[e], wg_buf.at[slot], w_sems.at[slot]),
                pltpu.make_async_copy(wu_hbm.at[e], wu_buf.at[slot], w_sems.at[slot]),
                pltpu.make_async_copy(wd_hbm.at[e], wd_buf.at[slot], w_sems.at[slot]))

    def start_rows(table_ref, slot, n_groups):
        def body(g, carry):
            for r in range(ROW_GROUP):
                j = g * ROW_GROUP + r
                pltpu.make_async_copy(x_hbm.at[pl.ds(table_ref[0, 0, j], 1)], x_buf.at[slot, pl.ds(j, 1)],
                                      x_sems.at[slot]).start()
            return carry
        lax.fori_loop(0, n_groups, body, 0)

    @pl.when(t == 0)
    def _():
        for cp in weight_copies(te_ref[0], 0):
            cp.start(priority=1)
        x_buf[...] = jnp.zeros(x_buf.shape, F32)
        start_rows(idx_ref, 0, groups_ref[0])

    @pl.when(t + 1 < n_used)
    def _():
        start_rows(idx_next_ref, (t + 1) % 2, groups_ref[t + 1])

    @pl.when(jnp.logical_and(t < n_used, fresh_ref[t] == 1))
    def _():
        slot = wslot_ref[t]
        for cp in weight_copies(te_ref[t], slot):
            cp.wait()

        @pl.when(nexte_ref[t] >= 0)
        def _():
            for cp in weight_copies(nexte_ref[t], 1 - slot):
                cp.start(priority=1)

        g_bf[...] = wg_buf[slot].astype(BF16)
        u_bf[...] = wu_buf[slot].astype(BF16)
        d_bf[...] = wd_buf[slot].astype(BF16)

    @pl.when(t < n_used)
    def _():
        cur = t % 2
        n_rows = pl.multiple_of(groups_ref[t] * ROW_GROUP, ROW_GROUP)
        pltpu.make_async_copy(x_hbm.at[pl.ds(0, n_rows)], x_buf.at[cur, pl.ds(0, n_rows)], x_sems.at[cur]).wait()
        x = x_buf[cur].astype(BF16)
        h = _silu(_dot(x, g_bf[...])) * _dot(x, u_bf[...])
        diag = (lax.broadcasted_iota(jnp.int32, (MOE_TM, MOE_TM), 0)
                == lax.broadcasted_iota(jnp.int32, (MOE_TM, MOE_TM), 1))
        w_col = jnp.sum(jnp.where(diag, ws_ref[0], 0.0), axis=1, keepdims=True)
        ys_ref[...] = _dot(h.astype(BF16), d_bf[...]) * w_col

    @pl.when(t >= n_used)
    def _():
        ys_ref[...] = jnp.zeros(ys_ref.shape, F32)


def _experts(tables, x_all, exp_gate, exp_up, exp_down):
    tile_expert, fresh, wslot, next_e, row_groups, n_used, slot_tok, slot_w = tables
    d = x_all.shape[1]
    n_tiles = tile_expert.shape[0]
    n_slots = n_tiles * MOE_TM
    idx_block = (1, 1, MOE_TM)
    any_spec = pl.BlockSpec(memory_space=pl.ANY)
    live = lambda t, nused: jnp.where(t < nused[0], t, n_tiles - 1)
    grid_spec = pltpu.PrefetchScalarGridSpec(
        num_scalar_prefetch=6,
        grid=(n_tiles,),
        in_specs=[pl.BlockSpec(idx_block, lambda t, *s: (live(t, s[5]), 0, 0), memory_space=pltpu.SMEM),
                  pl.BlockSpec(idx_block, lambda t, *s: (live(t + 1, s[5]), 0, 0), memory_space=pltpu.SMEM),
                  pl.BlockSpec(idx_block, lambda t, *s: (live(t, s[5]), 0, 0)),
                  any_spec, any_spec, any_spec, any_spec],
        out_specs=pl.BlockSpec((MOE_TM, d), lambda t, *s: (live(t, s[5]), 0)),
        scratch_shapes=[pltpu.VMEM((2, d, D_EXPERT), F32), pltpu.VMEM((2, d, D_EXPERT), F32),
                        pltpu.VMEM((2, D_EXPERT, d), F32),
                        pltpu.VMEM((d, D_EXPERT), BF16), pltpu.VMEM((d, D_EXPERT), BF16),
                        pltpu.VMEM((D_EXPERT, d), BF16),
                        pltpu.VMEM((2, MOE_TM, d), F32),
                        pltpu.SemaphoreType.DMA((2,)), pltpu.SemaphoreType.DMA((2,))],
    )
    idx_tiles = slot_tok.reshape(n_tiles, 1, MOE_TM)
    return pl.pallas_call(
        _experts_kernel,
        grid_spec=grid_spec,
        out_shape=jax.ShapeDtypeStruct((n_slots, d), F32),
        compiler_params=pltpu.CompilerParams(dimension_semantics=("arbitrary",),
                                             vmem_limit_bytes=VMEM_LIMIT),
    )(tile_expert, fresh, wslot, next_e, row_groups, n_used, idx_tiles, idx_tiles,
      slot_w.reshape(n_tiles, 1, MOE_TM), x_all, exp_gate, exp_up, exp_down)


ASSIGN_BITS = 17


def _dispatch(chosen, top_idx, top_w):
    n = chosen.shape[0]
    n_assign = n * TOP_K
    low_mask = (1 << ASSIGN_BITS) - 1
    assert n_assign + MOE_TM <= low_mask and N_EXPERTS << ASSIGN_BITS < 2 ** 31
    n_tiles = -(-n_assign // MOE_TM) + N_EXPERTS
    n_slots = n_tiles * MOE_TM
    sel = chosen.astype(jnp.int32)
    counts = jnp.sum(sel, axis=0)
    rank = jnp.cumsum(sel, axis=0) - sel
    tiles_e = (counts + MOE_TM - 1) // MOE_TM
    tile_end = jnp.cumsum(tiles_e)
    slot = (tile_end - tiles_e)[None, :] * MOE_TM + rank
    expert_ids = jnp.arange(N_EXPERTS, dtype=jnp.int32)
    slot_of_pair = jnp.stack([jnp.sum(jnp.where(top_idx[:, k:k + 1] == expert_ids[None, :], slot, 0), axis=1)
                              for k in range(TOP_K)])
    shift = 1 << ASSIGN_BITS
    int_max = jnp.iinfo(jnp.int32).max
    real_key = top_idx.reshape(-1) * shift + jnp.arange(n_assign, dtype=jnp.int32)
    pad_j = jnp.arange(MOE_TM - 1, dtype=jnp.int32)[None, :]
    pad_key = jnp.where(pad_j < (tiles_e * MOE_TM - counts)[:, None],
                        expert_ids[:, None] * shift + n_assign + pad_j, int_max)
    n_rest = n_slots - n_assign
    keys = jnp.concatenate([real_key, pad_key.reshape(-1),
                            jnp.full((n_rest - pad_key.size,), int_max, jnp.int32)])
    weights = jnp.concatenate([top_w.reshape(-1), jnp.zeros((n_rest,), F32)])
    sorted_key, sorted_w = lax.sort((keys, weights), num_keys=1)
    pair = sorted_key & low_mask
    is_real = pair < n_assign
    slot_tok = jnp.where(is_real, pair // TOP_K, 0)
    slot_w = jnp.where(is_real, sorted_w, 0.0)
    n_used = tile_end[-1]
    tile_ids = jnp.minimum(jnp.arange(n_tiles, dtype=jnp.int32), n_used - 1)
    tile_expert = jnp.minimum(jnp.searchsorted(tile_end, tile_ids, side="right"), N_EXPERTS - 1).astype(jnp.int32)
    in_use = jnp.arange(n_tiles, dtype=jnp.int32) < n_used
    prev_expert = jnp.concatenate([jnp.full((1,), -1, jnp.int32), tile_expert[:-1]])
    fresh = jnp.logical_and(in_use, tile_expert != prev_expert).astype(jnp.int32)
    wslot = (jnp.cumsum(fresh) - 1) % 2
    later = jnp.where(counts > 0, expert_ids, N_EXPERTS)
    next_used = jnp.concatenate([lax.cummin(later, reverse=True)[1:], jnp.full((1,), N_EXPERTS, jnp.int32)])
    next_used = jnp.where(next_used >= N_EXPERTS, -1, next_used)
    tile_in_expert = jnp.arange(n_tiles, dtype=jnp.int32) - (tile_end - tiles_e)[tile_expert]
    rows_in_tile = jnp.clip(counts[tile_expert] - tile_in_expert * MOE_TM, 0, MOE_TM)
    row_groups = jnp.where(in_use, (rows_in_tile + ROW_GROUP - 1) // ROW_GROUP, 0)
    tables = (tile_expert, fresh, wslot.astype(jnp.int32), next_used[tile_expert].astype(jnp.int32),
              row_groups.astype(jnp.int32), n_used.reshape(1).astype(jnp.int32), slot_tok.astype(jnp.int32), slot_w)
    return slot_of_pair, tables


def _final_kernel(idx_ref, idx_next_ref, shared_ref, ys_hbm, x1_ref, g2_ref, npost_ref, o_ref, rows_buf, sems, *,
                  tm):
    i = pl.program_id(0)
    n_rows = TOP_K * tm

    def start_gather(table_ref, buf_slot):
        def body(j, carry):
            pltpu.make_async_copy(ys_hbm.at[pl.ds(table_ref[0, 0, j], 1)], rows_buf.at[buf_slot, pl.ds(j, 1)],
                                  sems.at[buf_slot]).start()
            return carry
        lax.fori_loop(0, n_rows, body, 0, unroll=8)

    @pl.when(i == 0)
    def _():
        start_gather(idx_ref, 0)

    @pl.when(i + 1 < pl.num_programs(0))
    def _():
        start_gather(idx_next_ref, (i + 1) % 2)

    cur = i % 2
    pltpu.make_async_copy(ys_hbm.at[pl.ds(0, n_rows)], rows_buf.at[cur], sems.at[cur]).wait()
    routed = rows_buf[cur, 0:tm]
    for kth in range(1, TOP_K):
        routed = routed + rows_buf[cur, kth * tm:(kth + 1) * tm]
    o_ref[...] = x1_ref[...] + g2_ref[0] * _rms(routed + shared_ref[...], npost_ref[...])


def _final(shared, ys, slot_tiles, x1, g2, n_post, *, tm, tiles_per_seq, tile_offset):
    m, d = x1.shape
    nt = m // tm
    if tiles_per_seq is None:
        mod_spec = pl.BlockSpec((1, tm, d), lambda i: (i, 0, 0))
    else:
        mod_spec = pl.BlockSpec((1, 1, d), lambda i: (i // tiles_per_seq, 0, 0))
    row = pl.BlockSpec((tm, d), lambda i: (i, 0))
    idx_block = (1, 1, TOP_K * tm)
    return pl.pallas_call(
        functools.partial(_final_kernel, tm=tm),
        grid=(nt,),
        in_specs=[pl.BlockSpec(idx_block, lambda i: (i + tile_offset, 0, 0), memory_space=pltpu.SMEM),
                  pl.BlockSpec(idx_block, lambda i: (jnp.minimum(i + 1, nt - 1) + tile_offset, 0, 0),
                               memory_space=pltpu.SMEM),
                  row, pl.BlockSpec(memory_space=pl.ANY), row, mod_spec, _const_spec((1, d), 1)],
        out_specs=row,
        out_shape=jax.ShapeDtypeStruct((m, d), F32),
        scratch_shapes=[pltpu.VMEM((2, TOP_K * tm, d), F32), pltpu.SemaphoreType.DMA((2,))],
        compiler_params=pltpu.CompilerParams(dimension_semantics=("arbitrary",),
                                             vmem_limit_bytes=VMEM_LIMIT),
    )(slot_tiles, slot_tiles, shared, ys, x1, g2, n_post.reshape(1, d))


def _pad_cols_in(w_in, mu_shift):
    d = w_in.shape[0]
    hp = w_in[:, :D_POOL]
    hr = w_in[:, D_POOL:]
    cuts = [3 * D_RWKV, 3 * D_RWKV + DECAY_LORA, 3 * D_RWKV + DECAY_LORA + AAA_LORA]
    rkv, wl, al, gl = jnp.split(hr, cuts, axis=1)
    zpad = lambda z: jnp.pad(z, ((0, 0), (0, LORA_PAD - z.shape[1])))
    w = jnp.concatenate([rkv, hp, zpad(wl), zpad(al), gl], axis=1).astype(BF16)
    mu_rkv, mu_wl, mu_al, mu_gl = jnp.split(mu_shift, cuts)
    zp1 = lambda z: jnp.pad(z, (0, LORA_PAD - z.shape[0]))
    mu = jnp.concatenate([mu_rkv, jnp.zeros((D_POOL,), F32), zp1(mu_wl), zp1(mu_al), mu_gl])
    assert w.shape == (d, N_IN_PAD) and mu.shape == (N_IN_PAD,)
    return w, mu


def kernel(x_prompt, x_sample, c_prompt, c_sample, state_shift, state_pool, state_wkv, w_ada, b_ada, norm_pre_mix, norm_post_mix, norm_pre_ffn, norm_post_ffn, w_in, mu_shift, pool_w, pool_scale, w0, w_decay_up, a0, w_aaa_up, w_gate_up, k_k, k_a, r_k, ln_x_w, ln_x_b, w_out, w_router, router_bias, exp_gate, exp_up, exp_down, sh_gate, sh_up, sh_down):
    bp, seq, d = x_prompt.shape
    bs = x_sample.shape[0]
    assert w_ada.shape[0] == 1 and x_sample.shape[1] == 1
    np_rows = bp * seq

    c_all = jnp.concatenate([c_prompt, c_sample], axis=0)
    pad_rows = (-c_all.shape[0]) % 16
    ada = _ada(jnp.pad(c_all, ((0, pad_rows), (0, 0))), w_ada[0], b_ada[0])
    sh1, sc1, g1, sh2, sc2, g2 = jnp.split(ada, 6, axis=-1)
    pr = lambda z: z[:bp].reshape(bp, 1, d)
    TS = min(128, bs)
    sr = lambda z: z[bp:bp + bs].reshape(bs // TS, TS, d)

    w_in_p, mu_p = _pad_cols_in(w_in[0], mu_shift[0])
    xp = x_prompt.reshape(np_rows, d)
    xs = x_sample.reshape(bs, d)

    TM_IN = min(1024, seq)
    hx_p, u_tail = _in_proj(xp, norm_pre_mix[0], pr(sc1), pr(sh1), w_in_p, mu_p, tm=TM_IN,
                            tiles_per_seq=seq // TM_IN)
    hx_s, u_s = _in_proj(xs, norm_pre_mix[0], sr(sc1), sr(sh1), w_in_p, mu_p, tm=TS, tiles_per_seq=1,
                         prev=state_shift[0])
    new_shift_prompt = u_tail.reshape(bp, seq // TM_IN, 8, d)[:, -1, -1][None]
    new_shift_sample = u_s[None]

    pool_w_b = pool_w[0].astype(BF16)
    TC_POOL = min(512, seq)
    pool_p = _pool(hx_p, COL_POOL // D_POOL, pool_w_b, pool_scale[0], tc=TC_POOL, tiles_per_seq=seq // TC_POOL,
                   full_count=False)
    hp_p = hx_p[:, COL_POOL:COL_POOL + D_POOL].reshape(bp, seq, D_POOL)
    new_pool_prompt = hp_p[:, seq - POOL_BUF:][None]
    hp_s = hx_s[:, COL_POOL:COL_POOL + D_POOL]
    ext_s = jnp.concatenate([state_pool[0], hp_s[:, None, :]], axis=1)
    assert PAST_LEN + 1 >= max(POOL_WINDOWS)
    pool_s = _pool(ext_s.reshape(bs * (POOL_BUF + 1), D_POOL), 0, pool_w_b, pool_scale[0],
                   tc=bs * (POOL_BUF + 1), tiles_per_seq=1, full_count=True)
    pool_s = pool_s.reshape(bs, POOL_BUF + 1, D_POOL)[:, -1]
    new_pool_sample = ext_s[:, 1:][None]

    rw = _rwkv_weights(w0[0], w_decay_up[0], a0[0], w_aaa_up[0], w_gate_up[0], k_k[0], k_a[0], r_k[0],
                       ln_x_w[0], ln_x_b[0])
    rwkv_p, wkv_p = _rwkv_seq(hx_p, rw, batch=bp, seq=seq)
    rwkv_s, wkv_s = _rwkv_tok(hx_s, state_wkv[0], rw, tb=8)

    w_out_b = w_out[0].astype(BF16)
    assert np_rows % TS == 0 and seq % TS == 0
    sg, su, sd = sh_gate[0].astype(BF16), sh_up[0].astype(BF16), sh_down[0].astype(BF16)
    TM_OUT = min(256, seq)
    outs_p = _out_proj(pool_p, rwkv_p, xp, pr(g1), pr(sc2), pr(sh2), w_out_b, norm_post_mix[0], norm_pre_ffn[0],
                       w_router[0], router_bias[0], sg, su, sd, tm=TM_OUT, tiles_per_seq=seq // TM_OUT)
    outs_s = _out_proj(pool_s, rwkv_s, xs, sr(g1), sr(sc2), sr(sh2), w_out_b, norm_post_mix[0], norm_pre_ffn[0],
                       w_router[0], router_bias[0], sg, su, sd, tm=TS, tiles_per_seq=None)
    (x1_p, u2_p, idx_p, topw_p, chosen_p, shared_p), (x1_s, u2_s, idx_s, topw_s, chosen_s, shared_s) = outs_p, outs_s
    u2_all = jnp.concatenate([u2_p, u2_s], axis=0)
    idx_all = jnp.concatenate([idx_p[:, :TOP_K], idx_s[:, :TOP_K]], axis=0)
    topw_all = jnp.concatenate([topw_p[:, :TOP_K], topw_s[:, :TOP_K]], axis=0)
    chosen_all = jnp.concatenate([chosen_p, chosen_s], axis=0)

    n_all = np_rows + bs
    slot_of_pair, tables = _dispatch(chosen_all, idx_all, topw_all)
    ys = _experts(tables, u2_all, exp_gate[0], exp_up[0], exp_down[0])

    slot_tiles = slot_of_pair.reshape(TOP_K, n_all // TS, TS).transpose(1, 0, 2).reshape(n_all // TS, 1, TOP_K * TS)
    y_p = _final(shared_p, ys, slot_tiles, x1_p, pr(g2), norm_post_ffn[0], tm=TS, tiles_per_seq=seq // TS,
                 tile_offset=0)
    y_s = _final(shared_s, ys, slot_tiles, x1_s, sr(g2), norm_post_ffn[0], tm=TS, tiles_per_seq=None,
                 tile_offset=np_rows // TS)

    return (y_p.reshape(bp, seq, d), y_s.reshape(bs, 1, d), new_shift_prompt, new_pool_prompt, wkv_p[None],
            new_shift_sample, new_pool_sample, wkv_s[None])
```

```python
import functools

import jax
import jax.numpy as jnp
from jax import lax
from jax.experimental import pallas as pl
from jax.experimental.pallas import tpu as pltpu

F32 = jnp.float32
BF16 = jnp.bfloat16

D_MODEL = 2048
D_POOL = 512
POOL_WINDOWS = (2, 4, 8, 16)
POOL_CH = 128
POOL_BUF = 15
D_RWKV = 1536
HEAD_DIM = 64
RWKV_HEADS = 24
DECAY_LORA = 96
AAA_LORA = 96
GATE_LORA = 256
LORA_PAD = 128
GN_EPS = 64e-5
N_EXPERTS = 256
TOP_K = 8
N_GROUPS = 8
GROUP_SIZE = N_EXPERTS // N_GROUPS
TOPK_GROUPS = 4
D_EXPERT = 512
ROUTE_SCALE = 2.5
EPS = 1e-6
PAST_LEN = 16384

COL_R, COL_K, COL_V = 0, D_RWKV, 2 * D_RWKV
COL_POOL = 3 * D_RWKV
COL_LORA = COL_POOL + D_POOL
N_IN_PAD = COL_LORA + 2 * LORA_PAD + GATE_LORA
LANES = 128
VMEM_LIMIT = 56 * 1024 * 1024


def _dot(a, b):
    return jnp.dot(a, b, preferred_element_type=F32)


def _split2(x):
    hi = x.astype(BF16)
    lo = (x - hi.astype(F32)).astype(BF16)
    return hi, lo


def _split3(x):
    hi = x.astype(BF16)
    r = x - hi.astype(F32)
    mid = r.astype(BF16)
    lo = (r - mid.astype(F32)).astype(BF16)
    return hi, mid, lo


def _dot_x3(a, b):
    ah, al = _split2(a)
    bh, bl = _split2(b)
    return _dot(ah, bh) + (_dot(ah, bl) + _dot(al, bh))


def _dot_exact_rhs(a, b_bf16):
    h, m, l = _split3(a)
    return _dot(h, b_bf16) + (_dot(m, b_bf16) + _dot(l, b_bf16))


def _sigmoid(x):
    return 1.0 / (1.0 + jnp.exp(-x))


def _silu(x):
    return x * _sigmoid(x)


def _pack_bf16_halves(y):
    half = y.shape[1] // 2
    bits = pltpu.bitcast(y.astype(BF16).astype(F32), jnp.uint32)
    return bits[:, :half] | (bits[:, half:] >> 16)


def _unpack_bf16_halves(p):
    return (pltpu.bitcast(p & jnp.uint32(0xFFFF0000), F32), pltpu.bitcast(p << 16, F32))


def _rms(x, g):
    return x * lax.rsqrt(jnp.mean(x * x, axis=-1, keepdims=True) + EPS) * g


def _ada_kernel(c_ref, w_ref, b_ref, o_ref):
    o_ref[...] = _dot_x3(_silu(c_ref[...]), w_ref[...]) + b_ref[...]


def _ada(c_all, w_ada, b_ada):
    m, d = c_all.shape
    n = w_ada.shape[1]
    tn = 512
    return pl.pallas_call(
        _ada_kernel,
        grid=(n // tn,),
        in_specs=[pl.BlockSpec((m, d), lambda j: (0, 0)),
                  pl.BlockSpec((d, tn), lambda j: (0, j)),
                  pl.BlockSpec((1, tn), lambda j: (0, j))],
        out_specs=pl.BlockSpec((m, tn), lambda j: (0, j)),
        out_shape=jax.ShapeDtypeStruct((m, n), F32),
        compiler_params=pltpu.CompilerParams(dimension_semantics=("arbitrary",),
                                             vmem_limit_bytes=VMEM_LIMIT),
    )(c_all, w_ada, b_ada.reshape(1, n))


def _in_proj_kernel(*refs, tiles_per_seq, explicit_prev, tm):
    if explicit_prev:
        x_ref, g_ref, sc_ref, sh_ref, prev_ref, w_ref, mu_ref, hx_ref, u_ref, ub_scr = refs
    else:
        x_ref, g_ref, sc_ref, sh_ref, w_ref, mu_ref, hx_ref, u_ref, ub_scr, carry_scr = refs
    i = pl.program_id(0)
    j = pl.program_id(1)

    @pl.when(j == 0)
    def _():
        u = _rms(x_ref[...], g_ref[...])
        u = u * (1.0 + sc_ref[0]) + sh_ref[0]
        ub_scr[...] = u.astype(BF16)
        if explicit_prev:
            u_ref[...] = u
        else:
            u_ref[...] = u[tm - 8:, :]

    w = w_ref[...]
    h = _dot(ub_scr[...], w)
    if explicit_prev:
        hp = _dot(prev_ref[...].astype(BF16), w)
    else:
        @pl.when(i == 0)
        def _():
            carry_scr[j] = jnp.zeros(carry_scr.shape[1:], F32)

        first = jnp.where(i % tiles_per_seq == 0, 0.0, carry_scr[j, 0:1, :])
        row0 = lax.broadcasted_iota(jnp.int32, h.shape, 0) == 0
        hp = jnp.where(row0, first, pltpu.roll(h, 1, axis=0))
        carry_scr[j, 0:1, :] = h[tm - 1:tm, :]
    hx_ref[...] = h + (hp - h) * mu_ref[...]


def _in_proj(x, gamma, sc, sh, w_bf16, mu_pad, *, tm, tiles_per_seq, prev=None):
    m, d = x.shape
    n = w_bf16.shape[1]
    tn = 512
    nt = n // tn
    explicit_prev = prev is not None
    if explicit_prev:
        mod_spec = pl.BlockSpec((1, tm, d), lambda i, j: (i, 0, 0))
    else:
        mod_spec = pl.BlockSpec((1, 1, d), lambda i, j: (i // tiles_per_seq, 0, 0))
    in_specs = [pl.BlockSpec((tm, d), lambda i, j: (i, 0)),
                pl.BlockSpec((1, d), lambda i, j: (0, 0)),
                mod_spec, mod_spec]
    args = [x, gamma.reshape(1, d), sc, sh]
    if explicit_prev:
        in_specs.append(pl.BlockSpec((tm, d), lambda i, j: (i, 0)))
        args.append(prev)
    in_specs += [pl.BlockSpec((d, tn), lambda i, j: (0, j)),
                 pl.BlockSpec((1, tn), lambda i, j: (0, j))]
    args += [w_bf16, mu_pad.reshape(1, n)]
    scratch = [pltpu.VMEM((tm, d), BF16)]
    if explicit_prev:
        u_shape, u_spec = (m, d), pl.BlockSpec((tm, d), lambda i, j: (i, 0))
    else:
        u_shape, u_spec = (m // tm * 8, d), pl.BlockSpec((8, d), lambda i, j: (i, 0))
        scratch.append(pltpu.VMEM((nt, 8, tn), F32))
    return pl.pallas_call(
        functools.partial(_in_proj_kernel, tiles_per_seq=tiles_per_seq, explicit_prev=explicit_prev, tm=tm),
        grid=(m // tm, nt),
        in_specs=in_specs,
        out_specs=[pl.BlockSpec((tm, tn), lambda i, j: (i, j)), u_spec],
        out_shape=[jax.ShapeDtypeStruct((m, n), F32), jax.ShapeDtypeStruct(u_shape, F32)],
        scratch_shapes=scratch,
        compiler_params=pltpu.CompilerParams(dimension_semantics=("arbitrary", "arbitrary"),
                                             vmem_limit_bytes=VMEM_LIMIT),
    )(*args)


POOL_HALO = 16


def _pool_kernel(p_ref, pw_ref, ps_ref, o_ref, ext_scr, *, tiles_per_seq, full_count, tc):
    i = pl.program_id(0)

    @pl.when(i % tiles_per_seq == 0)
    def _():
        ext_scr[0:POOL_HALO, :] = jnp.zeros((POOL_HALO, D_POOL), F32)

    p = p_ref[...]
    ext_scr[POOL_HALO:POOL_HALO + tc, :] = p
    pos = (i % tiles_per_seq) * tc + lax.broadcasted_iota(jnp.int32, (tc, 1), 0)
    outs = []
    for gi, win in enumerate(POOL_WINDOWS):
        lo = gi * POOL_CH
        pg = p[:, lo:lo + POOL_CH]
        acc = pg
        for s in range(1, win):
            acc = acc + ext_scr[POOL_HALO - s:POOL_HALO - s + tc, lo:lo + POOL_CH]
        if full_count:
            dgi = acc / float(win) - pg
        else:
            cnt = jnp.minimum(pos + 1, win).astype(F32)
            dgi = acc / cnt - pg
        outs.append(_dot(dgi.astype(BF16), pw_ref[gi]))
    y = jnp.concatenate(outs, axis=-1) * ps_ref[...]
    o_ref[...] = y.astype(BF16)
    ext_scr[0:POOL_HALO, :] = ext_scr[tc:tc + POOL_HALO, :]


def _pool(src, col_block, pool_w_bf16, pool_scale, *, tc, tiles_per_seq, full_count):
    m = src.shape[0]
    return pl.pallas_call(
        functools.partial(_pool_kernel, tiles_per_seq=tiles_per_seq, full_count=full_count, tc=tc),
        grid=(m // tc,),
        in_specs=[pl.BlockSpec((tc, D_POOL), lambda i: (i, col_block)),
                  pl.BlockSpec((len(POOL_WINDOWS), POOL_CH, POOL_CH), lambda i: (0, 0, 0)),
                  pl.BlockSpec((1, D_POOL), lambda i: (0, 0))],
        out_specs=pl.BlockSpec((tc, D_POOL), lambda i: (i, 0)),
        out_shape=jax.ShapeDtypeStruct((m, D_POOL), BF16),
        scratch_shapes=[pltpu.VMEM((POOL_HALO + tc, D_POOL), F32)],
        compiler_params=pltpu.CompilerParams(dimension_semantics=("arbitrary",),
                                             vmem_limit_bytes=VMEM_LIMIT),
    )(src, pool_w_bf16, pool_scale.reshape(1, D_POOL))


def _softplus(z):
    return jnp.maximum(z, 0.0) + jnp.log1p(jnp.exp(-jnp.abs(z)))


def _rwkv_prep(r, k, v, lora, prm, seg, segt):
    (w0, wd, a0, wa, wg, k_k, k_a, r_k) = prm
    wl = lora[:, 0:LORA_PAD]
    al = lora[:, LORA_PAD:2 * LORA_PAD]
    gl = lora[:, 2 * LORA_PAD:]
    logw = -_softplus(-(w0 + _dot_x3(jnp.tanh(wl), wd))) - 0.5
    log_decay = -jnp.exp(logw)
    decay = jnp.exp(log_decay)
    a = _sigmoid(a0 + _dot_x3(al, wa))
    gate = _dot(_sigmoid(gl).astype(BF16), wg.astype(BF16))
    kk = k * k_k
    ss = _dot_exact_rhs(kk * kk, seg)
    inv = 1.0 / jnp.maximum(jnp.sqrt(ss), 1e-12)
    kk = kk * _dot_exact_rhs(inv, segt)
    k_mod = k * (1.0 + (a - 1.0) * k_a)
    bonus = _dot_exact_rhs(_dot_exact_rhs(r * k_mod * r_k, seg), segt) * v
    return decay, log_decay, k_mod, -kk, kk * a, gate, bonus


def _group_norm_head(y):
    mu = jnp.mean(y, axis=-1, keepdims=True)
    yc = y - mu
    var = jnp.mean(yc * yc, axis=-1, keepdims=True)
    return yc * lax.rsqrt(var + GN_EPS)


def _group_norm_rows(y, seg, segt):
    inv_n = 1.0 / HEAD_DIM
    mu = _dot_exact_rhs(_dot_exact_rhs(y, seg) * inv_n, segt)
    yc = y - mu
    var = _dot_exact_rhs(yc * yc, seg) * inv_n
    return yc * _dot_exact_rhs(lax.rsqrt(var + GN_EPS), segt)


WKV_CHUNK = HEAD_DIM
PAIR_LANES = 2 * HEAD_DIM
N_PAIRS = RWKV_HEADS // 2
_NT = (((1,), (1,)), ((), ()))
_TN = (((0,), (0,)), ((), ()))


def _dg_x3(a, b, dims):
    ah, al = _split2(a)
    bh, bl = _split2(b)
    dg = lambda x, y: lax.dot_general(x, y, dims, preferred_element_type=F32)
    return dg(ah, bh) + (dg(ah, bl) + dg(al, bh))


def _wkv_masks():
    n = PAIR_LANES
    ri = lax.broadcasted_iota(jnp.int32, (n, n), 0)
    ci = lax.broadcasted_iota(jnp.int32, (n, n), 1)
    same_head = (ri // HEAD_DIM) == (ci // HEAD_DIM)
    same16 = (ri // 16) == (ci // 16)
    same32 = (ri // 32) == (ci // 32)
    return dict(same_head=same_head, strict=same_head & (ci < ri), incl=same_head & (ci <= ri), eye=ri == ci,
                same16=same16, only32=same32 & jnp.logical_not(same16), not32=jnp.logical_not(same32),
                head0=lax.broadcasted_iota(jnp.int32, (1, n), 1) < HEAD_DIM)


def _each(fn, *lists):
    return [fn(*xs) for xs in zip(*lists)]


def _unit_lower_inverse(labs, mk):
    ds = [jnp.where(mk["same16"], lab, 0.0) for lab in labs]
    xs = [jnp.where(mk["eye"], 1.0, 0.0) + d for d in ds]
    for _ in range(3):
        ds = _each(_dot_x3, ds, ds)
        xs = _each(lambda x, d: x + _dot_x3(x, d), xs, ds)
    for level in ("only32", "not32"):
        es = [jnp.where(mk[level], lab, 0.0) for lab in labs]
        exs = _each(_dot_x3, es, xs)
        xs = _each(lambda x, ex: x + _dot_x3(x, ex), xs, exs)
    return xs


WKV_GROUP = 4


def _wkv_chunk_group(rows, s_bds, mk):
    c = WKV_CHUNK
    stack = lambda x: jnp.concatenate([jnp.where(mk["head0"], x, 0.0), jnp.where(mk["head0"], 0.0, x)], axis=0)
    dup = lambda x: jnp.concatenate([x, x], axis=0)
    col = lambda i: [row[i] for row in rows]
    n = PAIR_LANES
    a_s, r_s, v_s = [_each(stack, col(i)) for i in (0, 1, 6)]
    bh2, kh2 = [_each(dup, col(i)) for i in (4, 5)]
    bk2 = _each(lambda bt, kt: jnp.concatenate([dup(bt), dup(kt)], axis=0), col(2), col(3))
    nt = lambda x, y: _dg_x3(x, y, _NT)
    tn = lambda x, y: _dg_x3(x, y, _TN)
    below = lambda x: jnp.where(mk["strict"], x, 0.0)
    upto = lambda x: jnp.where(mk["incl"], x, 0.0)
    heads = lambda x: jnp.where(mk["same_head"], x, 0.0)
    a_bk = _each(nt, a_s, bk2)
    lab = [below(x[:, :n]) for x in a_bk]
    lak = [below(x[:, n:]) for x in a_bk]
    nt1 = lambda x, y: lax.dot_general(x.astype(BF16), y.astype(BF16), _NT, preferred_element_type=F32)
    mm1 = lambda x, y: _dot(x.astype(BF16), y.astype(BF16))
    r_bk = _each(nt1, r_s, bk2)
    lrb = [upto(x[:, :n]) for x in r_bk]
    lrk = [upto(x[:, n:]) for x in r_bk]
    t_inv = _unit_lower_inverse(lab, mk)
    lakv = _each(_dot_x3, lak, v_s)
    ta_u = _each(lambda t, a, w: _dot_x3(t, jnp.concatenate([a, w], axis=1)), t_inv, a_s, lakv)
    ta_s = [x[:, :n] for x in ta_u]
    u_s = [x[:, n:] for x in ta_u]
    m_bd = _each(lambda m, w: heads(m) + jnp.where(mk["eye"], w, 0.0), _each(tn, ta_s, bh2), col(7))
    g_bd = _each(lambda g1, g2: heads(g1 + g2), _each(tn, u_s, bh2), _each(tn, v_s, kh2))
    lrb_tau = _each(mm1, lrb, ta_u)
    q_s = [x[:, :n] for x in lrb_tau]
    y0_s = _each(lambda x, p2: x[:, n:] + p2, lrb_tau, _each(mm1, lrk, v_s))
    q = _each(lambda rt, qs: rt + (qs[:c] + qs[c:]), col(1), q_s)
    y = _each(lambda yq, y0: yq + (y0[:c] + y0[c:]), _each(nt1, q, s_bds), y0_s)
    s_new = _each(lambda sm, g: sm + g, _each(_dot_x3, s_bds, m_bd), g_bd)
    return y, s_new


def _rwkv_seq_kernel(r_ref, k_ref, v_ref, lora_ref, w0_ref, wd_ref, a0_ref, wa_ref, wg_ref, kk_ref, ka_ref,
                     rk_ref, lnw_ref, lnb_ref, seg_ref, segt_ref, o_ref, sout_ref, s_scr, y_scr):
    c = pl.program_id(1)

    @pl.when(c == 0)
    def _():
        s_scr[...] = jnp.zeros(s_scr.shape, F32)

    r = r_ref[...]
    v = v_ref[...]
    seg = seg_ref[...]
    segt = segt_ref[...]
    prm = (w0_ref[...], wd_ref[...], a0_ref[...], wa_ref[...], wg_ref[...], kk_ref[...], ka_ref[...], rk_ref[...])
    decay, log_decay, k_mod, neg_kk, b, gate, bonus = _rwkv_prep(r, k_ref[...], v, lora_ref[...], prm, seg, segt)

    n = WKV_CHUNK
    tri = (lax.broadcasted_iota(jnp.int32, (n, n), 1) <= lax.broadcasted_iota(jnp.int32, (n, n), 0)).astype(BF16)
    ld_h, ld_m, ld_l = _split3(log_decay)
    cw = _dot(tri, ld_h) + (_dot(tri, ld_m) + _dot(tri, ld_l))
    cw_prev = cw - log_decay
    cw_last = cw[n - 1:n, :]
    e_prev = jnp.exp(cw_prev)
    e_inv = jnp.exp(-cw)
    e_rem = jnp.exp(cw_last - cw)
    at = neg_kk * e_prev
    rt = r * (e_prev * decay)
    bt = b * e_inv
    kt = k_mod * e_inv
    bh = b * e_rem
    kh = k_mod * e_rem
    w_last = jnp.exp(cw_last)

    mk = _wkv_masks()
    for p0 in range(0, N_PAIRS, WKV_GROUP):
        pairs = range(p0, p0 + WKV_GROUP)
        lanes = [slice(p * PAIR_LANES, (p + 1) * PAIR_LANES) for p in pairs]
        rows = [tuple(z[:, sl] for z in (at, rt, bt, kt, bh, kh, v, w_last)) for sl in lanes]
        ys, s_new = _wkv_chunk_group(rows, [s_scr[p] for p in pairs], mk)
        for p, sl, y, s in zip(pairs, lanes, ys, s_new):
            s_scr[p] = s
            y_scr[:, sl] = y

    out = (_group_norm_rows(y_scr[...], seg, segt) * lnw_ref[...] + lnb_ref[...] + bonus) * gate
    o_ref[...] = out.astype(BF16)

    @pl.when(c == pl.num_programs(1) - 1)
    def _():
        for p in range(N_PAIRS):
            sout_ref[0, 2 * p] = s_scr[p, 0:HEAD_DIM, 0:HEAD_DIM]
            sout_ref[0, 2 * p + 1] = s_scr[p, HEAD_DIM:, HEAD_DIM:]


def _rwkv_weights(w0, w_decay_up, a0, w_aaa_up, w_gate_up, k_k, k_a, r_k, ln_x_w, ln_x_b):
    row = lambda z: z.reshape(1, D_RWKV)
    pad = lambda z: jnp.pad(z, ((0, LORA_PAD - z.shape[0]), (0, 0)))
    head_of_col = jnp.arange(D_RWKV, dtype=jnp.int32) // HEAD_DIM
    seg = (head_of_col[:, None] == jnp.arange(LANES, dtype=jnp.int32)[None, :]).astype(BF16)
    return (row(w0), pad(w_decay_up), row(a0), pad(w_aaa_up), w_gate_up, row(k_k), row(k_a), row(r_k),
            row(ln_x_w), row(ln_x_b), seg, seg.T)


def _const_spec(shape, grid_rank):
    zeros = (0,) * len(shape)
    if grid_rank == 1:
        return pl.BlockSpec(shape, lambda i: zeros)
    return pl.BlockSpec(shape, lambda i, j: zeros)


def _rwkv_seq(hx, weights, *, batch, seq):
    tc = WKV_CHUNK
    nchunk = seq // tc
    rowmap = lambda cb: (lambda b, c: (b * nchunk + c, cb))
    in_specs = [pl.BlockSpec((tc, D_RWKV), rowmap(0)),
                pl.BlockSpec((tc, D_RWKV), rowmap(1)),
                pl.BlockSpec((tc, D_RWKV), rowmap(2)),
                pl.BlockSpec((tc, 2 * LORA_PAD + GATE_LORA), rowmap(COL_LORA // (2 * LORA_PAD + GATE_LORA)))]
    in_specs += [_const_spec(w.shape, 2) for w in weights]
    return pl.pallas_call(
        _rwkv_seq_kernel,
        grid=(batch, nchunk),
        in_specs=in_specs,
        out_specs=[pl.BlockSpec((tc, D_RWKV), lambda b, c: (b * nchunk + c, 0)),
                   pl.BlockSpec((1, RWKV_HEADS, HEAD_DIM, HEAD_DIM), lambda b, c: (b, 0, 0, 0))],
        out_shape=[jax.ShapeDtypeStruct((batch * seq, D_RWKV), BF16),
                   jax.ShapeDtypeStruct((batch, RWKV_HEADS, HEAD_DIM, HEAD_DIM), F32)],
        scratch_shapes=[pltpu.VMEM((N_PAIRS, PAIR_LANES, PAIR_LANES), F32), pltpu.VMEM((tc, D_RWKV), F32)],
        compiler_params=pltpu.CompilerParams(dimension_semantics=("arbitrary", "arbitrary"),
                                             vmem_limit_bytes=VMEM_LIMIT),
    )(hx, hx, hx, hx, *weights)


def _rwkv_tok_prep_kernel(r_ref, k_ref, v_ref, lora_ref, w0_ref, wd_ref, a0_ref, wa_ref, wg_ref, kk_ref, ka_ref,
                          rk_ref, seg_ref, segt_ref, w_o, k_o, a_o, b_o, g_o, bonus_o):
    prm = (w0_ref[...], wd_ref[...], a0_ref[...], wa_ref[...], wg_ref[...], kk_ref[...], ka_ref[...], rk_ref[...])
    decay, _, k_mod, neg_kk, b, gate, bonus = _rwkv_prep(r_ref[...], k_ref[...], v_ref[...], lora_ref[...], prm,
                                                          seg_ref[...], segt_ref[...])
    w_o[...] = decay
    k_o[...] = k_mod
    a_o[...] = neg_kk
    b_o[...] = b
    g_o[...] = gate
    bonus_o[...] = bonus


def _rwkv_tok_step_kernel(r_ref, w_ref, k_ref, v_ref, a_ref, b_ref, g_ref, bonus_ref, lnw_ref, lnb_ref, s0_ref,
                          o_ref, s_ref, *, tb):
    eye = (lax.broadcasted_iota(jnp.int32, (HEAD_DIM, HEAD_DIM), 0)
           == lax.broadcasted_iota(jnp.int32, (HEAD_DIM, HEAD_DIM), 1))

    def body(bi, carry):
        s = s0_ref[bi]
        sa = jnp.sum(s * a_ref[bi], axis=2, keepdims=True)
        v_col = jnp.sum(jnp.where(eye, v_ref[bi], 0.0), axis=2, keepdims=True)
        s = s * w_ref[bi] + sa * b_ref[bi] + v_col * k_ref[bi]
        s_ref[bi] = s
        y_col = jnp.sum(s * r_ref[bi], axis=2, keepdims=True)
        y = jnp.sum(jnp.where(eye, y_col, 0.0), axis=1, keepdims=True)
        o_ref[bi] = (_group_norm_head(y) * lnw_ref[0] + lnb_ref[0] + bonus_ref[bi]) * g_ref[bi]
        return carry

    lax.fori_loop(0, tb, body, 0)


def _rwkv_tok(hx, state, weights, *, tb):
    nb = hx.shape[0]
    (w0, wd, a0, wa, wg, k_k, k_a, r_k, ln_w, ln_b, seg, segt) = weights
    prep_w = (w0, wd, a0, wa, wg, k_k, k_a, r_k, seg, segt)
    colmap = lambda cb: (lambda i: (0, cb))
    in_specs = [pl.BlockSpec((nb, D_RWKV), colmap(0)),
                pl.BlockSpec((nb, D_RWKV), colmap(1)),
                pl.BlockSpec((nb, D_RWKV), colmap(2)),
                pl.BlockSpec((nb, 2 * LORA_PAD + GATE_LORA), colmap(COL_LORA // (2 * LORA_PAD + GATE_LORA)))]
    in_specs += [_const_spec(w.shape, 1) for w in prep_w]
    rows = jax.ShapeDtypeStruct((nb, D_RWKV), F32)
    prepped = pl.pallas_call(
        _rwkv_tok_prep_kernel,
        grid=(1,),
        in_specs=in_specs,
        out_specs=[pl.BlockSpec((nb, D_RWKV), lambda i: (0, 0))] * 6,
        out_shape=[rows] * 6,
        compiler_params=pltpu.CompilerParams(dimension_semantics=("arbitrary",),
                                             vmem_limit_bytes=VMEM_LIMIT),
    )(hx, hx, hx, hx, *prep_w)
    decay, k_mod, neg_kk, b, gate, bonus = prepped
    heads = lambda z: z.reshape(nb, RWKV_HEADS, 1, HEAD_DIM)
    vecs = [heads(z) for z in (hx[:, COL_R:COL_R + D_RWKV], decay, k_mod, hx[:, COL_V:COL_V + D_RWKV], neg_kk, b,
                               gate, bonus)]
    vec_spec = pl.BlockSpec((tb, RWKV_HEADS, 1, HEAD_DIM), lambda i: (i, 0, 0, 0))
    ln_spec = pl.BlockSpec((1, RWKV_HEADS, 1, HEAD_DIM), lambda i: (0, 0, 0, 0))
    state_spec = pl.BlockSpec((tb, RWKV_HEADS, HEAD_DIM, HEAD_DIM), lambda i: (i, 0, 0, 0))
    out, new_state = pl.pallas_call(
        functools.partial(_rwkv_tok_step_kernel, tb=tb),
        grid=(nb // tb,),
        in_specs=[vec_spec] * 8 + [ln_spec, ln_spec, state_spec],
        out_specs=[vec_spec, state_spec],
        out_shape=[jax.ShapeDtypeStruct((nb, RWKV_HEADS, 1, HEAD_DIM), F32),
                   jax.ShapeDtypeStruct(state.shape, F32)],
        compiler_params=pltpu.CompilerParams(dimension_semantics=("arbitrary",),
                                             vmem_limit_bytes=VMEM_LIMIT),
    )(*vecs, ln_w.reshape(1, RWKV_HEADS, 1, HEAD_DIM), ln_b.reshape(1, RWKV_HEADS, 1, HEAD_DIM), state)
    return out.reshape(nb, D_RWKV).astype(BF16), new_state


def _route(logits, bias):
    scores = _sigmoid(logits)
    sel = scores + bias
    rows = logits.shape[0]
    lane_i = lax.broadcasted_iota(jnp.int32, (rows, N_EXPERTS), 1)
    grp = lane_i // GROUP_SIZE
    lane = lane_i.astype(F32)
    neg_inf = -jnp.inf

    def take_max(x):
        m = jnp.max(x, axis=1, keepdims=True)
        idx = jnp.min(jnp.where(x == m, lane, float(N_EXPERTS)), axis=1, keepdims=True)
        return m, lane == idx

    gscore = []
    for g in range(N_GROUPS):
        x = jnp.where(grp == g, sel, neg_inf)
        m1, hit = take_max(x)
        m2 = jnp.max(jnp.where(hit, neg_inf, x), axis=1, keepdims=True)
        gscore.append(m1 + m2)
    emask = jnp.zeros((rows, N_EXPERTS), jnp.bool_)
    for g in range(N_GROUPS):
        ahead = jnp.zeros((rows, 1), jnp.int32)
        for g2 in range(N_GROUPS):
            if g2 == g:
                continue
            beats = (gscore[g2] >= gscore[g]) if g2 < g else (gscore[g2] > gscore[g])
            ahead = ahead + beats.astype(jnp.int32)
        emask = emask | ((grp == g) & (ahead < TOPK_GROUPS))
    x = jnp.where(emask, sel, neg_inf)
    chosen = jnp.zeros((rows, N_EXPERTS), jnp.bool_)
    slot_lane = lax.broadcasted_iota(jnp.int32, (rows, LANES), 1)
    top_idx = jnp.zeros((rows, LANES), F32)
    top_w = jnp.zeros((rows, LANES), F32)
    for kth in range(TOP_K):
        m = jnp.max(x, axis=1, keepdims=True)
        idx = jnp.min(jnp.where(x == m, lane, float(N_EXPERTS)), axis=1, keepdims=True)
        hit = lane == idx
        chosen = chosen | hit
        x = jnp.where(hit, neg_inf, x)
        top_idx = jnp.where(slot_lane == kth, idx, top_idx)
        top_w = jnp.where(slot_lane == kth, jnp.sum(jnp.where(hit, scores, 0.0), axis=1, keepdims=True), top_w)
    top_w = top_w / jnp.sum(top_w, axis=1, keepdims=True) * ROUTE_SCALE
    return top_idx.astype(jnp.int32), top_w, chosen.astype(F32)


def _out_proj_kernel(pool_ref, rwkv_ref, x_ref, g1_ref, sc_ref, sh_ref, wp_ref, wr_ref, npost_ref, npre_ref,
                     wrt_ref, rb_ref, sg_ref, su_ref, sd_ref, x1_ref, u2_ref, idx_ref, topw_ref, chosen_ref,
                     shared_ref):
    m = _dot(pool_ref[...], wp_ref[...]) + _dot(rwkv_ref[...], wr_ref[...])
    x1 = x_ref[...] + g1_ref[0] * _rms(m, npost_ref[...])
    x1_ref[...] = x1
    u2 = _rms(x1, npre_ref[...]) * (1.0 + sc_ref[0]) + sh_ref[0]
    u2_ref[...] = u2
    top_idx, top_w, chosen = _route(_dot_x3(u2, wrt_ref[...]), rb_ref[...])
    idx_ref[...] = top_idx
    topw_ref[...] = top_w
    chosen_ref[...] = chosen
    u2b = u2.astype(BF16)
    h = _silu(_dot(u2b, sg_ref[...])) * _dot(u2b, su_ref[...])
    shared_ref[...] = _dot(h.astype(BF16), sd_ref[...])


def _out_proj(pool_out, rwkv_out, x, g1, sc2, sh2, w_out_bf16, n_post, n_pre, w_router, router_bias, sg, su, sd, *,
              tm, tiles_per_seq):
    m, d = x.shape
    if tiles_per_seq is None:
        mod_spec = pl.BlockSpec((1, tm, d), lambda i: (i, 0, 0))
    else:
        mod_spec = pl.BlockSpec((1, 1, d), lambda i: (i // tiles_per_seq, 0, 0))
    row = lambda width: pl.BlockSpec((tm, width), lambda i: (i, 0))
    return pl.pallas_call(
        _out_proj_kernel,
        grid=(m // tm,),
        in_specs=[row(D_POOL), row(D_RWKV), row(d), mod_spec, mod_spec, mod_spec,
                  _const_spec((D_POOL, d), 1), _const_spec((D_RWKV, d), 1),
                  _const_spec((1, d), 1), _const_spec((1, d), 1),
                  _const_spec((d, N_EXPERTS), 1), _const_spec((1, N_EXPERTS), 1),
                  _const_spec(sg.shape, 1), _const_spec(su.shape, 1), _const_spec(sd.shape, 1)],
        out_specs=[row(d), row(d), row(LANES), row(LANES), row(N_EXPERTS), row(d)],
        out_shape=[jax.ShapeDtypeStruct((m, d), F32), jax.ShapeDtypeStruct((m, d), F32),
                   jax.ShapeDtypeStruct((m, LANES), jnp.int32), jax.ShapeDtypeStruct((m, LANES), F32),
                   jax.ShapeDtypeStruct((m, N_EXPERTS), F32), jax.ShapeDtypeStruct((m, d), F32)],
        compiler_params=pltpu.CompilerParams(dimension_semantics=("arbitrary",),
                                             vmem_limit_bytes=VMEM_LIMIT),
    )(pool_out, rwkv_out, x, g1, sc2, sh2, w_out_bf16[:D_POOL], w_out_bf16[D_POOL:],
      n_post.reshape(1, d), n_pre.reshape(1, d), w_router, router_bias.reshape(1, N_EXPERTS), sg, su, sd)


MOE_TM = 384


ROW_GROUP = 8


def _experts_kernel(te_ref, fresh_ref, wslot_ref, nexte_ref, groups_ref, nused_ref, idx_ref, idx_next_ref, ws_ref,
                    x_hbm, wg_hbm, wu_hbm, wd_hbm, ys_ref, wg_buf, wu_buf, wd_buf, g_bf, u_bf, d_bf, x_buf,
                    w_sems, x_sems):
    t = pl.program_id(0)
    n_used = nused_ref[0]

    def weight_copies(e, slot):
        return (pltpu.make_async_copy(wg_hbm.at[e], wg_buf.at[slot], w_sems.at[slot]),
                pltpu.make_async_copy(wu_hbm.at[e], wu_buf.at[slot], w_sems.at[slot]),
                pltpu.make_async_copy(wd_hbm.at[e], wd_buf.at[slot], w_sems.at[slot]))

    def start_rows(table_ref, slot, n_groups):
        def body(g, carry):
            for r in range(ROW_GROUP):
                j = g * ROW_GROUP + r
                pltpu.make_async_copy(x_hbm.at[pl.ds(table_ref[0, 0, j], 1)], x_buf.at[slot, pl.ds(j, 1)],
                                      x_sems.at[slot]).start()
            return carry
        lax.fori_loop(0, n_groups, body, 0)

    @pl.when(t == 0)
    def _():
        for cp in weight_copies(te_ref[0], 0):
            cp.start(priority=1)
        x_buf[...] = jnp.zeros(x_buf.shape, F32)
        start_rows(idx_ref, 0, groups_ref[0])

    @pl.when(t + 1 < n_used)
    def _():
        start_rows(idx_next_ref, (t + 1) % 2, groups_ref[t + 1])

    @pl.when(jnp.logical_and(t < n_used, fresh_ref[t] == 1))
    def _():
        slot = wslot_ref[t]
        for cp in weight_copies(te_ref[t], slot):
            cp.wait()

        @pl.when(nexte_ref[t] >= 0)
        def _():
            for cp in weight_copies(nexte_ref[t], 1 - slot):
                cp.start(priority=1)

        g_bf[...] = wg_buf[slot].astype(BF16)
        u_bf[...] = wu_buf[slot].astype(BF16)
        d_bf[...] = wd_buf[slot].astype(BF16)

    @pl.when(t < n_used)
    def _():
        cur = t % 2
        n_rows = pl.multiple_of(groups_ref[t] * ROW_GROUP, ROW_GROUP)
        pltpu.make_async_copy(x_hbm.at[pl.ds(0, n_rows)], x_buf.at[cur, pl.ds(0, n_rows)], x_sems.at[cur]).wait()
        x = x_buf[cur].astype(BF16)
        h = _silu(_dot(x, g_bf[...])) * _dot(x, u_bf[...])
        diag = (lax.broadcasted_iota(jnp.int32, (MOE_TM, MOE_TM), 0)
                == lax.broadcasted_iota(jnp.int32, (MOE_TM, MOE_TM), 1))
        w_col = jnp.sum(jnp.where(diag, ws_ref[0], 0.0), axis=1, keepdims=True)
        ys_ref[...] = _pack_bf16_halves(_dot(h.astype(BF16), d_bf[...]) * w_col)

    @pl.when(t >= n_used)
    def _():
        ys_ref[...] = jnp.zeros(ys_ref.shape, jnp.uint32)


def _experts(tables, x_all, exp_gate, exp_up, exp_down):
    tile_expert, fresh, wslot, next_e, row_groups, n_used, slot_tok, slot_w = tables
    d = x_all.shape[1]
    n_tiles = tile_expert.shape[0]
    n_slots = n_tiles * MOE_TM
    idx_block = (1, 1, MOE_TM)
    any_spec = pl.BlockSpec(memory_space=pl.ANY)
    live = lambda t, nused: jnp.where(t < nused[0], t, n_tiles - 1)
    grid_spec = pltpu.PrefetchScalarGridSpec(
        num_scalar_prefetch=6,
        grid=(n_tiles,),
        in_specs=[pl.BlockSpec(idx_block, lambda t, *s: (live(t, s[5]), 0, 0), memory_space=pltpu.SMEM),
                  pl.BlockSpec(idx_block, lambda t, *s: (live(t + 1, s[5]), 0, 0), memory_space=pltpu.SMEM),
                  pl.BlockSpec(idx_block, lambda t, *s: (live(t, s[5]), 0, 0)),
                  any_spec, any_spec, any_spec, any_spec],
        out_specs=pl.BlockSpec((MOE_TM, d // 2), lambda t, *s: (live(t, s[5]), 0)),
        scratch_shapes=[pltpu.VMEM((2, d, D_EXPERT), F32), pltpu.VMEM((2, d, D_EXPERT), F32),
                        pltpu.VMEM((2, D_EXPERT, d), F32),
                        pltpu.VMEM((d, D_EXPERT), BF16), pltpu.VMEM((d, D_EXPERT), BF16),
                        pltpu.VMEM((D_EXPERT, d), BF16),
                        pltpu.VMEM((2, MOE_TM, d), F32),
                        pltpu.SemaphoreType.DMA((2,)), pltpu.SemaphoreType.DMA((2,))],
    )
    idx_tiles = slot_tok.reshape(n_tiles, 1, MOE_TM)
    return pl.pallas_call(
        _experts_kernel,
        grid_spec=grid_spec,
        out_shape=jax.ShapeDtypeStruct((n_slots, d // 2), jnp.uint32),
        compiler_params=pltpu.CompilerParams(dimension_semantics=("arbitrary",),
                                             vmem_limit_bytes=VMEM_LIMIT),
    )(tile_expert, fresh, wslot, next_e, row_groups, n_used, idx_tiles, idx_tiles,
      slot_w.reshape(n_tiles, 1, MOE_TM), x_all, exp_gate, exp_up, exp_down)


ASSIGN_BITS = 17


def _dispatch(chosen, top_idx, top_w):
    n = chosen.shape[0]
    n_assign = n * TOP_K
    low_mask = (1 << ASSIGN_BITS) - 1
    assert n_assign + MOE_TM <= low_mask and N_EXPERTS << ASSIGN_BITS < 2 ** 31
    n_tiles = -(-n_assign // MOE_TM) + N_EXPERTS
    n_slots = n_tiles * MOE_TM
    sel = chosen.astype(jnp.int32)
    counts = jnp.sum(sel, axis=0)
    rank = jnp.cumsum(sel, axis=0) - sel
    tiles_e = (counts + MOE_TM - 1) // MOE_TM
    tile_end = jnp.cumsum(tiles_e)
    slot = (tile_end - tiles_e)[None, :] * MOE_TM + rank
    expert_ids = jnp.arange(N_EXPERTS, dtype=jnp.int32)
    slot_of_pair = jnp.stack([jnp.sum(jnp.where(top_idx[:, k:k + 1] == expert_ids[None, :], slot, 0), axis=1)
                              for k in range(TOP_K)])
    shift = 1 << ASSIGN_BITS
    int_max = jnp.iinfo(jnp.int32).max
    real_key = top_idx.reshape(-1) * shift + jnp.arange(n_assign, dtype=jnp.int32)
    pad_j = jnp.arange(MOE_TM - 1, dtype=jnp.int32)[None, :]
    pad_key = jnp.where(pad_j < (tiles_e * MOE_TM - counts)[:, None],
                        expert_ids[:, None] * shift + n_assign + pad_j, int_max)
    n_rest = n_slots - n_assign
    keys = jnp.concatenate([real_key, pad_key.reshape(-1),
                            jnp.full((n_rest - pad_key.size,), int_max, jnp.int32)])
    weights = jnp.concatenate([top_w.reshape(-1), jnp.zeros((n_rest,), F32)])
    sorted_key, sorted_w = lax.sort((keys, weights), num_keys=1)
    pair = sorted_key & low_mask
    is_real = pair < n_assign
    slot_tok = jnp.where(is_real, pair // TOP_K, 0)
    slot_w = jnp.where(is_real, sorted_w, 0.0)
    n_used = tile_end[-1]
    tile_ids = jnp.minimum(jnp.arange(n_tiles, dtype=jnp.int32), n_used - 1)
    tile_expert = jnp.minimum(jnp.searchsorted(tile_end, tile_ids, side="right"), N_EXPERTS - 1).astype(jnp.int32)
    in_use = jnp.arange(n_tiles, dtype=jnp.int32) < n_used
    prev_expert = jnp.concatenate([jnp.full((1,), -1, jnp.int32), tile_expert[:-1]])
    fresh = jnp.logical_and(in_use, tile_expert != prev_expert).astype(jnp.int32)
    wslot = (jnp.cumsum(fresh) - 1) % 2
    later = jnp.where(counts > 0, expert_ids, N_EXPERTS)
    next_used = jnp.concatenate([lax.cummin(later, reverse=True)[1:], jnp.full((1,), N_EXPERTS, jnp.int32)])
    next_used = jnp.where(next_used >= N_EXPERTS, -1, next_used)
    tile_in_expert = jnp.arange(n_tiles, dtype=jnp.int32) - (tile_end - tiles_e)[tile_expert]
    rows_in_tile = jnp.clip(counts[tile_expert] - tile_in_expert * MOE_TM, 0, MOE_TM)
    row_groups = jnp.where(in_use, (rows_in_tile + ROW_GROUP - 1) // ROW_GROUP, 0)
    tables = (tile_expert, fresh, wslot.astype(jnp.int32), next_used[tile_expert].astype(jnp.int32),
              row_groups.astype(jnp.int32), n_used.reshape(1).astype(jnp.int32), slot_tok.astype(jnp.int32), slot_w)
    return slot_of_pair, tables


def _final_kernel(idx_ref, idx_next_ref, shared_ref, ys_hbm, x1_ref, g2_ref, npost_ref, o_ref, rows_buf, sems, *,
                  tm):
    i = pl.program_id(0)
    n_rows = TOP_K * tm

    def start_gather(table_ref, buf_slot):
        def body(j, carry):
            pltpu.make_async_copy(ys_hbm.at[pl.ds(table_ref[0, 0, j], 1)], rows_buf.at[buf_slot, pl.ds(j, 1)],
                                  sems.at[buf_slot]).start()
            return carry
        lax.fori_loop(0, n_rows, body, 0, unroll=8)

    @pl.when(i == 0)
    def _():
        start_gather(idx_ref, 0)

    @pl.when(i + 1 < pl.num_programs(0))
    def _():
        start_gather(idx_next_ref, (i + 1) % 2)

    cur = i % 2
    pltpu.make_async_copy(ys_hbm.at[pl.ds(0, n_rows)], rows_buf.at[cur], sems.at[cur]).wait()
    left, right = _unpack_bf16_halves(rows_buf[cur, 0:tm])
    for kth in range(1, TOP_K):
        l_k, r_k = _unpack_bf16_halves(rows_buf[cur, kth * tm:(kth + 1) * tm])
        left, right = left + l_k, right + r_k
    routed = jnp.concatenate([left, right], axis=1)
    o_ref[...] = x1_ref[...] + g2_ref[0] * _rms(routed + shared_ref[...], npost_ref[...])


def _final(shared, ys, slot_tiles, x1, g2, n_post, *, tm, tiles_per_seq, tile_offset):
    m, d = x1.shape
    nt = m // tm
    if tiles_per_seq is None:
        mod_spec = pl.BlockSpec((1, tm, d), lambda i: (i, 0, 0))
    else:
        mod_spec = pl.BlockSpec((1, 1, d), lambda i: (i // tiles_per_seq, 0, 0))
    row = pl.BlockSpec((tm, d), lambda i: (i, 0))
    idx_block = (1, 1, TOP_K * tm)
    return pl.pallas_call(
        functools.partial(_final_kernel, tm=tm),
        grid=(nt,),
        in_specs=[pl.BlockSpec(idx_block, lambda i: (i + tile_offset, 0, 0), memory_space=pltpu.SMEM),
                  pl.BlockSpec(idx_block, lambda i: (jnp.minimum(i + 1, nt - 1) + tile_offset, 0, 0),
                               memory_space=pltpu.SMEM),
                  row, pl.BlockSpec(memory_space=pl.ANY), row, mod_spec, _const_spec((1, d), 1)],
        out_specs=row,
        out_shape=jax.ShapeDtypeStruct((m, d), F32),
        scratch_shapes=[pltpu.VMEM((2, TOP_K * tm, d // 2), jnp.uint32), pltpu.SemaphoreType.DMA((2,))],
        compiler_params=pltpu.CompilerParams(dimension_semantics=("arbitrary",),
                                             vmem_limit_bytes=VMEM_LIMIT),
    )(slot_tiles, slot_tiles, shared, ys, x1, g2, n_post.reshape(1, d))


def _pad_cols_in(w_in, mu_shift):
    d = w_in.shape[0]
    hp = w_in[:, :D_POOL]
    hr = w_in[:, D_POOL:]
    cuts = [3 * D_RWKV, 3 * D_RWKV + DECAY_LORA, 3 * D_RWKV + DECAY_LORA + AAA_LORA]
    rkv, wl, al, gl = jnp.split(hr, cuts, axis=1)
    zpad = lambda z: jnp.pad(z, ((0, 0), (0, LORA_PAD - z.shape[1])))
    w = jnp.concatenate([rkv, hp, zpad(wl), zpad(al), gl], axis=1).astype(BF16)
    mu_rkv, mu_wl, mu_al, mu_gl = jnp.split(mu_shift, cuts)
    zp1 = lambda z: jnp.pad(z, (0, LORA_PAD - z.shape[0]))
    mu = jnp.concatenate([mu_rkv, jnp.zeros((D_POOL,), F32), zp1(mu_wl), zp1(mu_al), mu_gl])
    assert w.shape == (d, N_IN_PAD) and mu.shape == (N_IN_PAD,)
    return w, mu


def kernel(x_prompt, x_sample, c_prompt, c_sample, state_shift, state_pool, state_wkv, w_ada, b_ada, norm_pre_mix, norm_post_mix, norm_pre_ffn, norm_post_ffn, w_in, mu_shift, pool_w, pool_scale, w0, w_decay_up, a0, w_aaa_up, w_gate_up, k_k, k_a, r_k, ln_x_w, ln_x_b, w_out, w_router, router_bias, exp_gate, exp_up, exp_down, sh_gate, sh_up, sh_down):
    bp, seq, d = x_prompt.shape
    bs = x_sample.shape[0]
    assert w_ada.shape[0] == 1 and x_sample.shape[1] == 1
    np_rows = bp * seq

    c_all = jnp.concatenate([c_prompt, c_sample], axis=0)
    pad_rows = (-c_all.shape[0]) % 16
    ada = _ada(jnp.pad(c_all, ((0, pad_rows), (0, 0))), w_ada[0], b_ada[0])
    sh1, sc1, g1, sh2, sc2, g2 = jnp.split(ada, 6, axis=-1)
    pr = lambda z: z[:bp].reshape(bp, 1, d)
    TS = min(128, bs)
    sr = lambda z: z[bp:bp + bs].reshape(bs // TS, TS, d)

    w_in_p, mu_p = _pad_cols_in(w_in[0], mu_shift[0])
    xp = x_prompt.reshape(np_rows, d)
    xs = x_sample.reshape(bs, d)

    TM_IN = min(1024, seq)
    hx_p, u_tail = _in_proj(xp, norm_pre_mix[0], pr(sc1), pr(sh1), w_in_p, mu_p, tm=TM_IN,
                            tiles_per_seq=seq // TM_IN)
    hx_s, u_s = _in_proj(xs, norm_pre_mix[0], sr(sc1), sr(sh1), w_in_p, mu_p, tm=TS, tiles_per_seq=1,
                         prev=state_shift[0])
    new_shift_prompt = u_tail.reshape(bp, seq // TM_IN, 8, d)[:, -1, -1][None]
    new_shift_sample = u_s[None]

    pool_w_b = pool_w[0].astype(BF16)
    TC_POOL = min(512, seq)
    pool_p = _pool(hx_p, COL_POOL // D_POOL, pool_w_b, pool_scale[0], tc=TC_POOL, tiles_per_seq=seq // TC_POOL,
                   full_count=False)
    hp_p = hx_p[:, COL_POOL:COL_POOL + D_POOL].reshape(bp, seq, D_POOL)
    new_pool_prompt = hp_p[:, seq - POOL_BUF:][None]
    hp_s = hx_s[:, COL_POOL:COL_POOL + D_POOL]
    ext_s = jnp.concatenate([state_pool[0], hp_s[:, None, :]], axis=1)
    assert PAST_LEN + 1 >= max(POOL_WINDOWS)
    pool_s = _pool(ext_s.reshape(bs * (POOL_BUF + 1), D_POOL), 0, pool_w_b, pool_scale[0],
                   tc=bs * (POOL_BUF + 1), tiles_per_seq=1, full_count=True)
    pool_s = pool_s.reshape(bs, POOL_BUF + 1, D_POOL)[:, -1]
    new_pool_sample = ext_s[:, 1:][None]

    rw = _rwkv_weights(w0[0], w_decay_up[0], a0[0], w_aaa_up[0], w_gate_up[0], k_k[0], k_a[0], r_k[0],
                       ln_x_w[0], ln_x_b[0])
    rwkv_p, wkv_p = _rwkv_seq(hx_p, rw, batch=bp, seq=seq)
    rwkv_s, wkv_s = _rwkv_tok(hx_s, state_wkv[0], rw, tb=8)

    w_out_b = w_out[0].astype(BF16)
    assert np_rows % TS == 0 and seq % TS == 0
    sg, su, sd = sh_gate[0].astype(BF16), sh_up[0].astype(BF16), sh_down[0].astype(BF16)
    TM_OUT = min(256, seq)
    outs_p = _out_proj(pool_p, rwkv_p, xp, pr(g1), pr(sc2), pr(sh2), w_out_b, norm_post_mix[0], norm_pre_ffn[0],
                       w_router[0], router_bias[0], sg, su, sd, tm=TM_OUT, tiles_per_seq=seq // TM_OUT)
    outs_s = _out_proj(pool_s, rwkv_s, xs, sr(g1), sr(sc2), sr(sh2), w_out_b, norm_post_mix[0], norm_pre_ffn[0],
                       w_router[0], router_bias[0], sg, su, sd, tm=TS, tiles_per_seq=None)
    (x1_p, u2_p, idx_p, topw_p, chosen_p, shared_p), (x1_s, u2_s, idx_s, topw_s, chosen_s, shared_s) = outs_p, outs_s
    u2_all = jnp.concatenate([u2_p, u2_s], axis=0)
    idx_all = jnp.concatenate([idx_p[:, :TOP_K], idx_s[:, :TOP_K]], axis=0)
    topw_all = jnp.concatenate([topw_p[:, :TOP_K], topw_s[:, :TOP_K]], axis=0)
    chosen_all = jnp.concatenate([chosen_p, chosen_s], axis=0)

    n_all = np_rows + bs
    slot_of_pair, tables = _dispatch(chosen_all, idx_all, topw_all)
    ys = _experts(tables, u2_all, exp_gate[0], exp_up[0], exp_down[0])

    slot_tiles = slot_of_pair.reshape(TOP_K, n_all // TS, TS).transpose(1, 0, 2).reshape(n_all // TS, 1, TOP_K * TS)
    y_p = _final(shared_p, ys, slot_tiles, x1_p, pr(g2), norm_post_ffn[0], tm=TS, tiles_per_seq=seq // TS,
                 tile_offset=0)
    y_s = _final(shared_s, ys, slot_tiles, x1_s, sr(g2), norm_post_ffn[0], tm=TS, tiles_per_seq=None,
                 tile_offset=np_rows // TS)

    return (y_p.reshape(bp, seq, d), y_s.reshape(bs, 1, d), new_shift_prompt, new_pool_prompt, wkv_p[None],
            new_shift_sample, new_pool_sample, wkv_s[None])
```

```python
import functools

import jax
import jax.numpy as jnp
from jax import lax
from jax.experimental import pallas as pl
from jax.experimental.pallas import tpu as pltpu

F32 = jnp.float32
BF16 = jnp.bfloat16

D_MODEL = 2048
D_POOL = 512
POOL_WINDOWS = (2, 4, 8, 16)
POOL_CH = 128
POOL_BUF = 15
D_RWKV = 1536
HEAD_DIM = 64
RWKV_HEADS = 24
DECAY_LORA = 96
AAA_LORA = 96
GATE_LORA = 256
LORA_PAD = 128
GN_EPS = 64e-5
N_EXPERTS = 256
TOP_K = 8
N_GROUPS = 8
GROUP_SIZE = N_EXPERTS // N_GROUPS
TOPK_GROUPS = 4
D_EXPERT = 512
ROUTE_SCALE = 2.5
EPS = 1e-6
PAST_LEN = 16384

COL_R, COL_K, COL_V = 0, D_RWKV, 2 * D_RWKV
COL_POOL = 3 * D_RWKV
COL_LORA = COL_POOL + D_POOL
N_IN_PAD = COL_LORA + 2 * LORA_PAD + GATE_LORA
LANES = 128
VMEM_LIMIT = 56 * 1024 * 1024


def _dot(a, b):
    return jnp.dot(a, b, preferred_element_type=F32)


def _split2(x):
    hi = x.astype(BF16)
    lo = (x - hi.astype(F32)).astype(BF16)
    return hi, lo


def _split3(x):
    hi = x.astype(BF16)
    r = x - hi.astype(F32)
    mid = r.astype(BF16)
    lo = (r - mid.astype(F32)).astype(BF16)
    return hi, mid, lo


def _dot_x3(a, b):
    ah, al = _split2(a)
    bh, bl = _split2(b)
    return _dot(ah, bh) + (_dot(ah, bl) + _dot(al, bh))


def _dot_exact_rhs(a, b_bf16):
    h, m, l = _split3(a)
    return _dot(h, b_bf16) + (_dot(m, b_bf16) + _dot(l, b_bf16))


def _sigmoid(x):
    return 1.0 / (1.0 + jnp.exp(-x))


def _silu(x):
    return x * _sigmoid(x)


def _pack_bf16_halves(y):
    half = y.shape[1] // 2
    bits = pltpu.bitcast(y.astype(BF16).astype(F32), jnp.uint32)
    return bits[:, :half] | (bits[:, half:] >> 16)


def _unpack_bf16_halves(p):
    return (pltpu.bitcast(p & jnp.uint32(0xFFFF0000), F32), pltpu.bitcast(p << 16, F32))


def _rms(x, g):
    return x * lax.rsqrt(jnp.mean(x * x, axis=-1, keepdims=True) + EPS) * g


def _ada_kernel(c_ref, w_ref, b_ref, o_ref):
    o_ref[...] = _dot_x3(_silu(c_ref[...]), w_ref[...]) + b_ref[...]


def _ada(c_all, w_ada, b_ada):
    m, d = c_all.shape
    n = w_ada.shape[1]
    tn = 512
    return pl.pallas_call(
        _ada_kernel,
        grid=(n // tn,),
        in_specs=[pl.BlockSpec((m, d), lambda j: (0, 0)),
                  pl.BlockSpec((d, tn), lambda j: (0, j)),
                  pl.BlockSpec((1, tn), lambda j: (0, j))],
        out_specs=pl.BlockSpec((m, tn), lambda j: (0, j)),
        out_shape=jax.ShapeDtypeStruct((m, n), F32),
        compiler_params=pltpu.CompilerParams(dimension_semantics=("arbitrary",),
                                             vmem_limit_bytes=VMEM_LIMIT),
    )(c_all, w_ada, b_ada.reshape(1, n))


def _in_proj_kernel(*refs, tiles_per_seq, explicit_prev, tm):
    if explicit_prev:
        x_ref, g_ref, sc_ref, sh_ref, prev_ref, w_ref, mu_ref, hx_ref, u_ref, ub_scr = refs
    else:
        x_ref, g_ref, sc_ref, sh_ref, w_ref, mu_ref, hx_ref, u_ref, ub_scr, carry_scr = refs
    i = pl.program_id(0)
    j = pl.program_id(1)

    @pl.when(j == 0)
    def _():
        u = _rms(x_ref[...], g_ref[...])
        u = u * (1.0 + sc_ref[0]) + sh_ref[0]
        ub_scr[...] = u.astype(BF16)
        if explicit_prev:
            u_ref[...] = u
        else:
            u_ref[...] = u[tm - 8:, :]

    w = w_ref[...]
    h = _dot(ub_scr[...], w)
    if explicit_prev:
        hp = _dot(prev_ref[...].astype(BF16), w)
    else:
        @pl.when(i == 0)
        def _():
            carry_scr[j] = jnp.zeros(carry_scr.shape[1:], F32)

        first = jnp.where(i % tiles_per_seq == 0, 0.0, carry_scr[j, 0:1, :])
        row0 = lax.broadcasted_iota(jnp.int32, h.shape, 0) == 0
        hp = jnp.where(row0, first, pltpu.roll(h, 1, axis=0))
        carry_scr[j, 0:1, :] = h[tm - 1:tm, :]
    hx_ref[...] = h + (hp - h) * mu_ref[...]


def _in_proj(x, gamma, sc, sh, w_bf16, mu_pad, *, tm, tiles_per_seq, prev=None):
    m, d = x.shape
    n = w_bf16.shape[1]
    tn = 512
    nt = n // tn
    explicit_prev = prev is not None
    if explicit_prev:
        mod_spec = pl.BlockSpec((1, tm, d), lambda i, j: (i, 0, 0))
    else:
        mod_spec = pl.BlockSpec((1, 1, d), lambda i, j: (i // tiles_per_seq, 0, 0))
    in_specs = [pl.BlockSpec((tm, d), lambda i, j: (i, 0)),
                pl.BlockSpec((1, d), lambda i, j: (0, 0)),
                mod_spec, mod_spec]
    args = [x, gamma.reshape(1, d), sc, sh]
    if explicit_prev:
        in_specs.append(pl.BlockSpec((tm, d), lambda i, j: (i, 0)))
        args.append(prev)
    in_specs += [pl.BlockSpec((d, tn), lambda i, j: (0, j)),
                 pl.BlockSpec((1, tn), lambda i, j: (0, j))]
    args += [w_bf16, mu_pad.reshape(1, n)]
    scratch = [pltpu.VMEM((tm, d), BF16)]
    if explicit_prev:
        u_shape, u_spec = (m, d), pl.BlockSpec((tm, d), lambda i, j: (i, 0))
    else:
        u_shape, u_spec = (m // tm * 8, d), pl.BlockSpec((8, d), lambda i, j: (i, 0))
        scratch.append(pltpu.VMEM((nt, 8, tn), F32))
    return pl.pallas_call(
        functools.partial(_in_proj_kernel, tiles_per_seq=tiles_per_seq, explicit_prev=explicit_prev, tm=tm),
        grid=(m // tm, nt),
        in_specs=in_specs,
        out_specs=[pl.BlockSpec((tm, tn), lambda i, j: (i, j)), u_spec],
        out_shape=[jax.ShapeDtypeStruct((m, n), F32), jax.ShapeDtypeStruct(u_shape, F32)],
        scratch_shapes=scratch,
        compiler_params=pltpu.CompilerParams(dimension_semantics=("arbitrary", "arbitrary"),
                                             vmem_limit_bytes=VMEM_LIMIT),
    )(*args)


POOL_HALO = 16


def _pool_kernel(p_ref, pw_ref, ps_ref, o_ref, ext_scr, *, tiles_per_seq, full_count, tc):
    i = pl.program_id(0)

    @pl.when(i % tiles_per_seq == 0)
    def _():
        ext_scr[0:POOL_HALO, :] = jnp.zeros((POOL_HALO, D_POOL), F32)

    p = p_ref[...]
    ext_scr[POOL_HALO:POOL_HALO + tc, :] = p
    pos = (i % tiles_per_seq) * tc + lax.broadcasted_iota(jnp.int32, (tc, 1), 0)
    outs = []
    for gi, win in enumerate(POOL_WINDOWS):
        lo = gi * POOL_CH
        pg = p[:, lo:lo + POOL_CH]
        acc = pg
        for s in range(1, win):
            acc = acc + ext_scr[POOL_HALO - s:POOL_HALO - s + tc, lo:lo + POOL_CH]
        if full_count:
            dgi = acc / float(win) - pg
        else:
            cnt = jnp.minimum(pos + 1, win).astype(F32)
            dgi = acc / cnt - pg
        outs.append(_dot(dgi.astype(BF16), pw_ref[gi]))
    y = jnp.concatenate(outs, axis=-1) * ps_ref[...]
    o_ref[...] = y.astype(BF16)
    ext_scr[0:POOL_HALO, :] = ext_scr[tc:tc + POOL_HALO, :]


def _pool(src, col_block, pool_w_bf16, pool_scale, *, tc, tiles_per_seq, full_count):
    m = src.shape[0]
    return pl.pallas_call(
        functools.partial(_pool_kernel, tiles_per_seq=tiles_per_seq, full_count=full_count, tc=tc),
        grid=(m // tc,),
        in_specs=[pl.BlockSpec((tc, D_POOL), lambda i: (i, col_block)),
                  pl.BlockSpec((len(POOL_WINDOWS), POOL_CH, POOL_CH), lambda i: (0, 0, 0)),
                  pl.BlockSpec((1, D_POOL), lambda i: (0, 0))],
        out_specs=pl.BlockSpec((tc, D_POOL), lambda i: (i, 0)),
        out_shape=jax.ShapeDtypeStruct((m, D_POOL), BF16),
        scratch_shapes=[pltpu.VMEM((POOL_HALO + tc, D_POOL), F32)],
        compiler_params=pltpu.CompilerParams(dimension_semantics=("arbitrary",),
                                             vmem_limit_bytes=VMEM_LIMIT),
    )(src, pool_w_bf16, pool_scale.reshape(1, D_POOL))


def _softplus(z):
    return jnp.maximum(z, 0.0) + jnp.log1p(jnp.exp(-jnp.abs(z)))


def _rwkv_prep(r, k, v, lora, prm, seg, segt):
    (w0, wd, a0, wa, wg, k_k, k_a, r_k) = prm
    wl = lora[:, 0:LORA_PAD]
    al = lora[:, LORA_PAD:2 * LORA_PAD]
    gl = lora[:, 2 * LORA_PAD:]
    logw = -_softplus(-(w0 + _dot_x3(jnp.tanh(wl), wd))) - 0.5
    log_decay = -jnp.exp(logw)
    decay = jnp.exp(log_decay)
    a = _sigmoid(a0 + _dot_x3(al, wa))
    gate = _dot(_sigmoid(gl).astype(BF16), wg.astype(BF16))
    kk = k * k_k
    ss = _dot_exact_rhs(kk * kk, seg)
    inv = 1.0 / jnp.maximum(jnp.sqrt(ss), 1e-12)
    kk = kk * _dot_exact_rhs(inv, segt)
    k_mod = k * (1.0 + (a - 1.0) * k_a)
    bonus = _dot_exact_rhs(_dot_exact_rhs(r * k_mod * r_k, seg), segt) * v
    return decay, log_decay, k_mod, -kk, kk * a, gate, bonus


def _group_norm_head(y):
    mu = jnp.mean(y, axis=-1, keepdims=True)
    yc = y - mu
    var = jnp.mean(yc * yc, axis=-1, keepdims=True)
    return yc * lax.rsqrt(var + GN_EPS)


def _group_norm_rows(y, seg, segt):
    inv_n = 1.0 / HEAD_DIM
    mu = _dot_exact_rhs(_dot_exact_rhs(y, seg) * inv_n, segt)
    yc = y - mu
    var = _dot_exact_rhs(yc * yc, seg) * inv_n
    return yc * _dot_exact_rhs(lax.rsqrt(var + GN_EPS), segt)


WKV_CHUNK = HEAD_DIM
PAIR_LANES = 2 * HEAD_DIM
N_PAIRS = RWKV_HEADS // 2
_NN = (((1,), (0,)), ((), ()))
_NT = (((1,), (1,)), ((), ()))
_TN = (((0,), (0,)), ((), ()))


def _dg_x3(a, b, dims):
    ah, al = _split2(a)
    bh, bl = _split2(b)
    (ca,), (cb,) = dims[0]
    dg = lambda x, y: lax.dot_general(x, y, dims, preferred_element_type=F32)
    return dg(jnp.concatenate([ah, al], axis=ca), jnp.concatenate([bh, bh], axis=cb)) + dg(ah, bl)


def _wkv_masks():
    n = PAIR_LANES
    ri = lax.broadcasted_iota(jnp.int32, (n, n), 0)
    ci = lax.broadcasted_iota(jnp.int32, (n, n), 1)
    same_head = (ri // HEAD_DIM) == (ci // HEAD_DIM)
    same16 = (ri // 16) == (ci // 16)
    same32 = (ri // 32) == (ci // 32)
    return dict(same_head=same_head, strict=same_head & (ci < ri), incl=same_head & (ci <= ri), eye=ri == ci,
                same16=same16, only32=same32 & jnp.logical_not(same16), not32=jnp.logical_not(same32),
                head0=lax.broadcasted_iota(jnp.int32, (1, n), 1) < HEAD_DIM)


def _each(fn, *lists):
    return [fn(*xs) for xs in zip(*lists)]


def _unit_lower_inverse(labs, mk):
    ds = [jnp.where(mk["same16"], lab, 0.0) for lab in labs]
    xs = [jnp.where(mk["eye"], 1.0, 0.0) + d for d in ds]
    mm = lambda x, y: _dg_x3(x, y, _NN)
    for _ in range(3):
        ds = _each(mm, ds, ds)
        xs = _each(lambda x, d: x + mm(x, d), xs, ds)
    for level in ("only32", "not32"):
        es = [jnp.where(mk[level], lab, 0.0) for lab in labs]
        exs = _each(mm, es, xs)
        xs = _each(lambda x, ex: x + mm(x, ex), xs, exs)
    return xs


WKV_GROUP = 4


def _wkv_chunk_group(rows, s_bds, mk):
    c = WKV_CHUNK
    stack = lambda x: jnp.concatenate([jnp.where(mk["head0"], x, 0.0), jnp.where(mk["head0"], 0.0, x)], axis=0)
    dup = lambda x: jnp.concatenate([x, x], axis=0)
    col = lambda i: [row[i] for row in rows]
    n = PAIR_LANES
    a_s, r_s, v_s = [_each(stack, col(i)) for i in (0, 1, 6)]
    bh2, kh2 = [_each(dup, col(i)) for i in (4, 5)]
    bk2 = _each(lambda bt, kt: jnp.concatenate([dup(bt), dup(kt)], axis=0), col(2), col(3))
    nt = lambda x, y: _dg_x3(x, y, _NT)
    tn = lambda x, y: _dg_x3(x, y, _TN)
    below = lambda x: jnp.where(mk["strict"], x, 0.0)
    upto = lambda x: jnp.where(mk["incl"], x, 0.0)
    heads = lambda x: jnp.where(mk["same_head"], x, 0.0)
    a_bk = _each(nt, a_s, bk2)
    lab = [below(x[:, :n]) for x in a_bk]
    lak = [below(x[:, n:]) for x in a_bk]
    nt1 = lambda x, y: lax.dot_general(x.astype(BF16), y.astype(BF16), _NT, preferred_element_type=F32)
    mm1 = lambda x, y: _dot(x.astype(BF16), y.astype(BF16))
    r_bk = _each(nt1, r_s, bk2)
    lrb = [upto(x[:, :n]) for x in r_bk]
    lrk = [upto(x[:, n:]) for x in r_bk]
    t_inv = _unit_lower_inverse(lab, mk)
    mm = lambda x, y: _dg_x3(x, y, _NN)
    lakv = _each(mm, lak, v_s)
    ta_u = _each(lambda t, a, w: mm(t, jnp.concatenate([a, w], axis=1)), t_inv, a_s, lakv)
    ta_s = [x[:, :n] for x in ta_u]
    u_s = [x[:, n:] for x in ta_u]
    m_bd = _each(lambda m, w: heads(m) + jnp.where(mk["eye"], w, 0.0), _each(tn, ta_s, bh2), col(7))
    g_bd = _each(lambda g1, g2: heads(g1 + g2), _each(tn, u_s, bh2), _each(tn, v_s, kh2))
    y_rhs = _each(lambda tau, vs: jnp.concatenate([tau, jnp.concatenate([jnp.zeros_like(vs), vs], axis=1)],
                                                  axis=0), ta_u, v_s)
    y_parts = _each(lambda lb, lk, rhs: mm1(jnp.concatenate([lb, lk], axis=1), rhs), lrb, lrk, y_rhs)
    q_s = [x[:, :n] for x in y_parts]
    y0_s = [x[:, n:] for x in y_parts]
    q = _each(lambda rt, qs: rt + (qs[:c] + qs[c:]), col(1), q_s)
    y = _each(lambda yq, y0: yq + (y0[:c] + y0[c:]), _each(nt1, q, s_bds), y0_s)
    s_new = _each(lambda sm, g: sm + g, _each(mm, s_bds, m_bd), g_bd)
    return y, s_new


def _rwkv_seq_kernel(r_ref, k_ref, v_ref, lora_ref, w0_ref, wd_ref, a0_ref, wa_ref, wg_ref, kk_ref, ka_ref,
                     rk_ref, lnw_ref, lnb_ref, seg_ref, segt_ref, o_ref, sout_ref, s_scr, y_scr):
    c = pl.program_id(1)

    @pl.when(c == 0)
    def _():
        s_scr[...] = jnp.zeros(s_scr.shape, F32)

    r = r_ref[...]
    v = v_ref[...]
    seg = seg_ref[...]
    segt = segt_ref[...]
    prm = (w0_ref[...], wd_ref[...], a0_ref[...], wa_ref[...], wg_ref[...], kk_ref[...], ka_ref[...], rk_ref[...])
    decay, log_decay, k_mod, neg_kk, b, gate, bonus = _rwkv_prep(r, k_ref[...], v, lora_ref[...], prm, seg, segt)

    n = WKV_CHUNK
    tri = (lax.broadcasted_iota(jnp.int32, (n, n), 1) <= lax.broadcasted_iota(jnp.int32, (n, n), 0)).astype(BF16)
    ld_h, ld_m, ld_l = _split3(log_decay)
    cw = _dot(tri, ld_h) + (_dot(tri, ld_m) + _dot(tri, ld_l))
    cw_prev = cw - log_decay
    cw_last = cw[n - 1:n, :]
    e_prev = jnp.exp(cw_prev)
    e_inv = jnp.exp(-cw)
    e_rem = jnp.exp(cw_last - cw)
    at = neg_kk * e_prev
    rt = r * (e_prev * decay)
    bt = b * e_inv
    kt = k_mod * e_inv
    bh = b * e_rem
    kh = k_mod * e_rem
    w_last = jnp.exp(cw_last)

    mk = _wkv_masks()
    for p0 in range(0, N_PAIRS, WKV_GROUP):
        pairs = range(p0, p0 + WKV_GROUP)
        lanes = [slice(p * PAIR_LANES, (p + 1) * PAIR_LANES) for p in pairs]
        rows = [tuple(z[:, sl] for z in (at, rt, bt, kt, bh, kh, v, w_last)) for sl in lanes]
        ys, s_new = _wkv_chunk_group(rows, [s_scr[p] for p in pairs], mk)
        for p, sl, y, s in zip(pairs, lanes, ys, s_new):
            s_scr[p] = s
            y_scr[:, sl] = y

    out = (_group_norm_rows(y_scr[...], seg, segt) * lnw_ref[...] + lnb_ref[...] + bonus) * gate
    o_ref[...] = out.astype(BF16)

    @pl.when(c == pl.num_programs(1) - 1)
    def _():
        for p in range(N_PAIRS):
            sout_ref[0, 2 * p] = s_scr[p, 0:HEAD_DIM, 0:HEAD_DIM]
            sout_ref[0, 2 * p + 1] = s_scr[p, HEAD_DIM:, HEAD_DIM:]


def _rwkv_weights(w0, w_decay_up, a0, w_aaa_up, w_gate_up, k_k, k_a, r_k, ln_x_w, ln_x_b):
    row = lambda z: z.reshape(1, D_RWKV)
    pad = lambda z: jnp.pad(z, ((0, LORA_PAD - z.shape[0]), (0, 0)))
    head_of_col = jnp.arange(D_RWKV, dtype=jnp.int32) // HEAD_DIM
    seg = (head_of_col[:, None] == jnp.arange(LANES, dtype=jnp.int32)[None, :]).astype(BF16)
    return (row(w0), pad(w_decay_up), row(a0), pad(w_aaa_up), w_gate_up, row(k_k), row(k_a), row(r_k),
            row(ln_x_w), row(ln_x_b), seg, seg.T)


def _const_spec(shape, grid_rank):
    zeros = (0,) * len(shape)
    if grid_rank == 1:
        return pl.BlockSpec(shape, lambda i: zeros)
    return pl.BlockSpec(shape, lambda i, j: zeros)


def _rwkv_seq(hx, weights, *, batch, seq):
    tc = WKV_CHUNK
    nchunk = seq // tc
    rowmap = lambda cb: (lambda b, c: (b * nchunk + c, cb))
    in_specs = [pl.BlockSpec((tc, D_RWKV), rowmap(0)),
                pl.BlockSpec((tc, D_RWKV), rowmap(1)),
                pl.BlockSpec((tc, D_RWKV), rowmap(2)),
                pl.BlockSpec((tc, 2 * LORA_PAD + GATE_LORA), rowmap(COL_LORA // (2 * LORA_PAD + GATE_LORA)))]
    in_specs += [_const_spec(w.shape, 2) for w in weights]
    return pl.pallas_call(
        _rwkv_seq_kernel,
        grid=(batch, nchunk),
        in_specs=in_specs,
        out_specs=[pl.BlockSpec((tc, D_RWKV), lambda b, c: (b * nchunk + c, 0)),
                   pl.BlockSpec((1, RWKV_HEADS, HEAD_DIM, HEAD_DIM), lambda b, c: (b, 0, 0, 0))],
        out_shape=[jax.ShapeDtypeStruct((batch * seq, D_RWKV), BF16),
                   jax.ShapeDtypeStruct((batch, RWKV_HEADS, HEAD_DIM, HEAD_DIM), F32)],
        scratch_shapes=[pltpu.VMEM((N_PAIRS, PAIR_LANES, PAIR_LANES), F32), pltpu.VMEM((tc, D_RWKV), F32)],
        compiler_params=pltpu.CompilerParams(dimension_semantics=("arbitrary", "arbitrary"),
                                             vmem_limit_bytes=VMEM_LIMIT),
    )(hx, hx, hx, hx, *weights)


def _rwkv_tok_prep_kernel(r_ref, k_ref, v_ref, lora_ref, w0_ref, wd_ref, a0_ref, wa_ref, wg_ref, kk_ref, ka_ref,
                          rk_ref, seg_ref, segt_ref, w_o, k_o, a_o, b_o, g_o, bonus_o):
    prm = (w0_ref[...], wd_ref[...], a0_ref[...], wa_ref[...], wg_ref[...], kk_ref[...], ka_ref[...], rk_ref[...])
    decay, _, k_mod, neg_kk, b, gate, bonus = _rwkv_prep(r_ref[...], k_ref[...], v_ref[...], lora_ref[...], prm,
                                                          seg_ref[...], segt_ref[...])
    w_o[...] = decay
    k_o[...] = k_mod
    a_o[...] = neg_kk
    b_o[...] = b
    g_o[...] = gate
    bonus_o[...] = bonus


def _rwkv_tok_step_kernel(r_ref, w_ref, k_ref, v_ref, a_ref, b_ref, g_ref, bonus_ref, lnw_ref, lnb_ref, s0_ref,
                          o_ref, s_ref, *, tb):
    eye = (lax.broadcasted_iota(jnp.int32, (HEAD_DIM, HEAD_DIM), 0)
           == lax.broadcasted_iota(jnp.int32, (HEAD_DIM, HEAD_DIM), 1))

    def body(bi, carry):
        s = s0_ref[bi]
        sa = jnp.sum(s * a_ref[bi], axis=2, keepdims=True)
        v_col = jnp.sum(jnp.where(eye, v_ref[bi], 0.0), axis=2, keepdims=True)
        s = s * w_ref[bi] + sa * b_ref[bi] + v_col * k_ref[bi]
        s_ref[bi] = s
        y_col = jnp.sum(s * r_ref[bi], axis=2, keepdims=True)
        y = jnp.sum(jnp.where(eye, y_col, 0.0), axis=1, keepdims=True)
        o_ref[bi] = (_group_norm_head(y) * lnw_ref[0] + lnb_ref[0] + bonus_ref[bi]) * g_ref[bi]
        return carry

    lax.fori_loop(0, tb, body, 0)


def _rwkv_tok(hx, state, weights, *, tb):
    nb = hx.shape[0]
    (w0, wd, a0, wa, wg, k_k, k_a, r_k, ln_w, ln_b, seg, segt) = weights
    prep_w = (w0, wd, a0, wa, wg, k_k, k_a, r_k, seg, segt)
    colmap = lambda cb: (lambda i: (0, cb))
    in_specs = [pl.BlockSpec((nb, D_RWKV), colmap(0)),
                pl.BlockSpec((nb, D_RWKV), colmap(1)),
                pl.BlockSpec((nb, D_RWKV), colmap(2)),
                pl.BlockSpec((nb, 2 * LORA_PAD + GATE_LORA), colmap(COL_LORA // (2 * LORA_PAD + GATE_LORA)))]
    in_specs += [_const_spec(w.shape, 1) for w in prep_w]
    rows = jax.ShapeDtypeStruct((nb, D_RWKV), F32)
    prepped = pl.pallas_call(
        _rwkv_tok_prep_kernel,
        grid=(1,),
        in_specs=in_specs,
        out_specs=[pl.BlockSpec((nb, D_RWKV), lambda i: (0, 0))] * 6,
        out_shape=[rows] * 6,
        compiler_params=pltpu.CompilerParams(dimension_semantics=("arbitrary",),
                                             vmem_limit_bytes=VMEM_LIMIT),
    )(hx, hx, hx, hx, *prep_w)
    decay, k_mod, neg_kk, b, gate, bonus = prepped
    heads = lambda z: z.reshape(nb, RWKV_HEADS, 1, HEAD_DIM)
    vecs = [heads(z) for z in (hx[:, COL_R:COL_R + D_RWKV], decay, k_mod, hx[:, COL_V:COL_V + D_RWKV], neg_kk, b,
                               gate, bonus)]
    vec_spec = pl.BlockSpec((tb, RWKV_HEADS, 1, HEAD_DIM), lambda i: (i, 0, 0, 0))
    ln_spec = pl.BlockSpec((1, RWKV_HEADS, 1, HEAD_DIM), lambda i: (0, 0, 0, 0))
    state_spec = pl.BlockSpec((tb, RWKV_HEADS, HEAD_DIM, HEAD_DIM), lambda i: (i, 0, 0, 0))
    out, new_state = pl.pallas_call(
        functools.partial(_rwkv_tok_step_kernel, tb=tb),
        grid=(nb // tb,),
        in_specs=[vec_spec] * 8 + [ln_spec, ln_spec, state_spec],
        out_specs=[vec_spec, state_spec],
        out_shape=[jax.ShapeDtypeStruct((nb, RWKV_HEADS, 1, HEAD_DIM), F32),
                   jax.ShapeDtypeStruct(state.shape, F32)],
        compiler_params=pltpu.CompilerParams(dimension_semantics=("arbitrary",),
                                             vmem_limit_bytes=VMEM_LIMIT),
    )(*vecs, ln_w.reshape(1, RWKV_HEADS, 1, HEAD_DIM), ln_b.reshape(1, RWKV_HEADS, 1, HEAD_DIM), state)
    return out.reshape(nb, D_RWKV).astype(BF16), new_state


def _route(logits, bias):
    scores = _sigmoid(logits)
    sel = scores + bias
    rows = logits.shape[0]
    lane_i = lax.broadcasted_iota(jnp.int32, (rows, N_EXPERTS), 1)
    grp = lane_i // GROUP_SIZE
    lane = lane_i.astype(F32)
    neg_inf = -jnp.inf

    def take_max(x):
        m = jnp.max(x, axis=1, keepdims=True)
        idx = jnp.min(jnp.where(x == m, lane, float(N_EXPERTS)), axis=1, keepdims=True)
        return m, lane == idx

    gscore = []
    for g in range(N_GROUPS):
        x = jnp.where(grp == g, sel, neg_inf)
        m1, hit = take_max(x)
        m2 = jnp.max(jnp.where(hit, neg_inf, x), axis=1, keepdims=True)
        gscore.append(m1 + m2)
    emask = jnp.zeros((rows, N_EXPERTS), jnp.bool_)
    for g in range(N_GROUPS):
        ahead = jnp.zeros((rows, 1), jnp.int32)
        for g2 in range(N_GROUPS):
            if g2 == g:
                continue
            beats = (gscore[g2] >= gscore[g]) if g2 < g else (gscore[g2] > gscore[g])
            ahead = ahead + beats.astype(jnp.int32)
        emask = emask | ((grp == g) & (ahead < TOPK_GROUPS))
    x = jnp.where(emask, sel, neg_inf)
    chosen = jnp.zeros((rows, N_EXPERTS), jnp.bool_)
    slot_lane = lax.broadcasted_iota(jnp.int32, (rows, LANES), 1)
    top_idx = jnp.zeros((rows, LANES), F32)
    top_w = jnp.zeros((rows, LANES), F32)
    for kth in range(TOP_K):
        m = jnp.max(x, axis=1, keepdims=True)
        idx = jnp.min(jnp.where(x == m, lane, float(N_EXPERTS)), axis=1, keepdims=True)
        hit = lane == idx
        chosen = chosen | hit
        x = jnp.where(hit, neg_inf, x)
        top_idx = jnp.where(slot_lane == kth, idx, top_idx)
        top_w = jnp.where(slot_lane == kth, jnp.sum(jnp.where(hit, scores, 0.0), axis=1, keepdims=True), top_w)
    top_w = top_w / jnp.sum(top_w, axis=1, keepdims=True) * ROUTE_SCALE
    return top_idx.astype(jnp.int32), top_w, chosen.astype(F32)


def _out_proj_kernel(pool_ref, rwkv_ref, x_ref, g1_ref, sc_ref, sh_ref, wp_ref, wr_ref, npost_ref, npre_ref,
                     wrt_ref, rb_ref, sg_ref, su_ref, sd_ref, x1_ref, u2_ref, idx_ref, topw_ref, chosen_ref,
                     shared_ref):
    m = _dot(pool_ref[...], wp_ref[...]) + _dot(rwkv_ref[...], wr_ref[...])
    x1 = x_ref[...] + g1_ref[0] * _rms(m, npost_ref[...])
    x1_ref[...] = x1
    u2 = _rms(x1, npre_ref[...]) * (1.0 + sc_ref[0]) + sh_ref[0]
    u2_ref[...] = u2
    top_idx, top_w, chosen = _route(_dot_x3(u2, wrt_ref[...]), rb_ref[...])
    idx_ref[...] = top_idx
    topw_ref[...] = top_w
    chosen_ref[...] = chosen
    u2b = u2.astype(BF16)
    h = _silu(_dot(u2b, sg_ref[...])) * _dot(u2b, su_ref[...])
    shared_ref[...] = _dot(h.astype(BF16), sd_ref[...])


def _out_proj(pool_out, rwkv_out, x, g1, sc2, sh2, w_out_bf16, n_post, n_pre, w_router, router_bias, sg, su, sd, *,
              tm, tiles_per_seq):
    m, d = x.shape
    if tiles_per_seq is None:
        mod_spec = pl.BlockSpec((1, tm, d), lambda i: (i, 0, 0))
    else:
        mod_spec = pl.BlockSpec((1, 1, d), lambda i: (i // tiles_per_seq, 0, 0))
    row = lambda width: pl.BlockSpec((tm, width), lambda i: (i, 0))
    return pl.pallas_call(
        _out_proj_kernel,
        grid=(m // tm,),
        in_specs=[row(D_POOL), row(D_RWKV), row(d), mod_spec, mod_spec, mod_spec,
                  _const_spec((D_POOL, d), 1), _const_spec((D_RWKV, d), 1),
                  _const_spec((1, d), 1), _const_spec((1, d), 1),
                  _const_spec((d, N_EXPERTS), 1), _const_spec((1, N_EXPERTS), 1),
                  _const_spec(sg.shape, 1), _const_spec(su.shape, 1), _const_spec(sd.shape, 1)],
        out_specs=[row(d), row(d), row(LANES), row(LANES), row(N_EXPERTS), row(d)],
        out_shape=[jax.ShapeDtypeStruct((m, d), F32), jax.ShapeDtypeStruct((m, d), F32),
                   jax.ShapeDtypeStruct((m, LANES), jnp.int32), jax.ShapeDtypeStruct((m, LANES), F32),
                   jax.ShapeDtypeStruct((m, N_EXPERTS), F32), jax.ShapeDtypeStruct((m, d), F32)],
        compiler_params=pltpu.CompilerParams(dimension_semantics=("arbitrary",),
                                             vmem_limit_bytes=VMEM_LIMIT),
    )(pool_out, rwkv_out, x, g1, sc2, sh2, w_out_bf16[:D_POOL], w_out_bf16[D_POOL:],
      n_post.reshape(1, d), n_pre.reshape(1, d), w_router, router_bias.reshape(1, N_EXPERTS), sg, su, sd)


MOE_TM = 384


ROW_GROUP = 8


def _experts_kernel(te_ref, fresh_ref, wslot_ref, nexte_ref, groups_ref, nused_ref, idx_ref, idx_next_ref, ws_ref,
                    x_hbm, wg_hbm, wu_hbm, wd_hbm, ys_ref, wg_buf, wu_buf, wd_buf, g_bf, u_bf, d_bf, x_buf,
                    w_sems, x_sems):
    t = pl.program_id(0)
    n_used = nused_ref[0]

    def weight_copies(e, slot):
        return (pltpu.make_async_copy(wg_hbm.at[e], wg_buf.at[slot], w_sems.at[slot]),
                pltpu.make_async_copy(wu_hbm.at[e], wu_buf.at[slot], w_sems.at[slot]),
                pltpu.make_async_copy(wd_hbm.at[e], wd_buf.at[slot], w_sems.at[slot]))

    def start_rows(table_ref, slot, n_groups):
        def body(g, carry):
            for r in range(ROW_GROUP):
                j = g * ROW_GROUP + r
                pltpu.make_async_copy(x_hbm.at[pl.ds(table_ref[0, 0, j], 1)], x_buf.at[slot, pl.ds(j, 1)],
                                      x_sems.at[slot]).start()
            return carry
        lax.fori_loop(0, n_groups, body, 0)

    @pl.when(t == 0)
    def _():
        for cp in weight_copies(te_ref[0], 0):
            cp.start(priority=1)
        x_buf[...] = jnp.zeros(x_buf.shape, F32)
        start_rows(idx_ref, 0, groups_ref[0])

    @pl.when(t + 1 < n_used)
    def _():
        start_rows(idx_next_ref, (t + 1) % 2, groups_ref[t + 1])

    @pl.when(jnp.logical_and(t < n_used, fresh_ref[t] == 1))
    def _():
        slot = wslot_ref[t]
        for cp in weight_copies(te_ref[t], slot):
            cp.wait()

        @pl.when(nexte_ref[t] >= 0)
        def _():
            for cp in weight_copies(nexte_ref[t], 1 - slot):
                cp.start(priority=1)

        g_bf[...] = wg_buf[slot].astype(BF16)
        u_bf[...] = wu_buf[slot].astype(BF16)
        d_bf[...] = wd_buf[slot].astype(BF16)

    @pl.when(t < n_used)
    def _():
        cur = t % 2
        n_rows = pl.multiple_of(groups_ref[t] * ROW_GROUP, ROW_GROUP)
        pltpu.make_async_copy(x_hbm.at[pl.ds(0, n_rows)], x_buf.at[cur, pl.ds(0, n_rows)], x_sems.at[cur]).wait()
        x = x_buf[cur].astype(BF16)
        h = _silu(_dot(x, g_bf[...])) * _dot(x, u_bf[...])
        diag = (lax.broadcasted_iota(jnp.int32, (MOE_TM, MOE_TM), 0)
                == lax.broadcasted_iota(jnp.int32, (MOE_TM, MOE_TM), 1))
        w_col = jnp.sum(jnp.where(diag, ws_ref[0], 0.0), axis=1, keepdims=True)
        ys_ref[...] = _pack_bf16_halves(_dot(h.astype(BF16), d_bf[...]) * w_col)

    @pl.when(t >= n_used)
    def _():
        ys_ref[...] = jnp.zeros(ys_ref.shape, jnp.uint32)


def _experts(tables, x_all, exp_gate, exp_up, exp_down):
    tile_expert, fresh, wslot, next_e, row_groups, n_used, slot_tok, slot_w = tables
    d = x_all.shape[1]
    n_tiles = tile_expert.shape[0]
    n_slots = n_tiles * MOE_TM
    idx_block = (1, 1, MOE_TM)
    any_spec = pl.BlockSpec(memory_space=pl.ANY)
    live = lambda t, nused: jnp.where(t < nused[0], t, n_tiles - 1)
    grid_spec = pltpu.PrefetchScalarGridSpec(
        num_scalar_prefetch=6,
        grid=(n_tiles,),
        in_specs=[pl.BlockSpec(idx_block, lambda t, *s: (live(t, s[5]), 0, 0), memory_space=pltpu.SMEM),
                  pl.BlockSpec(idx_block, lambda t, *s: (live(t + 1, s[5]), 0, 0), memory_space=pltpu.SMEM),
                  pl.BlockSpec(idx_block, lambda t, *s: (live(t, s[5]), 0, 0)),
                  any_spec, any_spec, any_spec, any_spec],
        out_specs=pl.BlockSpec((MOE_TM, d // 2), lambda t, *s: (live(t, s[5]), 0)),
        scratch_shapes=[pltpu.VMEM((2, d, D_EXPERT), F32), pltpu.VMEM((2, d, D_EXPERT), F32),
                        pltpu.VMEM((2, D_EXPERT, d), F32),
                        pltpu.VMEM((d, D_EXPERT), BF16), pltpu.VMEM((d, D_EXPERT), BF16),
                        pltpu.VMEM((D_EXPERT, d), BF16),
                        pltpu.VMEM((2, MOE_TM, d), F32),
                        pltpu.SemaphoreType.DMA((2,)), pltpu.SemaphoreType.DMA((2,))],
    )
    idx_tiles = slot_tok.reshape(n_tiles, 1, MOE_TM)
    return pl.pallas_call(
        _experts_kernel,
        grid_spec=grid_spec,
        out_shape=jax.ShapeDtypeStruct((n_slots, d // 2), jnp.uint32),
        compiler_params=pltpu.CompilerParams(dimension_semantics=("arbitrary",),
                                             vmem_limit_bytes=VMEM_LIMIT),
    )(tile_expert, fresh, wslot, next_e, row_groups, n_used, idx_tiles, idx_tiles,
      slot_w.reshape(n_tiles, 1, MOE_TM), x_all, exp_gate, exp_up, exp_down)


ASSIGN_BITS = 17


def _dispatch(chosen, top_idx, top_w):
    n = chosen.shape[0]
    n_assign = n * TOP_K
    low_mask = (1 << ASSIGN_BITS) - 1
    assert n_assign + MOE_TM <= low_mask and N_EXPERTS << ASSIGN_BITS < 2 ** 31
    n_tiles = -(-n_assign // MOE_TM) + N_EXPERTS
    n_slots = n_tiles * MOE_TM
    sel = chosen.astype(jnp.int32)
    counts = jnp.sum(sel, axis=0)
    rank = jnp.cumsum(sel, axis=0) - sel
    tiles_e = (counts + MOE_TM - 1) // MOE_TM
    tile_end = jnp.cumsum(tiles_e)
    slot = (tile_end - tiles_e)[None, :] * MOE_TM + rank
    expert_ids = jnp.arange(N_EXPERTS, dtype=jnp.int32)
    slot_of_pair = jnp.stack([jnp.sum(jnp.where(top_idx[:, k:k + 1] == expert_ids[None, :], slot, 0), axis=1)
                              for k in range(TOP_K)])
    shift = 1 << ASSIGN_BITS
    int_max = jnp.iinfo(jnp.int32).max
    real_key = top_idx.reshape(-1) * shift + jnp.arange(n_assign, dtype=jnp.int32)
    pad_j = jnp.arange(MOE_TM - 1, dtype=jnp.int32)[None, :]
    pad_key = jnp.where(pad_j < (tiles_e * MOE_TM - counts)[:, None],
                        expert_ids[:, None] * shift + n_assign + pad_j, int_max)
    n_rest = n_slots - n_assign
    keys = jnp.concatenate([real_key, pad_key.reshape(-1),
                            jnp.full((n_rest - pad_key.size,), int_max, jnp.int32)])
    weights = jnp.concatenate([top_w.reshape(-1), jnp.zeros((n_rest,), F32)])
    sorted_key, sorted_w = lax.sort((keys, weights), num_keys=1)
    pair = sorted_key & low_mask
    is_real = pair < n_assign
    slot_tok = jnp.where(is_real, pair // TOP_K, 0)
    slot_w = jnp.where(is_real, sorted_w, 0.0)
    n_used = tile_end[-1]
    tile_ids = jnp.minimum(jnp.arange(n_tiles, dtype=jnp.int32), n_used - 1)
    tile_expert = jnp.minimum(jnp.searchsorted(tile_end, tile_ids, side="right"), N_EXPERTS - 1).astype(jnp.int32)
    in_use = jnp.arange(n_tiles, dtype=jnp.int32) < n_used
    prev_expert = jnp.concatenate([jnp.full((1,), -1, jnp.int32), tile_expert[:-1]])
    fresh = jnp.logical_and(in_use, tile_expert != prev_expert).astype(jnp.int32)
    wslot = (jnp.cumsum(fresh) - 1) % 2
    later = jnp.where(counts > 0, expert_ids, N_EXPERTS)
    next_used = jnp.concatenate([lax.cummin(later, reverse=True)[1:], jnp.full((1,), N_EXPERTS, jnp.int32)])
    next_used = jnp.where(next_used >= N_EXPERTS, -1, next_used)
    tile_in_expert = jnp.arange(n_tiles, dtype=jnp.int32) - (tile_end - tiles_e)[tile_expert]
    rows_in_tile = jnp.clip(counts[tile_expert] - tile_in_expert * MOE_TM, 0, MOE_TM)
    row_groups = jnp.where(in_use, (rows_in_tile + ROW_GROUP - 1) // ROW_GROUP, 0)
    tables = (tile_expert, fresh, wslot.astype(jnp.int32), next_used[tile_expert].astype(jnp.int32),
              row_groups.astype(jnp.int32), n_used.reshape(1).astype(jnp.int32), slot_tok.astype(jnp.int32), slot_w)
    return slot_of_pair, tables


def _final_kernel(idx_ref, idx_next_ref, shared_ref, ys_hbm, x1_ref, g2_ref, npost_ref, o_ref, rows_buf, sems, *,
                  tm):
    i = pl.program_id(0)
    n_rows = TOP_K * tm

    def start_gather(table_ref, buf_slot):
        def body(j, carry):
            pltpu.make_async_copy(ys_hbm.at[pl.ds(table_ref[0, 0, j], 1)], rows_buf.at[buf_slot, pl.ds(j, 1)],
                                  sems.at[buf_slot]).start()
            return carry
        lax.fori_loop(0, n_rows, body, 0, unroll=8)

    @pl.when(i == 0)
    def _():
        start_gather(idx_ref, 0)

    @pl.when(i + 1 < pl.num_programs(0))
    def _():
        start_gather(idx_next_ref, (i + 1) % 2)

    cur = i % 2
    pltpu.make_async_copy(ys_hbm.at[pl.ds(0, n_rows)], rows_buf.at[cur], sems.at[cur]).wait()
    left, right = _unpack_bf16_halves(rows_buf[cur, 0:tm])
    for kth in range(1, TOP_K):
        l_k, r_k = _unpack_bf16_halves(rows_buf[cur, kth * tm:(kth + 1) * tm])
        left, right = left + l_k, right + r_k
    routed = jnp.concatenate([left, right], axis=1)
    o_ref[...] = x1_ref[...] + g2_ref[0] * _rms(routed + shared_ref[...], npost_ref[...])


def _final(shared, ys, slot_tiles, x1, g2, n_post, *, tm, tiles_per_seq, tile_offset):
    m, d = x1.shape
    nt = m // tm
    if tiles_per_seq is None:
        mod_spec = pl.BlockSpec((1, tm, d), lambda i: (i, 0, 0))
    else:
        mod_spec = pl.BlockSpec((1, 1, d), lambda i: (i // tiles_per_seq, 0, 0))
    row = pl.BlockSpec((tm, d), lambda i: (i, 0))
    idx_block = (1, 1, TOP_K * tm)
    return pl.pallas_call(
        functools.partial(_final_kernel, tm=tm),
        grid=(nt,),
        in_specs=[pl.BlockSpec(idx_block, lambda i: (i + tile_offset, 0, 0), memory_space=pltpu.SMEM),
                  pl.BlockSpec(idx_block, lambda i: (jnp.minimum(i + 1, nt - 1) + tile_offset, 0, 0),
                               memory_space=pltpu.SMEM),
                  row, pl.BlockSpec(memory_space=pl.ANY), row, mod_spec, _const_spec((1, d), 1)],
        out_specs=row,
        out_shape=jax.ShapeDtypeStruct((m, d), F32),
        scratch_shapes=[pltpu.VMEM((2, TOP_K * tm, d // 2), jnp.uint32), pltpu.SemaphoreType.DMA((2,))],
        compiler_params=pltpu.CompilerParams(dimension_semantics=("arbitrary",),
                                             vmem_limit_bytes=VMEM_LIMIT),
    )(slot_tiles, slot_tiles, shared, ys, x1, g2, n_post.reshape(1, d))


def _pad_cols_in(w_in, mu_shift):
    d = w_in.shape[0]
    hp = w_in[:, :D_POOL]
    hr = w_in[:, D_POOL:]
    cuts = [3 * D_RWKV, 3 * D_RWKV + DECAY_LORA, 3 * D_RWKV + DECAY_LORA + AAA_LORA]
    rkv, wl, al, gl = jnp.split(hr, cuts, axis=1)
    zpad = lambda z: jnp.pad(z, ((0, 0), (0, LORA_PAD - z.shape[1])))
    w = jnp.concatenate([rkv, hp, zpad(wl), zpad(al), gl], axis=1).astype(BF16)
    mu_rkv, mu_wl, mu_al, mu_gl = jnp.split(mu_shift, cuts)
    zp1 = lambda z: jnp.pad(z, (0, LORA_PAD - z.shape[0]))
    mu = jnp.concatenate([mu_rkv, jnp.zeros((D_POOL,), F32), zp1(mu_wl), zp1(mu_al), mu_gl])
    assert w.shape == (d, N_IN_PAD) and mu.shape == (N_IN_PAD,)
    return w, mu


def kernel(x_prompt, x_sample, c_prompt, c_sample, state_shift, state_pool, state_wkv, w_ada, b_ada, norm_pre_mix, norm_post_mix, norm_pre_ffn, norm_post_ffn, w_in, mu_shift, pool_w, pool_scale, w0, w_decay_up, a0, w_aaa_up, w_gate_up, k_k, k_a, r_k, ln_x_w, ln_x_b, w_out, w_router, router_bias, exp_gate, exp_up, exp_down, sh_gate, sh_up, sh_down):
    bp, seq, d = x_prompt.shape
    bs = x_sample.shape[0]
    assert w_ada.shape[0] == 1 and x_sample.shape[1] == 1
    np_rows = bp * seq

    c_all = jnp.concatenate([c_prompt, c_sample], axis=0)
    pad_rows = (-c_all.shape[0]) % 16
    ada = _ada(jnp.pad(c_all, ((0, pad_rows), (0, 0))), w_ada[0], b_ada[0])
    sh1, sc1, g1, sh2, sc2, g2 = jnp.split(ada, 6, axis=-1)
    pr = lambda z: z[:bp].reshape(bp, 1, d)
    TS = min(128, bs)
    sr = lambda z: z[bp:bp + bs].reshape(bs // TS, TS, d)

    w_in_p, mu_p = _pad_cols_in(w_in[0], mu_shift[0])
    xp = x_prompt.reshape(np_rows, d)
    xs = x_sample.reshape(bs, d)

    TM_IN = min(1024, seq)
    hx_p, u_tail = _in_proj(xp, norm_pre_mix[0], pr(sc1), pr(sh1), w_in_p, mu_p, tm=TM_IN,
                            tiles_per_seq=seq // TM_IN)
    hx_s, u_s = _in_proj(xs, norm_pre_mix[0], sr(sc1), sr(sh1), w_in_p, mu_p, tm=TS, tiles_per_seq=1,
                         prev=state_shift[0])
    new_shift_prompt = u_tail.reshape(bp, seq // TM_IN, 8, d)[:, -1, -1][None]
    new_shift_sample = u_s[None]

    pool_w_b = pool_w[0].astype(BF16)
    TC_POOL = min(512, seq)
    pool_p = _pool(hx_p, COL_POOL // D_POOL, pool_w_b, pool_scale[0], tc=TC_POOL, tiles_per_seq=seq // TC_POOL,
                   full_count=False)
    hp_p = hx_p[:, COL_POOL:COL_POOL + D_POOL].reshape(bp, seq, D_POOL)
    new_pool_prompt = hp_p[:, seq - POOL_BUF:][None]
    hp_s = hx_s[:, COL_POOL:COL_POOL + D_POOL]
    ext_s = jnp.concatenate([state_pool[0], hp_s[:, None, :]], axis=1)
    assert PAST_LEN + 1 >= max(POOL_WINDOWS)
    pool_s = _pool(ext_s.reshape(bs * (POOL_BUF + 1), D_POOL), 0, pool_w_b, pool_scale[0],
                   tc=bs * (POOL_BUF + 1), tiles_per_seq=1, full_count=True)
    pool_s = pool_s.reshape(bs, POOL_BUF + 1, D_POOL)[:, -1]
    new_pool_sample = ext_s[:, 1:][None]

    rw = _rwkv_weights(w0[0], w_decay_up[0], a0[0], w_aaa_up[0], w_gate_up[0], k_k[0], k_a[0], r_k[0],
                       ln_x_w[0], ln_x_b[0])
    rwkv_p, wkv_p = _rwkv_seq(hx_p, rw, batch=bp, seq=seq)
    rwkv_s, wkv_s = _rwkv_tok(hx_s, state_wkv[0], rw, tb=8)

    w_out_b = w_out[0].astype(BF16)
    assert np_rows % TS == 0 and seq % TS == 0
    sg, su, sd = sh_gate[0].astype(BF16), sh_up[0].astype(BF16), sh_down[0].astype(BF16)
    TM_OUT = min(256, seq)
    outs_p = _out_proj(pool_p, rwkv_p, xp, pr(g1), pr(sc2), pr(sh2), w_out_b, norm_post_mix[0], norm_pre_ffn[0],
                       w_router[0], router_bias[0], sg, su, sd, tm=TM_OUT, tiles_per_seq=seq // TM_OUT)
    outs_s = _out_proj(pool_s, rwkv_s, xs, sr(g1), sr(sc2), sr(sh2), w_out_b, norm_post_mix[0], norm_pre_ffn[0],
                       w_router[0], router_bias[0], sg, su, sd, tm=TS, tiles_per_seq=None)
    (x1_p, u2_p, idx_p, topw_p, chosen_p, shared_p), (x1_s, u2_s, idx_s, topw_s, chosen_s, shared_s) = outs_p, outs_s
    u2_all = jnp.concatenate([u2_p, u2_s], axis=0)
    idx_all = jnp.concatenate([idx_p[:, :TOP_K], idx_s[:, :TOP_K]], axis=0)
    topw_all = jnp.concatenate([topw_p[:, :TOP_K], topw_s[:, :TOP_K]], axis=0)
    chosen_all = jnp.concatenate([chosen_p, chosen_s], axis=0)

    n_all = np_rows + bs
    slot_of_pair, tables = _dispatch(chosen_all, idx_all, topw_all)
    ys = _experts(tables, u2_all, exp_gate[0], exp_up[0], exp_down[0])

    slot_tiles = slot_of_pair.reshape(TOP_K, n_all // TS, TS).transpose(1, 0, 2).reshape(n_all // TS, 1, TOP_K * TS)
    y_p = _final(shared_p, ys, slot_tiles, x1_p, pr(g2), norm_post_ffn[0], tm=TS, tiles_per_seq=seq // TS,
                 tile_offset=0)
    y_s = _final(shared_s, ys, slot_tiles, x1_s, sr(g2), norm_post_ffn[0], tm=TS, tiles_per_seq=None,
                 tile_offset=np_rows // TS)

    return (y_p.reshape(bp, seq, d), y_s.reshape(bs, 1, d), new_shift_prompt, new_pool_prompt, wkv_p[None],
            new_shift_sample, new_pool_sample, wkv_s[None])
```

```python
import functools

import jax
import jax.numpy as jnp
from jax import lax
from jax.experimental import pallas as pl
from jax.experimental.pallas import tpu as pltpu

F32 = jnp.float32
BF16 = jnp.bfloat16

D_MODEL = 2048
D_POOL = 512
POOL_WINDOWS = (2, 4, 8, 16)
POOL_CH = 128
POOL_BUF = 15
D_RWKV = 1536
HEAD_DIM = 64
RWKV_HEADS = 24
DECAY_LORA = 96
AAA_LORA = 96
GATE_LORA = 256
LORA_PAD = 128
GN_EPS = 64e-5
N_EXPERTS = 256
TOP_K = 8
N_GROUPS = 8
GROUP_SIZE = N_EXPERTS // N_GROUPS
TOPK_GROUPS = 4
D_EXPERT = 512
ROUTE_SCALE = 2.5
EPS = 1e-6
PAST_LEN = 16384

COL_R, COL_K, COL_V = 0, D_RWKV, 2 * D_RWKV
COL_POOL = 3 * D_RWKV
COL_LORA = COL_POOL + D_POOL
N_IN_PAD = COL_LORA + 2 * LORA_PAD + GATE_LORA
LANES = 128
VMEM_LIMIT = 56 * 1024 * 1024


def _dot(a, b):
    return jnp.dot(a, b, preferred_element_type=F32)


def _split2(x):
    hi = x.astype(BF16)
    lo = (x - hi.astype(F32)).astype(BF16)
    return hi, lo


def _split3(x):
    hi = x.astype(BF16)
    r = x - hi.astype(F32)
    mid = r.astype(BF16)
    lo = (r - mid.astype(F32)).astype(BF16)
    return hi, mid, lo


def _dot_x3(a, b):
    ah, al = _split2(a)
    bh, bl = _split2(b)
    return _dot(ah, bh) + (_dot(ah, bl) + _dot(al, bh))


def _dot_exact_rhs(a, b_bf16):
    h, m, l = _split3(a)
    return _dot(h, b_bf16) + (_dot(m, b_bf16) + _dot(l, b_bf16))


def _sigmoid(x):
    return 1.0 / (1.0 + jnp.exp(-x))


def _silu(x):
    return x * _sigmoid(x)


def _pack_bf16_halves(y):
    half = y.shape[1] // 2
    bits = pltpu.bitcast(y.astype(BF16).astype(F32), jnp.uint32)
    return bits[:, :half] | (bits[:, half:] >> 16)


def _unpack_bf16_halves(p):
    return (pltpu.bitcast(p & jnp.uint32(0xFFFF0000), F32), pltpu.bitcast(p << 16, F32))


def _rms(x, g):
    return x * lax.rsqrt(jnp.mean(x * x, axis=-1, keepdims=True) + EPS) * g


def _ada_kernel(c_ref, w_ref, b_ref, o_ref):
    o_ref[...] = _dot_x3(_silu(c_ref[...]), w_ref[...]) + b_ref[...]


def _ada(c_all, w_ada, b_ada):
    m, d = c_all.shape
    n = w_ada.shape[1]
    tn = 512
    return pl.pallas_call(
        _ada_kernel,
        grid=(n // tn,),
        in_specs=[pl.BlockSpec((m, d), lambda j: (0, 0)),
                  pl.BlockSpec((d, tn), lambda j: (0, j)),
                  pl.BlockSpec((1, tn), lambda j: (0, j))],
        out_specs=pl.BlockSpec((m, tn), lambda j: (0, j)),
        out_shape=jax.ShapeDtypeStruct((m, n), F32),
        compiler_params=pltpu.CompilerParams(dimension_semantics=("arbitrary",),
                                             vmem_limit_bytes=VMEM_LIMIT),
    )(c_all, w_ada, b_ada.reshape(1, n))


def _in_proj_kernel(*refs, tiles_per_seq, explicit_prev, tm):
    if explicit_prev:
        x_ref, g_ref, sc_ref, sh_ref, prev_ref, w_ref, mu_ref, hx_ref, u_ref, ub_scr = refs
    else:
        x_ref, g_ref, sc_ref, sh_ref, w_ref, mu_ref, hx_ref, u_ref, ub_scr, carry_scr = refs
    i = pl.program_id(0)
    j = pl.program_id(1)

    @pl.when(j == 0)
    def _():
        u = _rms(x_ref[...], g_ref[...])
        u = u * (1.0 + sc_ref[0]) + sh_ref[0]
        ub_scr[...] = u.astype(BF16)
        if explicit_prev:
            u_ref[...] = u
        else:
            u_ref[...] = u[tm - 8:, :]

    w = w_ref[...]
    h = _dot(ub_scr[...], w)
    if explicit_prev:
        hp = _dot(prev_ref[...].astype(BF16), w)
    else:
        @pl.when(i == 0)
        def _():
            carry_scr[j] = jnp.zeros(carry_scr.shape[1:], F32)

        first = jnp.where(i % tiles_per_seq == 0, 0.0, carry_scr[j, 0:1, :])
        row0 = lax.broadcasted_iota(jnp.int32, h.shape, 0) == 0
        hp = jnp.where(row0, first, pltpu.roll(h, 1, axis=0))
        carry_scr[j, 0:1, :] = h[tm - 1:tm, :]
    hx_ref[...] = h + (hp - h) * mu_ref[...]


def _in_proj(x, gamma, sc, sh, w_bf16, mu_pad, *, tm, tiles_per_seq, prev=None):
    m, d = x.shape
    n = w_bf16.shape[1]
    tn = 512
    nt = n // tn
    explicit_prev = prev is not None
    if explicit_prev:
        mod_spec = pl.BlockSpec((1, tm, d), lambda i, j: (i, 0, 0))
    else:
        mod_spec = pl.BlockSpec((1, 1, d), lambda i, j: (i // tiles_per_seq, 0, 0))
    in_specs = [pl.BlockSpec((tm, d), lambda i, j: (i, 0)),
                pl.BlockSpec((1, d), lambda i, j: (0, 0)),
                mod_spec, mod_spec]
    args = [x, gamma.reshape(1, d), sc, sh]
    if explicit_prev:
        in_specs.append(pl.BlockSpec((tm, d), lambda i, j: (i, 0)))
        args.append(prev)
    in_specs += [pl.BlockSpec((d, tn), lambda i, j: (0, j)),
                 pl.BlockSpec((1, tn), lambda i, j: (0, j))]
    args += [w_bf16, mu_pad.reshape(1, n)]
    scratch = [pltpu.VMEM((tm, d), BF16)]
    if explicit_prev:
        u_shape, u_spec = (m, d), pl.BlockSpec((tm, d), lambda i, j: (i, 0))
    else:
        u_shape, u_spec = (m // tm * 8, d), pl.BlockSpec((8, d), lambda i, j: (i, 0))
        scratch.append(pltpu.VMEM((nt, 8, tn), F32))
    return pl.pallas_call(
        functools.partial(_in_proj_kernel, tiles_per_seq=tiles_per_seq, explicit_prev=explicit_prev, tm=tm),
        grid=(m // tm, nt),
        in_specs=in_specs,
        out_specs=[pl.BlockSpec((tm, tn), lambda i, j: (i, j)), u_spec],
        out_shape=[jax.ShapeDtypeStruct((m, n), F32), jax.ShapeDtypeStruct(u_shape, F32)],
        scratch_shapes=scratch,
        compiler_params=pltpu.CompilerParams(dimension_semantics=("arbitrary", "arbitrary"),
                                             vmem_limit_bytes=VMEM_LIMIT),
    )(*args)


POOL_HALO = 16


def _pool_kernel(p_ref, pw_ref, ps_ref, o_ref, ext_scr, *, tiles_per_seq, full_count, tc):
    i = pl.program_id(0)

    @pl.when(i % tiles_per_seq == 0)
    def _():
        ext_scr[0:POOL_HALO, :] = jnp.zeros((POOL_HALO, D_POOL), F32)

    p = p_ref[...]
    ext_scr[POOL_HALO:POOL_HALO + tc, :] = p
    pos = (i % tiles_per_seq) * tc + lax.broadcasted_iota(jnp.int32, (tc, 1), 0)
    outs = []
    for gi, win in enumerate(POOL_WINDOWS):
        lo = gi * POOL_CH
        pg = p[:, lo:lo + POOL_CH]
        acc = pg
        for s in range(1, win):
            acc = acc + ext_scr[POOL_HALO - s:POOL_HALO - s + tc, lo:lo + POOL_CH]
        if full_count:
            dgi = acc / float(win) - pg
        else:
            cnt = jnp.minimum(pos + 1, win).astype(F32)
            dgi = acc / cnt - pg
        outs.append(_dot(dgi.astype(BF16), pw_ref[gi]))
    y = jnp.concatenate(outs, axis=-1) * ps_ref[...]
    o_ref[...] = y.astype(BF16)
    ext_scr[0:POOL_HALO, :] = ext_scr[tc:tc + POOL_HALO, :]


def _pool(src, col_block, pool_w_bf16, pool_scale, *, tc, tiles_per_seq, full_count):
    m = src.shape[0]
    return pl.pallas_call(
        functools.partial(_pool_kernel, tiles_per_seq=tiles_per_seq, full_count=full_count, tc=tc),
        grid=(m // tc,),
        in_specs=[pl.BlockSpec((tc, D_POOL), lambda i: (i, col_block)),
                  pl.BlockSpec((len(POOL_WINDOWS), POOL_CH, POOL_CH), lambda i: (0, 0, 0)),
                  pl.BlockSpec((1, D_POOL), lambda i: (0, 0))],
        out_specs=pl.BlockSpec((tc, D_POOL), lambda i: (i, 0)),
        out_shape=jax.ShapeDtypeStruct((m, D_POOL), BF16),
        scratch_shapes=[pltpu.VMEM((POOL_HALO + tc, D_POOL), F32)],
        compiler_params=pltpu.CompilerParams(dimension_semantics=("arbitrary",),
                                             vmem_limit_bytes=VMEM_LIMIT),
    )(src, pool_w_bf16, pool_scale.reshape(1, D_POOL))


def _softplus(z):
    return jnp.maximum(z, 0.0) + jnp.log1p(jnp.exp(-jnp.abs(z)))


def _rwkv_prep(r, k, v, lora, prm, seg, segt):
    (w0, wd, a0, wa, wg, k_k, k_a, r_k) = prm
    wl = lora[:, 0:LORA_PAD]
    al = lora[:, LORA_PAD:2 * LORA_PAD]
    gl = lora[:, 2 * LORA_PAD:]
    logw = -_softplus(-(w0 + _dot_x3(jnp.tanh(wl), wd))) - 0.5
    log_decay = -jnp.exp(logw)
    decay = jnp.exp(log_decay)
    a = _sigmoid(a0 + _dot_x3(al, wa))
    gate = _dot(_sigmoid(gl).astype(BF16), wg.astype(BF16))
    kk = k * k_k
    ss = _dot_exact_rhs(kk * kk, seg)
    inv = 1.0 / jnp.maximum(jnp.sqrt(ss), 1e-12)
    kk = kk * _dot_exact_rhs(inv, segt)
    k_mod = k * (1.0 + (a - 1.0) * k_a)
    bonus = _dot_exact_rhs(_dot_exact_rhs(r * k_mod * r_k, seg), segt) * v
    return decay, log_decay, k_mod, -kk, kk * a, gate, bonus


def _group_norm_head(y):
    mu = jnp.mean(y, axis=-1, keepdims=True)
    yc = y - mu
    var = jnp.mean(yc * yc, axis=-1, keepdims=True)
    return yc * lax.rsqrt(var + GN_EPS)


def _group_norm_rows(y, seg, segt):
    inv_n = 1.0 / HEAD_DIM
    mu = _dot_exact_rhs(_dot_exact_rhs(y, seg) * inv_n, segt)
    yc = y - mu
    var = _dot_exact_rhs(yc * yc, seg) * inv_n
    return yc * _dot_exact_rhs(lax.rsqrt(var + GN_EPS), segt)


WKV_CHUNK = HEAD_DIM
PAIR_LANES = 2 * HEAD_DIM
N_PAIRS = RWKV_HEADS // 2
_NN = (((1,), (0,)), ((), ()))
_NT = (((1,), (1,)), ((), ()))
_TN = (((0,), (0,)), ((), ()))


def _dg_x3(a, b, dims):
    ah, al = _split2(a)
    bh, bl = _split2(b)
    (ca,), (cb,) = dims[0]
    dg = lambda x, y: lax.dot_general(x, y, dims, preferred_element_type=F32)
    return dg(jnp.concatenate([ah, al], axis=ca), jnp.concatenate([bh, bh], axis=cb)) + dg(ah, bl)


def _wkv_masks():
    n = PAIR_LANES
    ri = lax.broadcasted_iota(jnp.int32, (n, n), 0)
    ci = lax.broadcasted_iota(jnp.int32, (n, n), 1)
    same_head = (ri // HEAD_DIM) == (ci // HEAD_DIM)
    same16 = (ri // 16) == (ci // 16)
    same32 = (ri // 32) == (ci // 32)
    return dict(same_head=same_head, strict=same_head & (ci < ri), incl=same_head & (ci <= ri), eye=ri == ci,
                same16=same16, only32=same32 & jnp.logical_not(same16), not32=jnp.logical_not(same32),
                head0=lax.broadcasted_iota(jnp.int32, (1, n), 1) < HEAD_DIM)


def _each(fn, *lists):
    return [fn(*xs) for xs in zip(*lists)]


def _unit_lower_inverse(labs, mk):
    ds = [jnp.where(mk["same16"], lab, 0.0) for lab in labs]
    xs = [jnp.where(mk["eye"], 1.0, 0.0) + d for d in ds]
    mm = lambda x, y: _dg_x3(x, y, _NN)
    for _ in range(3):
        ds = _each(mm, ds, ds)
        xs = _each(lambda x, d: x + mm(x, d), xs, ds)
    for level in ("only32", "not32"):
        es = [jnp.where(mk[level], lab, 0.0) for lab in labs]
        exs = _each(mm, es, xs)
        xs = _each(lambda x, ex: x + mm(x, ex), xs, exs)
    return xs


WKV_GROUP = 4


def _wkv_chunk_group(rows, s_bds, mk):
    c = WKV_CHUNK
    stack = lambda x: jnp.concatenate([jnp.where(mk["head0"], x, 0.0), jnp.where(mk["head0"], 0.0, x)], axis=0)
    dup = lambda x: jnp.concatenate([x, x], axis=0)
    col = lambda i: [row[i] for row in rows]
    n = PAIR_LANES
    a_s, r_s, v_s = [_each(stack, col(i)) for i in (0, 1, 6)]
    bh2, kh2 = [_each(dup, col(i)) for i in (4, 5)]
    bk2 = _each(lambda bt, kt: jnp.concatenate([dup(bt), dup(kt)], axis=0), col(2), col(3))
    nt = lambda x, y: _dg_x3(x, y, _NT)
    tn = lambda x, y: _dg_x3(x, y, _TN)
    below = lambda x: jnp.where(mk["strict"], x, 0.0)
    upto = lambda x: jnp.where(mk["incl"], x, 0.0)
    heads = lambda x: jnp.where(mk["same_head"], x, 0.0)
    a_bk = _each(nt, a_s, bk2)
    lab = [below(x[:, :n]) for x in a_bk]
    lak = [below(x[:, n:]) for x in a_bk]
    nt1 = lambda x, y: lax.dot_general(x.astype(BF16), y.astype(BF16), _NT, preferred_element_type=F32)
    mm1 = lambda x, y: _dot(x.astype(BF16), y.astype(BF16))
    r_bk = _each(nt1, r_s, bk2)
    lrb = [upto(x[:, :n]) for x in r_bk]
    lrk = [upto(x[:, n:]) for x in r_bk]
    t_inv = _unit_lower_inverse(lab, mk)
    mm = lambda x, y: _dg_x3(x, y, _NN)
    lakv = _each(mm, lak, v_s)
    ta_u = _each(lambda t, a, w: mm(t, jnp.concatenate([a, w], axis=1)), t_inv, a_s, lakv)
    ta_s = [x[:, :n] for x in ta_u]
    u_s = [x[:, n:] for x in ta_u]
    m_bd = _each(lambda m, w: heads(m) + jnp.where(mk["eye"], w, 0.0), _each(tn, ta_s, bh2), col(7))
    g_bd = _each(lambda g1, g2: heads(g1 + g2), _each(tn, u_s, bh2), _each(tn, v_s, kh2))
    y_rhs = _each(lambda tau, vs: jnp.concatenate([tau, jnp.concatenate([jnp.zeros_like(vs), vs], axis=1)],
                                                  axis=0), ta_u, v_s)
    y_parts = _each(lambda lb, lk, rhs: mm1(jnp.concatenate([lb, lk], axis=1), rhs), lrb, lrk, y_rhs)
    q_s = [x[:, :n] for x in y_parts]
    y0_s = [x[:, n:] for x in y_parts]
    q = _each(lambda rt, qs: rt + (qs[:c] + qs[c:]), col(1), q_s)
    y = _each(lambda yq, y0: yq + (y0[:c] + y0[c:]), _each(nt1, q, s_bds), y0_s)
    s_new = _each(lambda sm, g: sm + g, _each(mm, s_bds, m_bd), g_bd)
    return y, s_new


def _rwkv_seq_kernel(r_ref, k_ref, v_ref, lora_ref, w0_ref, wd_ref, a0_ref, wa_ref, wg_ref, kk_ref, ka_ref,
                     rk_ref, lnw_ref, lnb_ref, seg_ref, segt_ref, o_ref, sout_ref, s_scr, y_scr):
    c = pl.program_id(1)

    @pl.when(c == 0)
    def _():
        s_scr[...] = jnp.zeros(s_scr.shape, F32)

    r = r_ref[...]
    v = v_ref[...]
    seg = seg_ref[...]
    segt = segt_ref[...]
    prm = (w0_ref[...], wd_ref[...], a0_ref[...], wa_ref[...], wg_ref[...], kk_ref[...], ka_ref[...], rk_ref[...])
    decay, log_decay, k_mod, neg_kk, b, gate, bonus = _rwkv_prep(r, k_ref[...], v, lora_ref[...], prm, seg, segt)

    n = WKV_CHUNK
    tri = (lax.broadcasted_iota(jnp.int32, (n, n), 1) <= lax.broadcasted_iota(jnp.int32, (n, n), 0)).astype(BF16)
    ld_h, ld_m, ld_l = _split3(log_decay)
    cw = _dot(tri, ld_h) + (_dot(tri, ld_m) + _dot(tri, ld_l))
    cw_prev = cw - log_decay
    cw_last = cw[n - 1:n, :]
    e_prev = jnp.exp(cw_prev)
    e_inv = jnp.exp(-cw)
    e_rem = jnp.exp(cw_last - cw)
    at = neg_kk * e_prev
    rt = r * (e_prev * decay)
    bt = b * e_inv
    kt = k_mod * e_inv
    bh = b * e_rem
    kh = k_mod * e_rem
    w_last = jnp.exp(cw_last)

    mk = _wkv_masks()
    for p0 in range(0, N_PAIRS, WKV_GROUP):
        pairs = range(p0, p0 + WKV_GROUP)
        lanes = [slice(p * PAIR_LANES, (p + 1) * PAIR_LANES) for p in pairs]
        rows = [tuple(z[:, sl] for z in (at, rt, bt, kt, bh, kh, v, w_last)) for sl in lanes]
        ys, s_new = _wkv_chunk_group(rows, [s_scr[p] for p in pairs], mk)
        for p, sl, y, s in zip(pairs, lanes, ys, s_new):
            s_scr[p] = s
            y_scr[:, sl] = y

    out = (_group_norm_rows(y_scr[...], seg, segt) * lnw_ref[...] + lnb_ref[...] + bonus) * gate
    o_ref[...] = out.astype(BF16)

    @pl.when(c == pl.num_programs(1) - 1)
    def _():
        for p in range(N_PAIRS):
            sout_ref[0, 2 * p] = s_scr[p, 0:HEAD_DIM, 0:HEAD_DIM]
            sout_ref[0, 2 * p + 1] = s_scr[p, HEAD_DIM:, HEAD_DIM:]


def _rwkv_weights(w0, w_decay_up, a0, w_aaa_up, w_gate_up, k_k, k_a, r_k, ln_x_w, ln_x_b):
    row = lambda z: z.reshape(1, D_RWKV)
    pad = lambda z: jnp.pad(z, ((0, LORA_PAD - z.shape[0]), (0, 0)))
    head_of_col = jnp.arange(D_RWKV, dtype=jnp.int32) // HEAD_DIM
    seg = (head_of_col[:, None] == jnp.arange(LANES, dtype=jnp.int32)[None, :]).astype(BF16)
    return (row(w0), pad(w_decay_up), row(a0), pad(w_aaa_up), w_gate_up, row(k_k), row(k_a), row(r_k),
            row(ln_x_w), row(ln_x_b), seg, seg.T)


def _const_spec(shape, grid_rank):
    zeros = (0,) * len(shape)
    if grid_rank == 1:
        return pl.BlockSpec(shape, lambda i: zeros)
    return pl.BlockSpec(shape, lambda i, j: zeros)


def _rwkv_seq(hx, weights, *, batch, seq):
    tc = WKV_CHUNK
    nchunk = seq // tc
    rowmap = lambda cb: (lambda b, c: (b * nchunk + c, cb))
    in_specs = [pl.BlockSpec((tc, D_RWKV), rowmap(0)),
                pl.BlockSpec((tc, D_RWKV), rowmap(1)),
                pl.BlockSpec((tc, D_RWKV), rowmap(2)),
                pl.BlockSpec((tc, 2 * LORA_PAD + GATE_LORA), rowmap(COL_LORA // (2 * LORA_PAD + GATE_LORA)))]
    in_specs += [_const_spec(w.shape, 2) for w in weights]
    return pl.pallas_call(
        _rwkv_seq_kernel,
        grid=(batch, nchunk),
        in_specs=in_specs,
        out_specs=[pl.BlockSpec((tc, D_RWKV), lambda b, c: (b * nchunk + c, 0)),
                   pl.BlockSpec((1, RWKV_HEADS, HEAD_DIM, HEAD_DIM), lambda b, c: (b, 0, 0, 0))],
        out_shape=[jax.ShapeDtypeStruct((batch * seq, D_RWKV), BF16),
                   jax.ShapeDtypeStruct((batch, RWKV_HEADS, HEAD_DIM, HEAD_DIM), F32)],
        scratch_shapes=[pltpu.VMEM((N_PAIRS, PAIR_LANES, PAIR_LANES), F32), pltpu.VMEM((tc, D_RWKV), F32)],
        compiler_params=pltpu.CompilerParams(dimension_semantics=("arbitrary", "arbitrary"),
                                             vmem_limit_bytes=VMEM_LIMIT),
    )(hx, hx, hx, hx, *weights)


def _rwkv_tok_prep_kernel(r_ref, k_ref, v_ref, lora_ref, w0_ref, wd_ref, a0_ref, wa_ref, wg_ref, kk_ref, ka_ref,
                          rk_ref, seg_ref, segt_ref, w_o, k_o, a_o, b_o, g_o, bonus_o):
    prm = (w0_ref[...], wd_ref[...], a0_ref[...], wa_ref[...], wg_ref[...], kk_ref[...], ka_ref[...], rk_ref[...])
    decay, _, k_mod, neg_kk, b, gate, bonus = _rwkv_prep(r_ref[...], k_ref[...], v_ref[...], lora_ref[...], prm,
                                                          seg_ref[...], segt_ref[...])
    w_o[...] = decay
    k_o[...] = k_mod
    a_o[...] = neg_kk
    b_o[...] = b
    g_o[...] = gate
    bonus_o[...] = bonus


def _rwkv_tok_step_kernel(r_ref, w_ref, k_ref, v_ref, a_ref, b_ref, g_ref, bonus_ref, lnw_ref, lnb_ref, s0_ref,
                          o_ref, s_ref, *, tb):
    eye = (lax.broadcasted_iota(jnp.int32, (HEAD_DIM, HEAD_DIM), 0)
           == lax.broadcasted_iota(jnp.int32, (HEAD_DIM, HEAD_DIM), 1))

    def body(bi, carry):
        s = s0_ref[bi]
        sa = jnp.sum(s * a_ref[bi], axis=2, keepdims=True)
        v_col = jnp.sum(jnp.where(eye, v_ref[bi], 0.0), axis=2, keepdims=True)
        s = s * w_ref[bi] + sa * b_ref[bi] + v_col * k_ref[bi]
        s_ref[bi] = s
        y_col = jnp.sum(s * r_ref[bi], axis=2, keepdims=True)
        y = jnp.sum(jnp.where(eye, y_col, 0.0), axis=1, keepdims=True)
        o_ref[bi] = (_group_norm_head(y) * lnw_ref[0] + lnb_ref[0] + bonus_ref[bi]) * g_ref[bi]
        return carry

    lax.fori_loop(0, tb, body, 0)


def _rwkv_tok(hx, state, weights, *, tb):
    nb = hx.shape[0]
    (w0, wd, a0, wa, wg, k_k, k_a, r_k, ln_w, ln_b, seg, segt) = weights
    prep_w = (w0, wd, a0, wa, wg, k_k, k_a, r_k, seg, segt)
    colmap = lambda cb: (lambda i: (0, cb))
    in_specs = [pl.BlockSpec((nb, D_RWKV), colmap(0)),
                pl.BlockSpec((nb, D_RWKV), colmap(1)),
                pl.BlockSpec((nb, D_RWKV), colmap(2)),
                pl.BlockSpec((nb, 2 * LORA_PAD + GATE_LORA), colmap(COL_LORA // (2 * LORA_PAD + GATE_LORA)))]
    in_specs += [_const_spec(w.shape, 1) for w in prep_w]
    rows = jax.ShapeDtypeStruct((nb, D_RWKV), F32)
    prepped = pl.pallas_call(
        _rwkv_tok_prep_kernel,
        grid=(1,),
        in_specs=in_specs,
        out_specs=[pl.BlockSpec((nb, D_RWKV), lambda i: (0, 0))] * 6,
        out_shape=[rows] * 6,
        compiler_params=pltpu.CompilerParams(dimension_semantics=("arbitrary",),
                                             vmem_limit_bytes=VMEM_LIMIT),
    )(hx, hx, hx, hx, *prep_w)
    decay, k_mod, neg_kk, b, gate, bonus = prepped
    heads = lambda z: z.reshape(nb, RWKV_HEADS, 1, HEAD_DIM)
    vecs = [heads(z) for z in (hx[:, COL_R:COL_R + D_RWKV], decay, k_mod, hx[:, COL_V:COL_V + D_RWKV], neg_kk, b,
                               gate, bonus)]
    vec_spec = pl.BlockSpec((tb, RWKV_HEADS, 1, HEAD_DIM), lambda i: (i, 0, 0, 0))
    ln_spec = pl.BlockSpec((1, RWKV_HEADS, 1, HEAD_DIM), lambda i: (0, 0, 0, 0))
    state_spec = pl.BlockSpec((tb, RWKV_HEADS, HEAD_DIM, HEAD_DIM), lambda i: (i, 0, 0, 0))
    out, new_state = pl.pallas_call(
        functools.partial(_rwkv_tok_step_kernel, tb=tb),
        grid=(nb // tb,),
        in_specs=[vec_spec] * 8 + [ln_spec, ln_spec, state_spec],
        out_specs=[vec_spec, state_spec],
        out_shape=[jax.ShapeDtypeStruct((nb, RWKV_HEADS, 1, HEAD_DIM), F32),
                   jax.ShapeDtypeStruct(state.shape, F32)],
        compiler_params=pltpu.CompilerParams(dimension_semantics=("arbitrary",),
                                             vmem_limit_bytes=VMEM_LIMIT),
    )(*vecs, ln_w.reshape(1, RWKV_HEADS, 1, HEAD_DIM), ln_b.reshape(1, RWKV_HEADS, 1, HEAD_DIM), state)
    return out.reshape(nb, D_RWKV).astype(BF16), new_state


def _route(logits, bias):
    scores = _sigmoid(logits)
    sel = scores + bias
    rows = logits.shape[0]
    lane_i = lax.broadcasted_iota(jnp.int32, (rows, N_EXPERTS), 1)
    grp = lane_i // GROUP_SIZE
    lane = lane_i.astype(F32)
    neg_inf = -jnp.inf

    def take_max(x):
        m = jnp.max(x, axis=1, keepdims=True)
        idx = jnp.min(jnp.where(x == m, lane, float(N_EXPERTS)), axis=1, keepdims=True)
        return m, lane == idx

    gscore = []
    for g in range(N_GROUPS):
        x = jnp.where(grp == g, sel, neg_inf)
        m1, hit = take_max(x)
        m2 = jnp.max(jnp.where(hit, neg_inf, x), axis=1, keepdims=True)
        gscore.append(m1 + m2)
    emask = jnp.zeros((rows, N_EXPERTS), jnp.bool_)
    for g in range(N_GROUPS):
        ahead = jnp.zeros((rows, 1), jnp.int32)
        for g2 in range(N_GROUPS):
            if g2 == g:
                continue
            beats = (gscore[g2] >= gscore[g]) if g2 < g else (gscore[g2] > gscore[g])
            ahead = ahead + beats.astype(jnp.int32)
        emask = emask | ((grp == g) & (ahead < TOPK_GROUPS))
    x = jnp.where(emask, sel, neg_inf)
    chosen = jnp.zeros((rows, N_EXPERTS), jnp.bool_)
    slot_lane = lax.broadcasted_iota(jnp.int32, (rows, LANES), 1)
    top_idx = jnp.zeros((rows, LANES), F32)
    top_w = jnp.zeros((rows, LANES), F32)
    for kth in range(TOP_K):
        m = jnp.max(x, axis=1, keepdims=True)
        idx = jnp.min(jnp.where(x == m, lane, float(N_EXPERTS)), axis=1, keepdims=True)
        hit = lane == idx
        chosen = chosen | hit
        x = jnp.where(hit, neg_inf, x)
        top_idx = jnp.where(slot_lane == kth, idx, top_idx)
        top_w = jnp.where(slot_lane == kth, jnp.sum(jnp.where(hit, scores, 0.0), axis=1, keepdims=True), top_w)
    top_w = top_w / jnp.sum(top_w, axis=1, keepdims=True) * ROUTE_SCALE
    return top_idx.astype(jnp.int32), top_w, chosen.astype(F32)


def _out_proj_kernel(pool_ref, rwkv_ref, x_ref, g1_ref, sc_ref, sh_ref, wp_ref, wr_ref, npost_ref, npre_ref,
                     wrt_ref, rb_ref, sg_ref, su_ref, sd_ref, x1_ref, u2_ref, idx_ref, topw_ref, chosen_ref,
                     shared_ref):
    m = _dot(pool_ref[...], wp_ref[...]) + _dot(rwkv_ref[...], wr_ref[...])
    x1 = x_ref[...] + g1_ref[0] * _rms(m, npost_ref[...])
    x1_ref[...] = x1
    u2 = _rms(x1, npre_ref[...]) * (1.0 + sc_ref[0]) + sh_ref[0]
    u2_ref[...] = u2
    top_idx, top_w, chosen = _route(_dot_x3(u2, wrt_ref[...]), rb_ref[...])
    idx_ref[...] = top_idx
    topw_ref[...] = top_w
    chosen_ref[...] = chosen
    u2b = u2.astype(BF16)
    h = _silu(_dot(u2b, sg_ref[...])) * _dot(u2b, su_ref[...])
    shared_ref[...] = _dot(h.astype(BF16), sd_ref[...])


def _out_proj(pool_out, rwkv_out, x, g1, sc2, sh2, w_out_bf16, n_post, n_pre, w_router, router_bias, sg, su, sd, *,
              tm, tiles_per_seq):
    m, d = x.shape
    if tiles_per_seq is None:
        mod_spec = pl.BlockSpec((1, tm, d), lambda i: (i, 0, 0))
    else:
        mod_spec = pl.BlockSpec((1, 1, d), lambda i: (i // tiles_per_seq, 0, 0))
    row = lambda width: pl.BlockSpec((tm, width), lambda i: (i, 0))
    return pl.pallas_call(
        _out_proj_kernel,
        grid=(m // tm,),
        in_specs=[row(D_POOL), row(D_RWKV), row(d), mod_spec, mod_spec, mod_spec,
                  _const_spec((D_POOL, d), 1), _const_spec((D_RWKV, d), 1),
                  _const_spec((1, d), 1), _const_spec((1, d), 1),
                  _const_spec((d, N_EXPERTS), 1), _const_spec((1, N_EXPERTS), 1),
                  _const_spec(sg.shape, 1), _const_spec(su.shape, 1), _const_spec(sd.shape, 1)],
        out_specs=[row(d), row(d), row(LANES), row(LANES), row(N_EXPERTS), row(d)],
        out_shape=[jax.ShapeDtypeStruct((m, d), F32), jax.ShapeDtypeStruct((m, d), F32),
                   jax.ShapeDtypeStruct((m, LANES), jnp.int32), jax.ShapeDtypeStruct((m, LANES), F32),
                   jax.ShapeDtypeStruct((m, N_EXPERTS), F32), jax.ShapeDtypeStruct((m, d), F32)],
        compiler_params=pltpu.CompilerParams(dimension_semantics=("arbitrary",),
                                             vmem_limit_bytes=VMEM_LIMIT),
    )(pool_out, rwkv_out, x, g1, sc2, sh2, w_out_bf16[:D_POOL], w_out_bf16[D_POOL:],
      n_post.reshape(1, d), n_pre.reshape(1, d), w_router, router_bias.reshape(1, N_EXPERTS), sg, su, sd)


MOE_TM = 384


ROW_GROUP = 8


def _experts_kernel(te_ref, fresh_ref, wslot_ref, nexte_ref, groups_ref, nused_ref, idx_ref, idx_next_ref, ws_ref,
                    x_hbm, wg_hbm, wu_hbm, wd_hbm, ys_ref, wg_buf, wu_buf, wd_buf, g_bf, u_bf, d_bf, x_buf,
                    w_sems, x_sems):
    t = pl.program_id(0)
    n_used = nused_ref[0]

    def weight_copies(e, slot):
        return (pltpu.make_async_copy(wg_hbm.at[e], wg_buf.at[slot], w_sems.at[slot]),
                pltpu.make_async_copy(wu_hbm.at[e], wu_buf.at[slot], w_sems.at[slot]),
                pltpu.make_async_copy(wd_hbm.at[e], wd_buf.at[slot], w_sems.at[slot]))

    def start_rows(table_ref, slot, n_groups):
        def body(g, carry):
            for r in range(ROW_GROUP):
                j = g * ROW_GROUP + r
                pltpu.make_async_copy(x_hbm.at[pl.ds(table_ref[0, 0, j], 1)], x_buf.at[slot, pl.ds(j, 1)],
                                      x_sems.at[slot]).start()
            return carry
        lax.fori_loop(0, n_groups, body, 0)

    @pl.when(t == 0)
    def _():
        for cp in weight_copies(te_ref[0], 0):
            cp.start(priority=1)
        x_buf[...] = jnp.zeros(x_buf.shape, F32)
        start_rows(idx_ref, 0, groups_ref[0])

    @pl.when(t + 1 < n_used)
    def _():
        start_rows(idx_next_ref, (t + 1) % 2, groups_ref[t + 1])

    @pl.when(jnp.logical_and(t < n_used, fresh_ref[t] == 1))
    def _():
        slot = wslot_ref[t]
        for cp in weight_copies(te_ref[t], slot):
            cp.wait()

        @pl.when(nexte_ref[t] >= 0)
        def _():
            for cp in weight_copies(nexte_ref[t], 1 - slot):
                cp.start(priority=1)

        g_bf[...] = wg_buf[slot].astype(BF16)
        u_bf[...] = wu_buf[slot].astype(BF16)
        d_bf[...] = wd_buf[slot].astype(BF16)

    @pl.when(t < n_used)
    def _():
        cur = t % 2
        n_rows = pl.multiple_of(groups_ref[t] * ROW_GROUP, ROW_GROUP)
        pltpu.make_async_copy(x_hbm.at[pl.ds(0, n_rows)], x_buf.at[cur, pl.ds(0, n_rows)], x_sems.at[cur]).wait()
        x = x_buf[cur].astype(BF16)
        h = _silu(_dot(x, g_bf[...])) * _dot(x, u_bf[...])
        diag = (lax.broadcasted_iota(jnp.int32, (MOE_TM, MOE_TM), 0)
                == lax.broadcasted_iota(jnp.int32, (MOE_TM, MOE_TM), 1))
        w_col = jnp.sum(jnp.where(diag, ws_ref[0], 0.0), axis=1, keepdims=True)
        ys_ref[...] = _pack_bf16_halves(_dot(h.astype(BF16), d_bf[...]) * w_col)

    @pl.when(t >= n_used)
    def _():
        ys_ref[...] = jnp.zeros(ys_ref.shape, jnp.uint32)


def _experts(tables, x_all, exp_gate, exp_up, exp_down):
    tile_expert, fresh, wslot, next_e, row_groups, n_used, slot_tok, slot_w = tables
    d = x_all.shape[1]
    n_tiles = tile_expert.shape[0]
    n_slots = n_tiles * MOE_TM
    idx_block = (1, 1, MOE_TM)
    any_spec = pl.BlockSpec(memory_space=pl.ANY)
    live = lambda t, nused: jnp.where(t < nused[0], t, n_tiles - 1)
    grid_spec = pltpu.PrefetchScalarGridSpec(
        num_scalar_prefetch=6,
        grid=(n_tiles,),
        in_specs=[pl.BlockSpec(idx_block, lambda t, *s: (live(t, s[5]), 0, 0), memory_space=pltpu.SMEM),
                  pl.BlockSpec(idx_block, lambda t, *s: (live(t + 1, s[5]), 0, 0), memory_space=pltpu.SMEM),
                  pl.BlockSpec(idx_block, lambda t, *s: (live(t, s[5]), 0, 0)),
                  any_spec, any_spec, any_spec, any_spec],
        out_specs=pl.BlockSpec((MOE_TM, d // 2), lambda t, *s: (live(t, s[5]), 0)),
        scratch_shapes=[pltpu.VMEM((2, d, D_EXPERT), F32), pltpu.VMEM((2, d, D_EXPERT), F32),
                        pltpu.VMEM((2, D_EXPERT, d), F32),
                        pltpu.VMEM((d, D_EXPERT), BF16), pltpu.VMEM((d, D_EXPERT), BF16),
                        pltpu.VMEM((D_EXPERT, d), BF16),
                        pltpu.VMEM((2, MOE_TM, d), F32),
                        pltpu.SemaphoreType.DMA((2,)), pltpu.SemaphoreType.DMA((2,))],
    )
    idx_tiles = slot_tok.reshape(n_tiles, 1, MOE_TM)
    return pl.pallas_call(
        _experts_kernel,
        grid_spec=grid_spec,
        out_shape=jax.ShapeDtypeStruct((n_slots, d // 2), jnp.uint32),
        compiler_params=pltpu.CompilerParams(dimension_semantics=("arbitrary",),
                                             vmem_limit_bytes=VMEM_LIMIT),
    )(tile_expert, fresh, wslot, next_e, row_groups, n_used, idx_tiles, idx_tiles,
      slot_w.reshape(n_tiles, 1, MOE_TM), x_all, exp_gate, exp_up, exp_down)


ASSIGN_BITS = 17


def _dispatch(chosen, top_idx, top_w):
    n = chosen.shape[0]
    n_assign = n * TOP_K
    low_mask = (1 << ASSIGN_BITS) - 1
    assert n_assign + MOE_TM <= low_mask and N_EXPERTS << ASSIGN_BITS < 2 ** 31
    n_tiles = -(-n_assign // MOE_TM) + N_EXPERTS
    n_slots = n_tiles * MOE_TM
    sel = chosen.astype(jnp.int32)
    counts = jnp.sum(sel, axis=0)
    rank = jnp.cumsum(sel, axis=0) - sel
    tiles_e = (counts + MOE_TM - 1) // MOE_TM
    tile_end = jnp.cumsum(tiles_e)
    slot = (tile_end - tiles_e)[None, :] * MOE_TM + rank
    expert_ids = jnp.arange(N_EXPERTS, dtype=jnp.int32)
    slot_of_pair = jnp.stack([jnp.sum(jnp.where(top_idx[:, k:k + 1] == expert_ids[None, :], slot, 0), axis=1)
                              for k in range(TOP_K)])
    shift = 1 << ASSIGN_BITS
    int_max = jnp.iinfo(jnp.int32).max
    real_key = top_idx.reshape(-1) * shift + jnp.arange(n_assign, dtype=jnp.int32)
    pad_j = jnp.arange(MOE_TM - 1, dtype=jnp.int32)[None, :]
    pad_key = jnp.where(pad_j < (tiles_e * MOE_TM - counts)[:, None],
                        expert_ids[:, None] * shift + n_assign + pad_j, int_max)
    n_rest = n_slots - n_assign
    keys = jnp.concatenate([real_key, pad_key.reshape(-1),
                            jnp.full((n_rest - pad_key.size,), int_max, jnp.int32)])
    weights = jnp.concatenate([top_w.reshape(-1), jnp.zeros((n_rest,), F32)])
    sorted_key, sorted_w = lax.sort((keys, weights), num_keys=1)
    pair = sorted_key & low_mask
    is_real = pair < n_assign
    slot_tok = jnp.where(is_real, pair // TOP_K, 0)
    slot_w = jnp.where(is_real, sorted_w, 0.0)
    n_used = tile_end[-1]
    tile_ids = jnp.minimum(jnp.arange(n_tiles, dtype=jnp.int32), n_used - 1)
    tile_expert = jnp.minimum(jnp.searchsorted(tile_end, tile_ids, side="right"), N_EXPERTS - 1).astype(jnp.int32)
    in_use = jnp.arange(n_tiles, dtype=jnp.int32) < n_used
    prev_expert = jnp.concatenate([jnp.full((1,), -1, jnp.int32), tile_expert[:-1]])
    fresh = jnp.logical_and(in_use, tile_expert != prev_expert).astype(jnp.int32)
    wslot = (jnp.cumsum(fresh) - 1) % 2
    later = jnp.where(counts > 0, expert_ids, N_EXPERTS)
    next_used = jnp.concatenate([lax.cummin(later, reverse=True)[1:], jnp.full((1,), N_EXPERTS, jnp.int32)])
    next_used = jnp.where(next_used >= N_EXPERTS, -1, next_used)
    tile_in_expert = jnp.arange(n_tiles, dtype=jnp.int32) - (tile_end - tiles_e)[tile_expert]
    rows_in_tile = jnp.clip(counts[tile_expert] - tile_in_expert * MOE_TM, 0, MOE_TM)
    row_groups = jnp.where(in_use, (rows_in_tile + ROW_GROUP - 1) // ROW_GROUP, 0)
    tables = (tile_expert, fresh, wslot.astype(jnp.int32), next_used[tile_expert].astype(jnp.int32),
              row_groups.astype(jnp.int32), n_used.reshape(1).astype(jnp.int32), slot_tok.astype(jnp.int32), slot_w)
    return slot_of_pair, tables


def _final_kernel(idx_ref, idx_next_ref, shared_ref, ys_hbm, x1_ref, g2_ref, npost_ref, o_ref, rows_buf, sems, *,
                  tm):
    i = pl.program_id(0)
    n_rows = TOP_K * tm

    def start_gather(table_ref, buf_slot):
        for j in range(n_rows):
            pltpu.make_async_copy(ys_hbm.at[pl.ds(table_ref[0, 0, j], 1)], rows_buf.at[buf_slot, pl.ds(j, 1)],
                                  sems.at[buf_slot]).start()

    @pl.when(i == 0)
    def _():
        start_gather(idx_ref, 0)

    @pl.when(i + 1 < pl.num_programs(0))
    def _():
        start_gather(idx_next_ref, (i + 1) % 2)

    cur = i % 2
    pltpu.make_async_copy(ys_hbm.at[pl.ds(0, n_rows)], rows_buf.at[cur], sems.at[cur]).wait()
    left, right = _unpack_bf16_halves(rows_buf[cur, 0:tm])
    for kth in range(1, TOP_K):
        l_k, r_k = _unpack_bf16_halves(rows_buf[cur, kth * tm:(kth + 1) * tm])
        left, right = left + l_k, right + r_k
    routed = jnp.concatenate([left, right], axis=1)
    o_ref[...] = x1_ref[...] + g2_ref[0] * _rms(routed + shared_ref[...], npost_ref[...])


def _final(shared, ys, slot_tiles, x1, g2, n_post, *, tm, tiles_per_seq, tile_offset):
    m, d = x1.shape
    nt = m // tm
    if tiles_per_seq is None:
        mod_spec = pl.BlockSpec((1, tm, d), lambda i: (i, 0, 0))
    else:
        mod_spec = pl.BlockSpec((1, 1, d), lambda i: (i // tiles_per_seq, 0, 0))
    row = pl.BlockSpec((tm, d), lambda i: (i, 0))
    idx_block = (1, 1, TOP_K * tm)
    return pl.pallas_call(
        functools.partial(_final_kernel, tm=tm),
        grid=(nt,),
        in_specs=[pl.BlockSpec(idx_block, lambda i: (i + tile_offset, 0, 0), memory_space=pltpu.SMEM),
                  pl.BlockSpec(idx_block, lambda i: (jnp.minimum(i + 1, nt - 1) + tile_offset, 0, 0),
                               memory_space=pltpu.SMEM),
                  row, pl.BlockSpec(memory_space=pl.ANY), row, mod_spec, _const_spec((1, d), 1)],
        out_specs=row,
        out_shape=jax.ShapeDtypeStruct((m, d), F32),
        scratch_shapes=[pltpu.VMEM((2, TOP_K * tm, d // 2), jnp.uint32), pltpu.SemaphoreType.DMA((2,))],
        compiler_params=pltpu.CompilerParams(dimension_semantics=("arbitrary",),
                                             vmem_limit_bytes=VMEM_LIMIT),
    )(slot_tiles, slot_tiles, shared, ys, x1, g2, n_post.reshape(1, d))


def _pad_cols_in(w_in, mu_shift):
    d = w_in.shape[0]
    hp = w_in[:, :D_POOL]
    hr = w_in[:, D_POOL:]
    cuts = [3 * D_RWKV, 3 * D_RWKV + DECAY_LORA, 3 * D_RWKV + DECAY_LORA + AAA_LORA]
    rkv, wl, al, gl = jnp.split(hr, cuts, axis=1)
    zpad = lambda z: jnp.pad(z, ((0, 0), (0, LORA_PAD - z.shape[1])))
    w = jnp.concatenate([rkv, hp, zpad(wl), zpad(al), gl], axis=1).astype(BF16)
    mu_rkv, mu_wl, mu_al, mu_gl = jnp.split(mu_shift, cuts)
    zp1 = lambda z: jnp.pad(z, (0, LORA_PAD - z.shape[0]))
    mu = jnp.concatenate([mu_rkv, jnp.zeros((D_POOL,), F32), zp1(mu_wl), zp1(mu_al), mu_gl])
    assert w.shape == (d, N_IN_PAD) and mu.shape == (N_IN_PAD,)
    return w, mu


def kernel(x_prompt, x_sample, c_prompt, c_sample, state_shift, state_pool, state_wkv, w_ada, b_ada, norm_pre_mix, norm_post_mix, norm_pre_ffn, norm_post_ffn, w_in, mu_shift, pool_w, pool_scale, w0, w_decay_up, a0, w_aaa_up, w_gate_up, k_k, k_a, r_k, ln_x_w, ln_x_b, w_out, w_router, router_bias, exp_gate, exp_up, exp_down, sh_gate, sh_up, sh_down):
    bp, seq, d = x_prompt.shape
    bs = x_sample.shape[0]
    assert w_ada.shape[0] == 1 and x_sample.shape[1] == 1
    np_rows = bp * seq

    c_all = jnp.concatenate([c_prompt, c_sample], axis=0)
    pad_rows = (-c_all.shape[0]) % 16
    ada = _ada(jnp.pad(c_all, ((0, pad_rows), (0, 0))), w_ada[0], b_ada[0])
    sh1, sc1, g1, sh2, sc2, g2 = jnp.split(ada, 6, axis=-1)
    pr = lambda z: z[:bp].reshape(bp, 1, d)
    TS = min(128, bs)
    sr = lambda z: z[bp:bp + bs].reshape(bs // TS, TS, d)

    w_in_p, mu_p = _pad_cols_in(w_in[0], mu_shift[0])
    xp = x_prompt.reshape(np_rows, d)
    xs = x_sample.reshape(bs, d)

    TM_IN = min(1024, seq)
    hx_p, u_tail = _in_proj(xp, norm_pre_mix[0], pr(sc1), pr(sh1), w_in_p, mu_p, tm=TM_IN,
                            tiles_per_seq=seq // TM_IN)
    hx_s, u_s = _in_proj(xs, norm_pre_mix[0], sr(sc1), sr(sh1), w_in_p, mu_p, tm=TS, tiles_per_seq=1,
                         prev=state_shift[0])
    new_shift_prompt = u_tail.reshape(bp, seq // TM_IN, 8, d)[:, -1, -1][None]
    new_shift_sample = u_s[None]

    pool_w_b = pool_w[0].astype(BF16)
    TC_POOL = min(512, seq)
    pool_p = _pool(hx_p, COL_POOL // D_POOL, pool_w_b, pool_scale[0], tc=TC_POOL, tiles_per_seq=seq // TC_POOL,
                   full_count=False)
    hp_p = hx_p[:, COL_POOL:COL_POOL + D_POOL].reshape(bp, seq, D_POOL)
    new_pool_prompt = hp_p[:, seq - POOL_BUF:][None]
    hp_s = hx_s[:, COL_POOL:COL_POOL + D_POOL]
    ext_s = jnp.concatenate([state_pool[0], hp_s[:, None, :]], axis=1)
    assert PAST_LEN + 1 >= max(POOL_WINDOWS)
    pool_s = _pool(ext_s.reshape(bs * (POOL_BUF + 1), D_POOL), 0, pool_w_b, pool_scale[0],
                   tc=bs * (POOL_BUF + 1), tiles_per_seq=1, full_count=True)
    pool_s = pool_s.reshape(bs, POOL_BUF + 1, D_POOL)[:, -1]
    new_pool_sample = ext_s[:, 1:][None]

    rw = _rwkv_weights(w0[0], w_decay_up[0], a0[0], w_aaa_up[0], w_gate_up[0], k_k[0], k_a[0], r_k[0],
                       ln_x_w[0], ln_x_b[0])
    rwkv_p, wkv_p = _rwkv_seq(hx_p, rw, batch=bp, seq=seq)
    rwkv_s, wkv_s = _rwkv_tok(hx_s, state_wkv[0], rw, tb=8)

    w_out_b = w_out[0].astype(BF16)
    assert np_rows % TS == 0 and seq % TS == 0
    sg, su, sd = sh_gate[0].astype(BF16), sh_up[0].astype(BF16), sh_down[0].astype(BF16)
    TM_OUT = min(256, seq)
    outs_p = _out_proj(pool_p, rwkv_p, xp, pr(g1), pr(sc2), pr(sh2), w_out_b, norm_post_mix[0], norm_pre_ffn[0],
                       w_router[0], router_bias[0], sg, su, sd, tm=TM_OUT, tiles_per_seq=seq // TM_OUT)
    outs_s = _out_proj(pool_s, rwkv_s, xs, sr(g1), sr(sc2), sr(sh2), w_out_b, norm_post_mix[0], norm_pre_ffn[0],
                       w_router[0], router_bias[0], sg, su, sd, tm=TS, tiles_per_seq=None)
    (x1_p, u2_p, idx_p, topw_p, chosen_p, shared_p), (x1_s, u2_s, idx_s, topw_s, chosen_s, shared_s) = outs_p, outs_s
    u2_all = jnp.concatenate([u2_p, u2_s], axis=0)
    idx_all = jnp.concatenate([idx_p[:, :TOP_K], idx_s[:, :TOP_K]], axis=0)
    topw_all = jnp.concatenate([topw_p[:, :TOP_K], topw_s[:, :TOP_K]], axis=0)
    chosen_all = jnp.concatenate([chosen_p, chosen_s], axis=0)

    n_all = np_rows + bs
    slot_of_pair, tables = _dispatch(chosen_all, idx_all, topw_all)
    ys = _experts(tables, u2_all, exp_gate[0], exp_up[0], exp_down[0])

    slot_tiles = slot_of_pair.reshape(TOP_K, n_all // TS, TS).transpose(1, 0, 2).reshape(n_all // TS, 1, TOP_K * TS)
    y_p = _final(shared_p, ys, slot_tiles, x1_p, pr(g2), norm_post_ffn[0], tm=TS, tiles_per_seq=seq // TS,
                 tile_offset=0)
    y_s = _final(shared_s, ys, slot_tiles, x1_s, sr(g2), norm_post_ffn[0], tm=TS, tiles_per_seq=None,
                 tile_offset=np_rows // TS)

    return (y_p.reshape(bp, seq, d), y_s.reshape(bs, 1, d), new_shift_prompt, new_pool_prompt, wkv_p[None],
            new_shift_sample, new_pool_sample, wkv_s[None])
```

```python
import functools

import jax
import jax.numpy as jnp
from jax import lax
from jax.experimental import pallas as pl
from jax.experimental.pallas import tpu as pltpu

F32 = jnp.float32
BF16 = jnp.bfloat16

D_MODEL = 2048
D_POOL = 512
POOL_WINDOWS = (2, 4, 8, 16)
POOL_CH = 128
POOL_BUF = 15
D_RWKV = 1536
HEAD_DIM = 64
RWKV_HEADS = 24
DECAY_LORA = 96
AAA_LORA = 96
GATE_LORA = 256
LORA_PAD = 128
GN_EPS = 64e-5
N_EXPERTS = 256
TOP_K = 8
N_GROUPS = 8
GROUP_SIZE = N_EXPERTS // N_GROUPS
TOPK_GROUPS = 4
D_EXPERT = 512
ROUTE_SCALE = 2.5
EPS = 1e-6
PAST_LEN = 16384

COL_R, COL_K, COL_V = 0, D_RWKV, 2 * D_RWKV
COL_POOL = 3 * D_RWKV
COL_LORA = COL_POOL + D_POOL
N_IN_PAD = COL_LORA + 2 * LORA_PAD + GATE_LORA
LANES = 128
VMEM_LIMIT = 56 * 1024 * 1024


def _dot(a, b):
    return jnp.dot(a, b, preferred_element_type=F32)


def _split2(x):
    hi = x.astype(BF16)
    lo = (x - hi.astype(F32)).astype(BF16)
    return hi, lo


def _split3(x):
    hi = x.astype(BF16)
    r = x - hi.astype(F32)
    mid = r.astype(BF16)
    lo = (r - mid.astype(F32)).astype(BF16)
    return hi, mid, lo


def _dot_x3(a, b):
    ah, al = _split2(a)
    bh, bl = _split2(b)
    return _dot(ah, bh) + (_dot(ah, bl) + _dot(al, bh))


def _dot_exact_rhs(a, b_bf16):
    h, m, l = _split3(a)
    return _dot(h, b_bf16) + (_dot(m, b_bf16) + _dot(l, b_bf16))


def _sigmoid(x):
    return 1.0 / (1.0 + jnp.exp(-x))


def _silu(x):
    return x * _sigmoid(x)


def _pack_bf16_halves(y):
    half = y.shape[1] // 2
    bits = pltpu.bitcast(y.astype(BF16).astype(F32), jnp.uint32)
    return bits[:, :half] | (bits[:, half:] >> 16)


def _unpack_bf16_halves(p):
    return (pltpu.bitcast(p & jnp.uint32(0xFFFF0000), F32), pltpu.bitcast(p << 16, F32))


def _rms(x, g):
    return x * lax.rsqrt(jnp.mean(x * x, axis=-1, keepdims=True) + EPS) * g


def _ada_kernel(c_ref, w_ref, b_ref, o_ref):
    o_ref[...] = _dot_x3(_silu(c_ref[...]), w_ref[...]) + b_ref[...]


def _ada(c_all, w_ada, b_ada):
    m, d = c_all.shape
    n = w_ada.shape[1]
    tn = 512
    return pl.pallas_call(
        _ada_kernel,
        grid=(n // tn,),
        in_specs=[pl.BlockSpec((m, d), lambda j: (0, 0)),
                  pl.BlockSpec((d, tn), lambda j: (0, j)),
                  pl.BlockSpec((1, tn), lambda j: (0, j))],
        out_specs=pl.BlockSpec((m, tn), lambda j: (0, j)),
        out_shape=jax.ShapeDtypeStruct((m, n), F32),
        compiler_params=pltpu.CompilerParams(dimension_semantics=("arbitrary",),
                                             vmem_limit_bytes=VMEM_LIMIT),
    )(c_all, w_ada, b_ada.reshape(1, n))


def _in_proj_kernel(*refs, tiles_per_seq, explicit_prev, tm):
    if explicit_prev:
        x_ref, g_ref, sc_ref, sh_ref, prev_ref, w_ref, mu_ref, hx_ref, u_ref, ub_scr = refs
    else:
        x_ref, g_ref, sc_ref, sh_ref, w_ref, mu_ref, hx_ref, u_ref, ub_scr, carry_scr = refs
    i = pl.program_id(0)
    j = pl.program_id(1)

    @pl.when(j == 0)
    def _():
        u = _rms(x_ref[...], g_ref[...])
        u = u * (1.0 + sc_ref[0]) + sh_ref[0]
        ub_scr[...] = u.astype(BF16)
        if explicit_prev:
            u_ref[...] = u
        else:
            u_ref[...] = u[tm - 8:, :]

    w = w_ref[...]
    h = _dot(ub_scr[...], w)
    if explicit_prev:
        hp = _dot(prev_ref[...].astype(BF16), w)
    else:
        @pl.when(i == 0)
        def _():
            carry_scr[j] = jnp.zeros(carry_scr.shape[1:], F32)

        first = jnp.where(i % tiles_per_seq == 0, 0.0, carry_scr[j, 0:1, :])
        row0 = lax.broadcasted_iota(jnp.int32, h.shape, 0) == 0
        hp = jnp.where(row0, first, pltpu.roll(h, 1, axis=0))
        carry_scr[j, 0:1, :] = h[tm - 1:tm, :]
    hx_ref[...] = h + (hp - h) * mu_ref[...]


def _in_proj(x, gamma, sc, sh, w_bf16, mu_pad, *, tm, tiles_per_seq, prev=None):
    m, d = x.shape
    n = w_bf16.shape[1]
    tn = 512
    nt = n // tn
    explicit_prev = prev is not None
    if explicit_prev:
        mod_spec = pl.BlockSpec((1, tm, d), lambda i, j: (i, 0, 0))
    else:
        mod_spec = pl.BlockSpec((1, 1, d), lambda i, j: (i // tiles_per_seq, 0, 0))
    in_specs = [pl.BlockSpec((tm, d), lambda i, j: (i, 0)),
                pl.BlockSpec((1, d), lambda i, j: (0, 0)),
                mod_spec, mod_spec]
    args = [x, gamma.reshape(1, d), sc, sh]
    if explicit_prev:
        in_specs.append(pl.BlockSpec((tm, d), lambda i, j: (i, 0)))
        args.append(prev)
    in_specs += [pl.BlockSpec((d, tn), lambda i, j: (0, j)),
                 pl.BlockSpec((1, tn), lambda i, j: (0, j))]
    args += [w_bf16, mu_pad.reshape(1, n)]
    scratch = [pltpu.VMEM((tm, d), BF16)]
    if explicit_prev:
        u_shape, u_spec = (m, d), pl.BlockSpec((tm, d), lambda i, j: (i, 0))
    else:
        u_shape, u_spec = (m // tm * 8, d), pl.BlockSpec((8, d), lambda i, j: (i, 0))
        scratch.append(pltpu.VMEM((nt, 8, tn), F32))
    return pl.pallas_call(
        functools.partial(_in_proj_kernel, tiles_per_seq=tiles_per_seq, explicit_prev=explicit_prev, tm=tm),
        grid=(m // tm, nt),
        in_specs=in_specs,
        out_specs=[pl.BlockSpec((tm, tn), lambda i, j: (i, j)), u_spec],
        out_shape=[jax.ShapeDtypeStruct((m, n), F32), jax.ShapeDtypeStruct(u_shape, F32)],
        scratch_shapes=scratch,
        compiler_params=pltpu.CompilerParams(dimension_semantics=("arbitrary", "arbitrary"),
                                             vmem_limit_bytes=VMEM_LIMIT),
    )(*args)


POOL_HALO = 16


def _pool_kernel(p_ref, pw_ref, ps_ref, o_ref, ext_scr, *, tiles_per_seq, full_count, tc):
    i = pl.program_id(0)

    @pl.when(i % tiles_per_seq == 0)
    def _():
        ext_scr[0:POOL_HALO, :] = jnp.zeros((POOL_HALO, D_POOL), F32)

    p = p_ref[...]
    ext_scr[POOL_HALO:POOL_HALO + tc, :] = p
    pos = (i % tiles_per_seq) * tc + lax.broadcasted_iota(jnp.int32, (tc, 1), 0)
    outs = []
    for gi, win in enumerate(POOL_WINDOWS):
        lo = gi * POOL_CH
        pg = p[:, lo:lo + POOL_CH]
        acc = pg
        for s in range(1, win):
            acc = acc + ext_scr[POOL_HALO - s:POOL_HALO - s + tc, lo:lo + POOL_CH]
        if full_count:
            dgi = acc / float(win) - pg
        else:
            cnt = jnp.minimum(pos + 1, win).astype(F32)
            dgi = acc / cnt - pg
        outs.append(_dot(dgi.astype(BF16), pw_ref[gi]))
    y = jnp.concatenate(outs, axis=-1) * ps_ref[...]
    o_ref[...] = y.astype(BF16)
    ext_scr[0:POOL_HALO, :] = ext_scr[tc:tc + POOL_HALO, :]


def _pool(src, col_block, pool_w_bf16, pool_scale, *, tc, tiles_per_seq, full_count):
    m = src.shape[0]
    return pl.pallas_call(
        functools.partial(_pool_kernel, tiles_per_seq=tiles_per_seq, full_count=full_count, tc=tc),
        grid=(m // tc,),
        in_specs=[pl.BlockSpec((tc, D_POOL), lambda i: (i, col_block)),
                  pl.BlockSpec((len(POOL_WINDOWS), POOL_CH, POOL_CH), lambda i: (0, 0, 0)),
                  pl.BlockSpec((1, D_POOL), lambda i: (0, 0))],
        out_specs=pl.BlockSpec((tc, D_POOL), lambda i: (i, 0)),
        out_shape=jax.ShapeDtypeStruct((m, D_POOL), BF16),
        scratch_shapes=[pltpu.VMEM((POOL_HALO + tc, D_POOL), F32)],
        compiler_params=pltpu.CompilerParams(dimension_semantics=("arbitrary",),
                                             vmem_limit_bytes=VMEM_LIMIT),
    )(src, pool_w_bf16, pool_scale.reshape(1, D_POOL))


def _softplus(z):
    return jnp.maximum(z, 0.0) + jnp.log1p(jnp.exp(-jnp.abs(z)))


def _rwkv_prep(r, k, v, lora, prm, seg, segt):
    (w0, wd, a0, wa, wg, k_k, k_a, r_k) = prm
    wl = lora[:, 0:LORA_PAD]
    al = lora[:, LORA_PAD:2 * LORA_PAD]
    gl = lora[:, 2 * LORA_PAD:]
    logw = -_softplus(-(w0 + _dot_x3(jnp.tanh(wl), wd))) - 0.5
    log_decay = -jnp.exp(logw)
    decay = jnp.exp(log_decay)
    a = _sigmoid(a0 + _dot_x3(al, wa))
    gate = _dot(_sigmoid(gl).astype(BF16), wg.astype(BF16))
    kk = k * k_k
    ss = _dot_exact_rhs(kk * kk, seg)
    inv = 1.0 / jnp.maximum(jnp.sqrt(ss), 1e-12)
    kk = kk * _dot_exact_rhs(inv, segt)
    k_mod = k * (1.0 + (a - 1.0) * k_a)
    bonus = _dot_exact_rhs(_dot_exact_rhs(r * k_mod * r_k, seg), segt) * v
    return decay, log_decay, k_mod, -kk, kk * a, gate, bonus


def _group_norm_head(y):
    mu = jnp.mean(y, axis=-1, keepdims=True)
    yc = y - mu
    var = jnp.mean(yc * yc, axis=-1, keepdims=True)
    return yc * lax.rsqrt(var + GN_EPS)


def _group_norm_rows(y, seg, segt):
    inv_n = 1.0 / HEAD_DIM
    mu = _dot_exact_rhs(_dot_exact_rhs(y, seg) * inv_n, segt)
    yc = y - mu
    var = _dot_exact_rhs(yc * yc, seg) * inv_n
    return yc * _dot_exact_rhs(lax.rsqrt(var + GN_EPS), segt)


WKV_CHUNK = HEAD_DIM
PAIR_LANES = 2 * HEAD_DIM
N_PAIRS = RWKV_HEADS // 2
_NN = (((1,), (0,)), ((), ()))
_NT = (((1,), (1,)), ((), ()))
_TN = (((0,), (0,)), ((), ()))


def _dg_x3(a, b, dims):
    ah, al = _split2(a)
    bh, bl = _split2(b)
    (ca,), (cb,) = dims[0]
    dg = lambda x, y: lax.dot_general(x, y, dims, preferred_element_type=F32)
    return dg(jnp.concatenate([ah, al], axis=ca), jnp.concatenate([bh, bh], axis=cb)) + dg(ah, bl)


def _wkv_masks():
    n = PAIR_LANES
    ri = lax.broadcasted_iota(jnp.int32, (n, n), 0)
    ci = lax.broadcasted_iota(jnp.int32, (n, n), 1)
    same_head = (ri // HEAD_DIM) == (ci // HEAD_DIM)
    same16 = (ri // 16) == (ci // 16)
    same32 = (ri // 32) == (ci // 32)
    return dict(same_head=same_head, strict=same_head & (ci < ri), incl=same_head & (ci <= ri), eye=ri == ci,
                same16=same16, only32=same32 & jnp.logical_not(same16), not32=jnp.logical_not(same32),
                head0=lax.broadcasted_iota(jnp.int32, (1, n), 1) < HEAD_DIM)


def _each(fn, *lists):
    return [fn(*xs) for xs in zip(*lists)]


def _unit_lower_inverse(labs, mk):
    ds = [jnp.where(mk["same16"], lab, 0.0) for lab in labs]
    xs = [jnp.where(mk["eye"], 1.0, 0.0) + d for d in ds]
    mm = lambda x, y: _dg_x3(x, y, _NN)
    for _ in range(3):
        ds = _each(mm, ds, ds)
        xs = _each(lambda x, d: x + mm(x, d), xs, ds)
    for level in ("only32", "not32"):
        es = [jnp.where(mk[level], lab, 0.0) for lab in labs]
        exs = _each(mm, es, xs)
        xs = _each(lambda x, ex: x + mm(x, ex), xs, exs)
    return xs


WKV_GROUP = 4


def _wkv_chunk_group(rows, s_bds, mk):
    c = WKV_CHUNK
    stack = lambda x: jnp.concatenate([jnp.where(mk["head0"], x, 0.0), jnp.where(mk["head0"], 0.0, x)], axis=0)
    dup = lambda x: jnp.concatenate([x, x], axis=0)
    col = lambda i: [row[i] for row in rows]
    n = PAIR_LANES
    a_s, r_s, v_s = [_each(stack, col(i)) for i in (0, 1, 6)]
    bh2, kh2 = [_each(dup, col(i)) for i in (4, 5)]
    bk2 = _each(lambda bt, kt: jnp.concatenate([dup(bt), dup(kt)], axis=0), col(2), col(3))
    nt = lambda x, y: _dg_x3(x, y, _NT)
    tn = lambda x, y: _dg_x3(x, y, _TN)
    below = lambda x: jnp.where(mk["strict"], x, 0.0)
    upto = lambda x: jnp.where(mk["incl"], x, 0.0)
    heads = lambda x: jnp.where(mk["same_head"], x, 0.0)
    a_bk = _each(nt, a_s, bk2)
    lab = [below(x[:, :n]) for x in a_bk]
    lak = [below(x[:, n:]) for x in a_bk]
    nt1 = lambda x, y: lax.dot_general(x.astype(BF16), y.astype(BF16), _NT, preferred_element_type=F32)
    mm1 = lambda x, y: _dot(x.astype(BF16), y.astype(BF16))
    r_bk = _each(nt1, r_s, bk2)
    lrb = [upto(x[:, :n]) for x in r_bk]
    lrk = [upto(x[:, n:]) for x in r_bk]
    t_inv = _unit_lower_inverse(lab, mk)
    mm = lambda x, y: _dg_x3(x, y, _NN)
    lakv = _each(mm, lak, v_s)
    ta_u = _each(lambda t, a, w: mm(t, jnp.concatenate([a, w], axis=1)), t_inv, a_s, lakv)
    ta_s = [x[:, :n] for x in ta_u]
    u_s = [x[:, n:] for x in ta_u]
    m_bd = _each(lambda m, w: heads(m) + jnp.where(mk["eye"], w, 0.0), _each(tn, ta_s, bh2), col(7))
    g_bd = _each(lambda g1, g2: heads(g1 + g2), _each(tn, u_s, bh2), _each(tn, v_s, kh2))
    y_rhs = _each(lambda tau, vs: jnp.concatenate([tau, jnp.concatenate([jnp.zeros_like(vs), vs], axis=1)],
                                                  axis=0), ta_u, v_s)
    y_parts = _each(lambda lb, lk, rhs: mm1(jnp.concatenate([lb, lk], axis=1), rhs), lrb, lrk, y_rhs)
    q_s = [x[:, :n] for x in y_parts]
    y0_s = [x[:, n:] for x in y_parts]
    q = _each(lambda rt, qs: rt + (qs[:c] + qs[c:]), col(1), q_s)
    y = _each(lambda yq, y0: yq + (y0[:c] + y0[c:]), _each(nt1, q, s_bds), y0_s)
    s_new = _each(lambda sm, g: sm + g, _each(mm, s_bds, m_bd), g_bd)
    return y, s_new


def _rwkv_seq_kernel(r_ref, k_ref, v_ref, lora_ref, w0_ref, wd_ref, a0_ref, wa_ref, wg_ref, kk_ref, ka_ref,
                     rk_ref, lnw_ref, lnb_ref, seg_ref, segt_ref, o_ref, sout_ref, s_scr, y_scr):
    c = pl.program_id(1)

    @pl.when(c == 0)
    def _():
        s_scr[...] = jnp.zeros(s_scr.shape, F32)

    r = r_ref[...]
    v = v_ref[...]
    seg = seg_ref[...]
    segt = segt_ref[...]
    prm = (w0_ref[...], wd_ref[...], a0_ref[...], wa_ref[...], wg_ref[...], kk_ref[...], ka_ref[...], rk_ref[...])
    decay, log_decay, k_mod, neg_kk, b, gate, bonus = _rwkv_prep(r, k_ref[...], v, lora_ref[...], prm, seg, segt)

    n = WKV_CHUNK
    tri = (lax.broadcasted_iota(jnp.int32, (n, n), 1) <= lax.broadcasted_iota(jnp.int32, (n, n), 0)).astype(BF16)
    ld_h, ld_m, ld_l = _split3(log_decay)
    cw = _dot(tri, ld_h) + (_dot(tri, ld_m) + _dot(tri, ld_l))
    cw_prev = cw - log_decay
    cw_last = cw[n - 1:n, :]
    e_prev = jnp.exp(cw_prev)
    e_inv = jnp.exp(-cw)
    e_rem = jnp.exp(cw_last - cw)
    at = neg_kk * e_prev
    rt = r * (e_prev * decay)
    bt = b * e_inv
    kt = k_mod * e_inv
    bh = b * e_rem
    kh = k_mod * e_rem
    w_last = jnp.exp(cw_last)

    mk = _wkv_masks()
    for p0 in range(0, N_PAIRS, WKV_GROUP):
        pairs = range(p0, p0 + WKV_GROUP)
        lanes = [slice(p * PAIR_LANES, (p + 1) * PAIR_LANES) for p in pairs]
        rows = [tuple(z[:, sl] for z in (at, rt, bt, kt, bh, kh, v, w_last)) for sl in lanes]
        ys, s_new = _wkv_chunk_group(rows, [s_scr[p] for p in pairs], mk)
        for p, sl, y, s in zip(pairs, lanes, ys, s_new):
            s_scr[p] = s
            y_scr[:, sl] = y

    out = (_group_norm_rows(y_scr[...], seg, segt) * lnw_ref[...] + lnb_ref[...] + bonus) * gate
    o_ref[...] = out.astype(BF16)

    @pl.when(c == pl.num_programs(1) - 1)
    def _():
        for p in range(N_PAIRS):
            sout_ref[0, 2 * p] = s_scr[p, 0:HEAD_DIM, 0:HEAD_DIM]
            sout_ref[0, 2 * p + 1] = s_scr[p, HEAD_DIM:, HEAD_DIM:]


def _rwkv_weights(w0, w_decay_up, a0, w_aaa_up, w_gate_up, k_k, k_a, r_k, ln_x_w, ln_x_b):
    row = lambda z: z.reshape(1, D_RWKV)
    pad = lambda z: jnp.pad(z, ((0, LORA_PAD - z.shape[0]), (0, 0)))
    head_of_col = jnp.arange(D_RWKV, dtype=jnp.int32) // HEAD_DIM
    seg = (head_of_col[:, None] == jnp.arange(LANES, dtype=jnp.int32)[None, :]).astype(BF16)
    return (row(w0), pad(w_decay_up), row(a0), pad(w_aaa_up), w_gate_up, row(k_k), row(k_a), row(r_k),
            row(ln_x_w), row(ln_x_b), seg, seg.T)


def _const_spec(shape, grid_rank):
    zeros = (0,) * len(shape)
    if grid_rank == 1:
        return pl.BlockSpec(shape, lambda i: zeros)
    return pl.BlockSpec(shape, lambda i, j: zeros)


def _rwkv_seq(hx, weights, *, batch, seq):
    tc = WKV_CHUNK
    nchunk = seq // tc
    rowmap = lambda cb: (lambda b, c: (b * nchunk + c, cb))
    in_specs = [pl.BlockSpec((tc, D_RWKV), rowmap(0)),
                pl.BlockSpec((tc, D_RWKV), rowmap(1)),
                pl.BlockSpec((tc, D_RWKV), rowmap(2)),
                pl.BlockSpec((tc, 2 * LORA_PAD + GATE_LORA), rowmap(COL_LORA // (2 * LORA_PAD + GATE_LORA)))]
    in_specs += [_const_spec(w.shape, 2) for w in weights]
    return pl.pallas_call(
        _rwkv_seq_kernel,
        grid=(batch, nchunk),
        in_specs=in_specs,
        out_specs=[pl.BlockSpec((tc, D_RWKV), lambda b, c: (b * nchunk + c, 0)),
                   pl.BlockSpec((1, RWKV_HEADS, HEAD_DIM, HEAD_DIM), lambda b, c: (b, 0, 0, 0))],
        out_shape=[jax.ShapeDtypeStruct((batch * seq, D_RWKV), BF16),
                   jax.ShapeDtypeStruct((batch, RWKV_HEADS, HEAD_DIM, HEAD_DIM), F32)],
        scratch_shapes=[pltpu.VMEM((N_PAIRS, PAIR_LANES, PAIR_LANES), F32), pltpu.VMEM((tc, D_RWKV), F32)],
        compiler_params=pltpu.CompilerParams(dimension_semantics=("arbitrary", "arbitrary"),
                                             vmem_limit_bytes=VMEM_LIMIT),
    )(hx, hx, hx, hx, *weights)


def _rwkv_tok_prep_kernel(r_ref, k_ref, v_ref, lora_ref, w0_ref, wd_ref, a0_ref, wa_ref, wg_ref, kk_ref, ka_ref,
                          rk_ref, seg_ref, segt_ref, w_o, k_o, a_o, b_o, g_o, bonus_o):
    prm = (w0_ref[...], wd_ref[...], a0_ref[...], wa_ref[...], wg_ref[...], kk_ref[...], ka_ref[...], rk_ref[...])
    decay, _, k_mod, neg_kk, b, gate, bonus = _rwkv_prep(r_ref[...], k_ref[...], v_ref[...], lora_ref[...], prm,
                                                          seg_ref[...], segt_ref[...])
    w_o[...] = decay
    k_o[...] = k_mod
    a_o[...] = neg_kk
    b_o[...] = b
    g_o[...] = gate
    bonus_o[...] = bonus


def _rwkv_tok_step_kernel(r_ref, w_ref, k_ref, v_ref, a_ref, b_ref, g_ref, bonus_ref, lnw_ref, lnb_ref, s0_ref,
                          o_ref, s_ref, *, tb):
    eye = (lax.broadcasted_iota(jnp.int32, (HEAD_DIM, HEAD_DIM), 0)
           == lax.broadcasted_iota(jnp.int32, (HEAD_DIM, HEAD_DIM), 1))

    def body(bi, carry):
        s = s0_ref[bi]
        sa = jnp.sum(s * a_ref[bi], axis=2, keepdims=True)
        v_col = jnp.sum(jnp.where(eye, v_ref[bi], 0.0), axis=2, keepdims=True)
        s = s * w_ref[bi] + sa * b_ref[bi] + v_col * k_ref[bi]
        s_ref[bi] = s
        y_col = jnp.sum(s * r_ref[bi], axis=2, keepdims=True)
        y = jnp.sum(jnp.where(eye, y_col, 0.0), axis=1, keepdims=True)
        o_ref[bi] = (_group_norm_head(y) * lnw_ref[0] + lnb_ref[0] + bonus_ref[bi]) * g_ref[bi]
        return carry

    lax.fori_loop(0, tb, body, 0)


def _rwkv_tok(hx, state, weights, *, tb):
    nb = hx.shape[0]
    (w0, wd, a0, wa, wg, k_k, k_a, r_k, ln_w, ln_b, seg, segt) = weights
    prep_w = (w0, wd, a0, wa, wg, k_k, k_a, r_k, seg, segt)
    colmap = lambda cb: (lambda i: (0, cb))
    in_specs = [pl.BlockSpec((nb, D_RWKV), colmap(0)),
                pl.BlockSpec((nb, D_RWKV), colmap(1)),
                pl.BlockSpec((nb, D_RWKV), colmap(2)),
                pl.BlockSpec((nb, 2 * LORA_PAD + GATE_LORA), colmap(COL_LORA // (2 * LORA_PAD + GATE_LORA)))]
    in_specs += [_const_spec(w.shape, 1) for w in prep_w]
    rows = jax.ShapeDtypeStruct((nb, D_RWKV), F32)
    prepped = pl.pallas_call(
        _rwkv_tok_prep_kernel,
        grid=(1,),
        in_specs=in_specs,
        out_specs=[pl.BlockSpec((nb, D_RWKV), lambda i: (0, 0))] * 6,
        out_shape=[rows] * 6,
        compiler_params=pltpu.CompilerParams(dimension_semantics=("arbitrary",),
                                             vmem_limit_bytes=VMEM_LIMIT),
    )(hx, hx, hx, hx, *prep_w)
    decay, k_mod, neg_kk, b, gate, bonus = prepped
    heads = lambda z: z.reshape(nb, RWKV_HEADS, 1, HEAD_DIM)
    vecs = [heads(z) for z in (hx[:, COL_R:COL_R + D_RWKV], decay, k_mod, hx[:, COL_V:COL_V + D_RWKV], neg_kk, b,
                               gate, bonus)]
    vec_spec = pl.BlockSpec((tb, RWKV_HEADS, 1, HEAD_DIM), lambda i: (i, 0, 0, 0))
    ln_spec = pl.BlockSpec((1, RWKV_HEADS, 1, HEAD_DIM), lambda i: (0, 0, 0, 0))
    state_spec = pl.BlockSpec((tb, RWKV_HEADS, HEAD_DIM, HEAD_DIM), lambda i: (i, 0, 0, 0))
    out, new_state = pl.pallas_call(
        functools.partial(_rwkv_tok_step_kernel, tb=tb),
        grid=(nb // tb,),
        in_specs=[vec_spec] * 8 + [ln_spec, ln_spec, state_spec],
        out_specs=[vec_spec, state_spec],
        out_shape=[jax.ShapeDtypeStruct((nb, RWKV_HEADS, 1, HEAD_DIM), F32),
                   jax.ShapeDtypeStruct(state.shape, F32)],
        compiler_params=pltpu.CompilerParams(dimension_semantics=("arbitrary",),
                                             vmem_limit_bytes=VMEM_LIMIT),
    )(*vecs, ln_w.reshape(1, RWKV_HEADS, 1, HEAD_DIM), ln_b.reshape(1, RWKV_HEADS, 1, HEAD_DIM), state)
    return out.reshape(nb, D_RWKV).astype(BF16), new_state


def _route(logits, bias):
    scores = _sigmoid(logits)
    sel = scores + bias
    rows = logits.shape[0]
    lane_i = lax.broadcasted_iota(jnp.int32, (rows, N_EXPERTS), 1)
    grp = lane_i // GROUP_SIZE
    lane = lane_i.astype(F32)
    neg_inf = -jnp.inf

    def take_max(x):
        m = jnp.max(x, axis=1, keepdims=True)
        idx = jnp.min(jnp.where(x == m, lane, float(N_EXPERTS)), axis=1, keepdims=True)
        return m, lane == idx

    gscore = []
    for g in range(N_GROUPS):
        x = jnp.where(grp == g, sel, neg_inf)
        m1, hit = take_max(x)
        m2 = jnp.max(jnp.where(hit, neg_inf, x), axis=1, keepdims=True)
        gscore.append(m1 + m2)
    emask = jnp.zeros((rows, N_EXPERTS), jnp.bool_)
    for g in range(N_GROUPS):
        ahead = jnp.zeros((rows, 1), jnp.int32)
        for g2 in range(N_GROUPS):
            if g2 == g:
                continue
            beats = (gscore[g2] >= gscore[g]) if g2 < g else (gscore[g2] > gscore[g])
            ahead = ahead + beats.astype(jnp.int32)
        emask = emask | ((grp == g) & (ahead < TOPK_GROUPS))
    x = jnp.where(emask, sel, neg_inf)
    chosen = jnp.zeros((rows, N_EXPERTS), jnp.bool_)
    slot_lane = lax.broadcasted_iota(jnp.int32, (rows, LANES), 1)
    top_idx = jnp.zeros((rows, LANES), F32)
    top_w = jnp.zeros((rows, LANES), F32)
    for kth in range(TOP_K):
        m = jnp.max(x, axis=1, keepdims=True)
        idx = jnp.min(jnp.where(x == m, lane, float(N_EXPERTS)), axis=1, keepdims=True)
        hit = lane == idx
        chosen = chosen | hit
        x = jnp.where(hit, neg_inf, x)
        top_idx = jnp.where(slot_lane == kth, idx, top_idx)
        top_w = jnp.where(slot_lane == kth, jnp.sum(jnp.where(hit, scores, 0.0), axis=1, keepdims=True), top_w)
    top_w = top_w / jnp.sum(top_w, axis=1, keepdims=True) * ROUTE_SCALE
    return top_idx.astype(jnp.int32), top_w, chosen.astype(F32)


def _out_proj_kernel(pool_ref, rwkv_ref, x_ref, g1_ref, sc_ref, sh_ref, wp_ref, wr_ref, npost_ref, npre_ref,
                     wrt_ref, rb_ref, sg_ref, su_ref, sd_ref, x1_ref, u2_ref, idx_ref, topw_ref, chosen_ref,
                     shared_ref):
    m = _dot(pool_ref[...], wp_ref[...]) + _dot(rwkv_ref[...], wr_ref[...])
    x1 = x_ref[...] + g1_ref[0] * _rms(m, npost_ref[...])
    x1_ref[...] = x1
    u2 = _rms(x1, npre_ref[...]) * (1.0 + sc_ref[0]) + sh_ref[0]
    u2_ref[...] = u2
    top_idx, top_w, chosen = _route(_dot_x3(u2, wrt_ref[...]), rb_ref[...])
    idx_ref[...] = top_idx
    topw_ref[...] = top_w
    chosen_ref[...] = chosen
    u2b = u2.astype(BF16)
    h = _silu(_dot(u2b, sg_ref[...])) * _dot(u2b, su_ref[...])
    shared_ref[...] = _dot(h.astype(BF16), sd_ref[...])


def _out_proj(pool_out, rwkv_out, x, g1, sc2, sh2, w_out_bf16, n_post, n_pre, w_router, router_bias, sg, su, sd, *,
              tm, tiles_per_seq):
    m, d = x.shape
    if tiles_per_seq is None:
        mod_spec = pl.BlockSpec((1, tm, d), lambda i: (i, 0, 0))
    else:
        mod_spec = pl.BlockSpec((1, 1, d), lambda i: (i // tiles_per_seq, 0, 0))
    row = lambda width: pl.BlockSpec((tm, width), lambda i: (i, 0))
    return pl.pallas_call(
        _out_proj_kernel,
        grid=(m // tm,),
        in_specs=[row(D_POOL), row(D_RWKV), row(d), mod_spec, mod_spec, mod_spec,
                  _const_spec((D_POOL, d), 1), _const_spec((D_RWKV, d), 1),
                  _const_spec((1, d), 1), _const_spec((1, d), 1),
                  _const_spec((d, N_EXPERTS), 1), _const_spec((1, N_EXPERTS), 1),
                  _const_spec(sg.shape, 1), _const_spec(su.shape, 1), _const_spec(sd.shape, 1)],
        out_specs=[row(d), row(d), row(LANES), row(LANES), row(N_EXPERTS), row(d)],
        out_shape=[jax.ShapeDtypeStruct((m, d), F32), jax.ShapeDtypeStruct((m, d), F32),
                   jax.ShapeDtypeStruct((m, LANES), jnp.int32), jax.ShapeDtypeStruct((m, LANES), F32),
                   jax.ShapeDtypeStruct((m, N_EXPERTS), F32), jax.ShapeDtypeStruct((m, d), F32)],
        compiler_params=pltpu.CompilerParams(dimension_semantics=("arbitrary",),
                                             vmem_limit_bytes=VMEM_LIMIT),
    )(pool_out, rwkv_out, x, g1, sc2, sh2, w_out_bf16[:D_POOL], w_out_bf16[D_POOL:],
      n_post.reshape(1, d), n_pre.reshape(1, d), w_router, router_bias.reshape(1, N_EXPERTS), sg, su, sd)


MOE_TM = 384


ROW_GROUP = 32


def _experts_kernel(te_ref, fresh_ref, wslot_ref, nexte_ref, groups_ref, nused_ref, idx_ref, idx_next_ref, ws_ref,
                    x_hbm, wg_hbm, wu_hbm, wd_hbm, ys_ref, wg_buf, wu_buf, wd_buf, g_bf, u_bf, d_bf, x_buf,
                    w_sems, x_sems):
    t = pl.program_id(0)
    n_used = nused_ref[0]

    def weight_copies(e, slot):
        return (pltpu.make_async_copy(wg_hbm.at[e], wg_buf.at[slot], w_sems.at[slot]),
                pltpu.make_async_copy(wu_hbm.at[e], wu_buf.at[slot], w_sems.at[slot]),
                pltpu.make_async_copy(wd_hbm.at[e], wd_buf.at[slot], w_sems.at[slot]))

    def start_rows(table_ref, slot, n_groups):
        for g in range(MOE_TM // ROW_GROUP):
            @pl.when(g < n_groups)
            def _():
                for j in range(g * ROW_GROUP, (g + 1) * ROW_GROUP):
                    pltpu.make_async_copy(x_hbm.at[pl.ds(table_ref[0, 0, j], 1)], x_buf.at[slot, pl.ds(j, 1)],
                                          x_sems.at[slot]).start()

    @pl.when(t == 0)
    def _():
        for cp in weight_copies(te_ref[0], 0):
            cp.start(priority=1)
        x_buf[...] = jnp.zeros(x_buf.shape, F32)
        start_rows(idx_ref, 0, groups_ref[0])

    @pl.when(t + 1 < n_used)
    def _():
        start_rows(idx_next_ref, (t + 1) % 2, groups_ref[t + 1])

    @pl.when(jnp.logical_and(t < n_used, fresh_ref[t] == 1))
    def _():
        slot = wslot_ref[t]
        for cp in weight_copies(te_ref[t], slot):
            cp.wait()

        @pl.when(nexte_ref[t] >= 0)
        def _():
            for cp in weight_copies(nexte_ref[t], 1 - slot):
                cp.start(priority=1)

        g_bf[...] = wg_buf[slot].astype(BF16)
        u_bf[...] = wu_buf[slot].astype(BF16)
        d_bf[...] = wd_buf[slot].astype(BF16)

    @pl.when(t < n_used)
    def _():
        cur = t % 2
        n_rows = pl.multiple_of(groups_ref[t] * ROW_GROUP, ROW_GROUP)
        pltpu.make_async_copy(x_hbm.at[pl.ds(0, n_rows)], x_buf.at[cur, pl.ds(0, n_rows)], x_sems.at[cur]).wait()
        x = x_buf[cur].astype(BF16)
        h = _silu(_dot(x, g_bf[...])) * _dot(x, u_bf[...])
        diag = (lax.broadcasted_iota(jnp.int32, (MOE_TM, MOE_TM), 0)
                == lax.broadcasted_iota(jnp.int32, (MOE_TM, MOE_TM), 1))
        w_col = jnp.sum(jnp.where(diag, ws_ref[0], 0.0), axis=1, keepdims=True)
        ys_ref[...] = _pack_bf16_halves(_dot(h.astype(BF16), d_bf[...]) * w_col)

    @pl.when(t >= n_used)
    def _():
        ys_ref[...] = jnp.zeros(ys_ref.shape, jnp.uint32)


def _experts(tables, x_all, exp_gate, exp_up, exp_down):
    tile_expert, fresh, wslot, next_e, row_groups, n_used, slot_tok, slot_w = tables
    d = x_all.shape[1]
    n_tiles = tile_expert.shape[0]
    n_slots = n_tiles * MOE_TM
    idx_block = (1, 1, MOE_TM)
    any_spec = pl.BlockSpec(memory_space=pl.ANY)
    live = lambda t, nused: jnp.where(t < nused[0], t, n_tiles - 1)
    grid_spec = pltpu.PrefetchScalarGridSpec(
        num_scalar_prefetch=6,
        grid=(n_tiles,),
        in_specs=[pl.BlockSpec(idx_block, lambda t, *s: (live(t, s[5]), 0, 0), memory_space=pltpu.SMEM),
                  pl.BlockSpec(idx_block, lambda t, *s: (live(t + 1, s[5]), 0, 0), memory_space=pltpu.SMEM),
                  pl.BlockSpec(idx_block, lambda t, *s: (live(t, s[5]), 0, 0)),
                  any_spec, any_spec, any_spec, any_spec],
        out_specs=pl.BlockSpec((MOE_TM, d // 2), lambda t, *s: (live(t, s[5]), 0)),
        scratch_shapes=[pltpu.VMEM((2, d, D_EXPERT), F32), pltpu.VMEM((2, d, D_EXPERT), F32),
                        pltpu.VMEM((2, D_EXPERT, d), F32),
                        pltpu.VMEM((d, D_EXPERT), BF16), pltpu.VMEM((d, D_EXPERT), BF16),
                        pltpu.VMEM((D_EXPERT, d), BF16),
                        pltpu.VMEM((2, MOE_TM, d), F32),
                        pltpu.SemaphoreType.DMA((2,)), pltpu.SemaphoreType.DMA((2,))],
    )
    idx_tiles = slot_tok.reshape(n_tiles, 1, MOE_TM)
    return pl.pallas_call(
        _experts_kernel,
        grid_spec=grid_spec,
        out_shape=jax.ShapeDtypeStruct((n_slots, d // 2), jnp.uint32),
        compiler_params=pltpu.CompilerParams(dimension_semantics=("arbitrary",),
                                             vmem_limit_bytes=VMEM_LIMIT),
    )(tile_expert, fresh, wslot, next_e, row_groups, n_used, idx_tiles, idx_tiles,
      slot_w.reshape(n_tiles, 1, MOE_TM), x_all, exp_gate, exp_up, exp_down)


ASSIGN_BITS = 17


def _dispatch(chosen, top_idx, top_w):
    n = chosen.shape[0]
    n_assign = n * TOP_K
    low_mask = (1 << ASSIGN_BITS) - 1
    assert n_assign + MOE_TM <= low_mask and N_EXPERTS << ASSIGN_BITS < 2 ** 31
    n_tiles = -(-n_assign // MOE_TM) + N_EXPERTS
    n_slots = n_tiles * MOE_TM
    sel = chosen.astype(jnp.int32)
    counts = jnp.sum(sel, axis=0)
    rank = jnp.cumsum(sel, axis=0) - sel
    tiles_e = (counts + MOE_TM - 1) // MOE_TM
    tile_end = jnp.cumsum(tiles_e)
    slot = (tile_end - tiles_e)[None, :] * MOE_TM + rank
    expert_ids = jnp.arange(N_EXPERTS, dtype=jnp.int32)
    slot_of_pair = jnp.stack([jnp.sum(jnp.where(top_idx[:, k:k + 1] == expert_ids[None, :], slot, 0), axis=1)
                              for k in range(TOP_K)])
    shift = 1 << ASSIGN_BITS
    int_max = jnp.iinfo(jnp.int32).max
    real_key = top_idx.reshape(-1) * shift + jnp.arange(n_assign, dtype=jnp.int32)
    pad_j = jnp.arange(MOE_TM - 1, dtype=jnp.int32)[None, :]
    pad_key = jnp.where(pad_j < (tiles_e * MOE_TM - counts)[:, None],
                        expert_ids[:, None] * shift + n_assign + pad_j, int_max)
    n_rest = n_slots - n_assign
    keys = jnp.concatenate([real_key, pad_key.reshape(-1),
                            jnp.full((n_rest - pad_key.size,), int_max, jnp.int32)])
    weights = jnp.concatenate([top_w.reshape(-1), jnp.zeros((n_rest,), F32)])
    sorted_key, sorted_w = lax.sort((keys, weights), num_keys=1)
    pair = sorted_key & low_mask
    is_real = pair < n_assign
    slot_tok = jnp.where(is_real, pair // TOP_K, 0)
    slot_w = jnp.where(is_real, sorted_w, 0.0)
    n_used = tile_end[-1]
    tile_ids = jnp.minimum(jnp.arange(n_tiles, dtype=jnp.int32), n_used - 1)
    tile_expert = jnp.minimum(jnp.searchsorted(tile_end, tile_ids, side="right"), N_EXPERTS - 1).astype(jnp.int32)
    in_use = jnp.arange(n_tiles, dtype=jnp.int32) < n_used
    prev_expert = jnp.concatenate([jnp.full((1,), -1, jnp.int32), tile_expert[:-1]])
    fresh = jnp.logical_and(in_use, tile_expert != prev_expert).astype(jnp.int32)
    wslot = (jnp.cumsum(fresh) - 1) % 2
    later = jnp.where(counts > 0, expert_ids, N_EXPERTS)
    next_used = jnp.concatenate([lax.cummin(later, reverse=True)[1:], jnp.full((1,), N_EXPERTS, jnp.int32)])
    next_used = jnp.where(next_used >= N_EXPERTS, -1, next_used)
    tile_in_expert = jnp.arange(n_tiles, dtype=jnp.int32) - (tile_end - tiles_e)[tile_expert]
    rows_in_tile = jnp.clip(counts[tile_expert] - tile_in_expert * MOE_TM, 0, MOE_TM)
    row_groups = jnp.where(in_use, (rows_in_tile + ROW_GROUP - 1) // ROW_GROUP, 0)
    tables = (tile_expert, fresh, wslot.astype(jnp.int32), next_used[tile_expert].astype(jnp.int32),
              row_groups.astype(jnp.int32), n_used.reshape(1).astype(jnp.int32), slot_tok.astype(jnp.int32), slot_w)
    return slot_of_pair, tables


def _final_kernel(idx_ref, idx_next_ref, shared_ref, ys_hbm, x1_ref, g2_ref, npost_ref, o_ref, rows_buf, sems, *,
                  tm):
    i = pl.program_id(0)
    n_rows = TOP_K * tm

    def start_gather(table_ref, buf_slot):
        for j in range(n_rows):
            pltpu.make_async_copy(ys_hbm.at[pl.ds(table_ref[0, 0, j], 1)], rows_buf.at[buf_slot, pl.ds(j, 1)],
                                  sems.at[buf_slot]).start()

    @pl.when(i == 0)
    def _():
        start_gather(idx_ref, 0)

    @pl.when(i + 1 < pl.num_programs(0))
    def _():
        start_gather(idx_next_ref, (i + 1) % 2)

    cur = i % 2
    pltpu.make_async_copy(ys_hbm.at[pl.ds(0, n_rows)], rows_buf.at[cur], sems.at[cur]).wait()
    left, right = _unpack_bf16_halves(rows_buf[cur, 0:tm])
    for kth in range(1, TOP_K):
        l_k, r_k = _unpack_bf16_halves(rows_buf[cur, kth * tm:(kth + 1) * tm])
        left, right = left + l_k, right + r_k
    routed = jnp.concatenate([left, right], axis=1)
    o_ref[...] = x1_ref[...] + g2_ref[0] * _rms(routed + shared_ref[...], npost_ref[...])


def _final(shared, ys, slot_tiles, x1, g2, n_post, *, tm, tiles_per_seq, tile_offset):
    m, d = x1.shape
    nt = m // tm
    if tiles_per_seq is None:
        mod_spec = pl.BlockSpec((1, tm, d), lambda i: (i, 0, 0))
    else:
        mod_spec = pl.BlockSpec((1, 1, d), lambda i: (i // tiles_per_seq, 0, 0))
    row = pl.BlockSpec((tm, d), lambda i: (i, 0))
    idx_block = (1, 1, TOP_K * tm)
    return pl.pallas_call(
        functools.partial(_final_kernel, tm=tm),
        grid=(nt,),
        in_specs=[pl.BlockSpec(idx_block, lambda i: (i + tile_offset, 0, 0), memory_space=pltpu.SMEM),
                  pl.BlockSpec(idx_block, lambda i: (jnp.minimum(i + 1, nt - 1) + tile_offset, 0, 0),
                               memory_space=pltpu.SMEM),
                  row, pl.BlockSpec(memory_space=pl.ANY), row, mod_spec, _const_spec((1, d), 1)],
        out_specs=row,
        out_shape=jax.ShapeDtypeStruct((m, d), F32),
        scratch_shapes=[pltpu.VMEM((2, TOP_K * tm, d // 2), jnp.uint32), pltpu.SemaphoreType.DMA((2,))],
        compiler_params=pltpu.CompilerParams(dimension_semantics=("arbitrary",),
                                             vmem_limit_bytes=VMEM_LIMIT),
    )(slot_tiles, slot_tiles, shared, ys, x1, g2, n_post.reshape(1, d))


def _pad_cols_in(w_in, mu_shift):
    d = w_in.shape[0]
    hp = w_in[:, :D_POOL]
    hr = w_in[:, D_POOL:]
    cuts = [3 * D_RWKV, 3 * D_RWKV + DECAY_LORA, 3 * D_RWKV + DECAY_LORA + AAA_LORA]
    rkv, wl, al, gl = jnp.split(hr, cuts, axis=1)
    zpad = lambda z: jnp.pad(z, ((0, 0), (0, LORA_PAD - z.shape[1])))
    w = jnp.concatenate([rkv, hp, zpad(wl), zpad(al), gl], axis=1).astype(BF16)
    mu_rkv, mu_wl, mu_al, mu_gl = jnp.split(mu_shift, cuts)
    zp1 = lambda z: jnp.pad(z, (0, LORA_PAD - z.shape[0]))
    mu = jnp.concatenate([mu_rkv, jnp.zeros((D_POOL,), F32), zp1(mu_wl), zp1(mu_al), mu_gl])
    assert w.shape == (d, N_IN_PAD) and mu.shape == (N_IN_PAD,)
    return w, mu


def kernel(x_prompt, x_sample, c_prompt, c_sample, state_shift, state_pool, state_wkv, w_ada, b_ada, norm_pre_mix, norm_post_mix, norm_pre_ffn, norm_post_ffn, w_in, mu_shift, pool_w, pool_scale, w0, w_decay_up, a0, w_aaa_up, w_gate_up, k_k, k_a, r_k, ln_x_w, ln_x_b, w_out, w_router, router_bias, exp_gate, exp_up, exp_down, sh_gate, sh_up, sh_down):
    bp, seq, d = x_prompt.shape
    bs = x_sample.shape[0]
    assert w_ada.shape[0] == 1 and x_sample.shape[1] == 1
    np_rows = bp * seq

    c_all = jnp.concatenate([c_prompt, c_sample], axis=0)
    pad_rows = (-c_all.shape[0]) % 16
    ada = _ada(jnp.pad(c_all, ((0, pad_rows), (0, 0))), w_ada[0], b_ada[0])
    sh1, sc1, g1, sh2, sc2, g2 = jnp.split(ada, 6, axis=-1)
    pr = lambda z: z[:bp].reshape(bp, 1, d)
    TS = min(128, bs)
    sr = lambda z: z[bp:bp + bs].reshape(bs // TS, TS, d)

    w_in_p, mu_p = _pad_cols_in(w_in[0], mu_shift[0])
    xp = x_prompt.reshape(np_rows, d)
    xs = x_sample.reshape(bs, d)

    TM_IN = min(1024, seq)
    hx_p, u_tail = _in_proj(xp, norm_pre_mix[0], pr(sc1), pr(sh1), w_in_p, mu_p, tm=TM_IN,
                            tiles_per_seq=seq // TM_IN)
    hx_s, u_s = _in_proj(xs, norm_pre_mix[0], sr(sc1), sr(sh1), w_in_p, mu_p, tm=TS, tiles_per_seq=1,
                         prev=state_shift[0])
    new_shift_prompt = u_tail.reshape(bp, seq // TM_IN, 8, d)[:, -1, -1][None]
    new_shift_sample = u_s[None]

    pool_w_b = pool_w[0].astype(BF16)
    TC_POOL = min(512, seq)
    pool_p = _pool(hx_p, COL_POOL // D_POOL, pool_w_b, pool_scale[0], tc=TC_POOL, tiles_per_seq=seq // TC_POOL,
                   full_count=False)
    hp_p = hx_p[:, COL_POOL:COL_POOL + D_POOL].reshape(bp, seq, D_POOL)
    new_pool_prompt = hp_p[:, seq - POOL_BUF:][None]
    hp_s = hx_s[:, COL_POOL:COL_POOL + D_POOL]
    ext_s = jnp.concatenate([state_pool[0], hp_s[:, None, :]], axis=1)
    assert PAST_LEN + 1 >= max(POOL_WINDOWS)
    pool_s = _pool(ext_s.reshape(bs * (POOL_BUF + 1), D_POOL), 0, pool_w_b, pool_scale[0],
                   tc=bs * (POOL_BUF + 1), tiles_per_seq=1, full_count=True)
    pool_s = pool_s.reshape(bs, POOL_BUF + 1, D_POOL)[:, -1]
    new_pool_sample = ext_s[:, 1:][None]

    rw = _rwkv_weights(w0[0], w_decay_up[0], a0[0], w_aaa_up[0], w_gate_up[0], k_k[0], k_a[0], r_k[0],
                       ln_x_w[0], ln_x_b[0])
    rwkv_p, wkv_p = _rwkv_seq(hx_p, rw, batch=bp, seq=seq)
    rwkv_s, wkv_s = _rwkv_tok(hx_s, state_wkv[0], rw, tb=8)

    w_out_b = w_out[0].astype(BF16)
    assert np_rows % TS == 0 and seq % TS == 0
    sg, su, sd = sh_gate[0].astype(BF16), sh_up[0].astype(BF16), sh_down[0].astype(BF16)
    TM_OUT = min(256, seq)
    outs_p = _out_proj(pool_p, rwkv_p, xp, pr(g1), pr(sc2), pr(sh2), w_out_b, norm_post_mix[0], norm_pre_ffn[0],
                       w_router[0], router_bias[0], sg, su, sd, tm=TM_OUT, tiles_per_seq=seq // TM_OUT)
    outs_s = _out_proj(pool_s, rwkv_s, xs, sr(g1), sr(sc2), sr(sh2), w_out_b, norm_post_mix[0], norm_pre_ffn[0],
                       w_router[0], router_bias[0], sg, su, sd, tm=TS, tiles_per_seq=None)
    (x1_p, u2_p, idx_p, topw_p, chosen_p, shared_p), (x1_s, u2_s, idx_s, topw_s, chosen_s, shared_s) = outs_p, outs_s
    u2_all = jnp.concatenate([u2_p, u2_s], axis=0)
    idx_all = jnp.concatenate([idx_p[:, :TOP_K], idx_s[:, :TOP_K]], axis=0)
    topw_all = jnp.concatenate([topw_p[:, :TOP_K], topw_s[:, :TOP_K]], axis=0)
    chosen_all = jnp.concatenate([chosen_p, chosen_s], axis=0)

    n_all = np_rows + bs
    slot_of_pair, tables = _dispatch(chosen_all, idx_all, topw_all)
    ys = _experts(tables, u2_all, exp_gate[0], exp_up[0], exp_down[0])

    slot_tiles = slot_of_pair.reshape(TOP_K, n_all // TS, TS).transpose(1, 0, 2).reshape(n_all // TS, 1, TOP_K * TS)
    y_p = _final(shared_p, ys, slot_tiles, x1_p, pr(g2), norm_post_ffn[0], tm=TS, tiles_per_seq=seq // TS,
                 tile_offset=0)
    y_s = _final(shared_s, ys, slot_tiles, x1_s, sr(g2), norm_post_ffn[0], tm=TS, tiles_per_seq=None,
                 tile_offset=np_rows // TS)

    return (y_p.reshape(bp, seq, d), y_s.reshape(bs, 1, d), new_shift_prompt, new_pool_prompt, wkv_p[None],
            new_shift_sample, new_pool_sample, wkv_s[None])
```

```python
import functools

import jax
import jax.numpy as jnp
from jax import lax
from jax.experimental import pallas as pl
from jax.experimental.pallas import tpu as pltpu

F32 = jnp.float32
BF16 = jnp.bfloat16

D_MODEL = 2048
D_POOL = 512
POOL_WINDOWS = (2, 4, 8, 16)
POOL_CH = 128
POOL_BUF = 15
D_RWKV = 1536
HEAD_DIM = 64
RWKV_HEADS = 24
DECAY_LORA = 96
AAA_LORA = 96
GATE_LORA = 256
LORA_PAD = 128
GN_EPS = 64e-5
N_EXPERTS = 256
TOP_K = 8
N_GROUPS = 8
GROUP_SIZE = N_EXPERTS // N_GROUPS
TOPK_GROUPS = 4
D_EXPERT = 512
ROUTE_SCALE = 2.5
EPS = 1e-6
PAST_LEN = 16384

COL_R, COL_K, COL_V = 0, D_RWKV, 2 * D_RWKV
COL_POOL = 3 * D_RWKV
COL_LORA = COL_POOL + D_POOL
N_IN_PAD = COL_LORA + 2 * LORA_PAD + GATE_LORA
LANES = 128
VMEM_LIMIT = 56 * 1024 * 1024


def _dot(a, b):
    return jnp.dot(a, b, preferred_element_type=F32)


def _split2(x):
    hi = x.astype(BF16)
    lo = (x - hi.astype(F32)).astype(BF16)
    return hi, lo


def _split3(x):
    hi = x.astype(BF16)
    r = x - hi.astype(F32)
    mid = r.astype(BF16)
    lo = (r - mid.astype(F32)).astype(BF16)
    return hi, mid, lo


def _dot_x3(a, b):
    ah, al = _split2(a)
    bh, bl = _split2(b)
    return _dot(ah, bh) + (_dot(ah, bl) + _dot(al, bh))


def _dot_exact_rhs(a, b_bf16):
    h, m, l = _split3(a)
    return _dot(h, b_bf16) + (_dot(m, b_bf16) + _dot(l, b_bf16))


def _sigmoid(x):
    return 1.0 / (1.0 + jnp.exp(-x))


def _silu(x):
    return x * _sigmoid(x)


def _pack_bf16_halves(y):
    half = y.shape[1] // 2
    bits = pltpu.bitcast(y.astype(BF16).astype(F32), jnp.uint32)
    return bits[:, :half] | (bits[:, half:] >> 16)


def _unpack_bf16_halves(p):
    return (pltpu.bitcast(p & jnp.uint32(0xFFFF0000), F32), pltpu.bitcast(p << 16, F32))


def _rms(x, g):
    return x * lax.rsqrt(jnp.mean(x * x, axis=-1, keepdims=True) + EPS) * g


def _ada_kernel(c_ref, w_ref, b_ref, o_ref):
    o_ref[...] = _dot_x3(_silu(c_ref[...]), w_ref[...]) + b_ref[...]


def _ada(c_all, w_ada, b_ada):
    m, d = c_all.shape
    n = w_ada.shape[1]
    tn = 512
    return pl.pallas_call(
        _ada_kernel,
        grid=(n // tn,),
        in_specs=[pl.BlockSpec((m, d), lambda j: (0, 0)),
                  pl.BlockSpec((d, tn), lambda j: (0, j)),
                  pl.BlockSpec((1, tn), lambda j: (0, j))],
        out_specs=pl.BlockSpec((m, tn), lambda j: (0, j)),
        out_shape=jax.ShapeDtypeStruct((m, n), F32),
        compiler_params=pltpu.CompilerParams(dimension_semantics=("arbitrary",),
                                             vmem_limit_bytes=VMEM_LIMIT),
    )(c_all, w_ada, b_ada.reshape(1, n))


def _in_proj_kernel(*refs, tiles_per_seq, explicit_prev, tm):
    if explicit_prev:
        x_ref, g_ref, sc_ref, sh_ref, prev_ref, w_ref, mu_ref, hx_ref, u_ref, ub_scr = refs
    else:
        x_ref, g_ref, sc_ref, sh_ref, w_ref, mu_ref, hx_ref, u_ref, ub_scr, carry_scr = refs
    i = pl.program_id(0)
    j = pl.program_id(1)

    @pl.when(j == 0)
    def _():
        u = _rms(x_ref[...], g_ref[...])
        u = u * (1.0 + sc_ref[0]) + sh_ref[0]
        ub_scr[...] = u.astype(BF16)
        if explicit_prev:
            u_ref[...] = u
        else:
            u_ref[...] = u[tm - 8:, :]

    w = w_ref[...]
    h = _dot(ub_scr[...], w)
    if explicit_prev:
        hp = _dot(prev_ref[...].astype(BF16), w)
    else:
        @pl.when(i == 0)
        def _():
            carry_scr[j] = jnp.zeros(carry_scr.shape[1:], F32)

        first = jnp.where(i % tiles_per_seq == 0, 0.0, carry_scr[j, 0:1, :])
        row0 = lax.broadcasted_iota(jnp.int32, h.shape, 0) == 0
        hp = jnp.where(row0, first, pltpu.roll(h, 1, axis=0))
        carry_scr[j, 0:1, :] = h[tm - 1:tm, :]
    hx_ref[...] = h + (hp - h) * mu_ref[...]


def _in_proj(x, gamma, sc, sh, w_bf16, mu_pad, *, tm, tiles_per_seq, prev=None):
    m, d = x.shape
    n = w_bf16.shape[1]
    tn = 512
    nt = n // tn
    explicit_prev = prev is not None
    if explicit_prev:
        mod_spec = pl.BlockSpec((1, tm, d), lambda i, j: (i, 0, 0))
    else:
        mod_spec = pl.BlockSpec((1, 1, d), lambda i, j: (i // tiles_per_seq, 0, 0))
    in_specs = [pl.BlockSpec((tm, d), lambda i, j: (i, 0)),
                pl.BlockSpec((1, d), lambda i, j: (0, 0)),
                mod_spec, mod_spec]
    args = [x, gamma.reshape(1, d), sc, sh]
    if explicit_prev:
        in_specs.append(pl.BlockSpec((tm, d), lambda i, j: (i, 0)))
        args.append(prev)
    in_specs += [pl.BlockSpec((d, tn), lambda i, j: (0, j)),
                 pl.BlockSpec((1, tn), lambda i, j: (0, j))]
    args += [w_bf16, mu_pad.reshape(1, n)]
    scratch = [pltpu.VMEM((tm, d), BF16)]
    if explicit_prev:
        u_shape, u_spec = (m, d), pl.BlockSpec((tm, d), lambda i, j: (i, 0))
    else:
        u_shape, u_spec = (m // tm * 8, d), pl.BlockSpec((8, d), lambda i, j: (i, 0))
        scratch.append(pltpu.VMEM((nt, 8, tn), F32))
    return pl.pallas_call(
        functools.partial(_in_proj_kernel, tiles_per_seq=tiles_per_seq, explicit_prev=explicit_prev, tm=tm),
        grid=(m // tm, nt),
        in_specs=in_specs,
        out_specs=[pl.BlockSpec((tm, tn), lambda i, j: (i, j)), u_spec],
        out_shape=[jax.ShapeDtypeStruct((m, n), F32), jax.ShapeDtypeStruct(u_shape, F32)],
        scratch_shapes=scratch,
        compiler_params=pltpu.CompilerParams(dimension_semantics=("arbitrary", "arbitrary"),
                                             vmem_limit_bytes=VMEM_LIMIT),
    )(*args)


POOL_HALO = 16


def _pool_kernel(p_ref, pw_ref, ps_ref, o_ref, ext_scr, *, tiles_per_seq, full_count, tc):
    i = pl.program_id(0)

    @pl.when(i % tiles_per_seq == 0)
    def _():
        ext_scr[0:POOL_HALO, :] = jnp.zeros((POOL_HALO, D_POOL), F32)

    p = p_ref[...]
    ext_scr[POOL_HALO:POOL_HALO + tc, :] = p
    pos = (i % tiles_per_seq) * tc + lax.broadcasted_iota(jnp.int32, (tc, 1), 0)
    outs = []
    for gi, win in enumerate(POOL_WINDOWS):
        lo = gi * POOL_CH
        pg = p[:, lo:lo + POOL_CH]
        acc = pg
        for s in range(1, win):
            acc = acc + ext_scr[POOL_HALO - s:POOL_HALO - s + tc, lo:lo + POOL_CH]
        if full_count:
            dgi = acc / float(win) - pg
        else:
            cnt = jnp.minimum(pos + 1, win).astype(F32)
            dgi = acc / cnt - pg
        outs.append(_dot(dgi.astype(BF16), pw_ref[gi]))
    y = jnp.concatenate(outs, axis=-1) * ps_ref[...]
    o_ref[...] = y.astype(BF16)
    ext_scr[0:POOL_HALO, :] = ext_scr[tc:tc + POOL_HALO, :]


def _pool(src, col_block, pool_w_bf16, pool_scale, *, tc, tiles_per_seq, full_count):
    m = src.shape[0]
    return pl.pallas_call(
        functools.partial(_pool_kernel, tiles_per_seq=tiles_per_seq, full_count=full_count, tc=tc),
        grid=(m // tc,),
        in_specs=[pl.BlockSpec((tc, D_POOL), lambda i: (i, col_block)),
                  pl.BlockSpec((len(POOL_WINDOWS), POOL_CH, POOL_CH), lambda i: (0, 0, 0)),
                  pl.BlockSpec((1, D_POOL), lambda i: (0, 0))],
        out_specs=pl.BlockSpec((tc, D_POOL), lambda i: (i, 0)),
        out_shape=jax.ShapeDtypeStruct((m, D_POOL), BF16),
        scratch_shapes=[pltpu.VMEM((POOL_HALO + tc, D_POOL), F32)],
        compiler_params=pltpu.CompilerParams(dimension_semantics=("arbitrary",),
                                             vmem_limit_bytes=VMEM_LIMIT),
    )(src, pool_w_bf16, pool_scale.reshape(1, D_POOL))


def _softplus(z):
    return jnp.maximum(z, 0.0) + jnp.log1p(jnp.exp(-jnp.abs(z)))


def _rwkv_prep(r, k, v, lora, prm, seg, segt):
    (w0, wd, a0, wa, wg, k_k, k_a, r_k) = prm
    wl = lora[:, 0:LORA_PAD]
    al = lora[:, LORA_PAD:2 * LORA_PAD]
    gl = lora[:, 2 * LORA_PAD:]
    logw = -_softplus(-(w0 + _dot_x3(jnp.tanh(wl), wd))) - 0.5
    log_decay = -jnp.exp(logw)
    decay = jnp.exp(log_decay)
    a = _sigmoid(a0 + _dot_x3(al, wa))
    gate = _dot(_sigmoid(gl).astype(BF16), wg.astype(BF16))
    kk = k * k_k
    ss = _dot_exact_rhs(kk * kk, seg)
    inv = 1.0 / jnp.maximum(jnp.sqrt(ss), 1e-12)
    kk = kk * _dot_exact_rhs(inv, segt)
    k_mod = k * (1.0 + (a - 1.0) * k_a)
    bonus = _dot_exact_rhs(_dot_exact_rhs(r * k_mod * r_k, seg), segt) * v
    return decay, log_decay, k_mod, -kk, kk * a, gate, bonus


def _group_norm_head(y):
    mu = jnp.mean(y, axis=-1, keepdims=True)
    yc = y - mu
    var = jnp.mean(yc * yc, axis=-1, keepdims=True)
    return yc * lax.rsqrt(var + GN_EPS)


def _group_norm_rows(y, seg, segt):
    inv_n = 1.0 / HEAD_DIM
    mu = _dot_exact_rhs(_dot_exact_rhs(y, seg) * inv_n, segt)
    yc = y - mu
    var = _dot_exact_rhs(yc * yc, seg) * inv_n
    return yc * _dot_exact_rhs(lax.rsqrt(var + GN_EPS), segt)


WKV_CHUNK = HEAD_DIM
PAIR_LANES = 2 * HEAD_DIM
N_PAIRS = RWKV_HEADS // 2
_NN = (((1,), (0,)), ((), ()))
_NT = (((1,), (1,)), ((), ()))
_TN = (((0,), (0,)), ((), ()))


def _dg_x3(a, b, dims):
    ah, al = _split2(a)
    bh, bl = _split2(b)
    (ca,), (cb,) = dims[0]
    dg = lambda x, y: lax.dot_general(x, y, dims, preferred_element_type=F32)
    return dg(jnp.concatenate([ah, al], axis=ca), jnp.concatenate([bh, bh], axis=cb)) + dg(ah, bl)


def _wkv_masks():
    n = PAIR_LANES
    ri = lax.broadcasted_iota(jnp.int32, (n, n), 0)
    ci = lax.broadcasted_iota(jnp.int32, (n, n), 1)
    same_head = (ri // HEAD_DIM) == (ci // HEAD_DIM)
    same16 = (ri // 16) == (ci // 16)
    same32 = (ri // 32) == (ci // 32)
    return dict(same_head=same_head, strict=same_head & (ci < ri), incl=same_head & (ci <= ri), eye=ri == ci,
                same16=same16, only32=same32 & jnp.logical_not(same16), not32=jnp.logical_not(same32),
                head0=lax.broadcasted_iota(jnp.int32, (1, n), 1) < HEAD_DIM)


def _each(fn, *lists):
    return [fn(*xs) for xs in zip(*lists)]


def _unit_lower_inverse(labs, mk):
    ds = [jnp.where(mk["same16"], lab, 0.0) for lab in labs]
    xs = [jnp.where(mk["eye"], 1.0, 0.0) + d for d in ds]
    mm = lambda x, y: _dg_x3(x, y, _NN)
    for _ in range(3):
        ds = _each(mm, ds, ds)
        xs = _each(lambda x, d: x + mm(x, d), xs, ds)
    for level in ("only32", "not32"):
        es = [jnp.where(mk[level], lab, 0.0) for lab in labs]
        exs = _each(mm, es, xs)
        xs = _each(lambda x, ex: x + mm(x, ex), xs, exs)
    return xs


WKV_GROUP = 4


def _wkv_chunk_group(rows, s_bds, mk):
    c = WKV_CHUNK
    stack = lambda x: jnp.concatenate([jnp.where(mk["head0"], x, 0.0), jnp.where(mk["head0"], 0.0, x)], axis=0)
    dup = lambda x: jnp.concatenate([x, x], axis=0)
    col = lambda i: [row[i] for row in rows]
    n = PAIR_LANES
    a_s, r_s, v_s = [_each(stack, col(i)) for i in (0, 1, 6)]
    bh2, kh2 = [_each(dup, col(i)) for i in (4, 5)]
    bk2 = _each(lambda bt, kt: jnp.concatenate([dup(bt), dup(kt)], axis=0), col(2), col(3))
    nt = lambda x, y: _dg_x3(x, y, _NT)
    tn = lambda x, y: _dg_x3(x, y, _TN)
    below = lambda x: jnp.where(mk["strict"], x, 0.0)
    upto = lambda x: jnp.where(mk["incl"], x, 0.0)
    heads = lambda x: jnp.where(mk["same_head"], x, 0.0)
    a_bk = _each(nt, a_s, bk2)
    lab = [below(x[:, :n]) for x in a_bk]
    lak = [below(x[:, n:]) for x in a_bk]
    nt1 = lambda x, y: lax.dot_general(x.astype(BF16), y.astype(BF16), _NT, preferred_element_type=F32)
    mm1 = lambda x, y: _dot(x.astype(BF16), y.astype(BF16))
    r_bk = _each(nt1, r_s, bk2)
    lrb = [upto(x[:, :n]) for x in r_bk]
    lrk = [upto(x[:, n:]) for x in r_bk]
    t_inv = _unit_lower_inverse(lab, mk)
    mm = lambda x, y: _dg_x3(x, y, _NN)
    lakv = _each(mm, lak, v_s)
    ta_u = _each(lambda t, a, w: mm(t, jnp.concatenate([a, w], axis=1)), t_inv, a_s, lakv)
    ta_s = [x[:, :n] for x in ta_u]
    u_s = [x[:, n:] for x in ta_u]
    m_bd = _each(lambda m, w: heads(m) + jnp.where(mk["eye"], w, 0.0), _each(tn, ta_s, bh2), col(7))
    g_bd = _each(lambda g1, g2: heads(g1 + g2), _each(tn, u_s, bh2), _each(tn, v_s, kh2))
    y_rhs = _each(lambda tau, vs: jnp.concatenate([tau, jnp.concatenate([jnp.zeros_like(vs), vs], axis=1)],
                                                  axis=0), ta_u, v_s)
    y_parts = _each(lambda lb, lk, rhs: mm1(jnp.concatenate([lb, lk], axis=1), rhs), lrb, lrk, y_rhs)
    q_s = [x[:, :n] for x in y_parts]
    y0_s = [x[:, n:] for x in y_parts]
    q = _each(lambda rt, qs: rt + (qs[:c] + qs[c:]), col(1), q_s)
    y = _each(lambda yq, y0: yq + (y0[:c] + y0[c:]), _each(nt1, q, s_bds), y0_s)
    s_new = _each(lambda sm, g: sm + g, _each(mm, s_bds, m_bd), g_bd)
    return y, s_new


def _rwkv_seq_kernel(r_ref, k_ref, v_ref, lora_ref, w0_ref, wd_ref, a0_ref, wa_ref, wg_ref, kk_ref, ka_ref,
                     rk_ref, lnw_ref, lnb_ref, seg_ref, segt_ref, o_ref, sout_ref, s_scr, y_scr):
    c = pl.program_id(1)

    @pl.when(c == 0)
    def _():
        s_scr[...] = jnp.zeros(s_scr.shape, F32)

    r = r_ref[...]
    v = v_ref[...]
    seg = seg_ref[...]
    segt = segt_ref[...]
    prm = (w0_ref[...], wd_ref[...], a0_ref[...], wa_ref[...], wg_ref[...], kk_ref[...], ka_ref[...], rk_ref[...])
    decay, log_decay, k_mod, neg_kk, b, gate, bonus = _rwkv_prep(r, k_ref[...], v, lora_ref[...], prm, seg, segt)

    n = WKV_CHUNK
    tri = (lax.broadcasted_iota(jnp.int32, (n, n), 1) <= lax.broadcasted_iota(jnp.int32, (n, n), 0)).astype(BF16)
    ld_h, ld_m, ld_l = _split3(log_decay)
    cw = _dot(tri, ld_h) + (_dot(tri, ld_m) + _dot(tri, ld_l))
    cw_prev = cw - log_decay
    cw_last = cw[n - 1:n, :]
    e_prev = jnp.exp(cw_prev)
    e_inv = jnp.exp(-cw)
    e_rem = jnp.exp(cw_last - cw)
    at = neg_kk * e_prev
    rt = r * (e_prev * decay)
    bt = b * e_inv
    kt = k_mod * e_inv
    bh = b * e_rem
    kh = k_mod * e_rem
    w_last = jnp.exp(cw_last)

    mk = _wkv_masks()
    for p0 in range(0, N_PAIRS, WKV_GROUP):
        pairs = range(p0, p0 + WKV_GROUP)
        lanes = [slice(p * PAIR_LANES, (p + 1) * PAIR_LANES) for p in pairs]
        rows = [tuple(z[:, sl] for z in (at, rt, bt, kt, bh, kh, v, w_last)) for sl in lanes]
        ys, s_new = _wkv_chunk_group(rows, [s_scr[p] for p in pairs], mk)
        for p, sl, y, s in zip(pairs, lanes, ys, s_new):
            s_scr[p] = s
            y_scr[:, sl] = y

    out = (_group_norm_rows(y_scr[...], seg, segt) * lnw_ref[...] + lnb_ref[...] + bonus) * gate
    o_ref[...] = out.astype(BF16)

    @pl.when(c == pl.num_programs(1) - 1)
    def _():
        for p in range(N_PAIRS):
            sout_ref[0, 2 * p] = s_scr[p, 0:HEAD_DIM, 0:HEAD_DIM]
            sout_ref[0, 2 * p + 1] = s_scr[p, HEAD_DIM:, HEAD_DIM:]


def _rwkv_weights(w0, w_decay_up, a0, w_aaa_up, w_gate_up, k_k, k_a, r_k, ln_x_w, ln_x_b):
    row = lambda z: z.reshape(1, D_RWKV)
    pad = lambda z: jnp.pad(z, ((0, LORA_PAD - z.shape[0]), (0, 0)))
    head_of_col = jnp.arange(D_RWKV, dtype=jnp.int32) // HEAD_DIM
    seg = (head_of_col[:, None] == jnp.arange(LANES, dtype=jnp.int32)[None, :]).astype(BF16)
    return (row(w0), pad(w_decay_up), row(a0), pad(w_aaa_up), w_gate_up, row(k_k), row(k_a), row(r_k),
            row(ln_x_w), row(ln_x_b), seg, seg.T)


def _const_spec(shape, grid_rank):
    zeros = (0,) * len(shape)
    if grid_rank == 1:
        return pl.BlockSpec(shape, lambda i: zeros)
    return pl.BlockSpec(shape, lambda i, j: zeros)


def _rwkv_seq(hx, weights, *, batch, seq):
    tc = WKV_CHUNK
    nchunk = seq // tc
    rowmap = lambda cb: (lambda b, c: (b * nchunk + c, cb))
    in_specs = [pl.BlockSpec((tc, D_RWKV), rowmap(0)),
                pl.BlockSpec((tc, D_RWKV), rowmap(1)),
                pl.BlockSpec((tc, D_RWKV), rowmap(2)),
                pl.BlockSpec((tc, 2 * LORA_PAD + GATE_LORA), rowmap(COL_LORA // (2 * LORA_PAD + GATE_LORA)))]
    in_specs += [_const_spec(w.shape, 2) for w in weights]
    return pl.pallas_call(
        _rwkv_seq_kernel,
        grid=(batch, nchunk),
        in_specs=in_specs,
        out_specs=[pl.BlockSpec((tc, D_RWKV), lambda b, c: (b * nchunk + c, 0)),
                   pl.BlockSpec((1, RWKV_HEADS, HEAD_DIM, HEAD_DIM), lambda b, c: (b, 0, 0, 0))],
        out_shape=[jax.ShapeDtypeStruct((batch * seq, D_RWKV), BF16),
                   jax.ShapeDtypeStruct((batch, RWKV_HEADS, HEAD_DIM, HEAD_DIM), F32)],
        scratch_shapes=[pltpu.VMEM((N_PAIRS, PAIR_LANES, PAIR_LANES), F32), pltpu.VMEM((tc, D_RWKV), F32)],
        compiler_params=pltpu.CompilerParams(dimension_semantics=("arbitrary", "arbitrary"),
                                             vmem_limit_bytes=VMEM_LIMIT),
    )(hx, hx, hx, hx, *weights)


def _rwkv_tok_prep_kernel(r_ref, k_ref, v_ref, lora_ref, w0_ref, wd_ref, a0_ref, wa_ref, wg_ref, kk_ref, ka_ref,
                          rk_ref, seg_ref, segt_ref, w_o, k_o, a_o, b_o, g_o, bonus_o):
    prm = (w0_ref[...], wd_ref[...], a0_ref[...], wa_ref[...], wg_ref[...], kk_ref[...], ka_ref[...], rk_ref[...])
    decay, _, k_mod, neg_kk, b, gate, bonus = _rwkv_prep(r_ref[...], k_ref[...], v_ref[...], lora_ref[...], prm,
                                                          seg_ref[...], segt_ref[...])
    w_o[...] = decay
    k_o[...] = k_mod
    a_o[...] = neg_kk
    b_o[...] = b
    g_o[...] = gate
    bonus_o[...] = bonus


def _rwkv_tok_step_kernel(r_ref, w_ref, k_ref, v_ref, a_ref, b_ref, g_ref, bonus_ref, lnw_ref, lnb_ref, s0_ref,
                          o_ref, s_ref, *, tb):
    eye = (lax.broadcasted_iota(jnp.int32, (HEAD_DIM, HEAD_DIM), 0)
           == lax.broadcasted_iota(jnp.int32, (HEAD_DIM, HEAD_DIM), 1))

    def body(bi, carry):
        s = s0_ref[bi]
        sa = jnp.sum(s * a_ref[bi], axis=2, keepdims=True)
        v_col = jnp.sum(jnp.where(eye, v_ref[bi], 0.0), axis=2, keepdims=True)
        s = s * w_ref[bi] + sa * b_ref[bi] + v_col * k_ref[bi]
        s_ref[bi] = s
        y_col = jnp.sum(s * r_ref[bi], axis=2, keepdims=True)
        y = jnp.sum(jnp.where(eye, y_col, 0.0), axis=1, keepdims=True)
        o_ref[bi] = (_group_norm_head(y) * lnw_ref[0] + lnb_ref[0] + bonus_ref[bi]) * g_ref[bi]
        return carry

    lax.fori_loop(0, tb, body, 0)


def _rwkv_tok(hx, state, weights, *, tb):
    nb = hx.shape[0]
    (w0, wd, a0, wa, wg, k_k, k_a, r_k, ln_w, ln_b, seg, segt) = weights
    prep_w = (w0, wd, a0, wa, wg, k_k, k_a, r_k, seg, segt)
    colmap = lambda cb: (lambda i: (0, cb))
    in_specs = [pl.BlockSpec((nb, D_RWKV), colmap(0)),
                pl.BlockSpec((nb, D_RWKV), colmap(1)),
                pl.BlockSpec((nb, D_RWKV), colmap(2)),
                pl.BlockSpec((nb, 2 * LORA_PAD + GATE_LORA), colmap(COL_LORA // (2 * LORA_PAD + GATE_LORA)))]
    in_specs += [_const_spec(w.shape, 1) for w in prep_w]
    rows = jax.ShapeDtypeStruct((nb, D_RWKV), F32)
    prepped = pl.pallas_call(
        _rwkv_tok_prep_kernel,
        grid=(1,),
        in_specs=in_specs,
        out_specs=[pl.BlockSpec((nb, D_RWKV), lambda i: (0, 0))] * 6,
        out_shape=[rows] * 6,
        compiler_params=pltpu.CompilerParams(dimension_semantics=("arbitrary",),
                                             vmem_limit_bytes=VMEM_LIMIT),
    )(hx, hx, hx, hx, *prep_w)
    decay, k_mod, neg_kk, b, gate, bonus = prepped
    heads = lambda z: z.reshape(nb, RWKV_HEADS, 1, HEAD_DIM)
    vecs = [heads(z) for z in (hx[:, COL_R:COL_R + D_RWKV], decay, k_mod, hx[:, COL_V:COL_V + D_RWKV], neg_kk, b,
                               gate, bonus)]
    vec_spec = pl.BlockSpec((tb, RWKV_HEADS, 1, HEAD_DIM), lambda i: (i, 0, 0, 0))
    ln_spec = pl.BlockSpec((1, RWKV_HEADS, 1, HEAD_DIM), lambda i: (0, 0, 0, 0))
    state_spec = pl.BlockSpec((tb, RWKV_HEADS, HEAD_DIM, HEAD_DIM), lambda i: (i, 0, 0, 0))
    out, new_state = pl.pallas_call(
        functools.partial(_rwkv_tok_step_kernel, tb=tb),
        grid=(nb // tb,),
        in_specs=[vec_spec] * 8 + [ln_spec, ln_spec, state_spec],
        out_specs=[vec_spec, state_spec],
        out_shape=[jax.ShapeDtypeStruct((nb, RWKV_HEADS, 1, HEAD_DIM), F32),
                   jax.ShapeDtypeStruct(state.shape, F32)],
        compiler_params=pltpu.CompilerParams(dimension_semantics=("arbitrary",),
                                             vmem_limit_bytes=VMEM_LIMIT),
    )(*vecs, ln_w.reshape(1, RWKV_HEADS, 1, HEAD_DIM), ln_b.reshape(1, RWKV_HEADS, 1, HEAD_DIM), state)
    return out.reshape(nb, D_RWKV).astype(BF16), new_state


def _route(logits, bias):
    scores = _sigmoid(logits)
    sel = scores + bias
    rows = logits.shape[0]
    lane_i = lax.broadcasted_iota(jnp.int32, (rows, N_EXPERTS), 1)
    grp = lane_i // GROUP_SIZE
    lane = lane_i.astype(F32)
    neg_inf = -jnp.inf

    def take_max(x):
        m = jnp.max(x, axis=1, keepdims=True)
        idx = jnp.min(jnp.where(x == m, lane, float(N_EXPERTS)), axis=1, keepdims=True)
        return m, lane == idx

    gscore = []
    for g in range(N_GROUPS):
        x = jnp.where(grp == g, sel, neg_inf)
        m1, hit = take_max(x)
        m2 = jnp.max(jnp.where(hit, neg_inf, x), axis=1, keepdims=True)
        gscore.append(m1 + m2)
    emask = jnp.zeros((rows, N_EXPERTS), jnp.bool_)
    for g in range(N_GROUPS):
        ahead = jnp.zeros((rows, 1), jnp.int32)
        for g2 in range(N_GROUPS):
            if g2 == g:
                continue
            beats = (gscore[g2] >= gscore[g]) if g2 < g else (gscore[g2] > gscore[g])
            ahead = ahead + beats.astype(jnp.int32)
        emask = emask | ((grp == g) & (ahead < TOPK_GROUPS))
    x = jnp.where(emask, sel, neg_inf)
    chosen = jnp.zeros((rows, N_EXPERTS), jnp.bool_)
    slot_lane = lax.broadcasted_iota(jnp.int32, (rows, LANES), 1)
    top_idx = jnp.zeros((rows, LANES), F32)
    top_w = jnp.zeros((rows, LANES), F32)
    for kth in range(TOP_K):
        m = jnp.max(x, axis=1, keepdims=True)
        idx = jnp.min(jnp.where(x == m, lane, float(N_EXPERTS)), axis=1, keepdims=True)
        hit = lane == idx
        chosen = chosen | hit
        x = jnp.where(hit, neg_inf, x)
        top_idx = jnp.where(slot_lane == kth, idx, top_idx)
        top_w = jnp.where(slot_lane == kth, jnp.sum(jnp.where(hit, scores, 0.0), axis=1, keepdims=True), top_w)
    top_w = top_w / jnp.sum(top_w, axis=1, keepdims=True) * ROUTE_SCALE
    return top_idx.astype(jnp.int32), top_w, chosen.astype(F32)


def _out_proj_kernel(pool_ref, rwkv_ref, x_ref, g1_ref, sc_ref, sh_ref, wp_ref, wr_ref, npost_ref, npre_ref,
                     wrt_ref, rb_ref, sg_ref, su_ref, sd_ref, x1_ref, u2_ref, idx_ref, topw_ref, chosen_ref,
                     shared_ref):
    m = _dot(pool_ref[...], wp_ref[...]) + _dot(rwkv_ref[...], wr_ref[...])
    x1 = x_ref[...] + g1_ref[0] * _rms(m, npost_ref[...])
    x1_ref[...] = x1
    u2 = _rms(x1, npre_ref[...]) * (1.0 + sc_ref[0]) + sh_ref[0]
    u2_ref[...] = u2
    top_idx, top_w, chosen = _route(_dot_x3(u2, wrt_ref[...]), rb_ref[...])
    idx_ref[...] = top_idx
    topw_ref[...] = top_w
    chosen_ref[...] = chosen
    u2b = u2.astype(BF16)
    h = _silu(_dot(u2b, sg_ref[...])) * _dot(u2b, su_ref[...])
    shared_ref[...] = _dot(h.astype(BF16), sd_ref[...])


def _out_proj(pool_out, rwkv_out, x, g1, sc2, sh2, w_out_bf16, n_post, n_pre, w_router, router_bias, sg, su, sd, *,
              tm, tiles_per_seq):
    m, d = x.shape
    if tiles_per_seq is None:
        mod_spec = pl.BlockSpec((1, tm, d), lambda i: (i, 0, 0))
    else:
        mod_spec = pl.BlockSpec((1, 1, d), lambda i: (i // tiles_per_seq, 0, 0))
    row = lambda width: pl.BlockSpec((tm, width), lambda i: (i, 0))
    return pl.pallas_call(
        _out_proj_kernel,
        grid=(m // tm,),
        in_specs=[row(D_POOL), row(D_RWKV), row(d), mod_spec, mod_spec, mod_spec,
                  _const_spec((D_POOL, d), 1), _const_spec((D_RWKV, d), 1),
                  _const_spec((1, d), 1), _const_spec((1, d), 1),
                  _const_spec((d, N_EXPERTS), 1), _const_spec((1, N_EXPERTS), 1),
                  _const_spec(sg.shape, 1), _const_spec(su.shape, 1), _const_spec(sd.shape, 1)],
        out_specs=[row(d), row(d), row(LANES), row(LANES), row(N_EXPERTS), row(d)],
        out_shape=[jax.ShapeDtypeStruct((m, d), F32), jax.ShapeDtypeStruct((m, d), F32),
                   jax.ShapeDtypeStruct((m, LANES), jnp.int32), jax.ShapeDtypeStruct((m, LANES), F32),
                   jax.ShapeDtypeStruct((m, N_EXPERTS), F32), jax.ShapeDtypeStruct((m, d), F32)],
        compiler_params=pltpu.CompilerParams(dimension_semantics=("arbitrary",),
                                             vmem_limit_bytes=VMEM_LIMIT),
    )(pool_out, rwkv_out, x, g1, sc2, sh2, w_out_bf16[:D_POOL], w_out_bf16[D_POOL:],
      n_post.reshape(1, d), n_pre.reshape(1, d), w_router, router_bias.reshape(1, N_EXPERTS), sg, su, sd)


MOE_TM = 384


ROW_GROUP = 32


def _experts_kernel(te_ref, fresh_ref, wslot_ref, nexte_ref, groups_ref, nused_ref, idx_ref, idx_next_ref, ws_ref,
                    x_hbm, wg_hbm, wu_hbm, wd_hbm, ys_ref, wg_buf, wu_buf, wd_buf, g_bf, u_bf, d_bf, x_buf,
                    w_sems, x_sems):
    t = pl.program_id(0)
    n_used = nused_ref[0]

    def weight_copies(e, slot):
        return (pltpu.make_async_copy(wg_hbm.at[e], wg_buf.at[slot], w_sems.at[slot]),
                pltpu.make_async_copy(wu_hbm.at[e], wu_buf.at[slot], w_sems.at[slot]),
                pltpu.make_async_copy(wd_hbm.at[e], wd_buf.at[slot], w_sems.at[slot]))

    def start_rows(table_ref, slot, n_groups):
        for g in range(MOE_TM // ROW_GROUP):
            @pl.when(g < n_groups)
            def _():
                for j in range(g * ROW_GROUP, (g + 1) * ROW_GROUP):
                    pltpu.make_async_copy(x_hbm.at[pl.ds(table_ref[0, 0, j], 1)], x_buf.at[slot, pl.ds(j, 1)],
                                          x_sems.at[slot]).start()

    @pl.when(t == 0)
    def _():
        for cp in weight_copies(te_ref[0], 0):
            cp.start(priority=1)
        x_buf[...] = jnp.zeros(x_buf.shape, F32)
        start_rows(idx_ref, 0, groups_ref[0])

    @pl.when(t + 1 < n_used)
    def _():
        start_rows(idx_next_ref, (t + 1) % 2, groups_ref[t + 1])

    @pl.when(jnp.logical_and(t < n_used, fresh_ref[t] == 1))
    def _():
        slot = wslot_ref[t]
        for cp in weight_copies(te_ref[t], slot):
            cp.wait()

        @pl.when(nexte_ref[t] >= 0)
        def _():
            for cp in weight_copies(nexte_ref[t], 1 - slot):
                cp.start(priority=1)

        g_bf[...] = wg_buf[slot].astype(BF16)
        u_bf[...] = wu_buf[slot].astype(BF16)
        d_bf[...] = wd_buf[slot].astype(BF16)

    @pl.when(t < n_used)
    def _():
        cur = t % 2
        n_rows = pl.multiple_of(groups_ref[t] * ROW_GROUP, ROW_GROUP)
        pltpu.make_async_copy(x_hbm.at[pl.ds(0, n_rows)], x_buf.at[cur, pl.ds(0, n_rows)], x_sems.at[cur]).wait()
        x = x_buf[cur].astype(BF16)
        h = _silu(_dot(x, g_bf[...])) * _dot(x, u_bf[...])
        diag = (lax.broadcasted_iota(jnp.int32, (MOE_TM, MOE_TM), 0)
                == lax.broadcasted_iota(jnp.int32, (MOE_TM, MOE_TM), 1))
        w_col = jnp.sum(jnp.where(diag, ws_ref[0], 0.0), axis=1, keepdims=True)
        ys_ref[...] = _pack_bf16_halves(_dot(h.astype(BF16), d_bf[...]) * w_col)

    @pl.when(t >= n_used)
    def _():
        ys_ref[...] = jnp.zeros(ys_ref.shape, jnp.uint32)


def _experts(tables, x_all, exp_gate, exp_up, exp_down):
    tile_expert, fresh, wslot, next_e, row_groups, n_used, slot_tok, slot_w = tables
    d = x_all.shape[1]
    n_tiles = tile_expert.shape[0]
    n_slots = n_tiles * MOE_TM
    idx_block = (1, 1, MOE_TM)
    any_spec = pl.BlockSpec(memory_space=pl.ANY)
    live = lambda t, nused: jnp.where(t < nused[0], t, n_tiles - 1)
    grid_spec = pltpu.PrefetchScalarGridSpec(
        num_scalar_prefetch=6,
        grid=(n_tiles,),
        in_specs=[pl.BlockSpec(idx_block, lambda t, *s: (live(t, s[5]), 0, 0), memory_space=pltpu.SMEM),
                  pl.BlockSpec(idx_block, lambda t, *s: (live(t + 1, s[5]), 0, 0), memory_space=pltpu.SMEM),
                  pl.BlockSpec(idx_block, lambda t, *s: (live(t, s[5]), 0, 0)),
                  any_spec, any_spec, any_spec, any_spec],
        out_specs=pl.BlockSpec((MOE_TM, d // 2), lambda t, *s: (live(t, s[5]), 0)),
        scratch_shapes=[pltpu.VMEM((2, d, D_EXPERT), F32), pltpu.VMEM((2, d, D_EXPERT), F32),
                        pltpu.VMEM((2, D_EXPERT, d), F32),
                        pltpu.VMEM((d, D_EXPERT), BF16), pltpu.VMEM((d, D_EXPERT), BF16),
                        pltpu.VMEM((D_EXPERT, d), BF16),
                        pltpu.VMEM((2, MOE_TM, d), F32),
                        pltpu.SemaphoreType.DMA((2,)), pltpu.SemaphoreType.DMA((2,))],
    )
    idx_tiles = slot_tok.reshape(n_tiles, 1, MOE_TM)
    return pl.pallas_call(
        _experts_kernel,
        grid_spec=grid_spec,
        out_shape=jax.ShapeDtypeStruct((n_slots, d // 2), jnp.uint32),
        compiler_params=pltpu.CompilerParams(dimension_semantics=("arbitrary",),
                                             vmem_limit_bytes=VMEM_LIMIT),
    )(tile_expert, fresh, wslot, next_e, row_groups, n_used, idx_tiles, idx_tiles,
      slot_w.reshape(n_tiles, 1, MOE_TM), x_all, exp_gate, exp_up, exp_down)


ASSIGN_BITS = 17


def _dispatch(chosen, top_idx, top_w):
    n = chosen.shape[0]
    n_assign = n * TOP_K
    low_mask = (1 << ASSIGN_BITS) - 1
    assert n_assign + MOE_TM <= low_mask and N_EXPERTS << ASSIGN_BITS < 2 ** 31
    n_tiles = -(-n_assign // MOE_TM) + N_EXPERTS
    n_slots = n_tiles * MOE_TM
    sel = chosen.astype(jnp.int32)
    counts = jnp.sum(sel, axis=0)
    rank = jnp.cumsum(sel, axis=0) - sel
    tiles_e = (counts + MOE_TM - 1) // MOE_TM
    tile_end = jnp.cumsum(tiles_e)
    slot = (tile_end - tiles_e)[None, :] * MOE_TM + rank
    expert_ids = jnp.arange(N_EXPERTS, dtype=jnp.int32)
    slot_of_pair = jnp.stack([jnp.sum(jnp.where(top_idx[:, k:k + 1] == expert_ids[None, :], slot, 0), axis=1)
                              for k in range(TOP_K)])
    shift = 1 << ASSIGN_BITS
    int_max = jnp.iinfo(jnp.int32).max
    real_key = top_idx.reshape(-1) * shift + jnp.arange(n_assign, dtype=jnp.int32)
    pad_j = jnp.arange(MOE_TM - 1, dtype=jnp.int32)[None, :]
    pad_key = jnp.where(pad_j < (tiles_e * MOE_TM - counts)[:, None],
                        expert_ids[:, None] * shift + n_assign + pad_j, int_max)
    n_rest = n_slots - n_assign
    keys = jnp.concatenate([real_key, pad_key.reshape(-1),
                            jnp.full((n_rest - pad_key.size,), int_max, jnp.int32)])
    weights = jnp.concatenate([top_w.reshape(-1), jnp.zeros((n_rest,), F32)])
    sorted_key, sorted_w = lax.sort((keys, weights), num_keys=1)
    pair = sorted_key & low_mask
    is_real = pair < n_assign
    slot_tok = jnp.where(is_real, pair // TOP_K, 0)
    slot_w = jnp.where(is_real, sorted_w, 0.0)
    n_used = tile_end[-1]
    tile_ids = jnp.minimum(jnp.arange(n_tiles, dtype=jnp.int32), n_used - 1)
    tile_expert = jnp.minimum(jnp.searchsorted(tile_end, tile_ids, side="right"), N_EXPERTS - 1).astype(jnp.int32)
    in_use = jnp.arange(n_tiles, dtype=jnp.int32) < n_used
    prev_expert = jnp.concatenate([jnp.full((1,), -1, jnp.int32), tile_expert[:-1]])
    fresh = jnp.logical_and(in_use, tile_expert != prev_expert).astype(jnp.int32)
    wslot = (jnp.cumsum(fresh) - 1) % 2
    later = jnp.where(counts > 0, expert_ids, N_EXPERTS)
    next_used = jnp.concatenate([lax.cummin(later, reverse=True)[1:], jnp.full((1,), N_EXPERTS, jnp.int32)])
    next_used = jnp.where(next_used >= N_EXPERTS, -1, next_used)
    tile_in_expert = jnp.arange(n_tiles, dtype=jnp.int32) - (tile_end - tiles_e)[tile_expert]
    rows_in_tile = jnp.clip(counts[tile_expert] - tile_in_expert * MOE_TM, 0, MOE_TM)
    row_groups = jnp.where(in_use, (rows_in_tile + ROW_GROUP - 1) // ROW_GROUP, 0)
    tables = (tile_expert, fresh, wslot.astype(jnp.int32), next_used[tile_expert].astype(jnp.int32),
              row_groups.astype(jnp.int32), n_used.reshape(1).astype(jnp.int32), slot_tok.astype(jnp.int32), slot_w)
    return slot_of_pair, tables


def _final_kernel(idx_ref, idx_next_ref, shared_ref, ys_hbm, x1_ref, g2_ref, npost_ref, o_ref, rows_buf, sems, *,
                  tm):
    i = pl.program_id(0)
    n_rows = TOP_K * tm

    def start_gather(table_ref, buf_slot):
        for j in range(n_rows):
            pltpu.make_async_copy(ys_hbm.at[pl.ds(table_ref[0, 0, j], 1)], rows_buf.at[buf_slot, pl.ds(j, 1)],
                                  sems.at[buf_slot]).start(priority=j % 2)

    @pl.when(i == 0)
    def _():
        start_gather(idx_ref, 0)

    @pl.when(i + 1 < pl.num_programs(0))
    def _():
        start_gather(idx_next_ref, (i + 1) % 2)

    cur = i % 2
    pltpu.make_async_copy(ys_hbm.at[pl.ds(0, n_rows)], rows_buf.at[cur], sems.at[cur]).wait()
    left, right = _unpack_bf16_halves(rows_buf[cur, 0:tm])
    for kth in range(1, TOP_K):
        l_k, r_k = _unpack_bf16_halves(rows_buf[cur, kth * tm:(kth + 1) * tm])
        left, right = left + l_k, right + r_k
    routed = jnp.concatenate([left, right], axis=1)
    o_ref[...] = x1_ref[...] + g2_ref[0] * _rms(routed + shared_ref[...], npost_ref[...])


def _final(shared, ys, slot_tiles, x1, g2, n_post, *, tm, tiles_per_seq, tile_offset):
    m, d = x1.shape
    nt = m // tm
    if tiles_per_seq is None:
        mod_spec = pl.BlockSpec((1, tm, d), lambda i: (i, 0, 0))
    else:
        mod_spec = pl.BlockSpec((1, 1, d), lambda i: (i // tiles_per_seq, 0, 0))
    row = pl.BlockSpec((tm, d), lambda i: (i, 0))
    idx_block = (1, 1, TOP_K * tm)
    return pl.pallas_call(
        functools.partial(_final_kernel, tm=tm),
        grid=(nt,),
        in_specs=[pl.BlockSpec(idx_block, lambda i: (i + tile_offset, 0, 0), memory_space=pltpu.SMEM),
                  pl.BlockSpec(idx_block, lambda i: (jnp.minimum(i + 1, nt - 1) + tile_offset, 0, 0),
                               memory_space=pltpu.SMEM),
                  row, pl.BlockSpec(memory_space=pl.ANY), row, mod_spec, _const_spec((1, d), 1)],
        out_specs=row,
        out_shape=jax.ShapeDtypeStruct((m, d), F32),
        scratch_shapes=[pltpu.VMEM((2, TOP_K * tm, d // 2), jnp.uint32), pltpu.SemaphoreType.DMA((2,))],
        compiler_params=pltpu.CompilerParams(dimension_semantics=("arbitrary",),
                                             vmem_limit_bytes=VMEM_LIMIT),
    )(slot_tiles, slot_tiles, shared, ys, x1, g2, n_post.reshape(1, d))


def _pad_cols_in(w_in, mu_shift):
    d = w_in.shape[0]
    hp = w_in[:, :D_POOL]
    hr = w_in[:, D_POOL:]
    cuts = [3 * D_RWKV, 3 * D_RWKV + DECAY_LORA, 3 * D_RWKV + DECAY_LORA + AAA_LORA]
    rkv, wl, al, gl = jnp.split(hr, cuts, axis=1)
    zpad = lambda z: jnp.pad(z, ((0, 0), (0, LORA_PAD - z.shape[1])))
    w = jnp.concatenate([rkv, hp, zpad(wl), zpad(al), gl], axis=1).astype(BF16)
    mu_rkv, mu_wl, mu_al, mu_gl = jnp.split(mu_shift, cuts)
    zp1 = lambda z: jnp.pad(z, (0, LORA_PAD - z.shape[0]))
    mu = jnp.concatenate([mu_rkv, jnp.zeros((D_POOL,), F32), zp1(mu_wl), zp1(mu_al), mu_gl])
    assert w.shape == (d, N_IN_PAD) and mu.shape == (N_IN_PAD,)
    return w, mu


def kernel(x_prompt, x_sample, c_prompt, c_sample, state_shift, state_pool, state_wkv, w_ada, b_ada, norm_pre_mix, norm_post_mix, norm_pre_ffn, norm_post_ffn, w_in, mu_shift, pool_w, pool_scale, w0, w_decay_up, a0, w_aaa_up, w_gate_up, k_k, k_a, r_k, ln_x_w, ln_x_b, w_out, w_router, router_bias, exp_gate, exp_up, exp_down, sh_gate, sh_up, sh_down):
    bp, seq, d = x_prompt.shape
    bs = x_sample.shape[0]
    assert w_ada.shape[0] == 1 and x_sample.shape[1] == 1
    np_rows = bp * seq

    c_all = jnp.concatenate([c_prompt, c_sample], axis=0)
    pad_rows = (-c_all.shape[0]) % 16
    ada = _ada(jnp.pad(c_all, ((0, pad_rows), (0, 0))), w_ada[0], b_ada[0])
    sh1, sc1, g1, sh2, sc2, g2 = jnp.split(ada, 6, axis=-1)
    pr = lambda z: z[:bp].reshape(bp, 1, d)
    TS = min(128, bs)
    sr = lambda z: z[bp:bp + bs].reshape(bs // TS, TS, d)

    w_in_p, mu_p = _pad_cols_in(w_in[0], mu_shift[0])
    xp = x_prompt.reshape(np_rows, d)
    xs = x_sample.reshape(bs, d)

    TM_IN = min(1024, seq)
    hx_p, u_tail = _in_proj(xp, norm_pre_mix[0], pr(sc1), pr(sh1), w_in_p, mu_p, tm=TM_IN,
                            tiles_per_seq=seq // TM_IN)
    hx_s, u_s = _in_proj(xs, norm_pre_mix[0], sr(sc1), sr(sh1), w_in_p, mu_p, tm=TS, tiles_per_seq=1,
                         prev=state_shift[0])
    new_shift_prompt = u_tail.reshape(bp, seq // TM_IN, 8, d)[:, -1, -1][None]
    new_shift_sample = u_s[None]

    pool_w_b = pool_w[0].astype(BF16)
    TC_POOL = min(512, seq)
    pool_p = _pool(hx_p, COL_POOL // D_POOL, pool_w_b, pool_scale[0], tc=TC_POOL, tiles_per_seq=seq // TC_POOL,
                   full_count=False)
    hp_p = hx_p[:, COL_POOL:COL_POOL + D_POOL].reshape(bp, seq, D_POOL)
    new_pool_prompt = hp_p[:, seq - POOL_BUF:][None]
    hp_s = hx_s[:, COL_POOL:COL_POOL + D_POOL]
    ext_s = jnp.concatenate([state_pool[0], hp_s[:, None, :]], axis=1)
    assert PAST_LEN + 1 >= max(POOL_WINDOWS)
    pool_s = _pool(ext_s.reshape(bs * (POOL_BUF + 1), D_POOL), 0, pool_w_b, pool_scale[0],
                   tc=bs * (POOL_BUF + 1), tiles_per_seq=1, full_count=True)
    pool_s = pool_s.reshape(bs, POOL_BUF + 1, D_POOL)[:, -1]
    new_pool_sample = ext_s[:, 1:][None]

    rw = _rwkv_weights(w0[0], w_decay_up[0], a0[0], w_aaa_up[0], w_gate_up[0], k_k[0], k_a[0], r_k[0],
                       ln_x_w[0], ln_x_b[0])
    rwkv_p, wkv_p = _rwkv_seq(hx_p, rw, batch=bp, seq=seq)
    rwkv_s, wkv_s = _rwkv_tok(hx_s, state_wkv[0], rw, tb=8)

    w_out_b = w_out[0].astype(BF16)
    assert np_rows % TS == 0 and seq % TS == 0
    sg, su, sd = sh_gate[0].astype(BF16), sh_up[0].astype(BF16), sh_down[0].astype(BF16)
    TM_OUT = min(256, seq)
    outs_p = _out_proj(pool_p, rwkv_p, xp, pr(g1), pr(sc2), pr(sh2), w_out_b, norm_post_mix[0], norm_pre_ffn[0],
                       w_router[0], router_bias[0], sg, su, sd, tm=TM_OUT, tiles_per_seq=seq // TM_OUT)
    outs_s = _out_proj(pool_s, rwkv_s, xs, sr(g1), sr(sc2), sr(sh2), w_out_b, norm_post_mix[0], norm_pre_ffn[0],
                       w_router[0], router_bias[0], sg, su, sd, tm=TS, tiles_per_seq=None)
    (x1_p, u2_p, idx_p, topw_p, chosen_p, shared_p), (x1_s, u2_s, idx_s, topw_s, chosen_s, shared_s) = outs_p, outs_s
    u2_all = jnp.concatenate([u2_p, u2_s], axis=0)
    idx_all = jnp.concatenate([idx_p[:, :TOP_K], idx_s[:, :TOP_K]], axis=0)
    topw_all = jnp.concatenate([topw_p[:, :TOP_K], topw_s[:, :TOP_K]], axis=0)
    chosen_all = jnp.concatenate([chosen_p, chosen_s], axis=0)

    n_all = np_rows + bs
    slot_of_pair, tables = _dispatch(chosen_all, idx_all, topw_all)
    ys = _experts(tables, u2_all, exp_gate[0], exp_up[0], exp_down[0])

    slot_tiles = slot_of_pair.reshape(TOP_K, n_all // TS, TS).transpose(1, 0, 2).reshape(n_all // TS, 1, TOP_K * TS)
    y_p = _final(shared_p, ys, slot_tiles, x1_p, pr(g2), norm_post_ffn[0], tm=TS, tiles_per_seq=seq // TS,
                 tile_offset=0)
    y_s = _final(shared_s, ys, slot_tiles, x1_s, sr(g2), norm_post_ffn[0], tm=TS, tiles_per_seq=None,
                 tile_offset=np_rows // TS)

    return (y_p.reshape(bp, seq, d), y_s.reshape(bs, 1, d), new_shift_prompt, new_pool_prompt, wkv_p[None],
            new_shift_sample, new_pool_sample, wkv_s[None])
```
